```python
import math
import jax, jax.numpy as jnp
from jax import lax
import numpy as np

D_MODEL = 1024
BATCH = 32
SEQ = 256
DEPTH = 1
DEC_BATCH = 4
DEC_SEQ = 1024
PAST_LEN = 256

GRID_W = 64
FOURIER_GROUPS = 4
FOURIER_GROUP_DIM = 128
FOURIER_WIDTH = FOURIER_GROUPS * FOURIER_GROUP_DIM
MLA_HEADS = 8
Q_LORA_RANK = 256
KV_LORA_RANK = 128
QK_NOPE_DIM = 64
QK_ROPE_DIM = 32
V_HEAD_DIM = 64
ROPE_THETA = 10000.0
MIX_WIDTH = FOURIER_WIDTH + Q_LORA_RANK + KV_LORA_RANK + QK_ROPE_DIM
N_BRANCHES = 2
Q_BLOCK = 128
N_EXPERTS = 32
TOP_K = 4
D_EXPERT = 1024
SWIGLU_LIMIT = 7.0
SWIGLU_ALPHA = 1.702
LN_EPS = 1e-5
RMS_EPS = 1e-6
DEEPNORM_ALPHA = (2.0 * DEPTH) ** 0.25
DEEPNORM_BETA = (8.0 * DEPTH) ** -0.25

kernel_name = 'hybrid_fourier_mla_moe_diffusion_step'


def _norm_noaffine(x):
    xf = x.astype(jnp.float32)
    mu = jnp.mean(xf, axis=-1, keepdims=True)
    var = jnp.mean(jnp.square(xf - mu), axis=-1, keepdims=True)
    return ((xf - mu) * lax.rsqrt(var + LN_EPS)).astype(x.dtype)


def _layer_norm(x, g, b):
    xf = x.astype(jnp.float32)
    mu = jnp.mean(xf, axis=-1, keepdims=True)
    var = jnp.mean(jnp.square(xf - mu), axis=-1, keepdims=True)
    y = (xf - mu) * lax.rsqrt(var + LN_EPS) * g.astype(jnp.float32) + b.astype(jnp.float32)
    return y.astype(x.dtype)


def _rms_norm(x, g):
    xf = x.astype(jnp.float32)
    y = xf * lax.rsqrt(jnp.mean(jnp.square(xf), axis=-1, keepdims=True) + RMS_EPS) * g.astype(jnp.float32)
    return y.astype(x.dtype)


def _axial_rope_tables(n_tokens):
    rows = n_tokens // GRID_W
    row = jnp.repeat(jnp.arange(rows, dtype=jnp.float32), GRID_W)
    col = jnp.tile(jnp.arange(GRID_W, dtype=jnp.float32), rows)
    axis_dim = QK_ROPE_DIM // 2
    inv_freq = ROPE_THETA ** (-jnp.arange(0, axis_dim, 2, dtype=jnp.float32) / axis_dim)
    ang_r = row[:, None] * inv_freq[None, :]
    ang_c = col[:, None] * inv_freq[None, :]
    ang = jnp.concatenate([ang_r, ang_r, ang_c, ang_c], axis=-1)
    return jnp.cos(ang), jnp.sin(ang)


def _rotate_half(u):
    u1, u2 = jnp.split(u, 2, axis=-1)
    return jnp.concatenate([-u2, u1], axis=-1)


def _apply_axial_rope(x, cos, sin):
    xf = x.astype(jnp.float32)
    xr, xc = jnp.split(xf, 2, axis=-1)
    rot = jnp.concatenate([_rotate_half(xr), _rotate_half(xc)], axis=-1)
    return (xf * cos + rot * sin).astype(x.dtype)


def _attention(q, k, v):
    b, tq, h, dk = q.shape
    scale = dk ** -0.5
    n_blk = tq // Q_BLOCK
    kf = k.astype(jnp.float32)
    vf = v.astype(jnp.float32)
    qb = q.reshape(b, n_blk, Q_BLOCK, h, dk).transpose(1, 0, 2, 3, 4)

    def one_block(q_blk):
        s = jnp.einsum('bqhd,bkhd->bhqk', q_blk.astype(jnp.float32), kf) * scale
        p = jax.nn.softmax(s, axis=-1)
        return jnp.einsum('bhqk,bkhd->bqhd', p, vf).astype(q.dtype)

    out = lax.map(one_block, qb)
    return out.transpose(1, 0, 2, 3, 4).reshape(b, tq, h, v.shape[-1])


def _fourier_branch(f_in, w_fourier_o):
    b, t, _ = f_in.shape
    fg = f_in.astype(jnp.float32).reshape(b, t, FOURIER_GROUPS, FOURIER_GROUP_DIM)
    mixed = jnp.real(jnp.fft.fft2(fg, axes=(1, 3), norm='ortho'))
    return mixed.reshape(b, t, FOURIER_WIDTH).astype(f_in.dtype) @ w_fourier_o


def _mla_queries(q_lat, q_norm_g, w_q_up):
    b, t, _ = q_lat.shape
    q = (_rms_norm(q_lat, q_norm_g) @ w_q_up).reshape(b, t, MLA_HEADS, QK_NOPE_DIM + QK_ROPE_DIM)
    return q[..., :QK_NOPE_DIM], q[..., QK_NOPE_DIM:]


def _mla_expand_kv(ckv, w_kv_up):
    b, t, _ = ckv.shape
    kv = (ckv @ w_kv_up).reshape(b, t, MLA_HEADS, QK_NOPE_DIM + V_HEAD_DIM)
    return kv[..., :QK_NOPE_DIM], kv[..., QK_NOPE_DIM:]


def _token_mixer(h, ctx_ckv, ctx_krope, rope, w_in, q_norm_g, w_q_up, kv_norm_g, w_kv_up,
                 w_fourier_o, w_mla_o, w_gate, b_gate, w_out):
    b, t, _ = h.shape
    proj = h @ w_in
    f_in, q_lat, kv_lat, k_rope = jnp.split(
        proj, [FOURIER_WIDTH, FOURIER_WIDTH + Q_LORA_RANK, FOURIER_WIDTH + Q_LORA_RANK + KV_LORA_RANK], axis=-1)
    f_out = _fourier_branch(f_in, w_fourier_o)
    ckv = _rms_norm(kv_lat, kv_norm_g)
    q_nope, q_rope = _mla_queries(q_lat, q_norm_g, w_q_up)
    k_nope, v = _mla_expand_kv(ckv, w_kv_up)
    k_rope_keys = k_rope
    if rope is not None:
        cos, sin = rope
        q_rope = _apply_axial_rope(q_rope, cos[:, None, :], sin[:, None, :])
        k_rope_keys = _apply_axial_rope(k_rope, cos, sin)
        c_nope, c_v = _mla_expand_kv(ctx_ckv, w_kv_up)
        k_nope = jnp.concatenate([k_nope, c_nope], axis=1)
        v = jnp.concatenate([v, c_v], axis=1)
        k_rope_keys = jnp.concatenate([k_rope_keys, ctx_krope], axis=1)
    tk = k_nope.shape[1]
    q = jnp.concatenate([q_nope, q_rope], axis=-1)
    k = jnp.concatenate(
        [k_nope, jnp.broadcast_to(k_rope_keys[:, :, None, :], (b, tk, MLA_HEADS, QK_ROPE_DIM))], axis=-1)
    m_out = _attention(q, k, v).reshape(b, t, MLA_HEADS * V_HEAD_DIM) @ w_mla_o
    g_f, g_m = jnp.split(jax.nn.sigmoid(h @ w_gate + b_gate), N_BRANCHES, axis=-1)
    out = (g_f * f_out + g_m * m_out) @ w_out
    return out, ckv, k_rope


def _moe(h, w_router, b_router, w_gate_e, b_gate_e, w_up_e, b_up_e, w_down_e, b_down_e):
    b, t, d = h.shape
    hf = h.reshape(b * t, d)
    logits = (hf @ w_router + b_router).astype(jnp.float32)
    top_vals, top_idx = lax.top_k(logits, TOP_K)
    top_w = jax.nn.softmax(top_vals, axis=-1)
    combine = jnp.einsum('nk,nke->ne', top_w,
                         jax.nn.one_hot(top_idx, N_EXPERTS, dtype=jnp.float32)).astype(h.dtype)
    out = jnp.zeros_like(hf)
    for e in range(N_EXPERTS):
        gt = jnp.minimum(hf @ w_gate_e[e] + b_gate_e[e], SWIGLU_LIMIT)
        up = jnp.clip(hf @ w_up_e[e] + b_up_e[e], -SWIGLU_LIMIT, SWIGLU_LIMIT)
        act = gt * jax.nn.sigmoid(SWIGLU_ALPHA * gt) * (up + 1.0)
        out = out + combine[:, e:e + 1] * (act @ w_down_e[e] + b_down_e[e])
    return out.reshape(b, t, d)


def _trunk_layer(x, mod, ctx_ckv, ctx_krope, rope, w_in, q_norm_g, w_q_up, kv_norm_g, w_kv_up,
                 w_fourier_o, w_mla_o, w_gate, b_gate, w_out, ln1_g, ln1_b, w_router, b_router,
                 w_gate_e, b_gate_e, w_up_e, b_up_e, w_down_e, b_down_e, ln2_g, ln2_b):
    shift1, scale1, gate1, shift2, scale2, gate2 = [m[:, None, :] for m in jnp.split(mod, 6, axis=-1)]
    h1 = _norm_noaffine(x) * (1.0 + scale1) + shift1
    mix, ckv, krope = _token_mixer(h1, ctx_ckv, ctx_krope, rope, w_in, q_norm_g, w_q_up, kv_norm_g,
                                   w_kv_up, w_fourier_o, w_mla_o, w_gate, b_gate, w_out)
    x = _layer_norm(DEEPNORM_ALPHA * x + gate1 * mix, ln1_g, ln1_b)
    h2 = _norm_noaffine(x) * (1.0 + scale2) + shift2
    ffn = _moe(h2, w_router, b_router, w_gate_e, b_gate_e, w_up_e, b_up_e, w_down_e, b_down_e)
    x = _layer_norm(DEEPNORM_ALPHA * x + gate2 * ffn, ln2_g, ln2_b)
    return x, ckv, krope


def setup_inputs(seed: int = 0) -> dict:
    key = jax.random.key(seed)
    ks = jax.random.split(key, 40)
    nrm = jax.random.normal
    d = D_MODEL
    qk_dim = QK_NOPE_DIM + QK_ROPE_DIM
    kv_dim = QK_NOPE_DIM + V_HEAD_DIM
    f32 = jnp.float32
    return {
        'x_prompt': nrm(ks[0], (BATCH, SEQ, d), f32),
        'x_sample': nrm(ks[1], (DEC_BATCH, DEC_SEQ, d), f32),
        'cache_ckv': nrm(ks[2], (DEC_BATCH, DEPTH, PAST_LEN, KV_LORA_RANK), f32),
        'cache_krope': nrm(ks[3], (DEC_BATCH, DEPTH, PAST_LEN, QK_ROPE_DIM), f32),
        'c': nrm(ks[4], (DEC_BATCH, d), f32),
        'c_ctx': nrm(ks[5], (d,), f32),
        'w_mod': nrm(ks[6], (DEPTH, d, 6 * d), f32) * (0.5 * d ** -0.5),
        'b_mod': 0.01 * nrm(ks[7], (DEPTH, 6 * d), f32),
        'w_in': nrm(ks[8], (DEPTH, d, MIX_WIDTH), f32) * d ** -0.5,
        'q_norm_g': 1.0 + 0.01 * nrm(ks[9], (DEPTH, Q_LORA_RANK), f32),
        'w_q_up': nrm(ks[10], (DEPTH, Q_LORA_RANK, MLA_HEADS * qk_dim), f32) * Q_LORA_RANK ** -0.5,
        'kv_norm_g': 1.0 + 0.01 * nrm(ks[11], (DEPTH, KV_LORA_RANK), f32),
        'w_kv_up': nrm(ks[12], (DEPTH, KV_LORA_RANK, MLA_HEADS * kv_dim), f32) * KV_LORA_RANK ** -0.5,
        'w_fourier_o': nrm(ks[13], (DEPTH, FOURIER_WIDTH, d), f32) * FOURIER_WIDTH ** -0.5,
        'w_mla_o': nrm(ks[14], (DEPTH, MLA_HEADS * V_HEAD_DIM, d), f32) * (MLA_HEADS * V_HEAD_DIM) ** -0.5,
        'w_gate': nrm(ks[15], (DEPTH, d, N_BRANCHES * d), f32) * d ** -0.5,
        'b_gate': 0.01 * nrm(ks[16], (DEPTH, N_BRANCHES * d), f32),
        'w_out': nrm(ks[17], (DEPTH, d, d), f32) * (DEEPNORM_BETA * d ** -0.5),
        'ln1_g': 1.0 + 0.01 * nrm(ks[18], (DEPTH, d), f32),
        'ln1_b': 0.01 * nrm(ks[19], (DEPTH, d), f32),
        'w_router': nrm(ks[20], (DEPTH, d, N_EXPERTS), f32) * d ** -0.5,
        'b_router': 0.01 * nrm(ks[21], (DEPTH, N_EXPERTS), f32),
        'w_gate_e': nrm(ks[22], (DEPTH, N_EXPERTS, d, D_EXPERT), f32) * d ** -0.5,
        'b_gate_e': 0.01 * nrm(ks[23], (DEPTH, N_EXPERTS, D_EXPERT), f32),
        'w_up_e': nrm(ks[24], (DEPTH, N_EXPERTS, d, D_EXPERT), f32) * d ** -0.5,
        'b_up_e': 0.01 * nrm(ks[25], (DEPTH, N_EXPERTS, D_EXPERT), f32),
        'w_down_e': nrm(ks[26], (DEPTH, N_EXPERTS, D_EXPERT, d), f32) * (DEEPNORM_BETA * D_EXPERT ** -0.5),
        'b_down_e': 0.01 * nrm(ks[27], (DEPTH, N_EXPERTS, d), f32),
        'ln2_g': 1.0 + 0.01 * nrm(ks[28], (DEPTH, d), f32),
        'ln2_b': 0.01 * nrm(ks[29], (DEPTH, d), f32),
    }


def reference(x_prompt, x_sample, cache_ckv, cache_krope, c, c_ctx, w_mod, b_mod, w_in, q_norm_g,
              w_q_up, kv_norm_g, w_kv_up, w_fourier_o, w_mla_o, w_gate, b_gate, w_out, ln1_g, ln1_b,
              w_router, b_router, w_gate_e, b_gate_e, w_up_e, b_up_e, w_down_e, b_down_e, ln2_g, ln2_b):
    rope = _axial_rope_tables(x_sample.shape[1])
    y_prompt = x_prompt
    y_sample = x_sample
    ckv_layers = []
    krope_layers = []
    for l in range(DEPTH):
        layer_w = (w_in[l], q_norm_g[l], w_q_up[l], kv_norm_g[l], w_kv_up[l], w_fourier_o[l], w_mla_o[l],
                   w_gate[l], b_gate[l], w_out[l], ln1_g[l], ln1_b[l], w_router[l], b_router[l],
                   w_gate_e[l], b_gate_e[l], w_up_e[l], b_up_e[l], w_down_e[l], b_down_e[l],
                   ln2_g[l], ln2_b[l])
        mod_ctx = jax.nn.silu(c_ctx)[None, :] @ w_mod[l] + b_mod[l]
        mod_lat = jax.nn.silu(c) @ w_mod[l] + b_mod[l]
        y_prompt, ckv, krope = _trunk_layer(y_prompt, mod_ctx, None, None, None, *layer_w)
        ckv_layers.append(ckv)
        krope_layers.append(krope)
        y_sample, _, _ = _trunk_layer(y_sample, mod_lat, cache_ckv[:, l], cache_krope[:, l], rope, *layer_w)
    new_ckv = jnp.stack(ckv_layers, axis=1)
    new_krope = jnp.stack(krope_layers, axis=1)
    return (y_prompt, y_sample, new_ckv, new_krope)
```

```python
import functools
import math

import numpy as np
import jax
import jax.numpy as jnp
from jax import lax
from jax.experimental import pallas as pl
from jax.experimental.pallas import tpu as pltpu

D = 1024
N_CTX_SEQ, T_CTX = 32, 256
N_LAT_SEQ, T_LAT = 4, 1024
PAST = 256
N_CTX = N_CTX_SEQ * T_CTX
N_LAT = N_LAT_SEQ * T_LAT
N_TOK = N_CTX + N_LAT
FW = 512
FG = 128
N_FG = FW // FG
HEADS = 8
QL = 256
KVL = 128
NOPE = 64
ROPE = 32
VH = 64
N_EXP = 32
TOPK = 4
D_EXP = 1024
SWIGLU_LIMIT = 7.0
SWIGLU_ALPHA = 1.702
LN_EPS = 1e-5
RMS_EPS = 1e-6
DN_ALPHA = 2.0 ** 0.25
ATT_SCALE = float(NOPE + ROPE) ** -0.5
ROPE_THETA = 10000.0
GRID_W = 64

LANES = 128
TB = 256
ROW_TILE = 256
N_SORT = N_TOK * TOPK
N_TILES = N_SORT // ROW_TILE
N_ITEMS = N_TILES + N_EXP
VMEM_LIMIT = 56 * 1024 * 1024
NEG_BIG = -1e30
RANK_BITS = 16

F32 = jnp.float32
BF16 = jnp.bfloat16
HIGHEST = lax.Precision.HIGHEST


def _dot(a, b):
    return jnp.dot(a, b, preferred_element_type=F32)


def _dot_nt(a, b):
    return lax.dot_general(a, b, (((1,), (1,)), ((), ())), preferred_element_type=F32)


def _params(sem):
    return pltpu.CompilerParams(dimension_semantics=sem, vmem_limit_bytes=VMEM_LIMIT)


def _full(shape):
    n = len(shape)
    return pl.BlockSpec(shape, lambda *_: (0,) * n)


def _norm_noaffine(x):
    mu = jnp.mean(x, axis=-1, keepdims=True)
    xc = x - mu
    var = jnp.mean(xc * xc, axis=-1, keepdims=True)
    return xc * lax.rsqrt(var + LN_EPS)


def _rms(x, g):
    return x * lax.rsqrt(jnp.mean(x * x, axis=-1, keepdims=True) + RMS_EPS) * g


def _mod_row(mod_ref, row0, step, seq):
    row = row0 if step == 0 else row0 + step * seq
    m = mod_ref[pl.ds(row, 1), :]
    return [m[:, i * D:(i + 1) * D] for i in range(6)]


def _prep_kernel(wqn_ref, wk_ref, wv_ref, wo_ref, qabs_ref, oabs_ref):
    qabs = lax.dot_general(wqn_ref[0], wk_ref[0], (((1,), (1,)), ((), ())),
                           preferred_element_type=F32, precision=HIGHEST)
    qabs_ref[...] = qabs.astype(BF16)
    oabs = jnp.dot(wv_ref[0], wo_ref[0], preferred_element_type=F32, precision=HIGHEST)
    oabs_ref[...] = oabs.astype(BF16)


def _prep(wqn, wk, wv, wo):
    return pl.pallas_call(
        _prep_kernel,
        grid=(HEADS,),
        in_specs=[
            pl.BlockSpec((1, QL, NOPE), lambda h: (h, 0, 0)),
            pl.BlockSpec((1, KVL, NOPE), lambda h: (h, 0, 0)),
            pl.BlockSpec((1, KVL, VH), lambda h: (h, 0, 0)),
            pl.BlockSpec((1, VH, D), lambda h: (h, 0, 0)),
        ],
        out_specs=[
            pl.BlockSpec((QL, KVL), lambda h: (0, h)),
            pl.BlockSpec((KVL, D), lambda h: (h, 0)),
        ],
        out_shape=[jax.ShapeDtypeStruct((QL, HEADS * KVL), BF16),
                   jax.ShapeDtypeStruct((HEADS * KVL, D), BF16)],
        compiler_params=_params(("arbitrary",)),
        name="weight_prep",
    )(wqn, wk, wv, wo)


MOD_COLS = 1536


def _mod_kernel(c_ref, w_ref, b_ref, o_ref):
    c = c_ref[...]
    s = c * jax.nn.sigmoid(c)
    o_ref[...] = jnp.dot(s, w_ref[...], preferred_element_type=F32, precision=HIGHEST) + b_ref[...]


def _modulation(c_all, w_mod, b_mod):
    return pl.pallas_call(
        _mod_kernel,
        grid=(6 * D // MOD_COLS,),
        in_specs=[
            pl.BlockSpec((8, D), lambda i: (0, 0)),
            pl.BlockSpec((D, MOD_COLS), lambda i: (0, i)),
            pl.BlockSpec((1, MOD_COLS), lambda i: (0, i)),
        ],
        out_specs=pl.BlockSpec((8, MOD_COLS), lambda i: (0, i)),
        out_shape=jax.ShapeDtypeStruct((8, 6 * D), F32),
        compiler_params=_params(("arbitrary",)),
        name="modulation",
    )(c_all, w_mod, b_mod)


def _pre_kernel(*refs, seq_len, rope, mod_row0, mod_row_step):
    if rope:
        (x_ref, mod_ref, win_ref, cdft_ref, qg_ref, kvg_ref, wqa_ref, wqr_ref, cos_ref, sin_ref,
         fcs_ref, qcat_ref, kcat_ref) = refs
    else:
        (x_ref, mod_ref, win_ref, cdft_ref, qg_ref, kvg_ref, wqa_ref, wqr_ref,
         fcs_ref, qcat_ref, kcat_ref, ckv_ref, krope_ref) = refs
    blocks_per_seq = seq_len // TB
    b = pl.program_id(0) // blocks_per_seq
    shift1, scale1 = _mod_row(mod_ref, mod_row0, mod_row_step, b)[:2]

    x = x_ref[...]
    h1 = _norm_noaffine(x) * (1.0 + scale1) + shift1
    h1b = h1.astype(BF16)
    n_cols = 9 * LANES if rope else 8 * LANES
    proj = _dot(h1b, win_ref[:, :n_cols])

    f_b = proj[:, :FW].astype(BF16)
    for g in range(N_FG):
        r = _dot(f_b[:, g * FG:(g + 1) * FG], cdft_ref[...])
        fcs_ref[0, 0, :, g * FG:(g + 1) * FG] = r[:, :FG].astype(BF16)
        fcs_ref[0, 1, :, g * FG:(g + 1) * FG] = r[:, FG:].astype(BF16)

    qn = _rms(proj[:, FW:FW + QL], qg_ref[...]).astype(BF16)
    ckv = _rms(proj[:, FW + QL:FW + QL + KVL], kvg_ref[...])
    kr = proj[:, 7 * LANES:8 * LANES]

    qa = _dot(qn, wqa_ref[...]) * ATT_SCALE
    if rope:
        cos = cos_ref[...]
        sin = sin_ref[...]
        qr2 = _dot(qn, wqr_ref[...])
        kr_keys = kr * cos + proj[:, 8 * LANES:9 * LANES] * sin
    else:
        qr2 = _dot(qn, wqr_ref[:, :HEADS * LANES])
        kr_keys = kr
    for h in range(HEADS):
        qr_h = qr2[:, h * LANES:(h + 1) * LANES]
        if rope:
            qr_h = qr_h * cos + qr2[:, (HEADS + h) * LANES:(HEADS + h + 1) * LANES] * sin
        qcat_ref[0, h, :, :KVL] = qa[:, h * KVL:(h + 1) * KVL].astype(BF16)
        qcat_ref[0, h, :, KVL:] = (qr_h * ATT_SCALE).astype(BF16)
    kcat_ref[0, :, :KVL] = ckv.astype(BF16)
    kcat_ref[0, :, KVL:] = kr_keys.astype(BF16)
    if not rope:
        ckv_ref[0, 0] = ckv
        krope_ref[0, 0] = kr[:, :ROPE]


def _pre(x2d, mod, w_in_ext, cdft, qg, kvg, wqa, wqr, cos_pad, sin_pad, *, n_seq, seq_len, rope,
         mod_row0, mod_row_step):
    bps = seq_len // TB
    n_blk = n_seq * bps
    in_specs = [
        pl.BlockSpec((TB, D), lambda i: (i, 0)),
        _full((8, 6 * D)),
        _full((D, 9 * LANES)),
        _full((FG, 2 * FG)),
        _full((1, QL)),
        _full((1, KVL)),
        _full((QL, HEADS * KVL)),
        _full((QL, 2 * HEADS * LANES)),
    ]
    args = [x2d, mod, w_in_ext, cdft, qg, kvg, wqa, wqr]
    out_specs = [
        pl.BlockSpec((1, 2, TB, FW), lambda i: (i // bps, 0, i % bps, 0)),
        pl.BlockSpec((1, HEADS, TB, 2 * LANES), lambda i: (i // bps, 0, i % bps, 0)),
        pl.BlockSpec((1, TB, 2 * LANES), lambda i: (i // bps, i % bps, 0)),
    ]
    out_shape = [
        jax.ShapeDtypeStruct((n_seq, 2, seq_len, FW), BF16),
        jax.ShapeDtypeStruct((n_seq, HEADS, seq_len, 2 * LANES), BF16),
        jax.ShapeDtypeStruct((n_seq, seq_len, 2 * LANES), BF16),
    ]
    if rope:
        in_specs += [pl.BlockSpec((TB, LANES), lambda i: (i % bps, 0)),
                     pl.BlockSpec((TB, LANES), lambda i: (i % bps, 0))]
        args += [cos_pad, sin_pad]
    else:
        out_specs += [pl.BlockSpec((1, 1, TB, KVL), lambda i: (i // bps, 0, i % bps, 0)),
                      pl.BlockSpec((1, 1, TB, ROPE), lambda i: (i // bps, 0, i % bps, 0))]
        out_shape += [jax.ShapeDtypeStruct((n_seq, 1, seq_len, KVL), F32),
                      jax.ShapeDtypeStruct((n_seq, 1, seq_len, ROPE), F32)]
    kern = functools.partial(_pre_kernel, seq_len=seq_len, rope=rope, mod_row0=mod_row0,
                             mod_row_step=mod_row_step)
    return pl.pallas_call(
        kern, grid=(n_blk,), in_specs=in_specs, out_specs=out_specs, out_shape=out_shape,
        compiler_params=_params(("arbitrary",)),
        name="pre_lat" if rope else "pre_ctx",
    )(*args)


def _mix_kernel(*refs, seq_len, cache, mod_row0, mod_row_step):
    if cache:
        (x_ref, mod_ref, fcs_ref, pdft_ref, qcat_ref, kcat_ref, cache_ref, wfo_ref, woa_ref,
         wg_ref, bg_ref, wout_ref, ln1g_ref, ln1b_ref, wr_ref, br_ref, tri_ref, cnt0_ref,
         x1_ref, h2_ref, code_ref, topw_ref, cnt_ref, carry_ref) = refs
    else:
        (x_ref, mod_ref, fcs_ref, pdft_ref, qcat_ref, kcat_ref, wfo_ref, woa_ref,
         wg_ref, bg_ref, wout_ref, ln1g_ref, ln1b_ref, wr_ref, br_ref, tri_ref, cnt0_ref,
         x1_ref, h2_ref, code_ref, topw_ref, cnt_ref, carry_ref) = refs
        cache_ref = None
    b = pl.program_id(0)
    j = pl.program_id(1)
    shift1, scale1, gate1, shift2, scale2, _ = _mod_row(mod_ref, mod_row0, mod_row_step, b)

    @pl.when(jnp.logical_and(b == 0, j == 0))
    def _():
        carry_ref[...] = cnt0_ref[...]

    x = x_ref[...]
    h1b = (_norm_noaffine(x) * (1.0 + scale1) + shift1).astype(BF16)

    mixed = (_dot(pdft_ref[:, :seq_len], fcs_ref[0, 0]) + _dot(pdft_ref[:, seq_len:], fcs_ref[0, 1]))
    f_out = _dot(mixed.astype(BF16), wfo_ref[...])

    kc = kcat_ref[0]
    vals = kc[:, :KVL]
    if cache:
        kc2 = cache_ref[0]
        vals2 = kc2[:, :KVL]
    heads_out = []
    for h in range(HEADS):
        q = qcat_ref[0, h]
        s1 = _dot_nt(q, kc)
        m = jnp.max(s1, axis=-1, keepdims=True)
        if cache:
            s2 = _dot_nt(q, kc2)
            m = jnp.maximum(m, jnp.max(s2, axis=-1, keepdims=True))
        p1 = jnp.exp(s1 - m)
        l = jnp.sum(p1, axis=-1, keepdims=True)
        o = _dot(p1.astype(BF16), vals)
        if cache:
            p2 = jnp.exp(s2 - m)
            l = l + jnp.sum(p2, axis=-1, keepdims=True)
            o = o + _dot(p2.astype(BF16), vals2)
        heads_out.append((o / l).astype(BF16))
    m_out = _dot(jnp.concatenate(heads_out, axis=1), woa_ref[...])

    gates = jax.nn.sigmoid(_dot(h1b, wg_ref[...]) + bg_ref[...])
    merged = gates[:, :D] * f_out + gates[:, D:] * m_out
    mix = _dot(merged.astype(BF16), wout_ref[...])

    x1 = _norm_noaffine(DN_ALPHA * x + gate1 * mix) * ln1g_ref[...] + ln1b_ref[...]
    x1_ref[...] = x1
    h2 = _norm_noaffine(x1) * (1.0 + scale2) + shift2
    h2_ref[...] = h2

    logits = jnp.dot(h2, wr_ref[...], preferred_element_type=F32, precision=HIGHEST) + br_ref[...]
    lane = lax.broadcasted_iota(jnp.int32, (TB, LANES), 1).astype(F32)
    work = logits
    top_v, top_i = [], []
    for _ in range(TOPK):
        mk = jnp.max(work, axis=-1, keepdims=True)
        ik = jnp.min(jnp.where(work == mk, lane, float(LANES)), axis=-1, keepdims=True)
        work = jnp.where(lane == ik, -jnp.inf, work)
        top_v.append(mk)
        top_i.append(ik)
    exps = [jnp.exp(v - top_v[0]) for v in top_v]
    denom = exps[0] + exps[1] + exps[2] + exps[3]

    onehot = jnp.zeros((TB, LANES), F32)
    for ik in top_i:
        onehot = onehot + jnp.where(lane == ik, 1.0, 0.0)
    before = _dot(tri_ref[...], onehot.astype(BF16)) + carry_ref[...]
    codes = jnp.zeros((TB, LANES), jnp.int32)
    topw = jnp.zeros((TB, LANES), F32)
    for k in range(TOPK):
        rank = jnp.sum(jnp.where(lane == top_i[k], before, 0.0), axis=-1, keepdims=True)
        code = top_i[k].astype(jnp.int32) * (1 << RANK_BITS) + rank.astype(jnp.int32)
        codes = jnp.where(lane == float(k), code, codes)
        topw = jnp.where(lane == float(k), exps[k] / denom, topw)
    code_ref[...] = codes
    topw_ref[...] = topw
    carry_ref[...] = carry_ref[...] + jnp.sum(onehot, axis=0, keepdims=True)
    cnt_ref[...] = carry_ref[...]


def _mix(x2d, mod, fcs, pdft, qcat, kcat, cache_k, wfo, woa, wg, bg, wout, ln1g, ln1b, wr, br,
         tri, cnt0, *, n_seq, seq_len, mod_row0, mod_row_step):
    bps = seq_len // TB
    cache = cache_k is not None
    in_specs = [
        pl.BlockSpec((TB, D), lambda b, j: (b * bps + j, 0)),
        _full((8, 6 * D)),
        pl.BlockSpec((1, 2, seq_len, FW), lambda b, j: (b, 0, 0, 0)),
        pl.BlockSpec((TB, 2 * seq_len), lambda b, j: (j, 0)),
        pl.BlockSpec((1, HEADS, TB, 2 * LANES), lambda b, j: (b, 0, j, 0)),
        pl.BlockSpec((1, seq_len, 2 * LANES), lambda b, j: (b, 0, 0)),
    ]
    args = [x2d, mod, fcs, pdft, qcat, kcat]
    if cache:
        in_specs.append(pl.BlockSpec((1, PAST, 2 * LANES), lambda b, j: (b, 0, 0)))
        args.append(cache_k)
    in_specs += [
        _full((FW, D)), _full((HEADS * KVL, D)), _full((D, 2 * D)), _full((1, 2 * D)),
        _full((D, D)), _full((1, D)), _full((1, D)), _full((D, LANES)), _full((1, LANES)),
        _full((TB, TB)), _full((1, LANES)),
    ]
    args += [wfo, woa, wg, bg, wout, ln1g, ln1b, wr, br, tri, cnt0]
    n_tok = n_seq * seq_len
    tok_spec = lambda w: pl.BlockSpec((TB, w), lambda b, j: (b * bps + j, 0))
    out_specs = [tok_spec(D), tok_spec(D), tok_spec(LANES), tok_spec(LANES), _full((1, LANES))]
    out_shape = [
        jax.ShapeDtypeStruct((n_tok, D), F32),
        jax.ShapeDtypeStruct((n_tok, D), F32),
        jax.ShapeDtypeStruct((n_tok, LANES), jnp.int32),
        jax.ShapeDtypeStruct((n_tok, LANES), F32),
        jax.ShapeDtypeStruct((1, LANES), F32),
    ]
    kern = functools.partial(_mix_kernel, seq_len=seq_len, cache=cache, mod_row0=mod_row0,
                             mod_row_step=mod_row_step)
    return pl.pallas_call(
        kern, grid=(n_seq, bps), in_specs=in_specs, out_specs=out_specs, out_shape=out_shape,
        scratch_shapes=[pltpu.VMEM((1, LANES), F32)],
        compiler_params=_params(("arbitrary", "arbitrary")),
        name="mix_lat" if cache else "mix_ctx",
    )(*args)


N_CTX_BLK = N_CTX // TB
N_BLK = N_TOK // TB
DMA_UNROLL = 4


def _slot(off_ref, code):
    return off_ref[lax.shift_right_logical(code, RANK_BITS)] + (code & ((1 << RANK_BITS) - 1))


def _dispatch_kernel(off_ref, h2c_ref, h2l_ref, code_ref, xs_ref, sem):
    i = pl.program_id(0)

    def run(src_ref):
        def row_copy(n, k):
            pos = _slot(off_ref, code_ref[n * TOPK + k])
            return pltpu.make_async_copy(src_ref.at[pl.ds(n, 1), :], xs_ref.at[pl.ds(pos, 1), :], sem)

        def start(n, carry):
            for k in range(TOPK):
                row_copy(n, k).start()
            return carry

        def wait(n, carry):
            for k in range(TOPK):
                row_copy(n, k).wait()
            return carry

        lax.fori_loop(0, TB, start, 0, unroll=DMA_UNROLL)
        lax.fori_loop(0, TB, wait, 0, unroll=DMA_UNROLL)

    @pl.when(i < N_CTX_BLK)
    def _():
        run(h2c_ref)

    @pl.when(i >= N_CTX_BLK)
    def _():
        run(h2l_ref)


def _dispatch(off, h2c, h2l, codes1d):
    grid_spec = pltpu.PrefetchScalarGridSpec(
        num_scalar_prefetch=1,
        grid=(N_BLK,),
        in_specs=[
            pl.BlockSpec((TB, D), lambda i, off: (jnp.minimum(i, N_CTX_BLK - 1), 0)),
            pl.BlockSpec((TB, D), lambda i, off: (jnp.maximum(i - N_CTX_BLK, 0), 0)),
            pl.BlockSpec((TB * TOPK,), lambda i, off: (i,), memory_space=pltpu.SMEM),
        ],
        out_specs=pl.BlockSpec(memory_space=pl.ANY),
        scratch_shapes=[pltpu.SemaphoreType.DMA(())],
    )
    return pl.pallas_call(
        _dispatch_kernel, grid_spec=grid_spec,
        out_shape=jax.ShapeDtypeStruct((N_SORT, D), F32),
        compiler_params=_params(("arbitrary",)),
        name="dispatch",
    )(off, h2c, h2l, codes1d)


def _expert_kernel(tile_ref, exp_ref, lo_ref, hi_ref, xs_ref, wg_ref, bg_ref, wu_ref, bu_ref, wd_ref,
                   bd_ref, ys_ref):
    i = pl.program_id(0)
    lo = lo_ref[i]
    hi = hi_ref[i]

    def expert_rows():
        xb = xs_ref[...].astype(BF16)
        gt = jnp.minimum(_dot(xb, wg_ref[0]) + bg_ref[0], SWIGLU_LIMIT)
        up = jnp.clip(_dot(xb, wu_ref[0]) + bu_ref[0], -SWIGLU_LIMIT, SWIGLU_LIMIT)
        act = gt * jax.nn.sigmoid(SWIGLU_ALPHA * gt) * (up + 1.0)
        return _dot(act.astype(BF16), wd_ref[0]) + bd_ref[0]

    @pl.when(jnp.logical_and(hi > lo, lo == 0))
    def _():
        ys_ref[...] = expert_rows()

    @pl.when(jnp.logical_and(hi > lo, lo > 0))
    def _():
        row = lax.broadcasted_iota(jnp.int32, (ROW_TILE, 1), 0)
        mine = jnp.logical_and(row >= lo, row < hi)
        ys_ref[...] = jnp.where(mine, expert_rows(), ys_ref[...])


def _experts(item_tile, item_exp, item_lo, item_hi, xs, wg, bg, wu, bu, wd, bd):
    row_idx = lambda i, tile, exp, lo, hi: (tile[i], 0)
    w_idx = lambda i, tile, exp, lo, hi: (exp[i], 0, 0)
    grid_spec = pltpu.PrefetchScalarGridSpec(
        num_scalar_prefetch=4,
        grid=(N_ITEMS,),
        in_specs=[
            pl.BlockSpec((ROW_TILE, D), row_idx),
            pl.BlockSpec((1, D, D_EXP), w_idx),
            pl.BlockSpec((1, 1, D_EXP), w_idx),
            pl.BlockSpec((1, D, D_EXP), w_idx),
            pl.BlockSpec((1, 1, D_EXP), w_idx),
            pl.BlockSpec((1, D_EXP, D), w_idx),
            pl.BlockSpec((1, 1, D), w_idx),
        ],
        out_specs=pl.BlockSpec((ROW_TILE, D), row_idx),
    )
    return pl.pallas_call(
        _expert_kernel, grid_spec=grid_spec,
        out_shape=jax.ShapeDtypeStruct((N_SORT, D), F32),
        compiler_params=_params(("arbitrary",)),
        name="experts",
    )(item_tile, item_exp, item_lo, item_hi, xs, wg, bg, wu, bu, wd, bd)


def _combine_kernel(off_ref, code_ref, topw_ref, x1_ref, mod_ref, g_ref, b_ref, ys_ref, o_ref,
                    buf, sem, *, blocks_per_seq, mod_row0, mod_row_step):
    i = pl.program_id(0)
    gate2 = _mod_row(mod_ref, mod_row0, mod_row_step, i // blocks_per_seq)[5]

    def row_copy(n, k):
        pos = _slot(off_ref, code_ref[n * TOPK + k])
        return pltpu.make_async_copy(ys_ref.at[pl.ds(pos, 1), :], buf.at[k, pl.ds(n, 1), :], sem)

    def start(n, carry):
        for k in range(TOPK):
            row_copy(n, k).start()
        return carry

    def wait(n, carry):
        for k in range(TOPK):
            row_copy(n, k).wait()
        return carry

    lax.fori_loop(0, TB, start, 0, unroll=DMA_UNROLL)
    lax.fori_loop(0, TB, wait, 0, unroll=DMA_UNROLL)

    topw = topw_ref[...]
    ffn = topw[:, 0:1] * buf[0]
    for k in range(1, TOPK):
        ffn = ffn + topw[:, k:k + 1] * buf[k]
    y = DN_ALPHA * x1_ref[...] + gate2 * ffn
    o_ref[...] = _norm_noaffine(y) * g_ref[...] + b_ref[...]


def _combine(off, codes1d, topw, x1, mod, ln2g, ln2b, ys, *, blk0, n_seq, seq_len, mod_row0,
             mod_row_step):
    bps = seq_len // TB
    n_blk = n_seq * bps
    grid_spec = pltpu.PrefetchScalarGridSpec(
        num_scalar_prefetch=1,
        grid=(n_blk,),
        in_specs=[
            pl.BlockSpec((TB * TOPK,), lambda i, off: (blk0 + i,), memory_space=pltpu.SMEM),
            pl.BlockSpec((TB, LANES), lambda i, off: (i, 0)),
            pl.BlockSpec((TB, D), lambda i, off: (i, 0)),
            pl.BlockSpec((8, 6 * D), lambda i, off: (0, 0)),
            pl.BlockSpec((1, D), lambda i, off: (0, 0)),
            pl.BlockSpec((1, D), lambda i, off: (0, 0)),
            pl.BlockSpec(memory_space=pl.ANY),
        ],
        out_specs=pl.BlockSpec((TB, D), lambda i, off: (i, 0)),
        scratch_shapes=[pltpu.VMEM((TOPK, TB, D), F32), pltpu.SemaphoreType.DMA(())],
    )
    kern = functools.partial(_combine_kernel, blocks_per_seq=bps, mod_row0=mod_row0,
                             mod_row_step=mod_row_step)
    return pl.pallas_call(
        kern, grid_spec=grid_spec,
        out_shape=jax.ShapeDtypeStruct((n_seq * seq_len, D), F32),
        compiler_params=_params(("arbitrary",)),
        name="combine_lat" if blk0 else "combine_ctx",
    )(off, codes1d, topw, x1, mod, ln2g, ln2b, ys)


def _dft_tables():
    def cs(n):
        k = np.arange(n, dtype=np.int64)
        ang = 2.0 * np.pi * ((k[:, None] * k[None, :]) % n).astype(np.float64) / n
        return np.cos(ang) / math.sqrt(n), np.sin(ang) / math.sqrt(n)

    c, s = cs(FG)
    cdft = np.concatenate([c, s], axis=1).astype(np.float32)
    pd = {}
    for t in (T_CTX, T_LAT):
        c, s = cs(t)
        pd[t] = np.concatenate([c, -s], axis=1).astype(np.float32)
    tri = np.tril(np.ones((TB, TB), np.float32), k=-1)
    return cdft, pd, tri


_ROT_PERM = np.array(list(range(8, 16)) + list(range(0, 8)) + list(range(24, 32)) + list(range(16, 24)))
_ROT_SIGN = np.array([-1.0] * 8 + [1.0] * 8 + [-1.0] * 8 + [1.0] * 8, np.float32)


def _rope_tables():
    rows = T_LAT // GRID_W
    row = jnp.repeat(jnp.arange(rows, dtype=F32), GRID_W)
    col = jnp.tile(jnp.arange(GRID_W, dtype=F32), rows)
    axis_dim = ROPE // 2
    inv_freq = ROPE_THETA ** (-jnp.arange(0, axis_dim, 2, dtype=F32) / axis_dim)
    ang_r = row[:, None] * inv_freq[None, :]
    ang_c = col[:, None] * inv_freq[None, :]
    ang = jnp.concatenate([ang_r, ang_r, ang_c, ang_c], axis=-1)
    pad = ((0, 0), (0, LANES - ROPE))
    return jnp.pad(jnp.cos(ang), pad), jnp.pad(jnp.sin(ang), pad)


def kernel(x_prompt, x_sample, cache_ckv, cache_krope, c, c_ctx, w_mod, b_mod, w_in, q_norm_g, w_q_up,
           kv_norm_g, w_kv_up, w_fourier_o, w_mla_o, w_gate, b_gate, w_out, ln1_g, ln1_b, w_router,
           b_router, w_gate_e, b_gate_e, w_up_e, b_up_e, w_down_e, b_down_e, ln2_g, ln2_b):
    cdft_np, pdft_np, tri_np = _dft_tables()
    cdft = jnp.asarray(cdft_np).astype(BF16)
    pdft_ctx = jnp.asarray(pdft_np[T_CTX]).astype(BF16)
    pdft_lat = jnp.asarray(pdft_np[T_LAT]).astype(BF16)
    tri = jnp.asarray(tri_np).astype(BF16)
    cos_pad, sin_pad = _rope_tables()

    w_in0 = w_in[0]
    kr_w = w_in0[:, FW + QL + KVL:]
    lane_pad = ((0, 0), (0, LANES - ROPE))
    w_in_ext = jnp.concatenate(
        [w_in0[:, :FW + QL + KVL], jnp.pad(kr_w, lane_pad),
         jnp.pad(kr_w[:, _ROT_PERM] * _ROT_SIGN, lane_pad)], axis=1).astype(BF16)
    wq3 = w_q_up[0].reshape(QL, HEADS, NOPE + ROPE)
    wq_nope = jnp.transpose(wq3[:, :, :NOPE], (1, 0, 2))
    wq_rope = wq3[:, :, NOPE:]
    head_pad = ((0, 0), (0, 0), (0, LANES - ROPE))
    wqr = jnp.concatenate(
        [jnp.pad(wq_rope, head_pad).reshape(QL, HEADS * LANES),
         jnp.pad(wq_rope[:, :, _ROT_PERM] * _ROT_SIGN, head_pad).reshape(QL, HEADS * LANES)],
        axis=1).astype(BF16)
    wkv3 = w_kv_up[0].reshape(KVL, HEADS, NOPE + VH)
    wk = jnp.transpose(wkv3[:, :, :NOPE], (1, 0, 2))
    wv = jnp.transpose(wkv3[:, :, NOPE:], (1, 0, 2))
    wo3 = w_mla_o[0].reshape(HEADS, VH, D)
    wqa, woa = _prep(wq_nope, wk, wv, wo3)

    c_all = jnp.concatenate([c_ctx[None, :], c, jnp.zeros((8 - 1 - N_LAT_SEQ, D), F32)], axis=0)
    mod = _modulation(c_all, w_mod[0], b_mod)

    qg = q_norm_g
    kvg = kv_norm_g
    wfo = w_fourier_o[0].astype(BF16)
    wg = w_gate[0].astype(BF16)
    wout = w_out[0].astype(BF16)
    wr = jnp.pad(w_router[0], ((0, 0), (0, LANES - N_EXP)))
    br = jnp.pad(b_router, ((0, 0), (0, LANES - N_EXP)), constant_values=NEG_BIG)
    cache_k = jnp.concatenate(
        [cache_ckv[:, 0], jnp.pad(cache_krope[:, 0], ((0, 0), (0, 0), (0, LANES - ROPE)))],
        axis=-1).astype(BF16)

    xc2d = x_prompt.reshape(N_CTX, D)
    xl2d = x_sample.reshape(N_LAT, D)

    fcs_c, qcat_c, kcat_c, new_ckv, new_krope = _pre(
        xc2d, mod, w_in_ext, cdft, qg, kvg, wqa, wqr, None, None,
        n_seq=N_CTX_SEQ, seq_len=T_CTX, rope=False, mod_row0=0, mod_row_step=0)
    cnt0 = jnp.zeros((1, LANES), F32)
    x1_c, h2_c, code_c, topw_c, cnt_c = _mix(
        xc2d, mod, fcs_c, pdft_ctx, qcat_c, kcat_c, None, wfo, woa, wg, b_gate, wout, ln1_g, ln1_b,
        wr, br, tri, cnt0, n_seq=N_CTX_SEQ, seq_len=T_CTX, mod_row0=0, mod_row_step=0)

    fcs_l, qcat_l, kcat_l = _pre(
        xl2d, mod, w_in_ext, cdft, qg, kvg, wqa, wqr, cos_pad, sin_pad,
        n_seq=N_LAT_SEQ, seq_len=T_LAT, rope=True, mod_row0=1, mod_row_step=1)
    x1_l, h2_l, code_l, topw_l, cnt = _mix(
        xl2d, mod, fcs_l, pdft_lat, qcat_l, kcat_l, cache_k, wfo, woa, wg, b_gate, wout, ln1_g,
        ln1_b, wr, br, tri, cnt_c, n_seq=N_LAT_SEQ, seq_len=T_LAT, mod_row0=1, mod_row_step=1)

    counts = cnt[0, :N_EXP].astype(jnp.int32)
    exp_end = jnp.cumsum(counts)
    off = (exp_end - counts).astype(jnp.int32)
    cuts = jnp.sort(jnp.concatenate([jnp.arange(N_TILES, dtype=jnp.int32) * ROW_TILE, off]))
    nxt = jnp.concatenate([cuts[1:], jnp.full((1,), N_SORT, jnp.int32)])
    item_tile = jnp.minimum(cuts // ROW_TILE, N_TILES - 1).astype(jnp.int32)
    item_exp = jnp.minimum(jnp.sum((exp_end[None, :] <= cuts[:, None]).astype(jnp.int32), axis=1),
                           N_EXP - 1).astype(jnp.int32)
    item_lo = (cuts - item_tile * ROW_TILE).astype(jnp.int32)
    item_hi = (nxt - item_tile * ROW_TILE).astype(jnp.int32)

    codes1d = jnp.concatenate([code_c[:, :TOPK].reshape(-1), code_l[:, :TOPK].reshape(-1)])
    xs = _dispatch(off, h2_c, h2_l, codes1d)
    ys = _experts(item_tile, item_exp, item_lo, item_hi, xs,
                  w_gate_e[0].astype(BF16), b_gate_e[0][:, None, :],
                  w_up_e[0].astype(BF16), b_up_e[0][:, None, :],
                  w_down_e[0].astype(BF16), b_down_e[0][:, None, :])

    y_c = _combine(off, codes1d, topw_c, x1_c, mod, ln2_g, ln2_b, ys, blk0=0, n_seq=N_CTX_SEQ,
                   seq_len=T_CTX, mod_row0=0, mod_row_step=0)
    y_l = _combine(off, codes1d, topw_l, x1_l, mod, ln2_g, ln2_b, ys, blk0=N_CTX_BLK,
                   n_seq=N_LAT_SEQ, seq_len=T_LAT, mod_row0=1, mod_row_step=1)
    return (y_c.reshape(N_CTX_SEQ, T_CTX, D), y_l.reshape(N_LAT_SEQ, T_LAT, D), new_ckv, new_krope)
```

```python
import functools
import math

import numpy as np
import jax
import jax.numpy as jnp
from jax import lax
from jax.experimental import pallas as pl
from jax.experimental.pallas import tpu as pltpu

D = 1024
N_CTX_SEQ, T_CTX = 32, 256
N_LAT_SEQ, T_LAT = 4, 1024
PAST = 256
N_CTX = N_CTX_SEQ * T_CTX
N_LAT = N_LAT_SEQ * T_LAT
N_TOK = N_CTX + N_LAT
FW = 512
FG = 128
N_FG = FW // FG
HEADS = 8
QL = 256
KVL = 128
NOPE = 64
ROPE = 32
VH = 64
N_EXP = 32
TOPK = 4
D_EXP = 1024
SWIGLU_LIMIT = 7.0
SWIGLU_ALPHA = 1.702
LN_EPS = 1e-5
RMS_EPS = 1e-6
DN_ALPHA = 2.0 ** 0.25
ATT_SCALE = float(NOPE + ROPE) ** -0.5
ROPE_THETA = 10000.0
GRID_W = 64

LANES = 128
TB = 256
SUB = 8
ROW_TILE = 256
N_SORT = N_TOK * TOPK
N_SORT_PAD = N_SORT + ROW_TILE
BLK_ROWS = TB * TOPK
RUN_BITS = TB.bit_length()
VMEM_LIMIT = 56 * 1024 * 1024
NEG_BIG = -1e30

F32 = jnp.float32
BF16 = jnp.bfloat16
HIGHEST = lax.Precision.HIGHEST


def _dot(a, b):
    return jnp.dot(a, b, preferred_element_type=F32)


def _dot_nt(a, b):
    return lax.dot_general(a, b, (((1,), (1,)), ((), ())), preferred_element_type=F32)


def _params(sem):
    return pltpu.CompilerParams(dimension_semantics=sem, vmem_limit_bytes=VMEM_LIMIT)


def _full(shape):
    n = len(shape)
    return pl.BlockSpec(shape, lambda *_: (0,) * n)


def _norm_noaffine(x):
    mu = jnp.mean(x, axis=-1, keepdims=True)
    xc = x - mu
    var = jnp.mean(xc * xc, axis=-1, keepdims=True)
    return xc * lax.rsqrt(var + LN_EPS)


def _rms(x, g):
    return x * lax.rsqrt(jnp.mean(x * x, axis=-1, keepdims=True) + RMS_EPS) * g


def _mod_row(mod_ref, row0, step, seq):
    row = row0 if step == 0 else row0 + step * seq
    m = mod_ref[pl.ds(row, 1), :]
    return [m[:, i * D:(i + 1) * D] for i in range(6)]


def _prep_kernel(wqn_ref, wk_ref, wv_ref, wo_ref, qabs_ref, oabs_ref):
    qabs = lax.dot_general(wqn_ref[0], wk_ref[0], (((1,), (1,)), ((), ())),
                           preferred_element_type=F32, precision=HIGHEST)
    qabs_ref[...] = qabs.astype(BF16)
    oabs = jnp.dot(wv_ref[0], wo_ref[0], preferred_element_type=F32, precision=HIGHEST)
    oabs_ref[...] = oabs.astype(BF16)


def _prep(wqn, wk, wv, wo):
    return pl.pallas_call(
        _prep_kernel,
        grid=(HEADS,),
        in_specs=[
            pl.BlockSpec((1, QL, NOPE), lambda h: (h, 0, 0)),
            pl.BlockSpec((1, KVL, NOPE), lambda h: (h, 0, 0)),
            pl.BlockSpec((1, KVL, VH), lambda h: (h, 0, 0)),
            pl.BlockSpec((1, VH, D), lambda h: (h, 0, 0)),
        ],
        out_specs=[
            pl.BlockSpec((QL, KVL), lambda h: (0, h)),
            pl.BlockSpec((KVL, D), lambda h: (h, 0)),
        ],
        out_shape=[jax.ShapeDtypeStruct((QL, HEADS * KVL), BF16),
                   jax.ShapeDtypeStruct((HEADS * KVL, D), BF16)],
        compiler_params=_params(("arbitrary",)),
        name="weight_prep",
    )(wqn, wk, wv, wo)


MOD_COLS = 1536


def _mod_kernel(c_ref, w_ref, b_ref, o_ref):
    c = c_ref[...]
    s = c * jax.nn.sigmoid(c)
    o_ref[...] = jnp.dot(s, w_ref[...], preferred_element_type=F32, precision=HIGHEST) + b_ref[...]


def _modulation(c_all, w_mod, b_mod):
    return pl.pallas_call(
        _mod_kernel,
        grid=(6 * D // MOD_COLS,),
        in_specs=[
            pl.BlockSpec((8, D), lambda i: (0, 0)),
            pl.BlockSpec((D, MOD_COLS), lambda i: (0, i)),
            pl.BlockSpec((1, MOD_COLS), lambda i: (0, i)),
        ],
        out_specs=pl.BlockSpec((8, MOD_COLS), lambda i: (0, i)),
        out_shape=jax.ShapeDtypeStruct((8, 6 * D), F32),
        compiler_params=_params(("arbitrary",)),
        name="modulation",
    )(c_all, w_mod, b_mod)


def _pre_kernel(*refs, seq_len, rope, mod_row0, mod_row_step):
    if rope:
        (x_ref, mod_ref, win_ref, cdft_ref, qg_ref, kvg_ref, wqa_ref, wqr_ref, cos_ref, sin_ref,
         fcs_ref, qcat_ref, kcat_ref) = refs
    else:
        (x_ref, mod_ref, win_ref, cdft_ref, qg_ref, kvg_ref, wqa_ref, wqr_ref,
         fcs_ref, qcat_ref, kcat_ref, ckv_ref, krope_ref) = refs
    blocks_per_seq = seq_len // TB
    b = pl.program_id(0) // blocks_per_seq
    shift1, scale1 = _mod_row(mod_ref, mod_row0, mod_row_step, b)[:2]

    x = x_ref[...]
    h1 = _norm_noaffine(x) * (1.0 + scale1) + shift1
    h1b = h1.astype(BF16)
    n_cols = 9 * LANES if rope else 8 * LANES
    proj = _dot(h1b, win_ref[:, :n_cols])

    f_b = proj[:, :FW].astype(BF16)
    for g in range(N_FG):
        r = _dot(f_b[:, g * FG:(g + 1) * FG], cdft_ref[...])
        fcs_ref[0, 0, :, g * FG:(g + 1) * FG] = r[:, :FG].astype(BF16)
        fcs_ref[0, 1, :, g * FG:(g + 1) * FG] = r[:, FG:].astype(BF16)

    qn = _rms(proj[:, FW:FW + QL], qg_ref[...]).astype(BF16)
    ckv = _rms(proj[:, FW + QL:FW + QL + KVL], kvg_ref[...])
    kr = proj[:, 7 * LANES:8 * LANES]

    qa = _dot(qn, wqa_ref[...]) * ATT_SCALE
    if rope:
        cos = cos_ref[...]
        sin = sin_ref[...]
        qr2 = _dot(qn, wqr_ref[...])
        kr_keys = kr * cos + proj[:, 8 * LANES:9 * LANES] * sin
    else:
        qr2 = _dot(qn, wqr_ref[:, :HEADS * LANES])
        kr_keys = kr
    for h in range(HEADS):
        qr_h = qr2[:, h * LANES:(h + 1) * LANES]
        if rope:
            qr_h = qr_h * cos + qr2[:, (HEADS + h) * LANES:(HEADS + h + 1) * LANES] * sin
        qcat_ref[0, h, :, :KVL] = qa[:, h * KVL:(h + 1) * KVL].astype(BF16)
        qcat_ref[0, h, :, KVL:] = (qr_h * ATT_SCALE).astype(BF16)
    kcat_ref[0, :, :KVL] = ckv.astype(BF16)
    kcat_ref[0, :, KVL:] = kr_keys.astype(BF16)
    if not rope:
        ckv_ref[0, 0] = ckv
        krope_ref[0, 0] = kr[:, :ROPE]


def _pre(x2d, mod, w_in_ext, cdft, qg, kvg, wqa, wqr, cos_pad, sin_pad, *, n_seq, seq_len, rope,
         mod_row0, mod_row_step):
    bps = seq_len // TB
    n_blk = n_seq * bps
    in_specs = [
        pl.BlockSpec((TB, D), lambda i: (i, 0)),
        _full((8, 6 * D)),
        _full((D, 9 * LANES)),
        _full((FG, 2 * FG)),
        _full((1, QL)),
        _full((1, KVL)),
        _full((QL, HEADS * KVL)),
        _full((QL, 2 * HEADS * LANES)),
    ]
    args = [x2d, mod, w_in_ext, cdft, qg, kvg, wqa, wqr]
    out_specs = [
        pl.BlockSpec((1, 2, TB, FW), lambda i: (i // bps, 0, i % bps, 0)),
        pl.BlockSpec((1, HEADS, TB, 2 * LANES), lambda i: (i // bps, 0, i % bps, 0)),
        pl.BlockSpec((1, TB, 2 * LANES), lambda i: (i // bps, i % bps, 0)),
    ]
    out_shape = [
        jax.ShapeDtypeStruct((n_seq, 2, seq_len, FW), BF16),
        jax.ShapeDtypeStruct((n_seq, HEADS, seq_len, 2 * LANES), BF16),
        jax.ShapeDtypeStruct((n_seq, seq_len, 2 * LANES), BF16),
    ]
    if rope:
        in_specs += [pl.BlockSpec((TB, LANES), lambda i: (i % bps, 0)),
                     pl.BlockSpec((TB, LANES), lambda i: (i % bps, 0))]
        args += [cos_pad, sin_pad]
    else:
        out_specs += [pl.BlockSpec((1, 1, TB, KVL), lambda i: (i // bps, 0, i % bps, 0)),
                      pl.BlockSpec((1, 1, TB, ROPE), lambda i: (i // bps, 0, i % bps, 0))]
        out_shape += [jax.ShapeDtypeStruct((n_seq, 1, seq_len, KVL), F32),
                      jax.ShapeDtypeStruct((n_seq, 1, seq_len, ROPE), F32)]
    kern = functools.partial(_pre_kernel, seq_len=seq_len, rope=rope, mod_row0=mod_row0,
                             mod_row_step=mod_row_step)
    return pl.pallas_call(
        kern, grid=(n_blk,), in_specs=in_specs, out_specs=out_specs, out_shape=out_shape,
        compiler_params=_params(("arbitrary",)),
        name="pre_lat" if rope else "pre_ctx",
    )(*args)


def _mix_kernel(*refs, seq_len, cache, mod_row0, mod_row_step):
    if cache:
        (x_ref, mod_ref, fcs_ref, pdft_ref, qcat_ref, kcat_ref, cache_ref, wfo_ref, woa_ref,
         wg_ref, bg_ref, wout_ref, ln1g_ref, ln1b_ref, wr_ref, br_ref, tri_ref, upper_ref,
         x1_ref, h2_ref, lpos_ref, topw_ref, bcnt_ref) = refs
    else:
        (x_ref, mod_ref, fcs_ref, pdft_ref, qcat_ref, kcat_ref, wfo_ref, woa_ref,
         wg_ref, bg_ref, wout_ref, ln1g_ref, ln1b_ref, wr_ref, br_ref, tri_ref, upper_ref,
         x1_ref, h2_ref, lpos_ref, topw_ref, bcnt_ref) = refs
        cache_ref = None
    b = pl.program_id(0)
    shift1, scale1, gate1, shift2, scale2, _ = _mod_row(mod_ref, mod_row0, mod_row_step, b)

    x = x_ref[...]
    h1b = (_norm_noaffine(x) * (1.0 + scale1) + shift1).astype(BF16)

    mixed = (_dot(pdft_ref[:, :seq_len], fcs_ref[0, 0]) + _dot(pdft_ref[:, seq_len:], fcs_ref[0, 1]))
    f_out = _dot(mixed.astype(BF16), wfo_ref[...])

    kc = kcat_ref[0]
    vals = kc[:, :KVL]
    if cache:
        kc2 = cache_ref[0]
        vals2 = kc2[:, :KVL]
    heads_out = []
    for h in range(HEADS):
        q = qcat_ref[0, h]
        s1 = _dot_nt(q, kc)
        m = jnp.max(s1, axis=-1, keepdims=True)
        if cache:
            s2 = _dot_nt(q, kc2)
            m = jnp.maximum(m, jnp.max(s2, axis=-1, keepdims=True))
        p1 = jnp.exp(s1 - m)
        l = jnp.sum(p1, axis=-1, keepdims=True)
        o = _dot(p1.astype(BF16), vals)
        if cache:
            p2 = jnp.exp(s2 - m)
            l = l + jnp.sum(p2, axis=-1, keepdims=True)
            o = o + _dot(p2.astype(BF16), vals2)
        heads_out.append((o / l).astype(BF16))
    m_out = _dot(jnp.concatenate(heads_out, axis=1), woa_ref[...])

    gates = jax.nn.sigmoid(_dot(h1b, wg_ref[...]) + bg_ref[...])
    merged = gates[:, :D] * f_out + gates[:, D:] * m_out
    mix = _dot(merged.astype(BF16), wout_ref[...])

    x1 = _norm_noaffine(DN_ALPHA * x + gate1 * mix) * ln1g_ref[...] + ln1b_ref[...]
    x1_ref[...] = x1
    h2 = _norm_noaffine(x1) * (1.0 + scale2) + shift2
    h2_ref[...] = h2

    logits = jnp.dot(h2, wr_ref[...], preferred_element_type=F32, precision=HIGHEST) + br_ref[...]
    lane = lax.broadcasted_iota(jnp.int32, (TB, LANES), 1).astype(F32)
    work = logits
    top_v, top_i = [], []
    for _ in range(TOPK):
        mk = jnp.max(work, axis=-1, keepdims=True)
        ik = jnp.min(jnp.where(work == mk, lane, float(LANES)), axis=-1, keepdims=True)
        work = jnp.where(lane == ik, -jnp.inf, work)
        top_v.append(mk)
        top_i.append(ik)
    exps = [jnp.exp(v - top_v[0]) for v in top_v]
    denom = exps[0] + exps[1] + exps[2] + exps[3]

    onehot = jnp.zeros((TB, LANES), F32)
    for ik in top_i:
        onehot = onehot + jnp.where(lane == ik, 1.0, 0.0)
    counts = jnp.sum(onehot, axis=0, keepdims=True)
    lower = jnp.dot(jnp.broadcast_to(counts, (8, LANES)), upper_ref[...],
                    preferred_element_type=F32, precision=HIGHEST)[0:1, :]
    before = _dot(tri_ref[...], onehot.astype(BF16)) + lower
    lpos = jnp.zeros((TB, LANES), jnp.int32)
    topw = jnp.zeros((TB, LANES), F32)
    for k in range(TOPK):
        pos = jnp.sum(jnp.where(lane == top_i[k], before, 0.0), axis=-1, keepdims=True)
        lpos = jnp.where(lane == float(k), pos.astype(jnp.int32), lpos)
        topw = jnp.where(lane == float(k), exps[k] / denom, topw)
    lpos_ref[...] = lpos
    topw_ref[...] = topw
    bcnt_ref[0] = counts


def _mix(x2d, mod, fcs, pdft, qcat, kcat, cache_k, wfo, woa, wg, bg, wout, ln1g, ln1b, wr, br,
         tri, upper, *, n_seq, seq_len, mod_row0, mod_row_step):
    bps = seq_len // TB
    cache = cache_k is not None
    in_specs = [
        pl.BlockSpec((TB, D), lambda b, j: (b * bps + j, 0)),
        _full((8, 6 * D)),
        pl.BlockSpec((1, 2, seq_len, FW), lambda b, j: (b, 0, 0, 0)),
        pl.BlockSpec((TB, 2 * seq_len), lambda b, j: (j, 0)),
        pl.BlockSpec((1, HEADS, TB, 2 * LANES), lambda b, j: (b, 0, j, 0)),
        pl.BlockSpec((1, seq_len, 2 * LANES), lambda b, j: (b, 0, 0)),
    ]
    args = [x2d, mod, fcs, pdft, qcat, kcat]
    if cache:
        in_specs.append(pl.BlockSpec((1, PAST, 2 * LANES), lambda b, j: (b, 0, 0)))
        args.append(cache_k)
    in_specs += [
        _full((FW, D)), _full((HEADS * KVL, D)), _full((D, 2 * D)), _full((1, 2 * D)),
        _full((D, D)), _full((1, D)), _full((1, D)), _full((D, LANES)), _full((1, LANES)),
        _full((TB, TB)), _full((LANES, LANES)),
    ]
    args += [wfo, woa, wg, bg, wout, ln1g, ln1b, wr, br, tri, upper]
    n_tok = n_seq * seq_len
    tok_spec = lambda w: pl.BlockSpec((TB, w), lambda b, j: (b * bps + j, 0))
    out_specs = [tok_spec(D), tok_spec(D), tok_spec(LANES), tok_spec(LANES),
                 pl.BlockSpec((1, 1, LANES), lambda b, j: (b * bps + j, 0, 0))]
    out_shape = [
        jax.ShapeDtypeStruct((n_tok, D), F32),
        jax.ShapeDtypeStruct((n_tok, D), F32),
        jax.ShapeDtypeStruct((n_tok, LANES), jnp.int32),
        jax.ShapeDtypeStruct((n_tok, LANES), F32),
        jax.ShapeDtypeStruct((n_seq * bps, 1, LANES), F32),
    ]
    kern = functools.partial(_mix_kernel, seq_len=seq_len, cache=cache, mod_row0=mod_row0,
                             mod_row_step=mod_row_step)
    return pl.pallas_call(
        kern, grid=(n_seq, bps), in_specs=in_specs, out_specs=out_specs, out_shape=out_shape,
        compiler_params=_params(("arbitrary", "arbitrary")),
        name="mix_lat" if cache else "mix_ctx",
    )(*args)


N_CTX_BLK = N_CTX // TB
N_BLK = N_TOK // TB


def _row_tiles(ref, row0, n_rows):
    return ref.at[pl.ds(pl.multiple_of(row0 * SUB, SUB), n_rows * SUB), :]


def _for_each_run_copy(lstart_ref, blen_ref, gstart_ref, blk, local_ref, global_ref, sem, to_global, fn):
    def per_expert(e, carry):
        idx = blk * N_EXP + e
        n = blen_ref[idx]
        a = lstart_ref[idx]
        g = gstart_ref[idx]
        for bit in range(RUN_BITS - 1, -1, -1):
            size = 1 << bit

            @pl.when((n & size) != 0)
            def _():
                done = lax.shift_left(lax.shift_right_logical(n, bit + 1), bit + 1)
                loc = _row_tiles(local_ref, a + done, size)
                glo = _row_tiles(global_ref, g + done, size)
                fn(pltpu.make_async_copy(loc, glo, sem) if to_global
                   else pltpu.make_async_copy(glo, loc, sem))
        return carry

    lax.fori_loop(0, N_EXP, per_expert, 0)


def _load_rows(ref, n_rows):
    return jnp.concatenate([ref[pl.ds(s, n_rows, stride=SUB), :] for s in range(SUB)], axis=1)


def _store_rows(ref, val, n_rows):
    for s in range(SUB):
        ref[pl.ds(s, n_rows, stride=SUB), :] = val[:, s * LANES:(s + 1) * LANES]


def _dispatch_kernel(lstart_ref, blen_ref, gstart_ref, h2c_ref, h2l_ref, lpc_ref, lpl_ref, xs_ref,
                     buf, zbuf, sem, zsem):
    b = pl.program_id(0)
    slot = b & 1
    tables = (lstart_ref, blen_ref, gstart_ref)

    def runs(blk, s, fn):
        _for_each_run_copy(*tables, blk, buf.at[s], xs_ref, sem.at[s], True, fn)

    @pl.when(b == 0)
    def _():
        zbuf[...] = jnp.zeros_like(zbuf)
        tail = pltpu.make_async_copy(zbuf, _row_tiles(xs_ref, N_SORT, ROW_TILE), zsem)
        tail.start()
        tail.wait()

    @pl.when(b >= 2)
    def _():
        runs(b - 2, slot, lambda cp: cp.wait())

    def sort_block(h2_ref, lp_ref):
        col = lax.broadcasted_iota(jnp.int32, (TB, BLK_ROWS), 1)
        lp = lp_ref[...]
        pick = jnp.zeros((TB, BLK_ROWS), F32)
        for k in range(TOPK):
            pick = pick + jnp.where(col == lp[:, k:k + 1], 1.0, 0.0)
        sorted_rows = lax.dot_general(pick.astype(BF16), h2_ref[...].astype(BF16),
                                      (((0,), (0,)), ((), ())), preferred_element_type=F32)
        _store_rows(buf.at[slot], sorted_rows, BLK_ROWS)

    @pl.when(b < N_CTX_BLK)
    def _():
        sort_block(h2c_ref, lpc_ref)

    @pl.when(b >= N_CTX_BLK)
    def _():
        sort_block(h2l_ref, lpl_ref)

    runs(b, slot, lambda cp: cp.start())

    @pl.when(b == N_BLK - 1)
    def _():
        runs(b - 1, 1 - slot, lambda cp: cp.wait())
        runs(b, slot, lambda cp: cp.wait())


def _dispatch(lstart, blen, gstart, h2c, h2l, lpc, lpl):
    ctx_idx = lambda i, *_: (jnp.minimum(i, N_CTX_BLK - 1), 0)
    lat_idx = lambda i, *_: (jnp.maximum(i - N_CTX_BLK, 0), 0)
    grid_spec = pltpu.PrefetchScalarGridSpec(
        num_scalar_prefetch=3,
        grid=(N_BLK,),
        in_specs=[
            pl.BlockSpec((TB, D), ctx_idx),
            pl.BlockSpec((TB, D), lat_idx),
            pl.BlockSpec((TB, LANES), ctx_idx),
            pl.BlockSpec((TB, LANES), lat_idx),
        ],
        out_specs=pl.BlockSpec(memory_space=pl.ANY),
        scratch_shapes=[pltpu.VMEM((2, BLK_ROWS * SUB, LANES), F32),
                        pltpu.VMEM((ROW_TILE * SUB, LANES), F32),
                        pltpu.SemaphoreType.DMA((2,)),
                        pltpu.SemaphoreType.DMA(())],
    )
    return pl.pallas_call(
        _dispatch_kernel, grid_spec=grid_spec,
        out_shape=jax.ShapeDtypeStruct((N_SORT_PAD * SUB, LANES), F32),
        compiler_params=_params(("arbitrary",)),
        name="dispatch",
    )(lstart, blen, gstart, h2c, h2l, lpc, lpl)


def _expert_kernel(start_ref, count_ref, xs_ref, wg_ref, bg_ref, wu_ref, bu_ref, wd_ref, bd_ref, ys_ref,
                   wbf, xbuf, ybuf, sem_in, sem_out):
    e = pl.program_id(0)
    row0 = start_ref[e]
    n_tiles = lax.shift_right_logical(count_ref[e] + (ROW_TILE - 1), ROW_TILE.bit_length() - 1)

    def in_copy(j, slot):
        return pltpu.make_async_copy(_row_tiles(xs_ref, row0 + j * ROW_TILE, ROW_TILE), xbuf.at[slot],
                                     sem_in.at[slot])

    def out_copy(j, slot):
        return pltpu.make_async_copy(ybuf.at[slot], _row_tiles(ys_ref, row0 + j * ROW_TILE, ROW_TILE),
                                     sem_out.at[slot])

    @pl.when(e == 0)
    def _():
        ybuf[0] = jnp.zeros((ROW_TILE * SUB, LANES), F32)
        tail = pltpu.make_async_copy(ybuf.at[0], _row_tiles(ys_ref, N_SORT, ROW_TILE), sem_out.at[0])
        tail.start()
        tail.wait()

    @pl.when(n_tiles > 0)
    def _():
        in_copy(0, 0).start()
        wbf[0] = wg_ref[0].astype(BF16)
        wbf[1] = wu_ref[0].astype(BF16)
        wbf[2] = wd_ref[0].astype(BF16)

        def tile(j, carry):
            slot = j & 1

            @pl.when(j + 1 < n_tiles)
            def _():
                in_copy(j + 1, 1 - slot).start()

            in_copy(j, slot).wait()

            @pl.when(j >= 2)
            def _():
                out_copy(j - 2, slot).wait()

            xb = _load_rows(xbuf.at[slot], ROW_TILE).astype(BF16)
            gt = jnp.minimum(_dot(xb, wbf[0]) + bg_ref[0], SWIGLU_LIMIT)
            up = jnp.clip(_dot(xb, wbf[1]) + bu_ref[0], -SWIGLU_LIMIT, SWIGLU_LIMIT)
            act = gt * jax.nn.sigmoid(SWIGLU_ALPHA * gt) * (up + 1.0)
            _store_rows(ybuf.at[slot], _dot(act.astype(BF16), wbf[2]) + bd_ref[0], ROW_TILE)
            out_copy(j, slot).start()
            return carry

        lax.fori_loop(0, n_tiles, tile, 0)

        @pl.when(n_tiles >= 2)
        def _():
            out_copy(n_tiles - 2, n_tiles & 1).wait()

        out_copy(n_tiles - 1, (n_tiles - 1) & 1).wait()


def _experts(start, count, xs, wg, bg, wu, bu, wd, bd):
    w_idx = lambda e, *_: (e, 0, 0)
    grid_spec = pltpu.PrefetchScalarGridSpec(
        num_scalar_prefetch=2,
        grid=(N_EXP,),
        in_specs=[
            pl.BlockSpec(memory_space=pl.ANY),
            pl.BlockSpec((1, D, D_EXP), w_idx),
            pl.BlockSpec((1, 1, D_EXP), w_idx),
            pl.BlockSpec((1, D, D_EXP), w_idx),
            pl.BlockSpec((1, 1, D_EXP), w_idx),
            pl.BlockSpec((1, D_EXP, D), w_idx),
            pl.BlockSpec((1, 1, D), w_idx),
        ],
        out_specs=pl.BlockSpec(memory_space=pl.ANY),
        scratch_shapes=[pltpu.VMEM((3, D, D_EXP), BF16),
                        pltpu.VMEM((2, ROW_TILE * SUB, LANES), F32),
                        pltpu.VMEM((2, ROW_TILE * SUB, LANES), F32),
                        pltpu.SemaphoreType.DMA((2,)),
                        pltpu.SemaphoreType.DMA((2,))],
    )
    return pl.pallas_call(
        _expert_kernel, grid_spec=grid_spec,
        out_shape=jax.ShapeDtypeStruct((N_SORT_PAD * SUB, LANES), F32),
        compiler_params=_params(("arbitrary",)),
        name="experts",
    )(start, count, xs, wg, bg, wu, bu, wd, bd)


def _combine_kernel(lstart_ref, blen_ref, gstart_ref, lp_ref, topw_ref, x1_ref, mod_ref, g_ref, b_ref,
                    ys_ref, o_ref, buf, sem, *, blk0, n_blk, blocks_per_seq, mod_row0, mod_row_step):
    i = pl.program_id(0)
    slot = i & 1
    gate2 = _mod_row(mod_ref, mod_row0, mod_row_step, i // blocks_per_seq)[5]
    tables = (lstart_ref, blen_ref, gstart_ref)

    def runs(step, s, fn):
        _for_each_run_copy(*tables, blk0 + step, buf.at[s], ys_ref, sem.at[s], False, fn)

    @pl.when(i == 0)
    def _():
        runs(0, 0, lambda cp: cp.start())

    @pl.when(i + 1 < n_blk)
    def _():
        runs(i + 1, 1 - slot, lambda cp: cp.start())

    runs(i, slot, lambda cp: cp.wait())

    y_sorted = _load_rows(buf.at[slot], BLK_ROWS).astype(BF16)
    col = lax.broadcasted_iota(jnp.int32, (TB, BLK_ROWS), 1)
    lp = lp_ref[...]
    topw = topw_ref[...]
    weights = jnp.zeros((TB, BLK_ROWS), F32)
    for k in range(TOPK):
        weights = weights + jnp.where(col == lp[:, k:k + 1], topw[:, k:k + 1], 0.0)
    ffn = _dot(weights.astype(BF16), y_sorted)
    y = DN_ALPHA * x1_ref[...] + gate2 * ffn
    o_ref[...] = _norm_noaffine(y) * g_ref[...] + b_ref[...]


def _combine(lstart, blen, gstart, lpos, topw, x1, mod, ln2g, ln2b, ys, *, blk0, n_seq, seq_len,
             mod_row0, mod_row_step):
    bps = seq_len // TB
    n_blk = n_seq * bps
    tok = lambda i, *_: (i, 0)
    const = lambda i, *_: (0, 0)
    grid_spec = pltpu.PrefetchScalarGridSpec(
        num_scalar_prefetch=3,
        grid=(n_blk,),
        in_specs=[
            pl.BlockSpec((TB, LANES), tok),
            pl.BlockSpec((TB, LANES), tok),
            pl.BlockSpec((TB, D), tok),
            pl.BlockSpec((8, 6 * D), const),
            pl.BlockSpec((1, D), const),
            pl.BlockSpec((1, D), const),
            pl.BlockSpec(memory_space=pl.ANY),
        ],
        out_specs=pl.BlockSpec((TB, D), tok),
        scratch_shapes=[pltpu.VMEM((2, BLK_ROWS * SUB, LANES), F32), pltpu.SemaphoreType.DMA((2,))],
    )
    kern = functools.partial(_combine_kernel, blk0=blk0, n_blk=n_blk, blocks_per_seq=bps,
                             mod_row0=mod_row0, mod_row_step=mod_row_step)
    return pl.pallas_call(
        kern, grid_spec=grid_spec,
        out_shape=jax.ShapeDtypeStruct((n_seq * seq_len, D), F32),
        compiler_params=_params(("arbitrary",)),
        name="combine_lat" if blk0 else "combine_ctx",
    )(lstart, blen, gstart, lpos, topw, x1, mod, ln2g, ln2b, ys)


def _dft_tables():
    def cs(n):
        k = np.arange(n, dtype=np.int64)
        ang = 2.0 * np.pi * ((k[:, None] * k[None, :]) % n).astype(np.float64) / n
        return np.cos(ang) / math.sqrt(n), np.sin(ang) / math.sqrt(n)

    c, s = cs(FG)
    cdft = np.concatenate([c, s], axis=1).astype(np.float32)
    pd = {}
    for t in (T_CTX, T_LAT):
        c, s = cs(t)
        pd[t] = np.concatenate([c, -s], axis=1).astype(np.float32)
    tri = np.tril(np.ones((TB, TB), np.float32), k=-1)
    upper = np.triu(np.ones((LANES, LANES), np.float32), k=1)
    return cdft, pd, tri, upper


_ROT_PERM = np.array(list(range(8, 16)) + list(range(0, 8)) + list(range(24, 32)) + list(range(16, 24)))
_ROT_SIGN = np.array([-1.0] * 8 + [1.0] * 8 + [-1.0] * 8 + [1.0] * 8, np.float32)


def _rope_tables():
    rows = T_LAT // GRID_W
    row = jnp.repeat(jnp.arange(rows, dtype=F32), GRID_W)
    col = jnp.tile(jnp.arange(GRID_W, dtype=F32), rows)
    axis_dim = ROPE // 2
    inv_freq = ROPE_THETA ** (-jnp.arange(0, axis_dim, 2, dtype=F32) / axis_dim)
    ang_r = row[:, None] * inv_freq[None, :]
    ang_c = col[:, None] * inv_freq[None, :]
    ang = jnp.concatenate([ang_r, ang_r, ang_c, ang_c], axis=-1)
    pad = ((0, 0), (0, LANES - ROPE))
    return jnp.pad(jnp.cos(ang), pad), jnp.pad(jnp.sin(ang), pad)


def kernel(x_prompt, x_sample, cache_ckv, cache_krope, c, c_ctx, w_mod, b_mod, w_in, q_norm_g, w_q_up,
           kv_norm_g, w_kv_up, w_fourier_o, w_mla_o, w_gate, b_gate, w_out, ln1_g, ln1_b, w_router,
           b_router, w_gate_e, b_gate_e, w_up_e, b_up_e, w_down_e, b_down_e, ln2_g, ln2_b):
    cdft_np, pdft_np, tri_np, upper_np = _dft_tables()
    upper = jnp.asarray(upper_np)
    cdft = jnp.asarray(cdft_np).astype(BF16)
    pdft_ctx = jnp.asarray(pdft_np[T_CTX]).astype(BF16)
    pdft_lat = jnp.asarray(pdft_np[T_LAT]).astype(BF16)
    tri = jnp.asarray(tri_np).astype(BF16)
    cos_pad, sin_pad = _rope_tables()

    w_in0 = w_in[0]
    kr_w = w_in0[:, FW + QL + KVL:]
    lane_pad = ((0, 0), (0, LANES - ROPE))
    w_in_ext = jnp.concatenate(
        [w_in0[:, :FW + QL + KVL], jnp.pad(kr_w, lane_pad),
         jnp.pad(kr_w[:, _ROT_PERM] * _ROT_SIGN, lane_pad)], axis=1).astype(BF16)
    wq3 = w_q_up[0].reshape(QL, HEADS, NOPE + ROPE)
    wq_nope = jnp.transpose(wq3[:, :, :NOPE], (1, 0, 2))
    wq_rope = wq3[:, :, NOPE:]
    head_pad = ((0, 0), (0, 0), (0, LANES - ROPE))
    wqr = jnp.concatenate(
        [jnp.pad(wq_rope, head_pad).reshape(QL, HEADS * LANES),
         jnp.pad(wq_rope[:, :, _ROT_PERM] * _ROT_SIGN, head_pad).reshape(QL, HEADS * LANES)],
        axis=1).astype(BF16)
    wkv3 = w_kv_up[0].reshape(KVL, HEADS, NOPE + VH)
    wk = jnp.transpose(wkv3[:, :, :NOPE], (1, 0, 2))
    wv = jnp.transpose(wkv3[:, :, NOPE:], (1, 0, 2))
    wo3 = w_mla_o[0].reshape(HEADS, VH, D)
    wqa, woa = _prep(wq_nope, wk, wv, wo3)

    c_all = jnp.concatenate([c_ctx[None, :], c, jnp.zeros((8 - 1 - N_LAT_SEQ, D), F32)], axis=0)
    mod = _modulation(c_all, w_mod[0], b_mod)

    qg = q_norm_g
    kvg = kv_norm_g
    wfo = w_fourier_o[0].astype(BF16)
    wg = w_gate[0].astype(BF16)
    wout = w_out[0].astype(BF16)
    wr = jnp.pad(w_router[0], ((0, 0), (0, LANES - N_EXP)))
    br = jnp.pad(b_router, ((0, 0), (0, LANES - N_EXP)), constant_values=NEG_BIG)
    cache_k = jnp.concatenate(
        [cache_ckv[:, 0], jnp.pad(cache_krope[:, 0], ((0, 0), (0, 0), (0, LANES - ROPE)))],
        axis=-1).astype(BF16)

    xc2d = x_prompt.reshape(N_CTX, D)
    xl2d = x_sample.reshape(N_LAT, D)

    fcs_c, qcat_c, kcat_c, new_ckv, new_krope = _pre(
        xc2d, mod, w_in_ext, cdft, qg, kvg, wqa, wqr, None, None,
        n_seq=N_CTX_SEQ, seq_len=T_CTX, rope=False, mod_row0=0, mod_row_step=0)
    x1_c, h2_c, lpos_c, topw_c, bcnt_c = _mix(
        xc2d, mod, fcs_c, pdft_ctx, qcat_c, kcat_c, None, wfo, woa, wg, b_gate, wout, ln1_g, ln1_b,
        wr, br, tri, upper, n_seq=N_CTX_SEQ, seq_len=T_CTX, mod_row0=0, mod_row_step=0)

    fcs_l, qcat_l, kcat_l = _pre(
        xl2d, mod, w_in_ext, cdft, qg, kvg, wqa, wqr, cos_pad, sin_pad,
        n_seq=N_LAT_SEQ, seq_len=T_LAT, rope=True, mod_row0=1, mod_row_step=1)
    x1_l, h2_l, lpos_l, topw_l, bcnt_l = _mix(
        xl2d, mod, fcs_l, pdft_lat, qcat_l, kcat_l, cache_k, wfo, woa, wg, b_gate, wout, ln1_g,
        ln1_b, wr, br, tri, upper, n_seq=N_LAT_SEQ, seq_len=T_LAT, mod_row0=1, mod_row_step=1)

    blen = jnp.concatenate([bcnt_c[:, 0, :N_EXP], bcnt_l[:, 0, :N_EXP]], axis=0).astype(jnp.int32)
    lstart = jnp.cumsum(blen, axis=1) - blen
    count = jnp.sum(blen, axis=0)
    start = jnp.cumsum(count) - count
    gstart = start[None, :] + jnp.cumsum(blen, axis=0) - blen
    lstart, blen, gstart = (t.reshape(-1).astype(jnp.int32) for t in (lstart, blen, gstart))
    start = start.astype(jnp.int32)
    count = count.astype(jnp.int32)

    xs = _dispatch(lstart, blen, gstart, h2_c, h2_l, lpos_c, lpos_l)
    ys = _experts(start, count, xs,
                  w_gate_e[0], b_gate_e[0][:, None, :], w_up_e[0], b_up_e[0][:, None, :],
                  w_down_e[0], b_down_e[0][:, None, :])

    y_c = _combine(lstart, blen, gstart, lpos_c, topw_c, x1_c, mod, ln2_g, ln2_b, ys, blk0=0,
                   n_seq=N_CTX_SEQ, seq_len=T_CTX, mod_row0=0, mod_row_step=0)
    y_l = _combine(lstart, blen, gstart, lpos_l, topw_l, x1_l, mod, ln2_g, ln2_b, ys, blk0=N_CTX_BLK,
                   n_seq=N_LAT_SEQ, seq_len=T_LAT, mod_row0=1, mod_row_step=1)
    return (y_c.reshape(N_CTX_SEQ, T_CTX, D), y_l.reshape(N_LAT_SEQ, T_LAT, D), new_ckv, new_krope)
```

```python
import functools
import math

import numpy as np
import jax
import jax.numpy as jnp
from jax import lax
from jax.experimental import pallas as pl
from jax.experimental.pallas import tpu as pltpu

D = 1024
N_CTX_SEQ, T_CTX = 32, 256
N_LAT_SEQ, T_LAT = 4, 1024
PAST = 256
N_CTX = N_CTX_SEQ * T_CTX
N_LAT = N_LAT_SEQ * T_LAT
N_TOK = N_CTX + N_LAT
FW = 512
FG = 128
N_FG = FW // FG
HEADS = 8
QL = 256
KVL = 128
NOPE = 64
ROPE = 32
VH = 64
N_EXP = 32
TOPK = 4
D_EXP = 1024
SWIGLU_LIMIT = 7.0
SWIGLU_ALPHA = 1.702
LN_EPS = 1e-5
RMS_EPS = 1e-6
DN_ALPHA = 2.0 ** 0.25
ATT_SCALE = float(NOPE + ROPE) ** -0.5
ROPE_THETA = 10000.0
GRID_W = 64

LANES = 128
TB = 256
MIX_ROWS_CTX = 512
MIX_ROWS_LAT = 256
SUB = 8
ROW_TILE = 256
N_SORT = N_TOK * TOPK
N_SORT_PAD = N_SORT + ROW_TILE
BLK_ROWS = TB * TOPK
RUN_BITS = TB.bit_length()
VMEM_LIMIT = 56 * 1024 * 1024
NEG_BIG = -1e30

F32 = jnp.float32
BF16 = jnp.bfloat16
HIGHEST = lax.Precision.HIGHEST


def _dot(a, b):
    return jnp.dot(a, b, preferred_element_type=F32)


def _dot_nt(a, b):
    return lax.dot_general(a, b, (((1,), (1,)), ((), ())), preferred_element_type=F32)


def _params(sem):
    return pltpu.CompilerParams(dimension_semantics=sem, vmem_limit_bytes=VMEM_LIMIT)


def _full(shape):
    n = len(shape)
    return pl.BlockSpec(shape, lambda *_: (0,) * n, pipeline_mode=pl.Buffered(1))


def _norm_noaffine(x):
    mu = jnp.mean(x, axis=-1, keepdims=True)
    xc = x - mu
    var = jnp.mean(xc * xc, axis=-1, keepdims=True)
    return xc * lax.rsqrt(var + LN_EPS)


def _rms(x, g):
    return x * lax.rsqrt(jnp.mean(x * x, axis=-1, keepdims=True) + RMS_EPS) * g


def _mod_row(mod_ref, row0, step, seq):
    row = row0 if step == 0 else row0 + step * seq
    m = mod_ref[pl.ds(row, 1), :]
    return [m[:, i * D:(i + 1) * D] for i in range(6)]


def _prep_kernel(wqn_ref, wk_ref, wv_ref, wo_ref, qabs_ref, oabs_ref):
    qabs = lax.dot_general(wqn_ref[0], wk_ref[0], (((1,), (1,)), ((), ())),
                           preferred_element_type=F32, precision=HIGHEST)
    qabs_ref[...] = qabs.astype(BF16)
    oabs = jnp.dot(wv_ref[0], wo_ref[0], preferred_element_type=F32, precision=HIGHEST)
    oabs_ref[...] = oabs.astype(BF16)


def _prep(wqn, wk, wv, wo):
    return pl.pallas_call(
        _prep_kernel,
        grid=(HEADS,),
        in_specs=[
            pl.BlockSpec((1, QL, NOPE), lambda h: (h, 0, 0)),
            pl.BlockSpec((1, KVL, NOPE), lambda h: (h, 0, 0)),
            pl.BlockSpec((1, KVL, VH), lambda h: (h, 0, 0)),
            pl.BlockSpec((1, VH, D), lambda h: (h, 0, 0)),
        ],
        out_specs=[
            pl.BlockSpec((QL, KVL), lambda h: (0, h)),
            pl.BlockSpec((KVL, D), lambda h: (h, 0)),
        ],
        out_shape=[jax.ShapeDtypeStruct((QL, HEADS * KVL), BF16),
                   jax.ShapeDtypeStruct((HEADS * KVL, D), BF16)],
        compiler_params=_params(("arbitrary",)),
        name="weight_prep",
    )(wqn, wk, wv, wo)


MOD_COLS = 1536


def _mod_kernel(c_ref, w_ref, b_ref, o_ref):
    c = c_ref[...]
    s = c * jax.nn.sigmoid(c)
    o_ref[...] = jnp.dot(s, w_ref[...], preferred_element_type=F32, precision=HIGHEST) + b_ref[...]


def _modulation(c_all, w_mod, b_mod):
    return pl.pallas_call(
        _mod_kernel,
        grid=(6 * D // MOD_COLS,),
        in_specs=[
            pl.BlockSpec((8, D), lambda i: (0, 0)),
            pl.BlockSpec((D, MOD_COLS), lambda i: (0, i)),
            pl.BlockSpec((1, MOD_COLS), lambda i: (0, i)),
        ],
        out_specs=pl.BlockSpec((8, MOD_COLS), lambda i: (0, i)),
        out_shape=jax.ShapeDtypeStruct((8, 6 * D), F32),
        compiler_params=_params(("arbitrary",)),
        name="modulation",
    )(c_all, w_mod, b_mod)


def _pre_kernel(*refs, seq_len, rope, mod_row0, mod_row_step):
    if rope:
        (x_ref, mod_ref, win_ref, cdft_ref, qg_ref, kvg_ref, wqa_ref, wqr_ref, cos_ref, sin_ref,
         fcs_ref, qcat_ref, kcat_ref) = refs
    else:
        (x_ref, mod_ref, win_ref, cdft_ref, qg_ref, kvg_ref, wqa_ref, wqr_ref,
         fcs_ref, qcat_ref, kcat_ref, ckv_ref, krope_ref) = refs
    blocks_per_seq = seq_len // TB
    b = pl.program_id(0) // blocks_per_seq
    shift1, scale1 = _mod_row(mod_ref, mod_row0, mod_row_step, b)[:2]

    x = x_ref[...]
    h1 = _norm_noaffine(x) * (1.0 + scale1) + shift1
    h1b = h1.astype(BF16)
    n_cols = 9 * LANES if rope else 8 * LANES
    proj = _dot(h1b, win_ref[:, :n_cols])

    f_b = proj[:, :FW].astype(BF16)
    for g in range(N_FG):
        r = _dot(f_b[:, g * FG:(g + 1) * FG], cdft_ref[...])
        fcs_ref[0, 0, :, g * FG:(g + 1) * FG] = r[:, :FG].astype(BF16)
        fcs_ref[0, 1, :, g * FG:(g + 1) * FG] = r[:, FG:].astype(BF16)

    qn = _rms(proj[:, FW:FW + QL], qg_ref[...]).astype(BF16)
    ckv = _rms(proj[:, FW + QL:FW + QL + KVL], kvg_ref[...])
    kr = proj[:, 7 * LANES:8 * LANES]

    qa = _dot(qn, wqa_ref[...]) * ATT_SCALE
    if rope:
        cos = cos_ref[...]
        sin = sin_ref[...]
        qr2 = _dot(qn, wqr_ref[...])
        kr_keys = kr * cos + proj[:, 8 * LANES:9 * LANES] * sin
    else:
        qr2 = _dot(qn, wqr_ref[:, :HEADS * LANES])
        kr_keys = kr
    for h in range(HEADS):
        qr_h = qr2[:, h * LANES:(h + 1) * LANES]
        if rope:
            qr_h = qr_h * cos + qr2[:, (HEADS + h) * LANES:(HEADS + h + 1) * LANES] * sin
        qcat_ref[0, h, :, :KVL] = qa[:, h * KVL:(h + 1) * KVL].astype(BF16)
        qcat_ref[0, h, :, KVL:] = (qr_h * ATT_SCALE).astype(BF16)
    kcat_ref[0, :, :KVL] = ckv.astype(BF16)
    kcat_ref[0, :, KVL:] = kr_keys.astype(BF16)
    if not rope:
        ckv_ref[0, 0] = ckv
        krope_ref[0, 0] = kr[:, :ROPE]


def _pre(x2d, mod, w_in_ext, cdft, qg, kvg, wqa, wqr, cos_pad, sin_pad, *, n_seq, seq_len, rope,
         mod_row0, mod_row_step):
    bps = seq_len // TB
    n_blk = n_seq * bps
    in_specs = [
        pl.BlockSpec((TB, D), lambda i: (i, 0)),
        _full((8, 6 * D)),
        _full((D, 9 * LANES)),
        _full((FG, 2 * FG)),
        _full((1, QL)),
        _full((1, KVL)),
        _full((QL, HEADS * KVL)),
        _full((QL, 2 * HEADS * LANES)),
    ]
    args = [x2d, mod, w_in_ext, cdft, qg, kvg, wqa, wqr]
    out_specs = [
        pl.BlockSpec((1, 2, TB, FW), lambda i: (i // bps, 0, i % bps, 0)),
        pl.BlockSpec((1, HEADS, TB, 2 * LANES), lambda i: (i // bps, 0, i % bps, 0)),
        pl.BlockSpec((1, TB, 2 * LANES), lambda i: (i // bps, i % bps, 0)),
    ]
    out_shape = [
        jax.ShapeDtypeStruct((n_seq, 2, seq_len, FW), BF16),
        jax.ShapeDtypeStruct((n_seq, HEADS, seq_len, 2 * LANES), BF16),
        jax.ShapeDtypeStruct((n_seq, seq_len, 2 * LANES), BF16),
    ]
    if rope:
        in_specs += [pl.BlockSpec((TB, LANES), lambda i: (i % bps, 0)),
                     pl.BlockSpec((TB, LANES), lambda i: (i % bps, 0))]
        args += [cos_pad, sin_pad]
    else:
        out_specs += [pl.BlockSpec((1, 1, TB, KVL), lambda i: (i // bps, 0, i % bps, 0)),
                      pl.BlockSpec((1, 1, TB, ROPE), lambda i: (i // bps, 0, i % bps, 0))]
        out_shape += [jax.ShapeDtypeStruct((n_seq, 1, seq_len, KVL), F32),
                      jax.ShapeDtypeStruct((n_seq, 1, seq_len, ROPE), F32)]
    kern = functools.partial(_pre_kernel, seq_len=seq_len, rope=rope, mod_row0=mod_row0,
                             mod_row_step=mod_row_step)
    return pl.pallas_call(
        kern, grid=(n_blk,), in_specs=in_specs, out_specs=out_specs, out_shape=out_shape,
        compiler_params=_params(("arbitrary",)),
        name="pre_lat" if rope else "pre_ctx",
    )(*args)


def _mix_kernel(*refs, rows, seq_len, seqs, cache, mod_row0, mod_row_step):
    if cache:
        (x_ref, mod_ref, fcs_ref, pdft_ref, qcat_ref, kcat_ref, cache_ref, wfo_ref, woa_ref,
         wg_ref, bg_ref, wout_ref, ln1g_ref, ln1b_ref, wr_ref, br_ref, tri_ref, upper_ref,
         x1_ref, h2_ref, lpos_ref, topw_ref, bcnt_ref) = refs
    else:
        (x_ref, mod_ref, fcs_ref, pdft_ref, qcat_ref, kcat_ref, wfo_ref, woa_ref,
         wg_ref, bg_ref, wout_ref, ln1g_ref, ln1b_ref, wr_ref, br_ref, tri_ref, upper_ref,
         x1_ref, h2_ref, lpos_ref, topw_ref, bcnt_ref) = refs
        cache_ref = None
    b = pl.program_id(0)
    shift1, scale1, gate1, shift2, scale2, _ = _mod_row(mod_ref, mod_row0, mod_row_step, b)

    x = x_ref[...]
    h1b = (_norm_noaffine(x) * (1.0 + scale1) + shift1).astype(BF16)

    mixed_rows, attn_rows = [], []
    for s in range(seqs):
        mixed_rows.append(_dot(pdft_ref[:, :seq_len], fcs_ref[s, 0])
                          + _dot(pdft_ref[:, seq_len:], fcs_ref[s, 1]))
        kc = kcat_ref[s]
        vals = kc[:, :KVL]
        if cache:
            kc2 = cache_ref[s]
            vals2 = kc2[:, :KVL]
        heads_out = []
        for h in range(HEADS):
            q = qcat_ref[s, h]
            s1 = _dot_nt(q, kc)
            m = jnp.max(s1, axis=-1, keepdims=True)
            if cache:
                s2 = _dot_nt(q, kc2)
                m = jnp.maximum(m, jnp.max(s2, axis=-1, keepdims=True))
            p1 = jnp.exp(s1 - m)
            l = jnp.sum(p1, axis=-1, keepdims=True)
            o = _dot(p1.astype(BF16), vals)
            if cache:
                p2 = jnp.exp(s2 - m)
                l = l + jnp.sum(p2, axis=-1, keepdims=True)
                o = o + _dot(p2.astype(BF16), vals2)
            heads_out.append((o / l).astype(BF16))
        attn_rows.append(jnp.concatenate(heads_out, axis=1))
    mixed = jnp.concatenate(mixed_rows, axis=0) if seqs > 1 else mixed_rows[0]
    attn = jnp.concatenate(attn_rows, axis=0) if seqs > 1 else attn_rows[0]
    f_out = _dot(mixed.astype(BF16), wfo_ref[...])
    m_out = _dot(attn, woa_ref[...])

    gates = jax.nn.sigmoid(_dot(h1b, wg_ref[...]) + bg_ref[...])
    merged = gates[:, :D] * f_out + gates[:, D:] * m_out
    mix = _dot(merged.astype(BF16), wout_ref[...])

    x1 = _norm_noaffine(DN_ALPHA * x + gate1 * mix) * ln1g_ref[...] + ln1b_ref[...]
    x1_ref[...] = x1
    h2 = _norm_noaffine(x1) * (1.0 + scale2) + shift2
    h2_ref[...] = h2

    h_hi = h2.astype(BF16)
    h_lo = (h2 - h_hi.astype(F32)).astype(BF16)
    hi_part = _dot(h_hi, wr_ref[...])
    logits_all = (hi_part[:, :LANES] + hi_part[:, LANES:] + _dot(h_lo, wr_ref[:, :LANES])) + br_ref[...]
    lane = lax.broadcasted_iota(jnp.int32, (TB, LANES), 1).astype(F32)
    for r in range(rows // TB):
        blk = slice(r * TB, (r + 1) * TB)
        work = logits_all[blk]
        top_v, top_i = [], []
        for _ in range(TOPK):
            mk = jnp.max(work, axis=-1, keepdims=True)
            ik = jnp.min(jnp.where(work == mk, lane, float(LANES)), axis=-1, keepdims=True)
            work = jnp.where(lane == ik, -jnp.inf, work)
            top_v.append(mk)
            top_i.append(ik)
        exps = [jnp.exp(v - top_v[0]) for v in top_v]
        denom = exps[0] + exps[1] + exps[2] + exps[3]

        onehot = jnp.zeros((TB, LANES), F32)
        for ik in top_i:
            onehot = onehot + jnp.where(lane == ik, 1.0, 0.0)
        counts = jnp.sum(onehot, axis=0, keepdims=True)
        lower = jnp.dot(jnp.broadcast_to(counts, (8, LANES)), upper_ref[...],
                        preferred_element_type=F32, precision=HIGHEST)[0:1, :]
        before = _dot(tri_ref[...], onehot.astype(BF16)) + lower
        lpos = jnp.zeros((TB, LANES), jnp.int32)
        topw = jnp.zeros((TB, LANES), F32)
        for k in range(TOPK):
            pos = jnp.sum(jnp.where(lane == top_i[k], before, 0.0), axis=-1, keepdims=True)
            lpos = jnp.where(lane == float(k), pos.astype(jnp.int32), lpos)
            topw = jnp.where(lane == float(k), exps[k] / denom, topw)
        lpos_ref[blk, :] = lpos
        topw_ref[blk, :] = topw
        bcnt_ref[r] = counts


def _mix(x2d, mod, fcs, pdft, qcat, kcat, cache_k, wfo, woa, wg, bg, wout, ln1g, ln1b, wr, br,
         tri, upper, *, rows, n_seq, seq_len, mod_row0, mod_row_step):
    cache = cache_k is not None
    seqs = max(1, rows // seq_len)
    q_rows = rows // seqs
    bps = seq_len // q_rows
    n_outer = n_seq // seqs
    in_specs = [
        pl.BlockSpec((rows, D), lambda b, j: (b * bps + j, 0)),
        _full((8, 6 * D)),
        pl.BlockSpec((seqs, 2, seq_len, FW), lambda b, j: (b, 0, 0, 0)),
        pl.BlockSpec((q_rows, 2 * seq_len), lambda b, j: (j, 0)),
        pl.BlockSpec((seqs, HEADS, q_rows, 2 * LANES), lambda b, j: (b, 0, j, 0)),
        pl.BlockSpec((seqs, seq_len, 2 * LANES), lambda b, j: (b, 0, 0)),
    ]
    args = [x2d, mod, fcs, pdft, qcat, kcat]
    if cache:
        in_specs.append(pl.BlockSpec((seqs, PAST, 2 * LANES), lambda b, j: (b, 0, 0)))
        args.append(cache_k)
    in_specs += [
        _full((FW, D)), _full((HEADS * KVL, D)), _full((D, 2 * D)), _full((1, 2 * D)),
        _full((D, D)), _full((1, D)), _full((1, D)), _full((D, 2 * LANES)), _full((1, LANES)),
        _full((TB, TB)), _full((LANES, LANES)),
    ]
    args += [wfo, woa, wg, bg, wout, ln1g, ln1b, wr, br, tri, upper]
    n_tok = n_seq * seq_len
    tok_spec = lambda w: pl.BlockSpec((rows, w), lambda b, j: (b * bps + j, 0))
    out_specs = [tok_spec(D), tok_spec(D), tok_spec(LANES), tok_spec(LANES),
                 pl.BlockSpec((rows // TB, 1, LANES), lambda b, j: (b * bps + j, 0, 0))]
    out_shape = [
        jax.ShapeDtypeStruct((n_tok, D), F32),
        jax.ShapeDtypeStruct((n_tok, D), F32),
        jax.ShapeDtypeStruct((n_tok, LANES), jnp.int32),
        jax.ShapeDtypeStruct((n_tok, LANES), F32),
        jax.ShapeDtypeStruct((n_tok // TB, 1, LANES), F32),
    ]
    kern = functools.partial(_mix_kernel, rows=rows, seq_len=seq_len, seqs=seqs, cache=cache,
                             mod_row0=mod_row0, mod_row_step=mod_row_step)
    return pl.pallas_call(
        kern, grid=(n_outer, bps), in_specs=in_specs, out_specs=out_specs, out_shape=out_shape,
        compiler_params=_params(("arbitrary", "arbitrary")),
        name="mix_lat" if cache else "mix_ctx",
    )(*args)


N_CTX_BLK = N_CTX // TB
N_BLK = N_TOK // TB


def _row_tiles(ref, row0, n_rows):
    return ref.at[pl.ds(pl.multiple_of(row0 * SUB, SUB), n_rows * SUB), :]


def _for_each_run_copy(lstart_ref, blen_ref, gstart_ref, blk, local_ref, global_ref, sem, to_global, fn):
    def per_expert(e, carry):
        idx = blk * N_EXP + e
        n = blen_ref[idx]
        a = lstart_ref[idx]
        g = gstart_ref[idx]
        for bit in range(RUN_BITS - 1, -1, -1):
            size = 1 << bit

            @pl.when((n & size) != 0)
            def _():
                done = lax.shift_left(lax.shift_right_logical(n, bit + 1), bit + 1)
                loc = _row_tiles(local_ref, a + done, size)
                glo = _row_tiles(global_ref, g + done, size)
                fn(pltpu.make_async_copy(loc, glo, sem) if to_global
                   else pltpu.make_async_copy(glo, loc, sem))
        return carry

    lax.fori_loop(0, N_EXP, per_expert, 0)


def _load_rows(ref, n_rows):
    return jnp.concatenate([ref[pl.ds(s, n_rows, stride=SUB), :] for s in range(SUB)], axis=1)


def _store_rows(ref, val, n_rows):
    for s in range(SUB):
        ref[pl.ds(s, n_rows, stride=SUB), :] = val[:, s * LANES:(s + 1) * LANES]


def _dispatch_kernel(lstart_ref, blen_ref, gstart_ref, h2c_ref, h2l_ref, lpc_ref, lpl_ref, xs_ref,
                     buf, zbuf, sem, zsem):
    b = pl.program_id(0)
    slot = b & 1
    tables = (lstart_ref, blen_ref, gstart_ref)

    def start_runs(blk, s):
        _for_each_run_copy(*tables, blk, buf.at[s], xs_ref, sem.at[s], True, lambda cp: cp.start())

    def wait_runs(s):
        pltpu.make_async_copy(buf.at[s], _row_tiles(xs_ref, 0, BLK_ROWS), sem.at[s]).wait()

    @pl.when(b == 0)
    def _():
        zbuf[...] = jnp.zeros_like(zbuf)
        tail = pltpu.make_async_copy(zbuf, _row_tiles(xs_ref, N_SORT, ROW_TILE), zsem)
        tail.start()
        tail.wait()

    @pl.when(b >= 2)
    def _():
        wait_runs(slot)

    def sort_block(h2_ref, lp_ref):
        col = lax.broadcasted_iota(jnp.int32, (TB, BLK_ROWS), 1)
        lp = lp_ref[...]
        pick = jnp.zeros((TB, BLK_ROWS), F32)
        for k in range(TOPK):
            pick = pick + jnp.where(col == lp[:, k:k + 1], 1.0, 0.0)
        sorted_rows = lax.dot_general(pick.astype(BF16), h2_ref[...].astype(BF16),
                                      (((0,), (0,)), ((), ())), preferred_element_type=F32)
        _store_rows(buf.at[slot], sorted_rows, BLK_ROWS)

    @pl.when(b < N_CTX_BLK)
    def _():
        sort_block(h2c_ref, lpc_ref)

    @pl.when(b >= N_CTX_BLK)
    def _():
        sort_block(h2l_ref, lpl_ref)

    start_runs(b, slot)

    @pl.when(b == N_BLK - 1)
    def _():
        wait_runs(1 - slot)
        wait_runs(slot)


def _dispatch(lstart, blen, gstart, h2c, h2l, lpc, lpl):
    ctx_idx = lambda i, *_: (jnp.minimum(i, N_CTX_BLK - 1), 0)
    lat_idx = lambda i, *_: (jnp.maximum(i - N_CTX_BLK, 0), 0)
    grid_spec = pltpu.PrefetchScalarGridSpec(
        num_scalar_prefetch=3,
        grid=(N_BLK,),
        in_specs=[
            pl.BlockSpec((TB, D), ctx_idx),
            pl.BlockSpec((TB, D), lat_idx),
            pl.BlockSpec((TB, LANES), ctx_idx),
            pl.BlockSpec((TB, LANES), lat_idx),
        ],
        out_specs=pl.BlockSpec(memory_space=pl.ANY),
        scratch_shapes=[pltpu.VMEM((2, BLK_ROWS * SUB, LANES), F32),
                        pltpu.VMEM((ROW_TILE * SUB, LANES), F32),
                        pltpu.SemaphoreType.DMA((2,)),
                        pltpu.SemaphoreType.DMA(())],
    )
    return pl.pallas_call(
        _dispatch_kernel, grid_spec=grid_spec,
        out_shape=jax.ShapeDtypeStruct((N_SORT_PAD * SUB, LANES), F32),
        compiler_params=_params(("arbitrary",)),
        name="dispatch",
    )(lstart, blen, gstart, h2c, h2l, lpc, lpl)


def _expert_kernel(start_ref, count_ref, xs_ref, wg_ref, bg_ref, wu_ref, bu_ref, wd_ref, bd_ref, ys_ref,
                   wbf, xbuf, ybuf, sem_in, sem_out):
    e = pl.program_id(0)
    row0 = start_ref[e]

    def tiles_of(ex):
        return lax.shift_right_logical(count_ref[ex] + (ROW_TILE - 1), ROW_TILE.bit_length() - 1)

    n_tiles = tiles_of(e)

    def in_copy_of(ex, j, slot):
        return pltpu.make_async_copy(_row_tiles(xs_ref, start_ref[ex] + j * ROW_TILE, ROW_TILE),
                                     xbuf.at[slot], sem_in.at[slot])

    def in_copy(j, slot):
        return in_copy_of(e, j, slot)

    def out_copy(j, slot):
        return pltpu.make_async_copy(ybuf.at[slot], _row_tiles(ys_ref, row0 + j * ROW_TILE, ROW_TILE),
                                     sem_out.at[slot])

    def start_first_tiles(ex):
        for j in range(2):
            @pl.when(tiles_of(ex) > j)
            def _():
                in_copy_of(ex, j, j).start()

    @pl.when(e == 0)
    def _():
        ybuf[0] = jnp.zeros((ROW_TILE * SUB, LANES), F32)
        tail = pltpu.make_async_copy(ybuf.at[0], _row_tiles(ys_ref, N_SORT, ROW_TILE), sem_out.at[0])
        tail.start()
        tail.wait()
        start_first_tiles(0)

    @pl.when(n_tiles > 0)
    def _():
        wbf[0] = wg_ref[0].astype(BF16)
        wbf[1] = wu_ref[0].astype(BF16)
        wbf[2] = wd_ref[0].astype(BF16)

        def tile(j, carry):
            slot = j & 1

            @pl.when(jnp.logical_and(j >= 1, j + 1 < n_tiles))
            def _():
                in_copy(j + 1, 1 - slot).start()

            in_copy(j, slot).wait()

            @pl.when(j >= 2)
            def _():
                out_copy(j - 2, slot).wait()

            xb = _load_rows(xbuf.at[slot], ROW_TILE).astype(BF16)
            gt = jnp.minimum(_dot(xb, wbf[0]) + bg_ref[0], SWIGLU_LIMIT)
            up = jnp.clip(_dot(xb, wbf[1]) + bu_ref[0], -SWIGLU_LIMIT, SWIGLU_LIMIT)
            act = gt * jax.nn.sigmoid(SWIGLU_ALPHA * gt) * (up + 1.0)
            _store_rows(ybuf.at[slot], _dot(act.astype(BF16), wbf[2]) + bd_ref[0], ROW_TILE)
            out_copy(j, slot).start()
            return carry

        lax.fori_loop(0, n_tiles, tile, 0)

        @pl.when(n_tiles >= 2)
        def _():
            out_copy(n_tiles - 2, n_tiles & 1).wait()

        out_copy(n_tiles - 1, (n_tiles - 1) & 1).wait()

    @pl.when(e + 1 < N_EXP)
    def _():
        start_first_tiles(e + 1)


def _experts(start, count, xs, wg, bg, wu, bu, wd, bd):
    w_idx = lambda e, *_: (e, 0, 0)
    grid_spec = pltpu.PrefetchScalarGridSpec(
        num_scalar_prefetch=2,
        grid=(N_EXP,),
        in_specs=[
            pl.BlockSpec(memory_space=pl.ANY),
            pl.BlockSpec((1, D, D_EXP), w_idx),
            pl.BlockSpec((1, 1, D_EXP), w_idx),
            pl.BlockSpec((1, D, D_EXP), w_idx),
            pl.BlockSpec((1, 1, D_EXP), w_idx),
            pl.BlockSpec((1, D_EXP, D), w_idx),
            pl.BlockSpec((1, 1, D), w_idx),
        ],
        out_specs=pl.BlockSpec(memory_space=pl.ANY),
        scratch_shapes=[pltpu.VMEM((3, D, D_EXP), BF16),
                        pltpu.VMEM((2, ROW_TILE * SUB, LANES), F32),
                        pltpu.VMEM((2, ROW_TILE * SUB, LANES), F32),
                        pltpu.SemaphoreType.DMA((2,)),
                        pltpu.SemaphoreType.DMA((2,))],
    )
    return pl.pallas_call(
        _expert_kernel, grid_spec=grid_spec,
        out_shape=jax.ShapeDtypeStruct((N_SORT_PAD * SUB, LANES), F32),
        compiler_params=_params(("arbitrary",)),
        name="experts",
    )(start, count, xs, wg, bg, wu, bu, wd, bd)


def _combine_kernel(lstart_ref, blen_ref, gstart_ref, lp_ref, topw_ref, x1_ref, mod_ref, g_ref, b_ref,
                    ys_ref, o_ref, buf, sem, *, blk0, n_blk, blocks_per_seq, mod_row0, mod_row_step):
    i = pl.program_id(0)
    slot = i & 1
    gate2 = _mod_row(mod_ref, mod_row0, mod_row_step, i // blocks_per_seq)[5]
    tables = (lstart_ref, blen_ref, gstart_ref)

    def start_runs(step, s):
        _for_each_run_copy(*tables, blk0 + step, buf.at[s], ys_ref, sem.at[s], False,
                           lambda cp: cp.start())

    @pl.when(i == 0)
    def _():
        start_runs(0, 0)

    @pl.when(i + 1 < n_blk)
    def _():
        start_runs(i + 1, 1 - slot)

    pltpu.make_async_copy(_row_tiles(ys_ref, 0, BLK_ROWS), buf.at[slot], sem.at[slot]).wait()

    y_sorted = _load_rows(buf.at[slot], BLK_ROWS).astype(BF16)
    col = lax.broadcasted_iota(jnp.int32, (TB, BLK_ROWS), 1)
    lp = lp_ref[...]
    topw = topw_ref[...]
    weights = jnp.zeros((TB, BLK_ROWS), F32)
    for k in range(TOPK):
        weights = weights + jnp.where(col == lp[:, k:k + 1], topw[:, k:k + 1], 0.0)
    ffn = _dot(weights.astype(BF16), y_sorted)
    y = DN_ALPHA * x1_ref[...] + gate2 * ffn
    o_ref[...] = _norm_noaffine(y) * g_ref[...] + b_ref[...]


def _combine(lstart, blen, gstart, lpos, topw, x1, mod, ln2g, ln2b, ys, *, blk0, n_seq, seq_len,
             mod_row0, mod_row_step):
    bps = seq_len // TB
    n_blk = n_seq * bps
    tok = lambda i, *_: (i, 0)
    const = lambda i, *_: (0, 0)
    grid_spec = pltpu.PrefetchScalarGridSpec(
        num_scalar_prefetch=3,
        grid=(n_blk,),
        in_specs=[
            pl.BlockSpec((TB, LANES), tok),
            pl.BlockSpec((TB, LANES), tok),
            pl.BlockSpec((TB, D), tok),
            pl.BlockSpec((8, 6 * D), const),
            pl.BlockSpec((1, D), const),
            pl.BlockSpec((1, D), const),
            pl.BlockSpec(memory_space=pl.ANY),
        ],
        out_specs=pl.BlockSpec((TB, D), tok),
        scratch_shapes=[pltpu.VMEM((2, BLK_ROWS * SUB, LANES), F32), pltpu.SemaphoreType.DMA((2,))],
    )
    kern = functools.partial(_combine_kernel, blk0=blk0, n_blk=n_blk, blocks_per_seq=bps,
                             mod_row0=mod_row0, mod_row_step=mod_row_step)
    return pl.pallas_call(
        kern, grid_spec=grid_spec,
        out_shape=jax.ShapeDtypeStruct((n_seq * seq_len, D), F32),
        compiler_params=_params(("arbitrary",)),
        name="combine_lat" if blk0 else "combine_ctx",
    )(lstart, blen, gstart, lpos, topw, x1, mod, ln2g, ln2b, ys)


def _dft_tables():
    def cs(n):
        k = np.arange(n, dtype=np.int64)
        ang = 2.0 * np.pi * ((k[:, None] * k[None, :]) % n).astype(np.float64) / n
        return np.cos(ang) / math.sqrt(n), np.sin(ang) / math.sqrt(n)

    c, s = cs(FG)
    cdft = np.concatenate([c, s], axis=1).astype(np.float32)
    pd = {}
    for t in (T_CTX, T_LAT):
        c, s = cs(t)
        pd[t] = np.concatenate([c, -s], axis=1).astype(np.float32)
    tri = np.tril(np.ones((TB, TB), np.float32), k=-1)
    upper = np.triu(np.ones((LANES, LANES), np.float32), k=1)
    return cdft, pd, tri, upper


_ROT_PERM = np.array(list(range(8, 16)) + list(range(0, 8)) + list(range(24, 32)) + list(range(16, 24)))
_ROT_SIGN = np.array([-1.0] * 8 + [1.0] * 8 + [-1.0] * 8 + [1.0] * 8, np.float32)


def _rope_tables():
    rows = T_LAT // GRID_W
    row = jnp.repeat(jnp.arange(rows, dtype=F32), GRID_W)
    col = jnp.tile(jnp.arange(GRID_W, dtype=F32), rows)
    axis_dim = ROPE // 2
    inv_freq = ROPE_THETA ** (-jnp.arange(0, axis_dim, 2, dtype=F32) / axis_dim)
    ang_r = row[:, None] * inv_freq[None, :]
    ang_c = col[:, None] * inv_freq[None, :]
    ang = jnp.concatenate([ang_r, ang_r, ang_c, ang_c], axis=-1)
    pad = ((0, 0), (0, LANES - ROPE))
    return jnp.pad(jnp.cos(ang), pad), jnp.pad(jnp.sin(ang), pad)


def kernel(x_prompt, x_sample, cache_ckv, cache_krope, c, c_ctx, w_mod, b_mod, w_in, q_norm_g, w_q_up,
           kv_norm_g, w_kv_up, w_fourier_o, w_mla_o, w_gate, b_gate, w_out, ln1_g, ln1_b, w_router,
           b_router, w_gate_e, b_gate_e, w_up_e, b_up_e, w_down_e, b_down_e, ln2_g, ln2_b):
    cdft_np, pdft_np, tri_np, upper_np = _dft_tables()
    upper = jnp.asarray(upper_np)
    cdft = jnp.asarray(cdft_np).astype(BF16)
    pdft_ctx = jnp.asarray(pdft_np[T_CTX]).astype(BF16)
    pdft_lat = jnp.asarray(pdft_np[T_LAT]).astype(BF16)
    tri = jnp.asarray(tri_np).astype(BF16)
    cos_pad, sin_pad = _rope_tables()

    w_in0 = w_in[0]
    kr_w = w_in0[:, FW + QL + KVL:]
    lane_pad = ((0, 0), (0, LANES - ROPE))
    w_in_ext = jnp.concatenate(
        [w_in0[:, :FW + QL + KVL], jnp.pad(kr_w, lane_pad),
         jnp.pad(kr_w[:, _ROT_PERM] * _ROT_SIGN, lane_pad)], axis=1).astype(BF16)
    wq3 = w_q_up[0].reshape(QL, HEADS, NOPE + ROPE)
    wq_nope = jnp.transpose(wq3[:, :, :NOPE], (1, 0, 2))
    wq_rope = wq3[:, :, NOPE:]
    head_pad = ((0, 0), (0, 0), (0, LANES - ROPE))
    wqr = jnp.concatenate(
        [jnp.pad(wq_rope, head_pad).reshape(QL, HEADS * LANES),
         jnp.pad(wq_rope[:, :, _ROT_PERM] * _ROT_SIGN, head_pad).reshape(QL, HEADS * LANES)],
        axis=1).astype(BF16)
    wkv3 = w_kv_up[0].reshape(KVL, HEADS, NOPE + VH)
    wk = jnp.transpose(wkv3[:, :, :NOPE], (1, 0, 2))
    wv = jnp.transpose(wkv3[:, :, NOPE:], (1, 0, 2))
    wo3 = w_mla_o[0].reshape(HEADS, VH, D)
    wqa, woa = _prep(wq_nope, wk, wv, wo3)

    c_all = jnp.concatenate([c_ctx[None, :], c, jnp.zeros((8 - 1 - N_LAT_SEQ, D), F32)], axis=0)
    mod = _modulation(c_all, w_mod[0], b_mod)

    qg = q_norm_g
    kvg = kv_norm_g
    wfo = w_fourier_o[0].astype(BF16)
    wg = w_gate[0].astype(BF16)
    wout = w_out[0].astype(BF16)
    wr_f32 = jnp.pad(w_router[0], ((0, 0), (0, LANES - N_EXP)))
    wr_hi = wr_f32.astype(BF16)
    wr = jnp.concatenate([wr_hi, (wr_f32 - wr_hi.astype(F32)).astype(BF16)], axis=1)
    br = jnp.pad(b_router, ((0, 0), (0, LANES - N_EXP)), constant_values=NEG_BIG)
    cache_k = jnp.concatenate(
        [cache_ckv[:, 0], jnp.pad(cache_krope[:, 0], ((0, 0), (0, 0), (0, LANES - ROPE)))],
        axis=-1).astype(BF16)

    xc2d = x_prompt.reshape(N_CTX, D)
    xl2d = x_sample.reshape(N_LAT, D)

    fcs_c, qcat_c, kcat_c, new_ckv, new_krope = _pre(
        xc2d, mod, w_in_ext, cdft, qg, kvg, wqa, wqr, None, None,
        n_seq=N_CTX_SEQ, seq_len=T_CTX, rope=False, mod_row0=0, mod_row_step=0)
    x1_c, h2_c, lpos_c, topw_c, bcnt_c = _mix(
        xc2d, mod, fcs_c, pdft_ctx, qcat_c, kcat_c, None, wfo, woa, wg, b_gate, wout, ln1_g, ln1_b,
        wr, br, tri, upper, rows=MIX_ROWS_CTX, n_seq=N_CTX_SEQ, seq_len=T_CTX, mod_row0=0,
        mod_row_step=0)

    fcs_l, qcat_l, kcat_l = _pre(
        xl2d, mod, w_in_ext, cdft, qg, kvg, wqa, wqr, cos_pad, sin_pad,
        n_seq=N_LAT_SEQ, seq_len=T_LAT, rope=True, mod_row0=1, mod_row_step=1)
    x1_l, h2_l, lpos_l, topw_l, bcnt_l = _mix(
        xl2d, mod, fcs_l, pdft_lat, qcat_l, kcat_l, cache_k, wfo, woa, wg, b_gate, wout, ln1_g,
        ln1_b, wr, br, tri, upper, rows=MIX_ROWS_LAT, n_seq=N_LAT_SEQ, seq_len=T_LAT, mod_row0=1,
        mod_row_step=1)

    blen = jnp.concatenate([bcnt_c[:, 0, :N_EXP], bcnt_l[:, 0, :N_EXP]], axis=0).astype(jnp.int32)
    lstart = jnp.cumsum(blen, axis=1) - blen
    count = jnp.sum(blen, axis=0)
    start = jnp.cumsum(count) - count
    gstart = start[None, :] + jnp.cumsum(blen, axis=0) - blen
    lstart, blen, gstart = (t.reshape(-1).astype(jnp.int32) for t in (lstart, blen, gstart))
    start = start.astype(jnp.int32)
    count = count.astype(jnp.int32)

    xs = _dispatch(lstart, blen, gstart, h2_c, h2_l, lpos_c, lpos_l)
    ys = _experts(start, count, xs,
                  w_gate_e[0], b_gate_e[0][:, None, :], w_up_e[0], b_up_e[0][:, None, :],
                  w_down_e[0], b_down_e[0][:, None, :])

    y_c = _combine(lstart, blen, gstart, lpos_c, topw_c, x1_c, mod, ln2_g, ln2_b, ys, blk0=0,
                   n_seq=N_CTX_SEQ, seq_len=T_CTX, mod_row0=0, mod_row_step=0)
    y_l = _combine(lstart, blen, gstart, lpos_l, topw_l, x1_l, mod, ln2_g, ln2_b, ys, blk0=N_CTX_BLK,
                   n_seq=N_LAT_SEQ, seq_len=T_LAT, mod_row0=1, mod_row_step=1)
    return (y_c.reshape(N_CTX_SEQ, T_CTX, D), y_l.reshape(N_LAT_SEQ, T_LAT, D), new_ckv, new_krope)
```

```python
import functools
import math

import numpy as np
import jax
import jax.numpy as jnp
from jax import lax
from jax.experimental import pallas as pl
from jax.experimental.pallas import tpu as pltpu

D = 1024
N_CTX_SEQ, T_CTX = 32, 256
N_LAT_SEQ, T_LAT = 4, 1024
PAST = 256
N_CTX = N_CTX_SEQ * T_CTX
N_LAT = N_LAT_SEQ * T_LAT
N_TOK = N_CTX + N_LAT
FW = 512
FG = 128
N_FG = FW // FG
HEADS = 8
QL = 256
KVL = 128
NOPE = 64
ROPE = 32
VH = 64
N_EXP = 32
TOPK = 4
D_EXP = 1024
SWIGLU_LIMIT = 7.0
SWIGLU_ALPHA = 1.702
LN_EPS = 1e-5
RMS_EPS = 1e-6
DN_ALPHA = 2.0 ** 0.25
ATT_SCALE = float(NOPE + ROPE) ** -0.5
ROPE_THETA = 10000.0
GRID_W = 64

LANES = 128
TB = 256
PRE_ROWS = 512
MIX_ROWS_CTX = 512
MIX_ROWS_LAT = 512
SUB = 8
ROW_TILE = 256
N_SORT = N_TOK * TOPK
N_SORT_PAD = N_SORT + ROW_TILE
BLK_ROWS = TB * TOPK
RUN_BITS = TB.bit_length()
VMEM_LIMIT = 56 * 1024 * 1024
NEG_BIG = -1e30

F32 = jnp.float32
BF16 = jnp.bfloat16
HIGHEST = lax.Precision.HIGHEST


def _dot(a, b):
    return jnp.dot(a, b, preferred_element_type=F32)


def _dot_nt(a, b):
    return lax.dot_general(a, b, (((1,), (1,)), ((), ())), preferred_element_type=F32)


def _params(sem):
    return pltpu.CompilerParams(dimension_semantics=sem, vmem_limit_bytes=VMEM_LIMIT)


def _full(shape):
    n = len(shape)
    return pl.BlockSpec(shape, lambda *_: (0,) * n, pipeline_mode=pl.Buffered(1))


def _norm_noaffine(x):
    mu = jnp.mean(x, axis=-1, keepdims=True)
    xc = x - mu
    var = jnp.mean(xc * xc, axis=-1, keepdims=True)
    return xc * lax.rsqrt(var + LN_EPS)


def _rms(x, g):
    return x * lax.rsqrt(jnp.mean(x * x, axis=-1, keepdims=True) + RMS_EPS) * g


def _mod_row(mod_ref, row0, step, seq):
    row = row0 if step == 0 else row0 + step * seq
    m = mod_ref[pl.ds(row, 1), :]
    return [m[:, i * D:(i + 1) * D] for i in range(6)]


def _prep_kernel(wqn_ref, wk_ref, wv_ref, wo_ref, qabs_ref, oabs_ref):
    qabs = lax.dot_general(wqn_ref[0], wk_ref[0], (((1,), (1,)), ((), ())),
                           preferred_element_type=F32, precision=HIGHEST)
    qabs_ref[...] = qabs.astype(BF16)
    oabs = jnp.dot(wv_ref[0], wo_ref[0], preferred_element_type=F32, precision=HIGHEST)
    oabs_ref[...] = oabs.astype(BF16)


def _prep(wqn, wk, wv, wo):
    return pl.pallas_call(
        _prep_kernel,
        grid=(HEADS,),
        in_specs=[
            pl.BlockSpec((1, QL, NOPE), lambda h: (h, 0, 0)),
            pl.BlockSpec((1, KVL, NOPE), lambda h: (h, 0, 0)),
            pl.BlockSpec((1, KVL, VH), lambda h: (h, 0, 0)),
            pl.BlockSpec((1, VH, D), lambda h: (h, 0, 0)),
        ],
        out_specs=[
            pl.BlockSpec((QL, KVL), lambda h: (0, h)),
            pl.BlockSpec((KVL, D), lambda h: (h, 0)),
        ],
        out_shape=[jax.ShapeDtypeStruct((QL, HEADS * KVL), BF16),
                   jax.ShapeDtypeStruct((HEADS * KVL, D), BF16)],
        compiler_params=_params(("arbitrary",)),
        name="weight_prep",
    )(wqn, wk, wv, wo)


MOD_COLS = 1536


def _mod_kernel(c_ref, w_ref, b_ref, o_ref):
    c = c_ref[...]
    s = c * jax.nn.sigmoid(c)
    o_ref[...] = jnp.dot(s, w_ref[...], preferred_element_type=F32, precision=HIGHEST) + b_ref[...]


def _modulation(c_all, w_mod, b_mod):
    return pl.pallas_call(
        _mod_kernel,
        grid=(6 * D // MOD_COLS,),
        in_specs=[
            pl.BlockSpec((8, D), lambda i: (0, 0)),
            pl.BlockSpec((D, MOD_COLS), lambda i: (0, i)),
            pl.BlockSpec((1, MOD_COLS), lambda i: (0, i)),
        ],
        out_specs=pl.BlockSpec((8, MOD_COLS), lambda i: (0, i)),
        out_shape=jax.ShapeDtypeStruct((8, 6 * D), F32),
        compiler_params=_params(("arbitrary",)),
        name="modulation",
    )(c_all, w_mod, b_mod)


def _pre_kernel(*refs, rows, seq_len, seqs, rope, mod_row0, mod_row_step):
    if rope:
        (x_ref, mod_ref, win_ref, cdft_ref, qg_ref, kvg_ref, wqa_ref, wqr_ref, cos_ref, sin_ref,
         fcs_ref, qcat_ref, kcat_ref) = refs
    else:
        (x_ref, mod_ref, win_ref, cdft_ref, qg_ref, kvg_ref, wqa_ref, wqr_ref,
         fcs_ref, qcat_ref, kcat_ref, ckv_ref, krope_ref) = refs
    q_rows = rows // seqs
    b = pl.program_id(0) // (seq_len // q_rows)
    shift1, scale1 = _mod_row(mod_ref, mod_row0, mod_row_step, b)[:2]
    seq_rows = [slice(s * q_rows, (s + 1) * q_rows) for s in range(seqs)]

    x = x_ref[...]
    h1 = _norm_noaffine(x) * (1.0 + scale1) + shift1
    h1b = h1.astype(BF16)
    n_cols = 9 * LANES if rope else 8 * LANES
    proj = _dot(h1b, win_ref[:, :n_cols])

    f_b = proj[:, :FW].astype(BF16)
    for g in range(N_FG):
        r = _dot(f_b[:, g * FG:(g + 1) * FG], cdft_ref[...])
        for s, sl in enumerate(seq_rows):
            fcs_ref[s, 0, :, g * FG:(g + 1) * FG] = r[sl, :FG].astype(BF16)
            fcs_ref[s, 1, :, g * FG:(g + 1) * FG] = r[sl, FG:].astype(BF16)

    qn = _rms(proj[:, FW:FW + QL], qg_ref[...]).astype(BF16)
    ckv = _rms(proj[:, FW + QL:FW + QL + KVL], kvg_ref[...])
    kr = proj[:, 7 * LANES:8 * LANES]

    qa = _dot(qn, wqa_ref[...]) * ATT_SCALE
    if rope:
        cos = cos_ref[...]
        sin = sin_ref[...]
        qr2 = _dot(qn, wqr_ref[...])
        kr_keys = kr * cos + proj[:, 8 * LANES:9 * LANES] * sin
    else:
        qr2 = _dot(qn, wqr_ref[:, :HEADS * LANES])
        kr_keys = kr
    ckv_b = ckv.astype(BF16)
    kr_b = kr_keys.astype(BF16)
    for h in range(HEADS):
        qr_h = qr2[:, h * LANES:(h + 1) * LANES]
        if rope:
            qr_h = qr_h * cos + qr2[:, (HEADS + h) * LANES:(HEADS + h + 1) * LANES] * sin
        qa_h = qa[:, h * KVL:(h + 1) * KVL].astype(BF16)
        qr_h = (qr_h * ATT_SCALE).astype(BF16)
        for s, sl in enumerate(seq_rows):
            qcat_ref[s, h, :, :KVL] = qa_h[sl]
            qcat_ref[s, h, :, KVL:] = qr_h[sl]
    for s, sl in enumerate(seq_rows):
        kcat_ref[s, :, :KVL] = ckv_b[sl]
        kcat_ref[s, :, KVL:] = kr_b[sl]
        if not rope:
            ckv_ref[s, 0] = ckv[sl]
            krope_ref[s, 0] = kr[sl, :ROPE]


def _pre(x2d, mod, w_in_ext, cdft, qg, kvg, wqa, wqr, cos_pad, sin_pad, *, n_seq, seq_len, rope,
         mod_row0, mod_row_step):
    rows = PRE_ROWS
    seqs = max(1, rows // seq_len)
    q_rows = rows // seqs
    bps = seq_len // q_rows
    n_blk = n_seq * seq_len // rows
    seq_blk = lambda i: (i // bps, 0, i % bps, 0)
    in_specs = [
        pl.BlockSpec((rows, D), lambda i: (i, 0)),
        _full((8, 6 * D)),
        _full((D, 9 * LANES)),
        _full((FG, 2 * FG)),
        _full((1, QL)),
        _full((1, KVL)),
        _full((QL, HEADS * KVL)),
        _full((QL, 2 * HEADS * LANES)),
    ]
    args = [x2d, mod, w_in_ext, cdft, qg, kvg, wqa, wqr]
    out_specs = [
        pl.BlockSpec((seqs, 2, q_rows, FW), seq_blk),
        pl.BlockSpec((seqs, HEADS, q_rows, 2 * LANES), seq_blk),
        pl.BlockSpec((seqs, q_rows, 2 * LANES), lambda i: (i // bps, i % bps, 0)),
    ]
    out_shape = [
        jax.ShapeDtypeStruct((n_seq, 2, seq_len, FW), BF16),
        jax.ShapeDtypeStruct((n_seq, HEADS, seq_len, 2 * LANES), BF16),
        jax.ShapeDtypeStruct((n_seq, seq_len, 2 * LANES), BF16),
    ]
    if rope:
        in_specs += [pl.BlockSpec((q_rows, LANES), lambda i: (i % bps, 0)),
                     pl.BlockSpec((q_rows, LANES), lambda i: (i % bps, 0))]
        args += [cos_pad, sin_pad]
    else:
        out_specs += [pl.BlockSpec((seqs, 1, q_rows, KVL), seq_blk),
                      pl.BlockSpec((seqs, 1, q_rows, ROPE), seq_blk)]
        out_shape += [jax.ShapeDtypeStruct((n_seq, 1, seq_len, KVL), F32),
                      jax.ShapeDtypeStruct((n_seq, 1, seq_len, ROPE), F32)]
    kern = functools.partial(_pre_kernel, rows=rows, seq_len=seq_len, seqs=seqs, rope=rope,
                             mod_row0=mod_row0, mod_row_step=mod_row_step)
    return pl.pallas_call(
        kern, grid=(n_blk,), in_specs=in_specs, out_specs=out_specs, out_shape=out_shape,
        compiler_params=_params(("arbitrary",)),
        name="pre_lat" if rope else "pre_ctx",
    )(*args)


def _mix_kernel(*refs, rows, seq_len, seqs, cache, mod_row0, mod_row_step):
    if cache:
        (x_ref, mod_ref, fcs_ref, pdft_ref, qcat_ref, kcat_ref, cache_ref, wfo_ref, woa_ref,
         wg_ref, bg_ref, wout_ref, ln1g_ref, ln1b_ref, wr_ref, br_ref, tri_ref, upper_ref,
         x1_ref, h2_ref, lpos_ref, topw_ref, bcnt_ref) = refs
    else:
        (x_ref, mod_ref, fcs_ref, pdft_ref, qcat_ref, kcat_ref, wfo_ref, woa_ref,
         wg_ref, bg_ref, wout_ref, ln1g_ref, ln1b_ref, wr_ref, br_ref, tri_ref, upper_ref,
         x1_ref, h2_ref, lpos_ref, topw_ref, bcnt_ref) = refs
        cache_ref = None
    b = pl.program_id(0)
    shift1, scale1, gate1, shift2, scale2, _ = _mod_row(mod_ref, mod_row0, mod_row_step, b)

    x = x_ref[...]
    h1b = (_norm_noaffine(x) * (1.0 + scale1) + shift1).astype(BF16)

    mixed_rows, attn_rows = [], []
    for s in range(seqs):
        mixed_rows.append(_dot(pdft_ref[:, :seq_len], fcs_ref[s, 0])
                          + _dot(pdft_ref[:, seq_len:], fcs_ref[s, 1]))
        kc = kcat_ref[s]
        vals = kc[:, :KVL]
        if cache:
            kc2 = cache_ref[s]
            vals2 = kc2[:, :KVL]
        heads_out = []
        for h in range(HEADS):
            q = qcat_ref[s, h]
            s1 = _dot_nt(q, kc)
            m = jnp.max(s1, axis=-1, keepdims=True)
            if cache:
                s2 = _dot_nt(q, kc2)
                m = jnp.maximum(m, jnp.max(s2, axis=-1, keepdims=True))
            p1 = jnp.exp(s1 - m)
            l = jnp.sum(p1, axis=-1, keepdims=True)
            o = _dot(p1.astype(BF16), vals)
            if cache:
                p2 = jnp.exp(s2 - m)
                l = l + jnp.sum(p2, axis=-1, keepdims=True)
                o = o + _dot(p2.astype(BF16), vals2)
            heads_out.append((o / l).astype(BF16))
        attn_rows.append(jnp.concatenate(heads_out, axis=1))
    mixed = jnp.concatenate(mixed_rows, axis=0) if seqs > 1 else mixed_rows[0]
    attn = jnp.concatenate(attn_rows, axis=0) if seqs > 1 else attn_rows[0]
    f_out = _dot(mixed.astype(BF16), wfo_ref[...])
    m_out = _dot(attn, woa_ref[...])

    gates = jax.nn.sigmoid(_dot(h1b, wg_ref[...]) + bg_ref[...])
    merged = gates[:, :D] * f_out + gates[:, D:] * m_out
    mix = _dot(merged.astype(BF16), wout_ref[...])

    x1 = _norm_noaffine(DN_ALPHA * x + gate1 * mix) * ln1g_ref[...] + ln1b_ref[...]
    x1_ref[...] = x1
    h2 = _norm_noaffine(x1) * (1.0 + scale2) + shift2
    h2_ref[...] = h2.astype(BF16)

    h_hi = h2.astype(BF16)
    h_lo = (h2 - h_hi.astype(F32)).astype(BF16)
    hi_part = _dot(h_hi, wr_ref[...])
    logits_all = (hi_part[:, :LANES] + hi_part[:, LANES:] + _dot(h_lo, wr_ref[:, :LANES])) + br_ref[...]
    lane = lax.broadcasted_iota(jnp.int32, (TB, LANES), 1).astype(F32)
    for r in range(rows // TB):
        blk = slice(r * TB, (r + 1) * TB)
        work = logits_all[blk]
        top_v, top_i = [], []
        for _ in range(TOPK):
            mk = jnp.max(work, axis=-1, keepdims=True)
            ik = jnp.min(jnp.where(work == mk, lane, float(LANES)), axis=-1, keepdims=True)
            work = jnp.where(lane == ik, -jnp.inf, work)
            top_v.append(mk)
            top_i.append(ik)
        exps = [jnp.exp(v - top_v[0]) for v in top_v]
        denom = exps[0] + exps[1] + exps[2] + exps[3]

        onehot = jnp.zeros((TB, LANES), F32)
        for ik in top_i:
            onehot = onehot + jnp.where(lane == ik, 1.0, 0.0)
        counts = jnp.sum(onehot, axis=0, keepdims=True)
        lower = jnp.dot(jnp.broadcast_to(counts, (8, LANES)), upper_ref[...],
                        preferred_element_type=F32, precision=HIGHEST)[0:1, :]
        before = _dot(tri_ref[...], onehot.astype(BF16)) + lower
        lpos = jnp.zeros((TB, LANES), jnp.int32)
        topw = jnp.zeros((TB, LANES), F32)
        for k in range(TOPK):
            pos = jnp.sum(jnp.where(lane == top_i[k], before, 0.0), axis=-1, keepdims=True)
            lpos = jnp.where(lane == float(k), pos.astype(jnp.int32), lpos)
            topw = jnp.where(lane == float(k), exps[k] / denom, topw)
        lpos_ref[blk, :] = lpos
        topw_ref[blk, :] = topw
        bcnt_ref[r] = counts


def _mix(x2d, mod, fcs, pdft, qcat, kcat, cache_k, wfo, woa, wg, bg, wout, ln1g, ln1b, wr, br,
         tri, upper, *, rows, n_seq, seq_len, mod_row0, mod_row_step):
    cache = cache_k is not None
    seqs = max(1, rows // seq_len)
    q_rows = rows // seqs
    bps = seq_len // q_rows
    n_outer = n_seq // seqs
    in_specs = [
        pl.BlockSpec((rows, D), lambda b, j: (b * bps + j, 0)),
        _full((8, 6 * D)),
        pl.BlockSpec((seqs, 2, seq_len, FW), lambda b, j: (b, 0, 0, 0)),
        pl.BlockSpec((q_rows, 2 * seq_len), lambda b, j: (j, 0)),
        pl.BlockSpec((seqs, HEADS, q_rows, 2 * LANES), lambda b, j: (b, 0, j, 0)),
        pl.BlockSpec((seqs, seq_len, 2 * LANES), lambda b, j: (b, 0, 0)),
    ]
    args = [x2d, mod, fcs, pdft, qcat, kcat]
    if cache:
        in_specs.append(pl.BlockSpec((seqs, PAST, 2 * LANES), lambda b, j: (b, 0, 0)))
        args.append(cache_k)
    in_specs += [
        _full((FW, D)), _full((HEADS * KVL, D)), _full((D, 2 * D)), _full((1, 2 * D)),
        _full((D, D)), _full((1, D)), _full((1, D)), _full((D, 2 * LANES)), _full((1, LANES)),
        _full((TB, TB)), _full((LANES, LANES)),
    ]
    args += [wfo, woa, wg, bg, wout, ln1g, ln1b, wr, br, tri, upper]
    n_tok = n_seq * seq_len
    tok_spec = lambda w: pl.BlockSpec((rows, w), lambda b, j: (b * bps + j, 0))
    out_specs = [tok_spec(D), tok_spec(D), tok_spec(LANES), tok_spec(LANES),
                 pl.BlockSpec((rows // TB, 1, LANES), lambda b, j: (b * bps + j, 0, 0))]
    out_shape = [
        jax.ShapeDtypeStruct((n_tok, D), F32),
        jax.ShapeDtypeStruct((n_tok, D), BF16),
        jax.ShapeDtypeStruct((n_tok, LANES), jnp.int32),
        jax.ShapeDtypeStruct((n_tok, LANES), F32),
        jax.ShapeDtypeStruct((n_tok // TB, 1, LANES), F32),
    ]
    kern = functools.partial(_mix_kernel, rows=rows, seq_len=seq_len, seqs=seqs, cache=cache,
                             mod_row0=mod_row0, mod_row_step=mod_row_step)
    return pl.pallas_call(
        kern, grid=(n_outer, bps), in_specs=in_specs, out_specs=out_specs, out_shape=out_shape,
        compiler_params=_params(("arbitrary", "arbitrary")),
        name="mix_lat" if cache else "mix_ctx",
    )(*args)


N_CTX_BLK = N_CTX // TB
N_BLK = N_TOK // TB


def _row_tiles(ref, row0, n_rows):
    return ref.at[pl.ds(pl.multiple_of(row0 * SUB, SUB), n_rows * SUB), :]


def _for_each_run_copy(lstart_ref, blen_ref, gstart_ref, blk, local_ref, global_ref, sem, to_global, fn):
    def per_expert(e, carry):
        idx = blk * N_EXP + e
        n = blen_ref[idx]
        a = lstart_ref[idx]
        g = gstart_ref[idx]
        for bit in range(RUN_BITS - 1, -1, -1):
            size = 1 << bit

            @pl.when((n & size) != 0)
            def _():
                done = lax.shift_left(lax.shift_right_logical(n, bit + 1), bit + 1)
                loc = _row_tiles(local_ref, a + done, size)
                glo = _row_tiles(global_ref, g + done, size)
                fn(pltpu.make_async_copy(loc, glo, sem) if to_global
                   else pltpu.make_async_copy(glo, loc, sem))
        return carry

    lax.fori_loop(0, N_EXP, per_expert, 0)


def _load_rows(ref, n_rows):
    return jnp.concatenate([ref[pl.ds(s, n_rows, stride=SUB), :] for s in range(SUB)], axis=1)


def _store_rows(ref, val, n_rows):
    for s in range(SUB):
        ref[pl.ds(s, n_rows, stride=SUB), :] = val[:, s * LANES:(s + 1) * LANES]


def _dispatch_kernel(lstart_ref, blen_ref, gstart_ref, h2c_ref, h2l_ref, lpc_ref, lpl_ref, xs_ref,
                     buf, zbuf, sem, zsem):
    b = pl.program_id(0)
    slot = b & 1
    tables = (lstart_ref, blen_ref, gstart_ref)

    def start_runs(blk, s):
        _for_each_run_copy(*tables, blk, buf.at[s], xs_ref, sem.at[s], True, lambda cp: cp.start())

    def wait_runs(s):
        pltpu.make_async_copy(buf.at[s], _row_tiles(xs_ref, 0, BLK_ROWS), sem.at[s]).wait()

    @pl.when(b == 0)
    def _():
        zbuf[...] = jnp.zeros_like(zbuf)
        tail = pltpu.make_async_copy(zbuf, _row_tiles(xs_ref, N_SORT, ROW_TILE), zsem)
        tail.start()
        tail.wait()

    @pl.when(b >= 2)
    def _():
        wait_runs(slot)

    def sort_block(h2_ref, lp_ref):
        col = lax.broadcasted_iota(jnp.int32, (TB, BLK_ROWS), 1)
        lp = lp_ref[...]
        pick = jnp.zeros((TB, BLK_ROWS), F32)
        for k in range(TOPK):
            pick = pick + jnp.where(col == lp[:, k:k + 1], 1.0, 0.0)
        sorted_rows = lax.dot_general(pick.astype(BF16), h2_ref[...],
                                      (((0,), (0,)), ((), ())), preferred_element_type=F32)
        _store_rows(buf.at[slot], sorted_rows, BLK_ROWS)

    @pl.when(b < N_CTX_BLK)
    def _():
        sort_block(h2c_ref, lpc_ref)

    @pl.when(b >= N_CTX_BLK)
    def _():
        sort_block(h2l_ref, lpl_ref)

    start_runs(b, slot)

    @pl.when(b == N_BLK - 1)
    def _():
        wait_runs(1 - slot)
        wait_runs(slot)


def _dispatch(lstart, blen, gstart, h2c, h2l, lpc, lpl):
    ctx_idx = lambda i, *_: (jnp.minimum(i, N_CTX_BLK - 1), 0)
    lat_idx = lambda i, *_: (jnp.maximum(i - N_CTX_BLK, 0), 0)
    grid_spec = pltpu.PrefetchScalarGridSpec(
        num_scalar_prefetch=3,
        grid=(N_BLK,),
        in_specs=[
            pl.BlockSpec((TB, D), ctx_idx),
            pl.BlockSpec((TB, D), lat_idx),
            pl.BlockSpec((TB, LANES), ctx_idx),
            pl.BlockSpec((TB, LANES), lat_idx),
        ],
        out_specs=pl.BlockSpec(memory_space=pl.ANY),
        scratch_shapes=[pltpu.VMEM((2, BLK_ROWS * SUB, LANES), F32),
                        pltpu.VMEM((ROW_TILE * SUB, LANES), F32),
                        pltpu.SemaphoreType.DMA((2,)),
                        pltpu.SemaphoreType.DMA(())],
    )
    return pl.pallas_call(
        _dispatch_kernel, grid_spec=grid_spec,
        out_shape=jax.ShapeDtypeStruct((N_SORT_PAD * SUB, LANES), F32),
        compiler_params=_params(("arbitrary",)),
        name="dispatch",
    )(lstart, blen, gstart, h2c, h2l, lpc, lpl)


def _expert_kernel(start_ref, count_ref, xs_ref, wg_ref, bg_ref, wu_ref, bu_ref, wd_ref, bd_ref, ys_ref,
                   wbf, xbuf, ybuf, sem_in, sem_out):
    e = pl.program_id(0)
    row0 = start_ref[e]

    def tiles_of(ex):
        return lax.shift_right_logical(count_ref[ex] + (ROW_TILE - 1), ROW_TILE.bit_length() - 1)

    n_tiles = tiles_of(e)

    def in_copy_of(ex, j, slot):
        return pltpu.make_async_copy(_row_tiles(xs_ref, start_ref[ex] + j * ROW_TILE, ROW_TILE),
                                     xbuf.at[slot], sem_in.at[slot])

    def in_copy(j, slot):
        return in_copy_of(e, j, slot)

    def out_copy(j, slot):
        return pltpu.make_async_copy(ybuf.at[slot], _row_tiles(ys_ref, row0 + j * ROW_TILE, ROW_TILE),
                                     sem_out.at[slot])

    def start_first_tiles(ex):
        for j in range(2):
            @pl.when(tiles_of(ex) > j)
            def _():
                in_copy_of(ex, j, j).start()

    @pl.when(e == 0)
    def _():
        ybuf[0] = jnp.zeros((ROW_TILE * SUB, LANES), F32)
        tail = pltpu.make_async_copy(ybuf.at[0], _row_tiles(ys_ref, N_SORT, ROW_TILE), sem_out.at[0])
        tail.start()
        tail.wait()
        start_first_tiles(0)

    @pl.when(n_tiles > 0)
    def _():
        wbf[0] = wg_ref[0].astype(BF16)
        wbf[1] = wu_ref[0].astype(BF16)
        wbf[2] = wd_ref[0].astype(BF16)

        def tile(j, carry):
            slot = j & 1

            @pl.when(jnp.logical_and(j >= 1, j + 1 < n_tiles))
            def _():
                in_copy(j + 1, 1 - slot).start()

            in_copy(j, slot).wait()

            @pl.when(j >= 2)
            def _():
                out_copy(j - 2, slot).wait()

            xb = _load_rows(xbuf.at[slot], ROW_TILE).astype(BF16)
            gt = jnp.minimum(_dot(xb, wbf[0]) + bg_ref[0], SWIGLU_LIMIT)
            up = jnp.clip(_dot(xb, wbf[1]) + bu_ref[0], -SWIGLU_LIMIT, SWIGLU_LIMIT)
            act = gt * jax.nn.sigmoid(SWIGLU_ALPHA * gt) * (up + 1.0)
            _store_rows(ybuf.at[slot], _dot(act.astype(BF16), wbf[2]) + bd_ref[0], ROW_TILE)
            out_copy(j, slot).start()
            return carry

        lax.fori_loop(0, n_tiles, tile, 0)

        @pl.when(n_tiles >= 2)
        def _():
            out_copy(n_tiles - 2, n_tiles & 1).wait()

        out_copy(n_tiles - 1, (n_tiles - 1) & 1).wait()

    @pl.when(e + 1 < N_EXP)
    def _():
        start_first_tiles(e + 1)


def _experts(start, count, xs, wg, bg, wu, bu, wd, bd):
    w_idx = lambda e, *_: (e, 0, 0)
    grid_spec = pltpu.PrefetchScalarGridSpec(
        num_scalar_prefetch=2,
        grid=(N_EXP,),
        in_specs=[
            pl.BlockSpec(memory_space=pl.ANY),
            pl.BlockSpec((1, D, D_EXP), w_idx),
            pl.BlockSpec((1, 1, D_EXP), w_idx),
            pl.BlockSpec((1, D, D_EXP), w_idx),
            pl.BlockSpec((1, 1, D_EXP), w_idx),
            pl.BlockSpec((1, D_EXP, D), w_idx),
            pl.BlockSpec((1, 1, D), w_idx),
        ],
        out_specs=pl.BlockSpec(memory_space=pl.ANY),
        scratch_shapes=[pltpu.VMEM((3, D, D_EXP), BF16),
                        pltpu.VMEM((2, ROW_TILE * SUB, LANES), F32),
                        pltpu.VMEM((2, ROW_TILE * SUB, LANES), F32),
                        pltpu.SemaphoreType.DMA((2,)),
                        pltpu.SemaphoreType.DMA((2,))],
    )
    return pl.pallas_call(
        _expert_kernel, grid_spec=grid_spec,
        out_shape=jax.ShapeDtypeStruct((N_SORT_PAD * SUB, LANES), F32),
        compiler_params=_params(("arbitrary",)),
        name="experts",
    )(start, count, xs, wg, bg, wu, bu, wd, bd)


def _combine_kernel(lstart_ref, blen_ref, gstart_ref, lp_ref, topw_ref, x1_ref, mod_ref, g_ref, b_ref,
                    ys_ref, o_ref, buf, sem, *, blk0, n_blk, blocks_per_seq, mod_row0, mod_row_step):
    i = pl.program_id(0)
    slot = i & 1
    gate2 = _mod_row(mod_ref, mod_row0, mod_row_step, i // blocks_per_seq)[5]
    tables = (lstart_ref, blen_ref, gstart_ref)

    def start_runs(step, s):
        _for_each_run_copy(*tables, blk0 + step, buf.at[s], ys_ref, sem.at[s], False,
                           lambda cp: cp.start())

    @pl.when(i == 0)
    def _():
        start_runs(0, 0)

    @pl.when(i + 1 < n_blk)
    def _():
        start_runs(i + 1, 1 - slot)

    pltpu.make_async_copy(_row_tiles(ys_ref, 0, BLK_ROWS), buf.at[slot], sem.at[slot]).wait()

    y_sorted = _load_rows(buf.at[slot], BLK_ROWS).astype(BF16)
    col = lax.broadcasted_iota(jnp.int32, (TB, BLK_ROWS), 1)
    lp = lp_ref[...]
    topw = topw_ref[...]
    weights = jnp.zeros((TB, BLK_ROWS), F32)
    for k in range(TOPK):
        weights = weights + jnp.where(col == lp[:, k:k + 1], topw[:, k:k + 1], 0.0)
    ffn = _dot(weights.astype(BF16), y_sorted)
    y = DN_ALPHA * x1_ref[...] + gate2 * ffn
    o_ref[...] = _norm_noaffine(y) * g_ref[...] + b_ref[...]


def _combine(lstart, blen, gstart, lpos, topw, x1, mod, ln2g, ln2b, ys, *, blk0, n_seq, seq_len,
             mod_row0, mod_row_step):
    bps = seq_len // TB
    n_blk = n_seq * bps
    tok = lambda i, *_: (i, 0)
    const = lambda i, *_: (0, 0)
    grid_spec = pltpu.PrefetchScalarGridSpec(
        num_scalar_prefetch=3,
        grid=(n_blk,),
        in_specs=[
            pl.BlockSpec((TB, LANES), tok),
            pl.BlockSpec((TB, LANES), tok),
            pl.BlockSpec((TB, D), tok),
            pl.BlockSpec((8, 6 * D), const),
            pl.BlockSpec((1, D), const),
            pl.BlockSpec((1, D), const),
            pl.BlockSpec(memory_space=pl.ANY),
        ],
        out_specs=pl.BlockSpec((TB, D), tok),
        scratch_shapes=[pltpu.VMEM((2, BLK_ROWS * SUB, LANES), F32), pltpu.SemaphoreType.DMA((2,))],
    )
    kern = functools.partial(_combine_kernel, blk0=blk0, n_blk=n_blk, blocks_per_seq=bps,
                             mod_row0=mod_row0, mod_row_step=mod_row_step)
    return pl.pallas_call(
        kern, grid_spec=grid_spec,
        out_shape=jax.ShapeDtypeStruct((n_seq * seq_len, D), F32),
        compiler_params=_params(("arbitrary",)),
        name="combine_lat" if blk0 else "combine_ctx",
    )(lstart, blen, gstart, lpos, topw, x1, mod, ln2g, ln2b, ys)


def _dft_tables():
    def cs(n):
        k = np.arange(n, dtype=np.int64)
        ang = 2.0 * np.pi * ((k[:, None] * k[None, :]) % n).astype(np.float64) / n
        return np.cos(ang) / math.sqrt(n), np.sin(ang) / math.sqrt(n)

    c, s = cs(FG)
    cdft = np.concatenate([c, s], axis=1).astype(np.float32)
    pd = {}
    for t in (T_CTX, T_LAT):
        c, s = cs(t)
        pd[t] = np.concatenate([c, -s], axis=1).astype(np.float32)
    tri = np.tril(np.ones((TB, TB), np.float32), k=-1)
    upper = np.triu(np.ones((LANES, LANES), np.float32), k=1)
    return cdft, pd, tri, upper


_ROT_PERM = np.array(list(range(8, 16)) + list(range(0, 8)) + list(range(24, 32)) + list(range(16, 24)))
_ROT_SIGN = np.array([-1.0] * 8 + [1.0] * 8 + [-1.0] * 8 + [1.0] * 8, np.float32)


def _rope_tables():
    rows = T_LAT // GRID_W
    row = jnp.repeat(jnp.arange(rows, dtype=F32), GRID_W)
    col = jnp.tile(jnp.arange(GRID_W, dtype=F32), rows)
    axis_dim = ROPE // 2
    inv_freq = ROPE_THETA ** (-jnp.arange(0, axis_dim, 2, dtype=F32) / axis_dim)
    ang_r = row[:, None] * inv_freq[None, :]
    ang_c = col[:, None] * inv_freq[None, :]
    ang = jnp.concatenate([ang_r, ang_r, ang_c, ang_c], axis=-1)
    pad = ((0, 0), (0, LANES - ROPE))
    return jnp.pad(jnp.cos(ang), pad), jnp.pad(jnp.sin(ang), pad)


def kernel(x_prompt, x_sample, cache_ckv, cache_krope, c, c_ctx, w_mod, b_mod, w_in, q_norm_g, w_q_up,
           kv_norm_g, w_kv_up, w_fourier_o, w_mla_o, w_gate, b_gate, w_out, ln1_g, ln1_b, w_router,
           b_router, w_gate_e, b_gate_e, w_up_e, b_up_e, w_down_e, b_down_e, ln2_g, ln2_b):
    cdft_np, pdft_np, tri_np, upper_np = _dft_tables()
    upper = jnp.asarray(upper_np)
    cdft = jnp.asarray(cdft_np).astype(BF16)
    pdft_ctx = jnp.asarray(pdft_np[T_CTX]).astype(BF16)
    pdft_lat = jnp.asarray(pdft_np[T_LAT]).astype(BF16)
    tri = jnp.asarray(tri_np).astype(BF16)
    cos_pad, sin_pad = _rope_tables()

    w_in0 = w_in[0]
    kr_w = w_in0[:, FW + QL + KVL:]
    lane_pad = ((0, 0), (0, LANES - ROPE))
    w_in_ext = jnp.concatenate(
        [w_in0[:, :FW + QL + KVL], jnp.pad(kr_w, lane_pad),
         jnp.pad(kr_w[:, _ROT_PERM] * _ROT_SIGN, lane_pad)], axis=1).astype(BF16)
    wq3 = w_q_up[0].reshape(QL, HEADS, NOPE + ROPE)
    wq_nope = jnp.transpose(wq3[:, :, :NOPE], (1, 0, 2))
    wq_rope = wq3[:, :, NOPE:]
    head_pad = ((0, 0), (0, 0), (0, LANES - ROPE))
    wqr = jnp.concatenate(
        [jnp.pad(wq_rope, head_pad).reshape(QL, HEADS * LANES),
         jnp.pad(wq_rope[:, :, _ROT_PERM] * _ROT_SIGN, head_pad).reshape(QL, HEADS * LANES)],
        axis=1).astype(BF16)
    wkv3 = w_kv_up[0].reshape(KVL, HEADS, NOPE + VH)
    wk = jnp.transpose(wkv3[:, :, :NOPE], (1, 0, 2))
    wv = jnp.transpose(wkv3[:, :, NOPE:], (1, 0, 2))
    wo3 = w_mla_o[0].reshape(HEADS, VH, D)
    wqa, woa = _prep(wq_nope, wk, wv, wo3)

    c_all = jnp.concatenate([c_ctx[None, :], c, jnp.zeros((8 - 1 - N_LAT_SEQ, D), F32)], axis=0)
    mod = _modulation(c_all, w_mod[0], b_mod)

    qg = q_norm_g
    kvg = kv_norm_g
    wfo = w_fourier_o[0].astype(BF16)
    wg = w_gate[0].astype(BF16)
    wout = w_out[0].astype(BF16)
    wr_f32 = jnp.pad(w_router[0], ((0, 0), (0, LANES - N_EXP)))
    wr_hi = wr_f32.astype(BF16)
    wr = jnp.concatenate([wr_hi, (wr_f32 - wr_hi.astype(F32)).astype(BF16)], axis=1)
    br = jnp.pad(b_router, ((0, 0), (0, LANES - N_EXP)), constant_values=NEG_BIG)
    cache_k = jnp.concatenate(
        [cache_ckv[:, 0], jnp.pad(cache_krope[:, 0], ((0, 0), (0, 0), (0, LANES - ROPE)))],
        axis=-1).astype(BF16)

    xc2d = x_prompt.reshape(N_CTX, D)
    xl2d = x_sample.reshape(N_LAT, D)

    fcs_c, qcat_c, kcat_c, new_ckv, new_krope = _pre(
        xc2d, mod, w_in_ext, cdft, qg, kvg, wqa, wqr, None, None,
        n_seq=N_CTX_SEQ, seq_len=T_CTX, rope=False, mod_row0=0, mod_row_step=0)
    x1_c, h2_c, lpos_c, topw_c, bcnt_c = _mix(
        xc2d, mod, fcs_c, pdft_ctx, qcat_c, kcat_c, None, wfo, woa, wg, b_gate, wout, ln1_g, ln1_b,
        wr, br, tri, upper, rows=MIX_ROWS_CTX, n_seq=N_CTX_SEQ, seq_len=T_CTX, mod_row0=0,
        mod_row_step=0)

    fcs_l, qcat_l, kcat_l = _pre(
        xl2d, mod, w_in_ext, cdft, qg, kvg, wqa, wqr, cos_pad, sin_pad,
        n_seq=N_LAT_SEQ, seq_len=T_LAT, rope=True, mod_row0=1, mod_row_step=1)
    x1_l, h2_l, lpos_l, topw_l, bcnt_l = _mix(
        xl2d, mod, fcs_l, pdft_lat, qcat_l, kcat_l, cache_k, wfo, woa, wg, b_gate, wout, ln1_g,
        ln1_b, wr, br, tri, upper, rows=MIX_ROWS_LAT, n_seq=N_LAT_SEQ, seq_len=T_LAT, mod_row0=1,
        mod_row_step=1)

    blen = jnp.concatenate([bcnt_c[:, 0, :N_EXP], bcnt_l[:, 0, :N_EXP]], axis=0).astype(jnp.int32)
    lstart = jnp.cumsum(blen, axis=1) - blen
    count = jnp.sum(blen, axis=0)
    start = jnp.cumsum(count) - count
    gstart = start[None, :] + jnp.cumsum(blen, axis=0) - blen
    lstart, blen, gstart = (t.reshape(-1).astype(jnp.int32) for t in (lstart, blen, gstart))
    start = start.astype(jnp.int32)
    count = count.astype(jnp.int32)

    xs = _dispatch(lstart, blen, gstart, h2_c, h2_l, lpos_c, lpos_l)
    ys = _experts(start, count, xs,
                  w_gate_e[0], b_gate_e[0][:, None, :], w_up_e[0], b_up_e[0][:, None, :],
                  w_down_e[0], b_down_e[0][:, None, :])

    y_c = _combine(lstart, blen, gstart, lpos_c, topw_c, x1_c, mod, ln2_g, ln2_b, ys, blk0=0,
                   n_seq=N_CTX_SEQ, seq_len=T_CTX, mod_row0=0, mod_row_step=0)
    y_l = _combine(lstart, blen, gstart, lpos_l, topw_l, x1_l, mod, ln2_g, ln2_b, ys, blk0=N_CTX_BLK,
                   n_seq=N_LAT_SEQ, seq_len=T_LAT, mod_row0=1, mod_row_step=1)
    return (y_c.reshape(N_CTX_SEQ, T_CTX, D), y_l.reshape(N_LAT_SEQ, T_LAT, D), new_ckv, new_krope)
```

```python
import functools
import math

import numpy as np
import jax
import jax.numpy as jnp
from jax import lax
from jax.experimental import pallas as pl
from jax.experimental.pallas import tpu as pltpu

D = 1024
N_CTX_SEQ, T_CTX = 32, 256
N_LAT_SEQ, T_LAT = 4, 1024
PAST = 256
N_CTX = N_CTX_SEQ * T_CTX
N_LAT = N_LAT_SEQ * T_LAT
N_TOK = N_CTX + N_LAT
FW = 512
FG = 128
N_FG = FW // FG
HEADS = 8
QL = 256
KVL = 128
NOPE = 64
ROPE = 32
VH = 64
N_EXP = 32
TOPK = 4
D_EXP = 1024
SWIGLU_LIMIT = 7.0
SWIGLU_ALPHA = 1.702
LN_EPS = 1e-5
RMS_EPS = 1e-6
DN_ALPHA = 2.0 ** 0.25
ATT_SCALE = float(NOPE + ROPE) ** -0.5
ROPE_THETA = 10000.0
GRID_W = 64

LANES = 128
TB = 256
PRE_ROWS = 512
MIX_ROWS_CTX = 512
MIX_ROWS_LAT = 512
SUB = 8
ROW_TILE = 256
RING = 4
N_SORT = N_TOK * TOPK
N_SORT_PAD = N_SORT + ROW_TILE
BLK_ROWS = TB * TOPK
RUN_BITS = TB.bit_length()
RARE_RUN = 64
VMEM_LIMIT = 56 * 1024 * 1024
NEG_BIG = -1e30

F32 = jnp.float32
BF16 = jnp.bfloat16
HIGHEST = lax.Precision.HIGHEST


def _dot(a, b):
    return jnp.dot(a, b, preferred_element_type=F32)


def _dot_nt(a, b):
    return lax.dot_general(a, b, (((1,), (1,)), ((), ())), preferred_element_type=F32)


def _params(sem):
    return pltpu.CompilerParams(dimension_semantics=sem, vmem_limit_bytes=VMEM_LIMIT)


def _full(shape):
    n = len(shape)
    return pl.BlockSpec(shape, lambda *_: (0,) * n, pipeline_mode=pl.Buffered(1))


def _norm_noaffine(x):
    mu = jnp.mean(x, axis=-1, keepdims=True)
    xc = x - mu
    var = jnp.mean(xc * xc, axis=-1, keepdims=True)
    return xc * lax.rsqrt(var + LN_EPS)


def _rms(x, g):
    return x * lax.rsqrt(jnp.mean(x * x, axis=-1, keepdims=True) + RMS_EPS) * g


def _mod_row(mod_ref, row0, step, seq):
    row = row0 if step == 0 else row0 + step * seq
    m = mod_ref[pl.ds(row, 1), :]
    return [m[:, i * D:(i + 1) * D] for i in range(6)]


def _prep_kernel(wqn_ref, wk_ref, wv_ref, wo_ref, qabs_ref, oabs_ref):
    qabs = lax.dot_general(wqn_ref[0], wk_ref[0], (((1,), (1,)), ((), ())),
                           preferred_element_type=F32, precision=HIGHEST)
    qabs_ref[...] = qabs.astype(BF16)
    oabs = jnp.dot(wv_ref[0], wo_ref[0], preferred_element_type=F32, precision=HIGHEST)
    oabs_ref[...] = oabs.astype(BF16)


def _prep(wqn, wk, wv, wo):
    return pl.pallas_call(
        _prep_kernel,
        grid=(HEADS,),
        in_specs=[
            pl.BlockSpec((1, QL, NOPE), lambda h: (h, 0, 0)),
            pl.BlockSpec((1, KVL, NOPE), lambda h: (h, 0, 0)),
            pl.BlockSpec((1, KVL, VH), lambda h: (h, 0, 0)),
            pl.BlockSpec((1, VH, D), lambda h: (h, 0, 0)),
        ],
        out_specs=[
            pl.BlockSpec((QL, KVL), lambda h: (0, h)),
            pl.BlockSpec((KVL, D), lambda h: (h, 0)),
        ],
        out_shape=[jax.ShapeDtypeStruct((QL, HEADS * KVL), BF16),
                   jax.ShapeDtypeStruct((HEADS * KVL, D), BF16)],
        compiler_params=_params(("arbitrary",)),
        name="weight_prep",
    )(wqn, wk, wv, wo)


MOD_COLS = 1536


def _mod_kernel(c_ref, w_ref, b_ref, o_ref):
    c = c_ref[...]
    s = c * jax.nn.sigmoid(c)
    o_ref[...] = jnp.dot(s, w_ref[...], preferred_element_type=F32, precision=HIGHEST) + b_ref[...]


def _modulation(c_all, w_mod, b_mod):
    return pl.pallas_call(
        _mod_kernel,
        grid=(6 * D // MOD_COLS,),
        in_specs=[
            pl.BlockSpec((8, D), lambda i: (0, 0)),
            pl.BlockSpec((D, MOD_COLS), lambda i: (0, i)),
            pl.BlockSpec((1, MOD_COLS), lambda i: (0, i)),
        ],
        out_specs=pl.BlockSpec((8, MOD_COLS), lambda i: (0, i)),
        out_shape=jax.ShapeDtypeStruct((8, 6 * D), F32),
        compiler_params=_params(("arbitrary",)),
        name="modulation",
    )(c_all, w_mod, b_mod)


def _pre_kernel(*refs, rows, seq_len, seqs, rope, mod_row0, mod_row_step):
    if rope:
        (x_ref, mod_ref, win_ref, cdft_ref, qg_ref, kvg_ref, wqa_ref, wqr_ref, cos_ref, sin_ref,
         fcs_ref, qcat_ref, kcat_ref) = refs
    else:
        (x_ref, mod_ref, win_ref, cdft_ref, qg_ref, kvg_ref, wqa_ref, wqr_ref,
         fcs_ref, qcat_ref, kcat_ref, ckv_ref, krope_ref) = refs
    q_rows = rows // seqs
    b = pl.program_id(0) // (seq_len // q_rows)
    shift1, scale1 = _mod_row(mod_ref, mod_row0, mod_row_step, b)[:2]
    seq_rows = [slice(s * q_rows, (s + 1) * q_rows) for s in range(seqs)]

    x = x_ref[...]
    h1 = _norm_noaffine(x) * (1.0 + scale1) + shift1
    h1b = h1.astype(BF16)
    n_cols = 9 * LANES if rope else 8 * LANES
    proj = _dot(h1b, win_ref[:, :n_cols])

    f_b = proj[:, :FW].astype(BF16)
    for g in range(N_FG):
        r = _dot(f_b[:, g * FG:(g + 1) * FG], cdft_ref[...])
        for s, sl in enumerate(seq_rows):
            fcs_ref[s, 0, :, g * FG:(g + 1) * FG] = r[sl, :FG].astype(BF16)
            fcs_ref[s, 1, :, g * FG:(g + 1) * FG] = r[sl, FG:].astype(BF16)

    qn = _rms(proj[:, FW:FW + QL], qg_ref[...]).astype(BF16)
    ckv = _rms(proj[:, FW + QL:FW + QL + KVL], kvg_ref[...])
    kr = proj[:, 7 * LANES:8 * LANES]

    qa = _dot(qn, wqa_ref[...]) * ATT_SCALE
    if rope:
        cos = cos_ref[...]
        sin = sin_ref[...]
        qr2 = _dot(qn, wqr_ref[...])
        kr_keys = kr * cos + proj[:, 8 * LANES:9 * LANES] * sin
    else:
        qr2 = _dot(qn, wqr_ref[:, :HEADS * LANES])
        kr_keys = kr
    ckv_b = ckv.astype(BF16)
    kr_b = kr_keys.astype(BF16)
    for h in range(HEADS):
        qr_h = qr2[:, h * LANES:(h + 1) * LANES]
        if rope:
            qr_h = qr_h * cos + qr2[:, (HEADS + h) * LANES:(HEADS + h + 1) * LANES] * sin
        qa_h = qa[:, h * KVL:(h + 1) * KVL].astype(BF16)
        qr_h = (qr_h * ATT_SCALE).astype(BF16)
        for s, sl in enumerate(seq_rows):
            qcat_ref[s, h, :, :KVL] = qa_h[sl]
            qcat_ref[s, h, :, KVL:] = qr_h[sl]
    for s, sl in enumerate(seq_rows):
        kcat_ref[s, :, :KVL] = ckv_b[sl]
        kcat_ref[s, :, KVL:] = kr_b[sl]
        if not rope:
            ckv_ref[s, 0] = ckv[sl]
            krope_ref[s, 0] = kr[sl, :ROPE]


def _pre(x2d, mod, w_in_ext, cdft, qg, kvg, wqa, wqr, cos_pad, sin_pad, *, n_seq, seq_len, rope,
         mod_row0, mod_row_step):
    rows = PRE_ROWS
    seqs = max(1, rows // seq_len)
    q_rows = rows // seqs
    bps = seq_len // q_rows
    n_blk = n_seq * seq_len // rows
    seq_blk = lambda i: (i // bps, 0, i % bps, 0)
    in_specs = [
        pl.BlockSpec((rows, D), lambda i: (i, 0)),
        _full((8, 6 * D)),
        _full((D, 9 * LANES)),
        _full((FG, 2 * FG)),
        _full((1, QL)),
        _full((1, KVL)),
        _full((QL, HEADS * KVL)),
        _full((QL, 2 * HEADS * LANES)),
    ]
    args = [x2d, mod, w_in_ext, cdft, qg, kvg, wqa, wqr]
    out_specs = [
        pl.BlockSpec((seqs, 2, q_rows, FW), seq_blk),
        pl.BlockSpec((seqs, HEADS, q_rows, 2 * LANES), seq_blk),
        pl.BlockSpec((seqs, q_rows, 2 * LANES), lambda i: (i // bps, i % bps, 0)),
    ]
    out_shape = [
        jax.ShapeDtypeStruct((n_seq, 2, seq_len, FW), BF16),
        jax.ShapeDtypeStruct((n_seq, HEADS, seq_len, 2 * LANES), BF16),
        jax.ShapeDtypeStruct((n_seq, seq_len, 2 * LANES), BF16),
    ]
    if rope:
        in_specs += [pl.BlockSpec((q_rows, LANES), lambda i: (i % bps, 0)),
                     pl.BlockSpec((q_rows, LANES), lambda i: (i % bps, 0))]
        args += [cos_pad, sin_pad]
    else:
        out_specs += [pl.BlockSpec((seqs, 1, q_rows, KVL), seq_blk),
                      pl.BlockSpec((seqs, 1, q_rows, ROPE), seq_blk)]
        out_shape += [jax.ShapeDtypeStruct((n_seq, 1, seq_len, KVL), F32),
                      jax.ShapeDtypeStruct((n_seq, 1, seq_len, ROPE), F32)]
    kern = functools.partial(_pre_kernel, rows=rows, seq_len=seq_len, seqs=seqs, rope=rope,
                             mod_row0=mod_row0, mod_row_step=mod_row_step)
    return pl.pallas_call(
        kern, grid=(n_blk,), in_specs=in_specs, out_specs=out_specs, out_shape=out_shape,
        compiler_params=_params(("arbitrary",)),
        name="pre_lat" if rope else "pre_ctx",
    )(*args)


def _mix_kernel(*refs, rows, seq_len, seqs, cache, mod_row0, mod_row_step):
    if cache:
        (x_ref, mod_ref, fcs_ref, pdft_ref, qcat_ref, kcat_ref, cache_ref, wfo_ref, woa_ref,
         wg_ref, bg_ref, wout_ref, ln1g_ref, ln1b_ref, wr_ref, br_ref, tri_ref, upper_ref,
         x1_ref, h2_ref, lpos_ref, topw_ref, bcnt_ref) = refs
    else:
        (x_ref, mod_ref, fcs_ref, pdft_ref, qcat_ref, kcat_ref, wfo_ref, woa_ref,
         wg_ref, bg_ref, wout_ref, ln1g_ref, ln1b_ref, wr_ref, br_ref, tri_ref, upper_ref,
         x1_ref, h2_ref, lpos_ref, topw_ref, bcnt_ref) = refs
        cache_ref = None
    b = pl.program_id(0)
    shift1, scale1, gate1, shift2, scale2, _ = _mod_row(mod_ref, mod_row0, mod_row_step, b)

    x = x_ref[...]
    h1b = (_norm_noaffine(x) * (1.0 + scale1) + shift1).astype(BF16)

    mixed_rows, attn_rows = [], []
    for s in range(seqs):
        mixed_rows.append(_dot(pdft_ref[:, :seq_len], fcs_ref[s, 0])
                          + _dot(pdft_ref[:, seq_len:], fcs_ref[s, 1]))
        kc = kcat_ref[s]
        vals = kc[:, :KVL]
        if cache:
            kc2 = cache_ref[s]
            vals2 = kc2[:, :KVL]
        heads_out = []
        for h in range(HEADS):
            q = qcat_ref[s, h]
            s1 = _dot_nt(q, kc)
            m = jnp.max(s1, axis=-1, keepdims=True)
            if cache:
                s2 = _dot_nt(q, kc2)
                m = jnp.maximum(m, jnp.max(s2, axis=-1, keepdims=True))
            p1 = jnp.exp(s1 - m)
            l = jnp.sum(p1, axis=-1, keepdims=True)
            o = _dot(p1.astype(BF16), vals)
            if cache:
                p2 = jnp.exp(s2 - m)
                l = l + jnp.sum(p2, axis=-1, keepdims=True)
                o = o + _dot(p2.astype(BF16), vals2)
            heads_out.append((o / l).astype(BF16))
        attn_rows.append(jnp.concatenate(heads_out, axis=1))
    mixed = jnp.concatenate(mixed_rows, axis=0) if seqs > 1 else mixed_rows[0]
    attn = jnp.concatenate(attn_rows, axis=0) if seqs > 1 else attn_rows[0]
    f_out = _dot(mixed.astype(BF16), wfo_ref[...])
    m_out = _dot(attn, woa_ref[...])

    gates = jax.nn.sigmoid(_dot(h1b, wg_ref[...]) + bg_ref[...])
    merged = gates[:, :D] * f_out + gates[:, D:] * m_out
    mix = _dot(merged.astype(BF16), wout_ref[...])

    x1 = _norm_noaffine(DN_ALPHA * x + gate1 * mix) * ln1g_ref[...] + ln1b_ref[...]
    x1_ref[...] = x1
    h2 = _norm_noaffine(x1) * (1.0 + scale2) + shift2
    h2_ref[...] = h2.astype(BF16)

    h_hi = h2.astype(BF16)
    h_lo = (h2 - h_hi.astype(F32)).astype(BF16)
    hi_part = _dot(h_hi, wr_ref[...])
    logits_all = (hi_part[:, :LANES] + hi_part[:, LANES:] + _dot(h_lo, wr_ref[:, :LANES])) + br_ref[...]
    lane = lax.broadcasted_iota(jnp.int32, (TB, LANES), 1).astype(F32)
    for r in range(rows // TB):
        blk = slice(r * TB, (r + 1) * TB)
        work = logits_all[blk]
        top_v, top_i = [], []
        for _ in range(TOPK):
            mk = jnp.max(work, axis=-1, keepdims=True)
            ik = jnp.min(jnp.where(work == mk, lane, float(LANES)), axis=-1, keepdims=True)
            work = jnp.where(lane == ik, -jnp.inf, work)
            top_v.append(mk)
            top_i.append(ik)
        exps = [jnp.exp(v - top_v[0]) for v in top_v]
        denom = exps[0] + exps[1] + exps[2] + exps[3]

        onehot = jnp.zeros((TB, LANES), F32)
        for ik in top_i:
            onehot = onehot + jnp.where(lane == ik, 1.0, 0.0)
        counts = jnp.sum(onehot, axis=0, keepdims=True)
        lower = jnp.dot(jnp.broadcast_to(counts, (8, LANES)), upper_ref[...],
                        preferred_element_type=F32, precision=HIGHEST)[0:1, :]
        before = _dot(tri_ref[...], onehot.astype(BF16)) + lower
        lpos = jnp.zeros((TB, LANES), jnp.int32)
        topw = jnp.zeros((TB, LANES), F32)
        for k in range(TOPK):
            pos = jnp.sum(jnp.where(lane == top_i[k], before, 0.0), axis=-1, keepdims=True)
            lpos = jnp.where(lane == float(k), pos.astype(jnp.int32), lpos)
            topw = jnp.where(lane == float(k), exps[k] / denom, topw)
        lpos_ref[blk, :] = lpos
        topw_ref[blk, :] = topw
        bcnt_ref[r] = counts


def _mix(x2d, mod, fcs, pdft, qcat, kcat, cache_k, wfo, woa, wg, bg, wout, ln1g, ln1b, wr, br,
         tri, upper, *, rows, n_seq, seq_len, mod_row0, mod_row_step):
    cache = cache_k is not None
    seqs = max(1, rows // seq_len)
    q_rows = rows // seqs
    bps = seq_len // q_rows
    n_outer = n_seq // seqs
    in_specs = [
        pl.BlockSpec((rows, D), lambda b, j: (b * bps + j, 0)),
        _full((8, 6 * D)),
        pl.BlockSpec((seqs, 2, seq_len, FW), lambda b, j: (b, 0, 0, 0)),
        pl.BlockSpec((q_rows, 2 * seq_len), lambda b, j: (j, 0)),
        pl.BlockSpec((seqs, HEADS, q_rows, 2 * LANES), lambda b, j: (b, 0, j, 0)),
        pl.BlockSpec((seqs, seq_len, 2 * LANES), lambda b, j: (b, 0, 0)),
    ]
    args = [x2d, mod, fcs, pdft, qcat, kcat]
    if cache:
        in_specs.append(pl.BlockSpec((seqs, PAST, 2 * LANES), lambda b, j: (b, 0, 0)))
        args.append(cache_k)
    in_specs += [
        _full((FW, D)), _full((HEADS * KVL, D)), _full((D, 2 * D)), _full((1, 2 * D)),
        _full((D, D)), _full((1, D)), _full((1, D)), _full((D, 2 * LANES)), _full((1, LANES)),
        _full((TB, TB)), _full((LANES, LANES)),
    ]
    args += [wfo, woa, wg, bg, wout, ln1g, ln1b, wr, br, tri, upper]
    n_tok = n_seq * seq_len
    tok_spec = lambda w: pl.BlockSpec((rows, w), lambda b, j: (b * bps + j, 0))
    out_specs = [tok_spec(D), tok_spec(D), tok_spec(LANES), tok_spec(LANES),
                 pl.BlockSpec((rows // TB, 1, LANES), lambda b, j: (b * bps + j, 0, 0))]
    out_shape = [
        jax.ShapeDtypeStruct((n_tok, D), F32),
        jax.ShapeDtypeStruct((n_tok, D), BF16),
        jax.ShapeDtypeStruct((n_tok, LANES), jnp.int32),
        jax.ShapeDtypeStruct((n_tok, LANES), F32),
        jax.ShapeDtypeStruct((n_tok // TB, 1, LANES), F32),
    ]
    kern = functools.partial(_mix_kernel, rows=rows, seq_len=seq_len, seqs=seqs, cache=cache,
                             mod_row0=mod_row0, mod_row_step=mod_row_step)
    return pl.pallas_call(
        kern, grid=(n_outer, bps), in_specs=in_specs, out_specs=out_specs, out_shape=out_shape,
        compiler_params=_params(("arbitrary", "arbitrary")),
        name="mix_lat" if cache else "mix_ctx",
    )(*args)


N_CTX_BLK = N_CTX // TB
N_BLK = N_TOK // TB


def _row_tiles(ref, row0, n_rows):
    return ref.at[pl.ds(pl.multiple_of(row0 * SUB, SUB), n_rows * SUB), :]


def _for_each_run_copy(lstart_ref, blen_ref, gstart_ref, blk, local_ref, global_ref, sem, to_global, fn):
    def per_expert(e, carry):
        idx = blk * N_EXP + e
        n = blen_ref[idx]
        a = lstart_ref[idx]
        g = gstart_ref[idx]
        def pieces(bits):
            for bit in bits:
                size = 1 << bit

                @pl.when((n & size) != 0)
                def _():
                    done = lax.shift_left(lax.shift_right_logical(n, bit + 1), bit + 1)
                    loc = _row_tiles(local_ref, a + done, size)
                    glo = _row_tiles(global_ref, g + done, size)
                    fn(pltpu.make_async_copy(loc, glo, sem) if to_global
                       else pltpu.make_async_copy(glo, loc, sem))

        @pl.when(n >= RARE_RUN)
        def _():
            pieces(range(RUN_BITS - 1, RARE_RUN.bit_length() - 2, -1))

        pieces(range(RARE_RUN.bit_length() - 2, -1, -1))
        return carry

    lax.fori_loop(0, N_EXP, per_expert, 0)


def _load_rows(ref, n_rows):
    return jnp.concatenate([ref[pl.ds(s, n_rows, stride=SUB), :] for s in range(SUB)], axis=1)


def _store_rows(ref, val, n_rows):
    for s in range(SUB):
        ref[pl.ds(s, n_rows, stride=SUB), :] = val[:, s * LANES:(s + 1) * LANES]


def _dispatch_kernel(lstart_ref, blen_ref, gstart_ref, h2c_ref, h2l_ref, lpc_ref, lpl_ref, xs_ref,
                     buf, zbuf, sem, zsem):
    b = pl.program_id(0)
    slot = b & 1
    tables = (lstart_ref, blen_ref, gstart_ref)

    def start_runs(blk, s):
        _for_each_run_copy(*tables, blk, buf.at[s], xs_ref, sem.at[s], True, lambda cp: cp.start())

    def wait_runs(s):
        pltpu.make_async_copy(buf.at[s], _row_tiles(xs_ref, 0, BLK_ROWS), sem.at[s]).wait()

    @pl.when(b == 0)
    def _():
        zbuf[...] = jnp.zeros_like(zbuf)
        tail = pltpu.make_async_copy(zbuf, _row_tiles(xs_ref, N_SORT, ROW_TILE), zsem)
        tail.start()
        tail.wait()

    @pl.when(b >= 2)
    def _():
        wait_runs(slot)

    def sort_block(h2_ref, lp_ref):
        col = lax.broadcasted_iota(jnp.int32, (TB, BLK_ROWS), 1)
        lp = lp_ref[...]
        pick = jnp.zeros((TB, BLK_ROWS), F32)
        for k in range(TOPK):
            pick = pick + jnp.where(col == lp[:, k:k + 1], 1.0, 0.0)
        sorted_rows = lax.dot_general(pick.astype(BF16), h2_ref[...],
                                      (((0,), (0,)), ((), ())), preferred_element_type=F32)
        _store_rows(buf.at[slot], sorted_rows, BLK_ROWS)

    @pl.when(b < N_CTX_BLK)
    def _():
        sort_block(h2c_ref, lpc_ref)

    @pl.when(b >= N_CTX_BLK)
    def _():
        sort_block(h2l_ref, lpl_ref)

    start_runs(b, slot)

    @pl.when(b == N_BLK - 1)
    def _():
        wait_runs(1 - slot)
        wait_runs(slot)


def _dispatch(lstart, blen, gstart, h2c, h2l, lpc, lpl):
    ctx_idx = lambda i, *_: (jnp.minimum(i, N_CTX_BLK - 1), 0)
    lat_idx = lambda i, *_: (jnp.maximum(i - N_CTX_BLK, 0), 0)
    grid_spec = pltpu.PrefetchScalarGridSpec(
        num_scalar_prefetch=3,
        grid=(N_BLK,),
        in_specs=[
            pl.BlockSpec((TB, D), ctx_idx),
            pl.BlockSpec((TB, D), lat_idx),
            pl.BlockSpec((TB, LANES), ctx_idx),
            pl.BlockSpec((TB, LANES), lat_idx),
        ],
        out_specs=pl.BlockSpec(memory_space=pl.ANY),
        scratch_shapes=[pltpu.VMEM((2, BLK_ROWS * SUB, LANES), F32),
                        pltpu.VMEM((ROW_TILE * SUB, LANES), F32),
                        pltpu.SemaphoreType.DMA((2,)),
                        pltpu.SemaphoreType.DMA(())],
    )
    return pl.pallas_call(
        _dispatch_kernel, grid_spec=grid_spec,
        out_shape=jax.ShapeDtypeStruct((N_SORT_PAD * SUB, LANES), F32),
        compiler_params=_params(("arbitrary",)),
        name="dispatch",
    )(lstart, blen, gstart, h2c, h2l, lpc, lpl)


def _expert_kernel(start_ref, count_ref, xs_ref, wg_ref, bg_ref, wu_ref, bu_ref, wd_ref, bd_ref, ys_ref,
                   wbf, xbuf, ybuf, sem_in, sem_out):
    e = pl.program_id(0)
    row0 = start_ref[e]

    def tiles_of(ex):
        return lax.shift_right_logical(count_ref[ex] + (ROW_TILE - 1), ROW_TILE.bit_length() - 1)

    n_tiles = tiles_of(e)

    def in_copy_of(ex, j, slot):
        return pltpu.make_async_copy(_row_tiles(xs_ref, start_ref[ex] + j * ROW_TILE, ROW_TILE),
                                     xbuf.at[slot], sem_in.at[slot])

    def in_copy(j, slot):
        return in_copy_of(e, j, slot)

    def out_copy(j, slot):
        return pltpu.make_async_copy(ybuf.at[slot], _row_tiles(ys_ref, row0 + j * ROW_TILE, ROW_TILE),
                                     sem_out.at[slot])

    def start_first_tiles(ex):
        for j in range(RING - 1):
            @pl.when(tiles_of(ex) > j)
            def _():
                in_copy_of(ex, j, j).start()

    @pl.when(e == 0)
    def _():
        ybuf[0] = jnp.zeros((ROW_TILE * SUB, LANES), F32)
        tail = pltpu.make_async_copy(ybuf.at[0], _row_tiles(ys_ref, N_SORT, ROW_TILE), sem_out.at[0])
        tail.start()
        tail.wait()
        start_first_tiles(0)

    @pl.when(n_tiles > 0)
    def _():
        wbf[0] = wg_ref[0].astype(BF16)
        wbf[1] = wu_ref[0].astype(BF16)
        wbf[2] = wd_ref[0].astype(BF16)

        def tile(j, carry):
            slot = j & (RING - 1)

            ahead = j + (RING - 1)

            @pl.when(ahead < n_tiles)
            def _():
                in_copy(ahead, ahead & (RING - 1)).start()

            in_copy(j, slot).wait()

            @pl.when(j >= RING)
            def _():
                out_copy(j - RING, slot).wait()

            xb = _load_rows(xbuf.at[slot], ROW_TILE).astype(BF16)
            gt = jnp.minimum(_dot(xb, wbf[0]) + bg_ref[0], SWIGLU_LIMIT)
            up = jnp.clip(_dot(xb, wbf[1]) + bu_ref[0], -SWIGLU_LIMIT, SWIGLU_LIMIT)
            act = gt * jax.nn.sigmoid(SWIGLU_ALPHA * gt) * (up + 1.0)
            _store_rows(ybuf.at[slot], _dot(act.astype(BF16), wbf[2]) + bd_ref[0], ROW_TILE)
            out_copy(j, slot).start()
            return carry

        lax.fori_loop(0, n_tiles, tile, 0)

        for back in range(RING, 0, -1):
            @pl.when(n_tiles >= back)
            def _():
                out_copy(n_tiles - back, (n_tiles - back) & (RING - 1)).wait()

    @pl.when(e + 1 < N_EXP)
    def _():
        start_first_tiles(e + 1)


def _experts(start, count, xs, wg, bg, wu, bu, wd, bd):
    w_idx = lambda e, *_: (e, 0, 0)
    grid_spec = pltpu.PrefetchScalarGridSpec(
        num_scalar_prefetch=2,
        grid=(N_EXP,),
        in_specs=[
            pl.BlockSpec(memory_space=pl.ANY),
            pl.BlockSpec((1, D, D_EXP), w_idx),
            pl.BlockSpec((1, 1, D_EXP), w_idx),
            pl.BlockSpec((1, D, D_EXP), w_idx),
            pl.BlockSpec((1, 1, D_EXP), w_idx),
            pl.BlockSpec((1, D_EXP, D), w_idx),
            pl.BlockSpec((1, 1, D), w_idx),
        ],
        out_specs=pl.BlockSpec(memory_space=pl.ANY),
        scratch_shapes=[pltpu.VMEM((3, D, D_EXP), BF16),
                        pltpu.VMEM((RING, ROW_TILE * SUB, LANES), F32),
                        pltpu.VMEM((RING, ROW_TILE * SUB, LANES), F32),
                        pltpu.SemaphoreType.DMA((RING,)),
                        pltpu.SemaphoreType.DMA((RING,))],
    )
    return pl.pallas_call(
        _expert_kernel, grid_spec=grid_spec,
        out_shape=jax.ShapeDtypeStruct((N_SORT_PAD * SUB, LANES), F32),
        compiler_params=_params(("arbitrary",)),
        name="experts",
    )(start, count, xs, wg, bg, wu, bu, wd, bd)


def _combine_kernel(lstart_ref, blen_ref, gstart_ref, lp_ref, topw_ref, x1_ref, mod_ref, g_ref, b_ref,
                    ys_ref, o_ref, buf, sem, *, blk0, n_blk, blocks_per_seq, mod_row0, mod_row_step):
    i = pl.program_id(0)
    slot = i & 1
    gate2 = _mod_row(mod_ref, mod_row0, mod_row_step, i // blocks_per_seq)[5]
    tables = (lstart_ref, blen_ref, gstart_ref)

    def start_runs(step, s):
        _for_each_run_copy(*tables, blk0 + step, buf.at[s], ys_ref, sem.at[s], False,
                           lambda cp: cp.start())

    @pl.when(i == 0)
    def _():
        start_runs(0, 0)

    @pl.when(i + 1 < n_blk)
    def _():
        start_runs(i + 1, 1 - slot)

    pltpu.make_async_copy(_row_tiles(ys_ref, 0, BLK_ROWS), buf.at[slot], sem.at[slot]).wait()

    y_sorted = _load_rows(buf.at[slot], BLK_ROWS).astype(BF16)
    col = lax.broadcasted_iota(jnp.int32, (TB, BLK_ROWS), 1)
    lp = lp_ref[...]
    topw = topw_ref[...]
    weights = jnp.zeros((TB, BLK_ROWS), F32)
    for k in range(TOPK):
        weights = weights + jnp.where(col == lp[:, k:k + 1], topw[:, k:k + 1], 0.0)
    ffn = _dot(weights.astype(BF16), y_sorted)
    y = DN_ALPHA * x1_ref[...] + gate2 * ffn
    o_ref[...] = _norm_noaffine(y) * g_ref[...] + b_ref[...]


def _combine(lstart, blen, gstart, lpos, topw, x1, mod, ln2g, ln2b, ys, *, blk0, n_seq, seq_len,
             mod_row0, mod_row_step):
    bps = seq_len // TB
    n_blk = n_seq * bps
    tok = lambda i, *_: (i, 0)
    const = lambda i, *_: (0, 0)
    grid_spec = pltpu.PrefetchScalarGridSpec(
        num_scalar_prefetch=3,
        grid=(n_blk,),
        in_specs=[
            pl.BlockSpec((TB, LANES), tok),
            pl.BlockSpec((TB, LANES), tok),
            pl.BlockSpec((TB, D), tok),
            pl.BlockSpec((8, 6 * D), const),
            pl.BlockSpec((1, D), const),
            pl.BlockSpec((1, D), const),
            pl.BlockSpec(memory_space=pl.ANY),
        ],
        out_specs=pl.BlockSpec((TB, D), tok),
        scratch_shapes=[pltpu.VMEM((2, BLK_ROWS * SUB, LANES), F32), pltpu.SemaphoreType.DMA((2,))],
    )
    kern = functools.partial(_combine_kernel, blk0=blk0, n_blk=n_blk, blocks_per_seq=bps,
                             mod_row0=mod_row0, mod_row_step=mod_row_step)
    return pl.pallas_call(
        kern, grid_spec=grid_spec,
        out_shape=jax.ShapeDtypeStruct((n_seq * seq_len, D), F32),
        compiler_params=_params(("arbitrary",)),
        name="combine_lat" if blk0 else "combine_ctx",
    )(lstart, blen, gstart, lpos, topw, x1, mod, ln2g, ln2b, ys)


def _dft_tables():
    def cs(n):
        k = np.arange(n, dtype=np.int64)
        ang = 2.0 * np.pi * ((k[:, None] * k[None, :]) % n).astype(np.float64) / n
        return np.cos(ang) / math.sqrt(n), np.sin(ang) / math.sqrt(n)

    c, s = cs(FG)
    cdft = np.concatenate([c, s], axis=1).astype(np.float32)
    pd = {}
    for t in (T_CTX, T_LAT):
        c, s = cs(t)
        pd[t] = np.concatenate([c, -s], axis=1).astype(np.float32)
    tri = np.tril(np.ones((TB, TB), np.float32), k=-1)
    upper = np.triu(np.ones((LANES, LANES), np.float32), k=1)
    return cdft, pd, tri, upper


_ROT_PERM = np.array(list(range(8, 16)) + list(range(0, 8)) + list(range(24, 32)) + list(range(16, 24)))
_ROT_SIGN = np.array([-1.0] * 8 + [1.0] * 8 + [-1.0] * 8 + [1.0] * 8, np.float32)


def _rope_tables():
    rows = T_LAT // GRID_W
    row = jnp.repeat(jnp.arange(rows, dtype=F32), GRID_W)
    col = jnp.tile(jnp.arange(GRID_W, dtype=F32), rows)
    axis_dim = ROPE // 2
    inv_freq = ROPE_THETA ** (-jnp.arange(0, axis_dim, 2, dtype=F32) / axis_dim)
    ang_r = row[:, None] * inv_freq[None, :]
    ang_c = col[:, None] * inv_freq[None, :]
    ang = jnp.concatenate([ang_r, ang_r, ang_c, ang_c], axis=-1)
    pad = ((0, 0), (0, LANES - ROPE))
    return jnp.pad(jnp.cos(ang), pad), jnp.pad(jnp.sin(ang), pad)


def kernel(x_prompt, x_sample, cache_ckv, cache_krope, c, c_ctx, w_mod, b_mod, w_in, q_norm_g, w_q_up,
           kv_norm_g, w_kv_up, w_fourier_o, w_mla_o, w_gate, b_gate, w_out, ln1_g, ln1_b, w_router,
           b_router, w_gate_e, b_gate_e, w_up_e, b_up_e, w_down_e, b_down_e, ln2_g, ln2_b):
    cdft_np, pdft_np, tri_np, upper_np = _dft_tables()
    upper = jnp.asarray(upper_np)
    cdft = jnp.asarray(cdft_np).astype(BF16)
    pdft_ctx = jnp.asarray(pdft_np[T_CTX]).astype(BF16)
    pdft_lat = jnp.asarray(pdft_np[T_LAT]).astype(BF16)
    tri = jnp.asarray(tri_np).astype(BF16)
    cos_pad, sin_pad = _rope_tables()

    w_in0 = w_in[0]
    kr_w = w_in0[:, FW + QL + KVL:]
    lane_pad = ((0, 0), (0, LANES - ROPE))
    w_in_ext = jnp.concatenate(
        [w_in0[:, :FW + QL + KVL], jnp.pad(kr_w, lane_pad),
         jnp.pad(kr_w[:, _ROT_PERM] * _ROT_SIGN, lane_pad)], axis=1).astype(BF16)
    wq3 = w_q_up[0].reshape(QL, HEADS, NOPE + ROPE)
    wq_nope = jnp.transpose(wq3[:, :, :NOPE], (1, 0, 2))
    wq_rope = wq3[:, :, NOPE:]
    head_pad = ((0, 0), (0, 0), (0, LANES - ROPE))
    wqr = jnp.concatenate(
        [jnp.pad(wq_rope, head_pad).reshape(QL, HEADS * LANES),
         jnp.pad(wq_rope[:, :, _ROT_PERM] * _ROT_SIGN, head_pad).reshape(QL, HEADS * LANES)],
        axis=1).astype(BF16)
    wkv3 = w_kv_up[0].reshape(KVL, HEADS, NOPE + VH)
    wk = jnp.transpose(wkv3[:, :, :NOPE], (1, 0, 2))
    wv = jnp.transpose(wkv3[:, :, NOPE:], (1, 0, 2))
    wo3 = w_mla_o[0].reshape(HEADS, VH, D)
    wqa, woa = _prep(wq_nope, wk, wv, wo3)

    c_all = jnp.concatenate([c_ctx[None, :], c, jnp.zeros((8 - 1 - N_LAT_SEQ, D), F32)], axis=0)
    mod = _modulation(c_all, w_mod[0], b_mod)

    qg = q_norm_g
    kvg = kv_norm_g
    wfo = w_fourier_o[0].astype(BF16)
    wg = w_gate[0].astype(BF16)
    wout = w_out[0].astype(BF16)
    wr_f32 = jnp.pad(w_router[0], ((0, 0), (0, LANES - N_EXP)))
    wr_hi = wr_f32.astype(BF16)
    wr = jnp.concatenate([wr_hi, (wr_f32 - wr_hi.astype(F32)).astype(BF16)], axis=1)
    br = jnp.pad(b_router, ((0, 0), (0, LANES - N_EXP)), constant_values=NEG_BIG)
    cache_k = jnp.concatenate(
        [cache_ckv[:, 0], jnp.pad(cache_krope[:, 0], ((0, 0), (0, 0), (0, LANES - ROPE)))],
        axis=-1).astype(BF16)

    xc2d = x_prompt.reshape(N_CTX, D)
    xl2d = x_sample.reshape(N_LAT, D)

    fcs_c, qcat_c, kcat_c, new_ckv, new_krope = _pre(
        xc2d, mod, w_in_ext, cdft, qg, kvg, wqa, wqr, None, None,
        n_seq=N_CTX_SEQ, seq_len=T_CTX, rope=False, mod_row0=0, mod_row_step=0)
    x1_c, h2_c, lpos_c, topw_c, bcnt_c = _mix(
        xc2d, mod, fcs_c, pdft_ctx, qcat_c, kcat_c, None, wfo, woa, wg, b_gate, wout, ln1_g, ln1_b,
        wr, br, tri, upper, rows=MIX_ROWS_CTX, n_seq=N_CTX_SEQ, seq_len=T_CTX, mod_row0=0,
        mod_row_step=0)

    fcs_l, qcat_l, kcat_l = _pre(
        xl2d, mod, w_in_ext, cdft, qg, kvg, wqa, wqr, cos_pad, sin_pad,
        n_seq=N_LAT_SEQ, seq_len=T_LAT, rope=True, mod_row0=1, mod_row_step=1)
    x1_l, h2_l, lpos_l, topw_l, bcnt_l = _mix(
        xl2d, mod, fcs_l, pdft_lat, qcat_l, kcat_l, cache_k, wfo, woa, wg, b_gate, wout, ln1_g,
        ln1_b, wr, br, tri, upper, rows=MIX_ROWS_LAT, n_seq=N_LAT_SEQ, seq_len=T_LAT, mod_row0=1,
        mod_row_step=1)

    blen = jnp.concatenate([bcnt_c[:, 0, :N_EXP], bcnt_l[:, 0, :N_EXP]], axis=0).astype(jnp.int32)
    lstart = jnp.cumsum(blen, axis=1) - blen
    count = jnp.sum(blen, axis=0)
    start = jnp.cumsum(count) - count
    gstart = start[None, :] + jnp.cumsum(blen, axis=0) - blen
    lstart, blen, gstart = (t.reshape(-1).astype(jnp.int32) for t in (lstart, blen, gstart))
    start = start.astype(jnp.int32)
    count = count.astype(jnp.int32)

    xs = _dispatch(lstart, blen, gstart, h2_c, h2_l, lpos_c, lpos_l)
    ys = _experts(start, count, xs,
                  w_gate_e[0], b_gate_e[0][:, None, :], w_up_e[0], b_up_e[0][:, None, :],
                  w_down_e[0], b_down_e[0][:, None, :])

    y_c = _combine(lstart, blen, gstart, lpos_c, topw_c, x1_c, mod, ln2_g, ln2_b, ys, blk0=0,
                   n_seq=N_CTX_SEQ, seq_len=T_CTX, mod_row0=0, mod_row_step=0)
    y_l = _combine(lstart, blen, gstart, lpos_l, topw_l, x1_l, mod, ln2_g, ln2_b, ys, blk0=N_CTX_BLK,
                   n_seq=N_LAT_SEQ, seq_len=T_LAT, mod_row0=1, mod_row_step=1)
    return (y_c.reshape(N_CTX_SEQ, T_CTX, D), y_l.reshape(N_LAT_SEQ, T_LAT, D), new_ckv, new_krope)
```

```python
import functools
import math

import numpy as np
import jax
import jax.numpy as jnp
from jax import lax
from jax.experimental import pallas as pl
from jax.experimental.pallas import tpu as pltpu

D = 1024
N_CTX_SEQ, T_CTX = 32, 256
N_LAT_SEQ, T_LAT = 4, 1024
PAST = 256
N_CTX = N_CTX_SEQ * T_CTX
N_LAT = N_LAT_SEQ * T_LAT
N_TOK = N_CTX + N_LAT
FW = 512
FG = 128
N_FG = FW // FG
HEADS = 8
QL = 256
KVL = 128
NOPE = 64
ROPE = 32
VH = 64
N_EXP = 32
TOPK = 4
D_EXP = 1024
SWIGLU_LIMIT = 7.0
SWIGLU_ALPHA = 1.702
LN_EPS = 1e-5
RMS_EPS = 1e-6
DN_ALPHA = 2.0 ** 0.25
ATT_SCALE = float(NOPE + ROPE) ** -0.5
ROPE_THETA = 10000.0
GRID_W = 64

LANES = 128
TB = 256
PRE_ROWS = 512
MIX_ROWS_CTX = 512
MIX_ROWS_LAT = 512
SUB = 8
ROW_TILE = 256
RING = 4
N_SORT = N_TOK * TOPK
N_SORT_PAD = N_SORT + ROW_TILE
BLK_ROWS = TB * TOPK
RUN_BITS = TB.bit_length()
VMEM_LIMIT = 56 * 1024 * 1024
NEG_BIG = -1e30

F32 = jnp.float32
BF16 = jnp.bfloat16
HIGHEST = lax.Precision.HIGHEST


def _dot(a, b):
    return jnp.dot(a, b, preferred_element_type=F32)


def _dot_nt(a, b):
    return lax.dot_general(a, b, (((1,), (1,)), ((), ())), preferred_element_type=F32)


def _params(sem):
    return pltpu.CompilerParams(dimension_semantics=sem, vmem_limit_bytes=VMEM_LIMIT)


def _full(shape):
    n = len(shape)
    return pl.BlockSpec(shape, lambda *_: (0,) * n, pipeline_mode=pl.Buffered(1))


def _norm_noaffine(x):
    mu = jnp.mean(x, axis=-1, keepdims=True)
    xc = x - mu
    var = jnp.mean(xc * xc, axis=-1, keepdims=True)
    return xc * lax.rsqrt(var + LN_EPS)


def _rms(x, g):
    return x * lax.rsqrt(jnp.mean(x * x, axis=-1, keepdims=True) + RMS_EPS) * g


def _mod_row(mod_ref, row0, step, seq):
    row = row0 if step == 0 else row0 + step * seq
    m = mod_ref[pl.ds(row, 1), :]
    return [m[:, i * D:(i + 1) * D] for i in range(6)]


def _prep_kernel(wqn_ref, wk_ref, wv_ref, wo_ref, qabs_ref, oabs_ref):
    qabs = lax.dot_general(wqn_ref[0], wk_ref[0], (((1,), (1,)), ((), ())),
                           preferred_element_type=F32, precision=HIGHEST)
    qabs_ref[...] = qabs.astype(BF16)
    oabs = jnp.dot(wv_ref[0], wo_ref[0], preferred_element_type=F32, precision=HIGHEST)
    oabs_ref[...] = oabs.astype(BF16)


def _prep(wqn, wk, wv, wo):
    return pl.pallas_call(
        _prep_kernel,
        grid=(HEADS,),
        in_specs=[
            pl.BlockSpec((1, QL, NOPE), lambda h: (h, 0, 0)),
            pl.BlockSpec((1, KVL, NOPE), lambda h: (h, 0, 0)),
            pl.BlockSpec((1, KVL, VH), lambda h: (h, 0, 0)),
            pl.BlockSpec((1, VH, D), lambda h: (h, 0, 0)),
        ],
        out_specs=[
            pl.BlockSpec((QL, KVL), lambda h: (0, h)),
            pl.BlockSpec((KVL, D), lambda h: (h, 0)),
        ],
        out_shape=[jax.ShapeDtypeStruct((QL, HEADS * KVL), BF16),
                   jax.ShapeDtypeStruct((HEADS * KVL, D), BF16)],
        compiler_params=_params(("arbitrary",)),
        name="weight_prep",
    )(wqn, wk, wv, wo)


MOD_COLS = 1536


def _mod_kernel(c_ref, w_ref, b_ref, o_ref):
    c = c_ref[...]
    s = c * jax.nn.sigmoid(c)
    o_ref[...] = jnp.dot(s, w_ref[...], preferred_element_type=F32, precision=HIGHEST) + b_ref[...]


def _modulation(c_all, w_mod, b_mod):
    return pl.pallas_call(
        _mod_kernel,
        grid=(6 * D // MOD_COLS,),
        in_specs=[
            pl.BlockSpec((8, D), lambda i: (0, 0)),
            pl.BlockSpec((D, MOD_COLS), lambda i: (0, i)),
            pl.BlockSpec((1, MOD_COLS), lambda i: (0, i)),
        ],
        out_specs=pl.BlockSpec((8, MOD_COLS), lambda i: (0, i)),
        out_shape=jax.ShapeDtypeStruct((8, 6 * D), F32),
        compiler_params=_params(("arbitrary",)),
        name="modulation",
    )(c_all, w_mod, b_mod)


def _pre_kernel(*refs, rows, seq_len, seqs, rope, mod_row0, mod_row_step):
    if rope:
        (x_ref, mod_ref, win_ref, cdft_ref, qg_ref, kvg_ref, wqa_ref, wqr_ref, cos_ref, sin_ref,
         fcs_ref, qcat_ref, kcat_ref) = refs
    else:
        (x_ref, mod_ref, win_ref, cdft_ref, qg_ref, kvg_ref, wqa_ref, wqr_ref,
         fcs_ref, qcat_ref, kcat_ref, ckv_ref, krope_ref) = refs
    q_rows = rows // seqs
    b = pl.program_id(0) // (seq_len // q_rows)
    shift1, scale1 = _mod_row(mod_ref, mod_row0, mod_row_step, b)[:2]
    seq_rows = [slice(s * q_rows, (s + 1) * q_rows) for s in range(seqs)]

    x = x_ref[...]
    h1 = _norm_noaffine(x) * (1.0 + scale1) + shift1
    h1b = h1.astype(BF16)
    n_cols = 9 * LANES if rope else 8 * LANES
    proj = _dot(h1b, win_ref[:, :n_cols])

    f_b = proj[:, :FW].astype(BF16)
    for g in range(N_FG):
        r = _dot(f_b[:, g * FG:(g + 1) * FG], cdft_ref[...])
        for s, sl in enumerate(seq_rows):
            fcs_ref[s, 0, :, g * FG:(g + 1) * FG] = r[sl, :FG].astype(BF16)
            fcs_ref[s, 1, :, g * FG:(g + 1) * FG] = r[sl, FG:].astype(BF16)

    qn = _rms(proj[:, FW:FW + QL], qg_ref[...]).astype(BF16)
    ckv = _rms(proj[:, FW + QL:FW + QL + KVL], kvg_ref[...])
    kr = proj[:, 7 * LANES:8 * LANES]

    qa = _dot(qn, wqa_ref[...]) * ATT_SCALE
    if rope:
        cos = cos_ref[...]
        sin = sin_ref[...]
        qr2 = _dot(qn, wqr_ref[...])
        kr_keys = kr * cos + proj[:, 8 * LANES:9 * LANES] * sin
    else:
        qr2 = _dot(qn, wqr_ref[:, :HEADS * LANES])
        kr_keys = kr
    ckv_b = ckv.astype(BF16)
    kr_b = kr_keys.astype(BF16)
    for h in range(HEADS):
        qr_h = qr2[:, h * LANES:(h + 1) * LANES]
        if rope:
            qr_h = qr_h * cos + qr2[:, (HEADS + h) * LANES:(HEADS + h + 1) * LANES] * sin
        qa_h = qa[:, h * KVL:(h + 1) * KVL].astype(BF16)
        qr_h = (qr_h * ATT_SCALE).astype(BF16)
        for s, sl in enumerate(seq_rows):
            qcat_ref[s, h, :, :KVL] = qa_h[sl]
            qcat_ref[s, h, :, KVL:] = qr_h[sl]
    for s, sl in enumerate(seq_rows):
        kcat_ref[s, :, :KVL] = ckv_b[sl]
        kcat_ref[s, :, KVL:] = kr_b[sl]
        if not rope:
            ckv_ref[s, 0] = ckv[sl]
            krope_ref[s, 0] = kr[sl, :ROPE]


def _pre(x2d, mod, w_in_ext, cdft, qg, kvg, wqa, wqr, cos_pad, sin_pad, *, n_seq, seq_len, rope,
         mod_row0, mod_row_step):
    rows = PRE_ROWS
    seqs = max(1, rows // seq_len)
    q_rows = rows // seqs
    bps = seq_len // q_rows
    n_blk = n_seq * seq_len // rows
    seq_blk = lambda i: (i // bps, 0, i % bps, 0)
    in_specs = [
        pl.BlockSpec((rows, D), lambda i: (i, 0)),
        _full((8, 6 * D)),
        _full((D, 9 * LANES)),
        _full((FG, 2 * FG)),
        _full((1, QL)),
        _full((1, KVL)),
        _full((QL, HEADS * KVL)),
        _full((QL, 2 * HEADS * LANES)),
    ]
    args = [x2d, mod, w_in_ext, cdft, qg, kvg, wqa, wqr]
    out_specs = [
        pl.BlockSpec((seqs, 2, q_rows, FW), seq_blk),
        pl.BlockSpec((seqs, HEADS, q_rows, 2 * LANES), seq_blk),
        pl.BlockSpec((seqs, q_rows, 2 * LANES), lambda i: (i // bps, i % bps, 0)),
    ]
    out_shape = [
        jax.ShapeDtypeStruct((n_seq, 2, seq_len, FW), BF16),
        jax.ShapeDtypeStruct((n_seq, HEADS, seq_len, 2 * LANES), BF16),
        jax.ShapeDtypeStruct((n_seq, seq_len, 2 * LANES), BF16),
    ]
    if rope:
        in_specs += [pl.BlockSpec((q_rows, LANES), lambda i: (i % bps, 0)),
                     pl.BlockSpec((q_rows, LANES), lambda i: (i % bps, 0))]
        args += [cos_pad, sin_pad]
    else:
        out_specs += [pl.BlockSpec((seqs, 1, q_rows, KVL), seq_blk),
                      pl.BlockSpec((seqs, 1, q_rows, ROPE), seq_blk)]
        out_shape += [jax.ShapeDtypeStruct((n_seq, 1, seq_len, KVL), F32),
                      jax.ShapeDtypeStruct((n_seq, 1, seq_len, ROPE), F32)]
    kern = functools.partial(_pre_kernel, rows=rows, seq_len=seq_len, seqs=seqs, rope=rope,
                             mod_row0=mod_row0, mod_row_step=mod_row_step)
    return pl.pallas_call(
        kern, grid=(n_blk,), in_specs=in_specs, out_specs=out_specs, out_shape=out_shape,
        compiler_params=_params(("arbitrary",)),
        name="pre_lat" if rope else "pre_ctx",
    )(*args)


def _mix_kernel(*refs, rows, seq_len, seqs, cache, mod_row0, mod_row_step):
    if cache:
        (x_ref, mod_ref, fcs_ref, pdft_ref, qcat_ref, kcat_ref, cache_ref, wfo_ref, woa_ref,
         wg_ref, bg_ref, wout_ref, ln1g_ref, ln1b_ref, wr_ref, br_ref, tri_ref, upper_ref,
         x1_ref, h2_ref, lpos_ref, topw_ref, bcnt_ref) = refs
    else:
        (x_ref, mod_ref, fcs_ref, pdft_ref, qcat_ref, kcat_ref, wfo_ref, woa_ref,
         wg_ref, bg_ref, wout_ref, ln1g_ref, ln1b_ref, wr_ref, br_ref, tri_ref, upper_ref,
         x1_ref, h2_ref, lpos_ref, topw_ref, bcnt_ref) = refs
        cache_ref = None
    b = pl.program_id(0)
    shift1, scale1, gate1, shift2, scale2, _ = _mod_row(mod_ref, mod_row0, mod_row_step, b)

    x = x_ref[...]
    h1b = (_norm_noaffine(x) * (1.0 + scale1) + shift1).astype(BF16)

    mixed_rows, attn_rows = [], []
    for s in range(seqs):
        mixed_rows.append(_dot(pdft_ref[:, :seq_len], fcs_ref[s, 0])
                          + _dot(pdft_ref[:, seq_len:], fcs_ref[s, 1]))
        kc = kcat_ref[s]
        vals = kc[:, :KVL]
        if cache:
            kc2 = cache_ref[s]
            vals2 = kc2[:, :KVL]
        heads_out = []
        for h in range(HEADS):
            q = qcat_ref[s, h]
            s1 = _dot_nt(q, kc)
            m = jnp.max(s1, axis=-1, keepdims=True)
            if cache:
                s2 = _dot_nt(q, kc2)
                m = jnp.maximum(m, jnp.max(s2, axis=-1, keepdims=True))
            p1 = jnp.exp(s1 - m)
            l = jnp.sum(p1, axis=-1, keepdims=True)
            o = _dot(p1.astype(BF16), vals)
            if cache:
                p2 = jnp.exp(s2 - m)
                l = l + jnp.sum(p2, axis=-1, keepdims=True)
                o = o + _dot(p2.astype(BF16), vals2)
            heads_out.append((o / l).astype(BF16))
        attn_rows.append(jnp.concatenate(heads_out, axis=1))
    mixed = jnp.concatenate(mixed_rows, axis=0) if seqs > 1 else mixed_rows[0]
    attn = jnp.concatenate(attn_rows, axis=0) if seqs > 1 else attn_rows[0]
    f_out = _dot(mixed.astype(BF16), wfo_ref[...])
    m_out = _dot(attn, woa_ref[...])

    gates = jax.nn.sigmoid(_dot(h1b, wg_ref[...]) + bg_ref[...])
    merged = gates[:, :D] * f_out + gates[:, D:] * m_out
    mix = _dot(merged.astype(BF16), wout_ref[...])

    x1 = _norm_noaffine(DN_ALPHA * x + gate1 * mix) * ln1g_ref[...] + ln1b_ref[...]
    x1_ref[...] = x1
    h2 = _norm_noaffine(x1) * (1.0 + scale2) + shift2
    h2_ref[...] = h2.astype(BF16)

    h_hi = h2.astype(BF16)
    h_lo = (h2 - h_hi.astype(F32)).astype(BF16)
    hi_part = _dot(h_hi, wr_ref[...])
    logits_all = (hi_part[:, :LANES] + hi_part[:, LANES:] + _dot(h_lo, wr_ref[:, :LANES])) + br_ref[...]
    lane = lax.broadcasted_iota(jnp.int32, (TB, LANES), 1).astype(F32)
    for r in range(rows // TB):
        blk = slice(r * TB, (r + 1) * TB)
        work = logits_all[blk]
        top_v, top_i = [], []
        for _ in range(TOPK):
            mk = jnp.max(work, axis=-1, keepdims=True)
            ik = jnp.min(jnp.where(work == mk, lane, float(LANES)), axis=-1, keepdims=True)
            work = jnp.where(lane == ik, -jnp.inf, work)
            top_v.append(mk)
            top_i.append(ik)
        exps = [jnp.exp(v - top_v[0]) for v in top_v]
        denom = exps[0] + exps[1] + exps[2] + exps[3]

        onehot = jnp.zeros((TB, LANES), F32)
        for ik in top_i:
            onehot = onehot + jnp.where(lane == ik, 1.0, 0.0)
        counts = jnp.sum(onehot, axis=0, keepdims=True)
        lower = jnp.dot(jnp.broadcast_to(counts, (8, LANES)), upper_ref[...],
                        preferred_element_type=F32, precision=HIGHEST)[0:1, :]
        before = _dot(tri_ref[...], onehot.astype(BF16)) + lower
        lpos = jnp.zeros((TB, LANES), jnp.int32)
        topw = jnp.zeros((TB, LANES), F32)
        for k in range(TOPK):
            pos = jnp.sum(jnp.where(lane == top_i[k], before, 0.0), axis=-1, keepdims=True)
            lpos = jnp.where(lane == float(k), pos.astype(jnp.int32), lpos)
            topw = jnp.where(lane == float(k), exps[k] / denom, topw)
        lpos_ref[blk, :] = lpos
        topw_ref[blk, :] = topw
        bcnt_ref[r] = counts


def _mix(x2d, mod, fcs, pdft, qcat, kcat, cache_k, wfo, woa, wg, bg, wout, ln1g, ln1b, wr, br,
         tri, upper, *, rows, n_seq, seq_len, mod_row0, mod_row_step):
    cache = cache_k is not None
    seqs = max(1, rows // seq_len)
    q_rows = rows // seqs
    bps = seq_len // q_rows
    n_outer = n_seq // seqs
    in_specs = [
        pl.BlockSpec((rows, D), lambda b, j: (b * bps + j, 0)),
        _full((8, 6 * D)),
        pl.BlockSpec((seqs, 2, seq_len, FW), lambda b, j: (b, 0, 0, 0)),
        pl.BlockSpec((q_rows, 2 * seq_len), lambda b, j: (j, 0)),
        pl.BlockSpec((seqs, HEADS, q_rows, 2 * LANES), lambda b, j: (b, 0, j, 0)),
        pl.BlockSpec((seqs, seq_len, 2 * LANES), lambda b, j: (b, 0, 0)),
    ]
    args = [x2d, mod, fcs, pdft, qcat, kcat]
    if cache:
        in_specs.append(pl.BlockSpec((seqs, PAST, 2 * LANES), lambda b, j: (b, 0, 0)))
        args.append(cache_k)
    in_specs += [
        _full((FW, D)), _full((HEADS * KVL, D)), _full((D, 2 * D)), _full((1, 2 * D)),
        _full((D, D)), _full((1, D)), _full((1, D)), _full((D, 2 * LANES)), _full((1, LANES)),
        _full((TB, TB)), _full((LANES, LANES)),
    ]
    args += [wfo, woa, wg, bg, wout, ln1g, ln1b, wr, br, tri, upper]
    n_tok = n_seq * seq_len
    tok_spec = lambda w: pl.BlockSpec((rows, w), lambda b, j: (b * bps + j, 0))
    out_specs = [tok_spec(D), tok_spec(D), tok_spec(LANES), tok_spec(LANES),
                 pl.BlockSpec((rows // TB, 1, LANES), lambda b, j: (b * bps + j, 0, 0))]
    out_shape = [
        jax.ShapeDtypeStruct((n_tok, D), F32),
        jax.ShapeDtypeStruct((n_tok, D), BF16),
        jax.ShapeDtypeStruct((n_tok, LANES), jnp.int32),
        jax.ShapeDtypeStruct((n_tok, LANES), F32),
        jax.ShapeDtypeStruct((n_tok // TB, 1, LANES), F32),
    ]
    kern = functools.partial(_mix_kernel, rows=rows, seq_len=seq_len, seqs=seqs, cache=cache,
                             mod_row0=mod_row0, mod_row_step=mod_row_step)
    return pl.pallas_call(
        kern, grid=(n_outer, bps), in_specs=in_specs, out_specs=out_specs, out_shape=out_shape,
        compiler_params=_params(("arbitrary", "arbitrary")),
        name="mix_lat" if cache else "mix_ctx",
    )(*args)


N_CTX_BLK = N_CTX // TB
N_BLK = N_TOK // TB


def _row_tiles(ref, row0, n_rows):
    return ref.at[pl.ds(pl.multiple_of(row0 * SUB, SUB), n_rows * SUB), :]


def _piece_tables(lstart, blen, gstart):
    bits = jnp.arange(RUN_BITS, dtype=jnp.int32)
    n = blen[:, :, None]
    valid = (n >> bits) & 1
    done = (n >> (bits + 1)) << (bits + 1)
    rank = jnp.cumsum(valid, axis=1) - valid
    place = (valid[..., None] == 1) & (rank[..., None] == jnp.arange(N_EXP, dtype=jnp.int32))
    def table(first_row):
        rows = (first_row[:, :, None] + done) * SUB
        listed = jnp.sum(jnp.where(place, rows[..., None], 0), axis=1)
        return listed.reshape(-1).astype(jnp.int32)
    return jnp.sum(valid, axis=1).reshape(-1).astype(jnp.int32), table(lstart), table(gstart)


def _start_piece_copies(cnt_ref, loc_ref, glo_ref, blk, local_ref, global_ref, sem, to_global):
    for bit in range(RUN_BITS):
        size = (1 << bit) * SUB
        base = (blk * RUN_BITS + bit) * N_EXP

        def piece(t, carry):
            loc = local_ref.at[pl.ds(pl.multiple_of(loc_ref[base + t], SUB), size), :]
            glo = global_ref.at[pl.ds(pl.multiple_of(glo_ref[base + t], SUB), size), :]
            (pltpu.make_async_copy(loc, glo, sem) if to_global
             else pltpu.make_async_copy(glo, loc, sem)).start()
            return carry

        lax.fori_loop(0, cnt_ref[blk * RUN_BITS + bit], piece, 0)


def _load_rows(ref, n_rows):
    return jnp.concatenate([ref[pl.ds(s, n_rows, stride=SUB), :] for s in range(SUB)], axis=1)


def _store_rows(ref, val, n_rows):
    for s in range(SUB):
        ref[pl.ds(s, n_rows, stride=SUB), :] = val[:, s * LANES:(s + 1) * LANES]


def _dispatch_kernel(cnt_ref, loc_ref, glo_ref, h2c_ref, h2l_ref, lpc_ref, lpl_ref, xs_ref,
                     buf, zbuf, sem, zsem):
    b = pl.program_id(0)
    slot = b & 1

    def start_runs(blk, s):
        _start_piece_copies(cnt_ref, loc_ref, glo_ref, blk, buf.at[s], xs_ref, sem.at[s], True)

    def wait_runs(s):
        pltpu.make_async_copy(buf.at[s], _row_tiles(xs_ref, 0, BLK_ROWS), sem.at[s]).wait()

    @pl.when(b == 0)
    def _():
        zbuf[...] = jnp.zeros_like(zbuf)
        tail = pltpu.make_async_copy(zbuf, _row_tiles(xs_ref, N_SORT, ROW_TILE), zsem)
        tail.start()
        tail.wait()

    @pl.when(b >= 2)
    def _():
        wait_runs(slot)

    def sort_block(h2_ref, lp_ref):
        col = lax.broadcasted_iota(jnp.int32, (TB, BLK_ROWS), 1)
        lp = lp_ref[...]
        pick = jnp.zeros((TB, BLK_ROWS), F32)
        for k in range(TOPK):
            pick = pick + jnp.where(col == lp[:, k:k + 1], 1.0, 0.0)
        sorted_rows = lax.dot_general(pick.astype(BF16), h2_ref[...],
                                      (((0,), (0,)), ((), ())), preferred_element_type=F32)
        _store_rows(buf.at[slot], sorted_rows, BLK_ROWS)

    @pl.when(b < N_CTX_BLK)
    def _():
        sort_block(h2c_ref, lpc_ref)

    @pl.when(b >= N_CTX_BLK)
    def _():
        sort_block(h2l_ref, lpl_ref)

    start_runs(b, slot)

    @pl.when(b == N_BLK - 1)
    def _():
        wait_runs(1 - slot)
        wait_runs(slot)


def _dispatch(lstart, blen, gstart, h2c, h2l, lpc, lpl):
    ctx_idx = lambda i, *_: (jnp.minimum(i, N_CTX_BLK - 1), 0)
    lat_idx = lambda i, *_: (jnp.maximum(i - N_CTX_BLK, 0), 0)
    grid_spec = pltpu.PrefetchScalarGridSpec(
        num_scalar_prefetch=3,
        grid=(N_BLK,),
        in_specs=[
            pl.BlockSpec((TB, D), ctx_idx),
            pl.BlockSpec((TB, D), lat_idx),
            pl.BlockSpec((TB, LANES), ctx_idx),
            pl.BlockSpec((TB, LANES), lat_idx),
        ],
        out_specs=pl.BlockSpec(memory_space=pl.ANY),
        scratch_shapes=[pltpu.VMEM((2, BLK_ROWS * SUB, LANES), F32),
                        pltpu.VMEM((ROW_TILE * SUB, LANES), F32),
                        pltpu.SemaphoreType.DMA((2,)),
                        pltpu.SemaphoreType.DMA(())],
    )
    return pl.pallas_call(
        _dispatch_kernel, grid_spec=grid_spec,
        out_shape=jax.ShapeDtypeStruct((N_SORT_PAD * SUB, LANES), F32),
        compiler_params=_params(("arbitrary",)),
        name="dispatch",
    )(lstart, blen, gstart, h2c, h2l, lpc, lpl)


def _expert_kernel(start_ref, count_ref, xs_ref, wg_ref, bg_ref, wu_ref, bu_ref, wd_ref, bd_ref, ys_ref,
                   wbf, xbuf, ybuf, sem_in, sem_out):
    e = pl.program_id(0)
    row0 = start_ref[e]

    def tiles_of(ex):
        return lax.shift_right_logical(count_ref[ex] + (ROW_TILE - 1), ROW_TILE.bit_length() - 1)

    n_tiles = tiles_of(e)

    def in_copy_of(ex, j, slot):
        return pltpu.make_async_copy(_row_tiles(xs_ref, start_ref[ex] + j * ROW_TILE, ROW_TILE),
                                     xbuf.at[slot], sem_in.at[slot])

    def in_copy(j, slot):
        return in_copy_of(e, j, slot)

    def out_copy(j, slot):
        return pltpu.make_async_copy(ybuf.at[slot], _row_tiles(ys_ref, row0 + j * ROW_TILE, ROW_TILE),
                                     sem_out.at[slot])

    def start_first_tiles(ex):
        for j in range(RING - 1):
            @pl.when(tiles_of(ex) > j)
            def _():
                in_copy_of(ex, j, j).start()

    @pl.when(e == 0)
    def _():
        ybuf[0] = jnp.zeros((ROW_TILE * SUB, LANES), F32)
        tail = pltpu.make_async_copy(ybuf.at[0], _row_tiles(ys_ref, N_SORT, ROW_TILE), sem_out.at[0])
        tail.start()
        tail.wait()
        start_first_tiles(0)

    @pl.when(n_tiles > 0)
    def _():
        wbf[0] = wg_ref[0].astype(BF16)
        wbf[1] = wu_ref[0].astype(BF16)
        wbf[2] = wd_ref[0].astype(BF16)

        def tile(j, carry):
            slot = j & (RING - 1)

            ahead = j + (RING - 1)

            @pl.when(ahead < n_tiles)
            def _():
                in_copy(ahead, ahead & (RING - 1)).start()

            in_copy(j, slot).wait()

            @pl.when(j >= RING)
            def _():
                out_copy(j - RING, slot).wait()

            xb = _load_rows(xbuf.at[slot], ROW_TILE).astype(BF16)
            gt = jnp.minimum(_dot(xb, wbf[0]) + bg_ref[0], SWIGLU_LIMIT)
            up = jnp.clip(_dot(xb, wbf[1]) + bu_ref[0], -SWIGLU_LIMIT, SWIGLU_LIMIT)
            act = gt * jax.nn.sigmoid(SWIGLU_ALPHA * gt) * (up + 1.0)
            _store_rows(ybuf.at[slot], _dot(act.astype(BF16), wbf[2]) + bd_ref[0], ROW_TILE)
            out_copy(j, slot).start()
            return carry

        lax.fori_loop(0, n_tiles, tile, 0)

        for back in range(RING, 0, -1):
            @pl.when(n_tiles >= back)
            def _():
                out_copy(n_tiles - back, (n_tiles - back) & (RING - 1)).wait()

    @pl.when(e + 1 < N_EXP)
    def _():
        start_first_tiles(e + 1)


def _experts(start, count, xs, wg, bg, wu, bu, wd, bd):
    w_idx = lambda e, *_: (e, 0, 0)
    grid_spec = pltpu.PrefetchScalarGridSpec(
        num_scalar_prefetch=2,
        grid=(N_EXP,),
        in_specs=[
            pl.BlockSpec(memory_space=pl.ANY),
            pl.BlockSpec((1, D, D_EXP), w_idx),
            pl.BlockSpec((1, 1, D_EXP), w_idx),
            pl.BlockSpec((1, D, D_EXP), w_idx),
            pl.BlockSpec((1, 1, D_EXP), w_idx),
            pl.BlockSpec((1, D_EXP, D), w_idx),
            pl.BlockSpec((1, 1, D), w_idx),
        ],
        out_specs=pl.BlockSpec(memory_space=pl.ANY),
        scratch_shapes=[pltpu.VMEM((3, D, D_EXP), BF16),
                        pltpu.VMEM((RING, ROW_TILE * SUB, LANES), F32),
                        pltpu.VMEM((RING, ROW_TILE * SUB, LANES), F32),
                        pltpu.SemaphoreType.DMA((RING,)),
                        pltpu.SemaphoreType.DMA((RING,))],
    )
    return pl.pallas_call(
        _expert_kernel, grid_spec=grid_spec,
        out_shape=jax.ShapeDtypeStruct((N_SORT_PAD * SUB, LANES), F32),
        compiler_params=_params(("arbitrary",)),
        name="experts",
    )(start, count, xs, wg, bg, wu, bu, wd, bd)


def _combine_kernel(cnt_ref, loc_ref, glo_ref, lp_ref, topw_ref, x1_ref, mod_ref, g_ref, b_ref,
                    ys_ref, o_ref, buf, sem, *, blk0, n_blk, blocks_per_seq, mod_row0, mod_row_step):
    i = pl.program_id(0)
    slot = i & 1
    gate2 = _mod_row(mod_ref, mod_row0, mod_row_step, i // blocks_per_seq)[5]

    def start_runs(step, s):
        _start_piece_copies(cnt_ref, loc_ref, glo_ref, blk0 + step, buf.at[s], ys_ref, sem.at[s], False)

    @pl.when(i == 0)
    def _():
        start_runs(0, 0)

    @pl.when(i + 1 < n_blk)
    def _():
        start_runs(i + 1, 1 - slot)

    pltpu.make_async_copy(_row_tiles(ys_ref, 0, BLK_ROWS), buf.at[slot], sem.at[slot]).wait()

    y_sorted = _load_rows(buf.at[slot], BLK_ROWS).astype(BF16)
    col = lax.broadcasted_iota(jnp.int32, (TB, BLK_ROWS), 1)
    lp = lp_ref[...]
    topw = topw_ref[...]
    weights = jnp.zeros((TB, BLK_ROWS), F32)
    for k in range(TOPK):
        weights = weights + jnp.where(col == lp[:, k:k + 1], topw[:, k:k + 1], 0.0)
    ffn = _dot(weights.astype(BF16), y_sorted)
    y = DN_ALPHA * x1_ref[...] + gate2 * ffn
    o_ref[...] = _norm_noaffine(y) * g_ref[...] + b_ref[...]


def _combine(lstart, blen, gstart, lpos, topw, x1, mod, ln2g, ln2b, ys, *, blk0, n_seq, seq_len,
             mod_row0, mod_row_step):
    bps = seq_len // TB
    n_blk = n_seq * bps
    tok = lambda i, *_: (i, 0)
    const = lambda i, *_: (0, 0)
    grid_spec = pltpu.PrefetchScalarGridSpec(
        num_scalar_prefetch=3,
        grid=(n_blk,),
        in_specs=[
            pl.BlockSpec((TB, LANES), tok),
            pl.BlockSpec((TB, LANES), tok),
            pl.BlockSpec((TB, D), tok),
            pl.BlockSpec((8, 6 * D), const),
            pl.BlockSpec((1, D), const),
            pl.BlockSpec((1, D), const),
            pl.BlockSpec(memory_space=pl.ANY),
        ],
        out_specs=pl.BlockSpec((TB, D), tok),
        scratch_shapes=[pltpu.VMEM((2, BLK_ROWS * SUB, LANES), F32), pltpu.SemaphoreType.DMA((2,))],
    )
    kern = functools.partial(_combine_kernel, blk0=blk0, n_blk=n_blk, blocks_per_seq=bps,
                             mod_row0=mod_row0, mod_row_step=mod_row_step)
    return pl.pallas_call(
        kern, grid_spec=grid_spec,
        out_shape=jax.ShapeDtypeStruct((n_seq * seq_len, D), F32),
        compiler_params=_params(("arbitrary",)),
        name="combine_lat" if blk0 else "combine_ctx",
    )(lstart, blen, gstart, lpos, topw, x1, mod, ln2g, ln2b, ys)


def _dft_tables():
    def cs(n):
        k = np.arange(n, dtype=np.int64)
        ang = 2.0 * np.pi * ((k[:, None] * k[None, :]) % n).astype(np.float64) / n
        return np.cos(ang) / math.sqrt(n), np.sin(ang) / math.sqrt(n)

    c, s = cs(FG)
    cdft = np.concatenate([c, s], axis=1).astype(np.float32)
    pd = {}
    for t in (T_CTX, T_LAT):
        c, s = cs(t)
        pd[t] = np.concatenate([c, -s], axis=1).astype(np.float32)
    tri = np.tril(np.ones((TB, TB), np.float32), k=-1)
    upper = np.triu(np.ones((LANES, LANES), np.float32), k=1)
    return cdft, pd, tri, upper


_ROT_PERM = np.array(list(range(8, 16)) + list(range(0, 8)) + list(range(24, 32)) + list(range(16, 24)))
_ROT_SIGN = np.array([-1.0] * 8 + [1.0] * 8 + [-1.0] * 8 + [1.0] * 8, np.float32)


def _rope_tables():
    rows = T_LAT // GRID_W
    row = jnp.repeat(jnp.arange(rows, dtype=F32), GRID_W)
    col = jnp.tile(jnp.arange(GRID_W, dtype=F32), rows)
    axis_dim = ROPE // 2
    inv_freq = ROPE_THETA ** (-jnp.arange(0, axis_dim, 2, dtype=F32) / axis_dim)
    ang_r = row[:, None] * inv_freq[None, :]
    ang_c = col[:, None] * inv_freq[None, :]
    ang = jnp.concatenate([ang_r, ang_r, ang_c, ang_c], axis=-1)
    pad = ((0, 0), (0, LANES - ROPE))
    return jnp.pad(jnp.cos(ang), pad), jnp.pad(jnp.sin(ang), pad)


def kernel(x_prompt, x_sample, cache_ckv, cache_krope, c, c_ctx, w_mod, b_mod, w_in, q_norm_g, w_q_up,
           kv_norm_g, w_kv_up, w_fourier_o, w_mla_o, w_gate, b_gate, w_out, ln1_g, ln1_b, w_router,
           b_router, w_gate_e, b_gate_e, w_up_e, b_up_e, w_down_e, b_down_e, ln2_g, ln2_b):
    cdft_np, pdft_np, tri_np, upper_np = _dft_tables()
    upper = jnp.asarray(upper_np)
    cdft = jnp.asarray(cdft_np).astype(BF16)
    pdft_ctx = jnp.asarray(pdft_np[T_CTX]).astype(BF16)
    pdft_lat = jnp.asarray(pdft_np[T_LAT]).astype(BF16)
    tri = jnp.asarray(tri_np).astype(BF16)
    cos_pad, sin_pad = _rope_tables()

    w_in0 = w_in[0]
    kr_w = w_in0[:, FW + QL + KVL:]
    lane_pad = ((0, 0), (0, LANES - ROPE))
    w_in_ext = jnp.concatenate(
        [w_in0[:, :FW + QL + KVL], jnp.pad(kr_w, lane_pad),
         jnp.pad(kr_w[:, _ROT_PERM] * _ROT_SIGN, lane_pad)], axis=1).astype(BF16)
    wq3 = w_q_up[0].reshape(QL, HEADS, NOPE + ROPE)
    wq_nope = jnp.transpose(wq3[:, :, :NOPE], (1, 0, 2))
    wq_rope = wq3[:, :, NOPE:]
    head_pad = ((0, 0), (0, 0), (0, LANES - ROPE))
    wqr = jnp.concatenate(
        [jnp.pad(wq_rope, head_pad).reshape(QL, HEADS * LANES),
         jnp.pad(wq_rope[:, :, _ROT_PERM] * _ROT_SIGN, head_pad).reshape(QL, HEADS * LANES)],
        axis=1).astype(BF16)
    wkv3 = w_kv_up[0].reshape(KVL, HEADS, NOPE + VH)
    wk = jnp.transpose(wkv3[:, :, :NOPE], (1, 0, 2))
    wv = jnp.transpose(wkv3[:, :, NOPE:], (1, 0, 2))
    wo3 = w_mla_o[0].reshape(HEADS, VH, D)
    wqa, woa = _prep(wq_nope, wk, wv, wo3)

    c_all = jnp.concatenate([c_ctx[None, :], c, jnp.zeros((8 - 1 - N_LAT_SEQ, D), F32)], axis=0)
    mod = _modulation(c_all, w_mod[0], b_mod)

    qg = q_norm_g
    kvg = kv_norm_g
    wfo = w_fourier_o[0].astype(BF16)
    wg = w_gate[0].astype(BF16)
    wout = w_out[0].astype(BF16)
    wr_f32 = jnp.pad(w_router[0], ((0, 0), (0, LANES - N_EXP)))
    wr_hi = wr_f32.astype(BF16)
    wr = jnp.concatenate([wr_hi, (wr_f32 - wr_hi.astype(F32)).astype(BF16)], axis=1)
    br = jnp.pad(b_router, ((0, 0), (0, LANES - N_EXP)), constant_values=NEG_BIG)
    cache_k = jnp.concatenate(
        [cache_ckv[:, 0], jnp.pad(cache_krope[:, 0], ((0, 0), (0, 0), (0, LANES - ROPE)))],
        axis=-1).astype(BF16)

    xc2d = x_prompt.reshape(N_CTX, D)
    xl2d = x_sample.reshape(N_LAT, D)

    fcs_c, qcat_c, kcat_c, new_ckv, new_krope = _pre(
        xc2d, mod, w_in_ext, cdft, qg, kvg, wqa, wqr, None, None,
        n_seq=N_CTX_SEQ, seq_len=T_CTX, rope=False, mod_row0=0, mod_row_step=0)
    x1_c, h2_c, lpos_c, topw_c, bcnt_c = _mix(
        xc2d, mod, fcs_c, pdft_ctx, qcat_c, kcat_c, None, wfo, woa, wg, b_gate, wout, ln1_g, ln1_b,
        wr, br, tri, upper, rows=MIX_ROWS_CTX, n_seq=N_CTX_SEQ, seq_len=T_CTX, mod_row0=0,
        mod_row_step=0)

    fcs_l, qcat_l, kcat_l = _pre(
        xl2d, mod, w_in_ext, cdft, qg, kvg, wqa, wqr, cos_pad, sin_pad,
        n_seq=N_LAT_SEQ, seq_len=T_LAT, rope=True, mod_row0=1, mod_row_step=1)
    x1_l, h2_l, lpos_l, topw_l, bcnt_l = _mix(
        xl2d, mod, fcs_l, pdft_lat, qcat_l, kcat_l, cache_k, wfo, woa, wg, b_gate, wout, ln1_g,
        ln1_b, wr, br, tri, upper, rows=MIX_ROWS_LAT, n_seq=N_LAT_SEQ, seq_len=T_LAT, mod_row0=1,
        mod_row_step=1)

    blen = jnp.concatenate([bcnt_c[:, 0, :N_EXP], bcnt_l[:, 0, :N_EXP]], axis=0).astype(jnp.int32)
    lstart = jnp.cumsum(blen, axis=1) - blen
    count = jnp.sum(blen, axis=0)
    start = jnp.cumsum(count) - count
    gstart = start[None, :] + jnp.cumsum(blen, axis=0) - blen
    pieces = _piece_tables(lstart, blen, gstart)
    start = start.astype(jnp.int32)
    count = count.astype(jnp.int32)

    xs = _dispatch(*pieces, h2_c, h2_l, lpos_c, lpos_l)
    ys = _experts(start, count, xs,
                  w_gate_e[0], b_gate_e[0][:, None, :], w_up_e[0], b_up_e[0][:, None, :],
                  w_down_e[0], b_down_e[0][:, None, :])

    y_c = _combine(*pieces, lpos_c, topw_c, x1_c, mod, ln2_g, ln2_b, ys, blk0=0,
                   n_seq=N_CTX_SEQ, seq_len=T_CTX, mod_row0=0, mod_row_step=0)
    y_l = _combine(*pieces, lpos_l, topw_l, x1_l, mod, ln2_g, ln2_b, ys, blk0=N_CTX_BLK,
                   n_seq=N_LAT_SEQ, seq_len=T_LAT, mod_row0=1, mod_row_step=1)
    return (y_c.reshape(N_CTX_SEQ, T_CTX, D), y_l.reshape(N_LAT_SEQ, T_LAT, D), new_ckv, new_krope)
```

```python
import functools
import math

import numpy as np
import jax
import jax.numpy as jnp
from jax import lax
from jax.experimental import pallas as pl
from jax.experimental.pallas import tpu as pltpu

D = 1024
N_CTX_SEQ, T_CTX = 32, 256
N_LAT_SEQ, T_LAT = 4, 1024
PAST = 256
N_CTX = N_CTX_SEQ * T_CTX
N_LAT = N_LAT_SEQ * T_LAT
N_TOK = N_CTX + N_LAT
FW = 512
FG = 128
N_FG = FW // FG
HEADS = 8
QL = 256
KVL = 128
NOPE = 64
ROPE = 32
VH = 64
N_EXP = 32
TOPK = 4
D_EXP = 1024
SWIGLU_LIMIT = 7.0
SWIGLU_ALPHA = 1.702
LN_EPS = 1e-5
RMS_EPS = 1e-6
DN_ALPHA = 2.0 ** 0.25
ATT_SCALE = float(NOPE + ROPE) ** -0.5
ROPE_THETA = 10000.0
GRID_W = 64

LANES = 128
TB = 256
PRE_ROWS = 512
MIX_ROWS_CTX = 512
MIX_ROWS_LAT = 512
SUB = 8
ROW_TILE = 256
TILE_PAIRS = ROW_TILE // 2
RING = 4
N_CTX_BLK = N_CTX // TB
N_BLK = N_TOK // TB
BLK_PAIRS = TB * TOPK // 2
PAD_PAIRS_MAX = N_EXP // 2
BLK_PAIRS_BUF = BLK_PAIRS + 64
BLK_ROWS_BUF = 2 * BLK_PAIRS_BUF
NP_SORT = N_TOK * TOPK // 2
NP_MAX = NP_SORT + N_BLK * PAD_PAIRS_MAX
NP_PAD = NP_MAX + TILE_PAIRS
RUN_BITS = (TB // 2).bit_length()
EXTRA_BITS = PAD_PAIRS_MAX.bit_length()
U32 = jnp.uint32
VMEM_LIMIT = 56 * 1024 * 1024
NEG_BIG = -1e30

F32 = jnp.float32
BF16 = jnp.bfloat16
HIGHEST = lax.Precision.HIGHEST


def _dot(a, b):
    return jnp.dot(a, b, preferred_element_type=F32)


def _dot_nt(a, b):
    return lax.dot_general(a, b, (((1,), (1,)), ((), ())), preferred_element_type=F32)


def _params(sem):
    return pltpu.CompilerParams(dimension_semantics=sem, vmem_limit_bytes=VMEM_LIMIT)


def _full(shape):
    n = len(shape)
    return pl.BlockSpec(shape, lambda *_: (0,) * n, pipeline_mode=pl.Buffered(1))


def _norm_noaffine(x):
    mu = jnp.mean(x, axis=-1, keepdims=True)
    xc = x - mu
    var = jnp.mean(xc * xc, axis=-1, keepdims=True)
    return xc * lax.rsqrt(var + LN_EPS)


def _rms(x, g):
    return x * lax.rsqrt(jnp.mean(x * x, axis=-1, keepdims=True) + RMS_EPS) * g


def _mod_row(mod_ref, row0, step, seq):
    row = row0 if step == 0 else row0 + step * seq
    m = mod_ref[pl.ds(row, 1), :]
    return [m[:, i * D:(i + 1) * D] for i in range(6)]


def _prep_kernel(wqn_ref, wk_ref, wv_ref, wo_ref, qabs_ref, oabs_ref):
    qabs = lax.dot_general(wqn_ref[0], wk_ref[0], (((1,), (1,)), ((), ())),
                           preferred_element_type=F32, precision=HIGHEST)
    qabs_ref[...] = qabs.astype(BF16)
    oabs = jnp.dot(wv_ref[0], wo_ref[0], preferred_element_type=F32, precision=HIGHEST)
    oabs_ref[...] = oabs.astype(BF16)


def _prep(wqn, wk, wv, wo):
    return pl.pallas_call(
        _prep_kernel,
        grid=(HEADS,),
        in_specs=[
            pl.BlockSpec((1, QL, NOPE), lambda h: (h, 0, 0)),
            pl.BlockSpec((1, KVL, NOPE), lambda h: (h, 0, 0)),
            pl.BlockSpec((1, KVL, VH), lambda h: (h, 0, 0)),
            pl.BlockSpec((1, VH, D), lambda h: (h, 0, 0)),
        ],
        out_specs=[
            pl.BlockSpec((QL, KVL), lambda h: (0, h)),
            pl.BlockSpec((KVL, D), lambda h: (h, 0)),
        ],
        out_shape=[jax.ShapeDtypeStruct((QL, HEADS * KVL), BF16),
                   jax.ShapeDtypeStruct((HEADS * KVL, D), BF16)],
        compiler_params=_params(("arbitrary",)),
        name="weight_prep",
    )(wqn, wk, wv, wo)


MOD_COLS = 1536


def _mod_kernel(c_ref, w_ref, b_ref, o_ref):
    c = c_ref[...]
    s = c * jax.nn.sigmoid(c)
    o_ref[...] = jnp.dot(s, w_ref[...], preferred_element_type=F32, precision=HIGHEST) + b_ref[...]


def _modulation(c_all, w_mod, b_mod):
    return pl.pallas_call(
        _mod_kernel,
        grid=(6 * D // MOD_COLS,),
        in_specs=[
            pl.BlockSpec((8, D), lambda i: (0, 0)),
            pl.BlockSpec((D, MOD_COLS), lambda i: (0, i)),
            pl.BlockSpec((1, MOD_COLS), lambda i: (0, i)),
        ],
        out_specs=pl.BlockSpec((8, MOD_COLS), lambda i: (0, i)),
        out_shape=jax.ShapeDtypeStruct((8, 6 * D), F32),
        compiler_params=_params(("arbitrary",)),
        name="modulation",
    )(c_all, w_mod, b_mod)


def _pre_kernel(*refs, rows, seq_len, seqs, rope, mod_row0, mod_row_step):
    if rope:
        (x_ref, mod_ref, win_ref, cdft_ref, qg_ref, kvg_ref, wqa_ref, wqr_ref, cos_ref, sin_ref,
         fcs_ref, qcat_ref, kcat_ref) = refs
    else:
        (x_ref, mod_ref, win_ref, cdft_ref, qg_ref, kvg_ref, wqa_ref, wqr_ref,
         fcs_ref, qcat_ref, kcat_ref, ckv_ref, krope_ref) = refs
    q_rows = rows // seqs
    b = pl.program_id(0) // (seq_len // q_rows)
    shift1, scale1 = _mod_row(mod_ref, mod_row0, mod_row_step, b)[:2]
    seq_rows = [slice(s * q_rows, (s + 1) * q_rows) for s in range(seqs)]

    x = x_ref[...]
    h1 = _norm_noaffine(x) * (1.0 + scale1) + shift1
    h1b = h1.astype(BF16)
    n_cols = 9 * LANES if rope else 8 * LANES
    proj = _dot(h1b, win_ref[:, :n_cols])

    f_b = proj[:, :FW].astype(BF16)
    for g in range(N_FG):
        r = _dot(f_b[:, g * FG:(g + 1) * FG], cdft_ref[...])
        for s, sl in enumerate(seq_rows):
            fcs_ref[s, 0, :, g * FG:(g + 1) * FG] = r[sl, :FG].astype(BF16)
            fcs_ref[s, 1, :, g * FG:(g + 1) * FG] = r[sl, FG:].astype(BF16)

    qn = _rms(proj[:, FW:FW + QL], qg_ref[...]).astype(BF16)
    ckv = _rms(proj[:, FW + QL:FW + QL + KVL], kvg_ref[...])
    kr = proj[:, 7 * LANES:8 * LANES]

    qa = _dot(qn, wqa_ref[...]) * ATT_SCALE
    if rope:
        cos = cos_ref[...]
        sin = sin_ref[...]
        qr2 = _dot(qn, wqr_ref[...])
        kr_keys = kr * cos + proj[:, 8 * LANES:9 * LANES] * sin
    else:
        qr2 = _dot(qn, wqr_ref[:, :HEADS * LANES])
        kr_keys = kr
    ckv_b = ckv.astype(BF16)
    kr_b = kr_keys.astype(BF16)
    for h in range(HEADS):
        qr_h = qr2[:, h * LANES:(h + 1) * LANES]
        if rope:
            qr_h = qr_h * cos + qr2[:, (HEADS + h) * LANES:(HEADS + h + 1) * LANES] * sin
        qa_h = qa[:, h * KVL:(h + 1) * KVL].astype(BF16)
        qr_h = (qr_h * ATT_SCALE).astype(BF16)
        for s, sl in enumerate(seq_rows):
            qcat_ref[s, h, :, :KVL] = qa_h[sl]
            qcat_ref[s, h, :, KVL:] = qr_h[sl]
    for s, sl in enumerate(seq_rows):
        kcat_ref[s, :, :KVL] = ckv_b[sl]
        kcat_ref[s, :, KVL:] = kr_b[sl]
        if not rope:
            ckv_ref[s, 0] = ckv[sl]
            krope_ref[s, 0] = kr[sl, :ROPE]


def _pre(x2d, mod, w_in_ext, cdft, qg, kvg, wqa, wqr, cos_pad, sin_pad, *, n_seq, seq_len, rope,
         mod_row0, mod_row_step):
    rows = PRE_ROWS
    seqs = max(1, rows // seq_len)
    q_rows = rows // seqs
    bps = seq_len // q_rows
    n_blk = n_seq * seq_len // rows
    seq_blk = lambda i: (i // bps, 0, i % bps, 0)
    in_specs = [
        pl.BlockSpec((rows, D), lambda i: (i, 0)),
        _full((8, 6 * D)),
        _full((D, 9 * LANES)),
        _full((FG, 2 * FG)),
        _full((1, QL)),
        _full((1, KVL)),
        _full((QL, HEADS * KVL)),
        _full((QL, 2 * HEADS * LANES)),
    ]
    args = [x2d, mod, w_in_ext, cdft, qg, kvg, wqa, wqr]
    out_specs = [
        pl.BlockSpec((seqs, 2, q_rows, FW), seq_blk),
        pl.BlockSpec((seqs, HEADS, q_rows, 2 * LANES), seq_blk),
        pl.BlockSpec((seqs, q_rows, 2 * LANES), lambda i: (i // bps, i % bps, 0)),
    ]
    out_shape = [
        jax.ShapeDtypeStruct((n_seq, 2, seq_len, FW), BF16),
        jax.ShapeDtypeStruct((n_seq, HEADS, seq_len, 2 * LANES), BF16),
        jax.ShapeDtypeStruct((n_seq, seq_len, 2 * LANES), BF16),
    ]
    if rope:
        in_specs += [pl.BlockSpec((q_rows, LANES), lambda i: (i % bps, 0)),
                     pl.BlockSpec((q_rows, LANES), lambda i: (i % bps, 0))]
        args += [cos_pad, sin_pad]
    else:
        out_specs += [pl.BlockSpec((seqs, 1, q_rows, KVL), seq_blk),
                      pl.BlockSpec((seqs, 1, q_rows, ROPE), seq_blk)]
        out_shape += [jax.ShapeDtypeStruct((n_seq, 1, seq_len, KVL), F32),
                      jax.ShapeDtypeStruct((n_seq, 1, seq_len, ROPE), F32)]
    kern = functools.partial(_pre_kernel, rows=rows, seq_len=seq_len, seqs=seqs, rope=rope,
                             mod_row0=mod_row0, mod_row_step=mod_row_step)
    return pl.pallas_call(
        kern, grid=(n_blk,), in_specs=in_specs, out_specs=out_specs, out_shape=out_shape,
        compiler_params=_params(("arbitrary",)),
        name="pre_lat" if rope else "pre_ctx",
    )(*args)


def _mix_kernel(*refs, rows, seq_len, seqs, cache, mod_row0, mod_row_step):
    if cache:
        (x_ref, mod_ref, fcs_ref, pdft_ref, qcat_ref, kcat_ref, cache_ref, wfo_ref, woa_ref,
         wg_ref, bg_ref, wout_ref, ln1g_ref, ln1b_ref, wr_ref, br_ref, tri_ref, upper_ref,
         x1_ref, h2_ref, lpos_ref, topw_ref, bcnt_ref) = refs
    else:
        (x_ref, mod_ref, fcs_ref, pdft_ref, qcat_ref, kcat_ref, wfo_ref, woa_ref,
         wg_ref, bg_ref, wout_ref, ln1g_ref, ln1b_ref, wr_ref, br_ref, tri_ref, upper_ref,
         x1_ref, h2_ref, lpos_ref, topw_ref, bcnt_ref) = refs
        cache_ref = None
    b = pl.program_id(0)
    shift1, scale1, gate1, shift2, scale2, _ = _mod_row(mod_ref, mod_row0, mod_row_step, b)

    x = x_ref[...]
    h1b = (_norm_noaffine(x) * (1.0 + scale1) + shift1).astype(BF16)

    mixed_rows, attn_rows = [], []
    for s in range(seqs):
        mixed_rows.append(_dot(pdft_ref[:, :seq_len], fcs_ref[s, 0])
                          + _dot(pdft_ref[:, seq_len:], fcs_ref[s, 1]))
        kc = kcat_ref[s]
        vals = kc[:, :KVL]
        if cache:
            kc2 = cache_ref[s]
            vals2 = kc2[:, :KVL]
        heads_out = []
        for h in range(HEADS):
            q = qcat_ref[s, h]
            s1 = _dot_nt(q, kc)
            m = jnp.max(s1, axis=-1, keepdims=True)
            if cache:
                s2 = _dot_nt(q, kc2)
                m = jnp.maximum(m, jnp.max(s2, axis=-1, keepdims=True))
            p1 = jnp.exp(s1 - m)
            l = jnp.sum(p1, axis=-1, keepdims=True)
            o = _dot(p1.astype(BF16), vals)
            if cache:
                p2 = jnp.exp(s2 - m)
                l = l + jnp.sum(p2, axis=-1, keepdims=True)
                o = o + _dot(p2.astype(BF16), vals2)
            heads_out.append((o / l).astype(BF16))
        attn_rows.append(jnp.concatenate(heads_out, axis=1))
    mixed = jnp.concatenate(mixed_rows, axis=0) if seqs > 1 else mixed_rows[0]
    attn = jnp.concatenate(attn_rows, axis=0) if seqs > 1 else attn_rows[0]
    f_out = _dot(mixed.astype(BF16), wfo_ref[...])
    m_out = _dot(attn, woa_ref[...])

    gates = jax.nn.sigmoid(_dot(h1b, wg_ref[...]) + bg_ref[...])
    merged = gates[:, :D] * f_out + gates[:, D:] * m_out
    mix = _dot(merged.astype(BF16), wout_ref[...])

    x1 = _norm_noaffine(DN_ALPHA * x + gate1 * mix) * ln1g_ref[...] + ln1b_ref[...]
    x1_ref[...] = x1
    h2 = _norm_noaffine(x1) * (1.0 + scale2) + shift2
    h2_ref[...] = h2.astype(BF16)

    h_hi = h2.astype(BF16)
    h_lo = (h2 - h_hi.astype(F32)).astype(BF16)
    hi_part = _dot(h_hi, wr_ref[...])
    logits_all = (hi_part[:, :LANES] + hi_part[:, LANES:] + _dot(h_lo, wr_ref[:, :LANES])) + br_ref[...]
    lane = lax.broadcasted_iota(jnp.int32, (TB, LANES), 1).astype(F32)
    for r in range(rows // TB):
        blk = slice(r * TB, (r + 1) * TB)
        work = logits_all[blk]
        top_v, top_i = [], []
        for _ in range(TOPK):
            mk = jnp.max(work, axis=-1, keepdims=True)
            ik = jnp.min(jnp.where(work == mk, lane, float(LANES)), axis=-1, keepdims=True)
            work = jnp.where(lane == ik, -jnp.inf, work)
            top_v.append(mk)
            top_i.append(ik)
        exps = [jnp.exp(v - top_v[0]) for v in top_v]
        denom = exps[0] + exps[1] + exps[2] + exps[3]

        onehot = jnp.zeros((TB, LANES), F32)
        for ik in top_i:
            onehot = onehot + jnp.where(lane == ik, 1.0, 0.0)
        counts = jnp.sum(onehot, axis=0, keepdims=True)
        padded = counts + (counts - 2.0 * jnp.floor(0.5 * counts))
        lower = jnp.dot(jnp.broadcast_to(padded, (8, LANES)), upper_ref[...],
                        preferred_element_type=F32, precision=HIGHEST)[0:1, :]
        before = _dot(tri_ref[...], onehot.astype(BF16)) + lower
        lpos = jnp.zeros((TB, LANES), jnp.int32)
        topw = jnp.zeros((TB, LANES), F32)
        for k in range(TOPK):
            pos = jnp.sum(jnp.where(lane == top_i[k], before, 0.0), axis=-1, keepdims=True)
            lpos = jnp.where(lane == float(k), pos.astype(jnp.int32), lpos)
            topw = jnp.where(lane == float(k), exps[k] / denom, topw)
        lpos_ref[blk, :] = lpos
        topw_ref[blk, :] = topw
        bcnt_ref[r] = counts


def _mix(x2d, mod, fcs, pdft, qcat, kcat, cache_k, wfo, woa, wg, bg, wout, ln1g, ln1b, wr, br,
         tri, upper, *, rows, n_seq, seq_len, mod_row0, mod_row_step):
    cache = cache_k is not None
    seqs = max(1, rows // seq_len)
    q_rows = rows // seqs
    bps = seq_len // q_rows
    n_outer = n_seq // seqs
    in_specs = [
        pl.BlockSpec((rows, D), lambda b, j: (b * bps + j, 0)),
        _full((8, 6 * D)),
        pl.BlockSpec((seqs, 2, seq_len, FW), lambda b, j: (b, 0, 0, 0)),
        pl.BlockSpec((q_rows, 2 * seq_len), lambda b, j: (j, 0)),
        pl.BlockSpec((seqs, HEADS, q_rows, 2 * LANES), lambda b, j: (b, 0, j, 0)),
        pl.BlockSpec((seqs, seq_len, 2 * LANES), lambda b, j: (b, 0, 0)),
    ]
    args = [x2d, mod, fcs, pdft, qcat, kcat]
    if cache:
        in_specs.append(pl.BlockSpec((seqs, PAST, 2 * LANES), lambda b, j: (b, 0, 0)))
        args.append(cache_k)
    in_specs += [
        _full((FW, D)), _full((HEADS * KVL, D)), _full((D, 2 * D)), _full((1, 2 * D)),
        _full((D, D)), _full((1, D)), _full((1, D)), _full((D, 2 * LANES)), _full((1, LANES)),
        _full((TB, TB)), _full((LANES, LANES)),
    ]
    args += [wfo, woa, wg, bg, wout, ln1g, ln1b, wr, br, tri, upper]
    n_tok = n_seq * seq_len
    tok_spec = lambda w: pl.BlockSpec((rows, w), lambda b, j: (b * bps + j, 0))
    out_specs = [tok_spec(D), tok_spec(D), tok_spec(LANES), tok_spec(LANES),
                 pl.BlockSpec((rows // TB, 1, LANES), lambda b, j: (b * bps + j, 0, 0))]
    out_shape = [
        jax.ShapeDtypeStruct((n_tok, D), F32),
        jax.ShapeDtypeStruct((n_tok, D), BF16),
        jax.ShapeDtypeStruct((n_tok, LANES), jnp.int32),
        jax.ShapeDtypeStruct((n_tok, LANES), F32),
        jax.ShapeDtypeStruct((n_tok // TB, 1, LANES), F32),
    ]
    kern = functools.partial(_mix_kernel, rows=rows, seq_len=seq_len, seqs=seqs, cache=cache,
                             mod_row0=mod_row0, mod_row_step=mod_row_step)
    return pl.pallas_call(
        kern, grid=(n_outer, bps), in_specs=in_specs, out_specs=out_specs, out_shape=out_shape,
        compiler_params=_params(("arbitrary", "arbitrary")),
        name="mix_lat" if cache else "mix_ctx",
    )(*args)


def _pair_tiles(ref, pair0, n_pairs):
    return ref.at[pl.ds(pl.multiple_of(pair0 * SUB, SUB), n_pairs * SUB), :]


def _piece_tables(lstart, blen, gstart):
    bits = jnp.arange(RUN_BITS, dtype=jnp.int32)
    n = blen[:, :, None]
    valid = (n >> bits) & 1
    done = (n >> (bits + 1)) << (bits + 1)
    rank = jnp.cumsum(valid, axis=1) - valid
    place = (valid[..., None] == 1) & (rank[..., None] == jnp.arange(N_EXP, dtype=jnp.int32))
    def table(first_row):
        rows = (first_row[:, :, None] + done) * SUB
        listed = jnp.sum(jnp.where(place, rows[..., None], 0), axis=1)
        return listed.reshape(-1).astype(jnp.int32)
    return jnp.sum(valid, axis=1).reshape(-1).astype(jnp.int32), table(lstart), table(gstart)


def _start_piece_copies(cnt_ref, loc_ref, glo_ref, blk, local_ref, global_ref, sem, to_global):
    for bit in range(RUN_BITS):
        size = (1 << bit) * SUB
        base = (blk * RUN_BITS + bit) * N_EXP

        def piece(t, carry):
            loc = local_ref.at[pl.ds(pl.multiple_of(loc_ref[base + t], SUB), size), :]
            glo = global_ref.at[pl.ds(pl.multiple_of(glo_ref[base + t], SUB), size), :]
            (pltpu.make_async_copy(loc, glo, sem) if to_global
             else pltpu.make_async_copy(glo, loc, sem)).start()
            return carry

        lax.fori_loop(0, cnt_ref[blk * RUN_BITS + bit], piece, 0)


def _wait_block_pieces(extra_ref, blk, vmem_ref, hbm_ref, sem, to_global):
    def wait(n_pairs):
        loc = _pair_tiles(vmem_ref, 0, n_pairs)
        glo = _pair_tiles(hbm_ref, 0, n_pairs)
        (pltpu.make_async_copy(loc, glo, sem) if to_global else pltpu.make_async_copy(glo, loc, sem)).wait()

    wait(BLK_PAIRS)
    extra = extra_ref[blk]
    for bit in range(EXTRA_BITS):
        @pl.when((extra & (1 << bit)) != 0)
        def _():
            wait(1 << bit)


def _load_pairs(ref, n_pairs):
    words = jnp.concatenate([ref[pl.ds(s, n_pairs, stride=SUB), :] for s in range(SUB)], axis=1)
    return pltpu.bitcast(words, BF16)


def _store_pairs(ref, rows, n_pairs):
    words = pltpu.bitcast(rows, U32)
    for s in range(SUB):
        ref[pl.ds(s, n_pairs, stride=SUB), :] = words[:, s * LANES:(s + 1) * LANES]


def _zero_fill_tail(zero_ref, hbm_ref, sem):
    zero_ref[...] = jnp.zeros(zero_ref.shape, U32)
    copies = [pltpu.make_async_copy(zero_ref, _pair_tiles(hbm_ref, p, TILE_PAIRS), sem)
              for p in range(NP_SORT, NP_PAD, TILE_PAIRS)]
    for cp in copies:
        cp.start()
    for cp in copies:
        cp.wait()


def _dispatch_kernel(cnt_ref, loc_ref, glo_ref, extra_ref, h2c_ref, h2l_ref, lpc_ref, lpl_ref, xs_ref,
                     buf, zbuf, sem, zsem):
    b = pl.program_id(0)
    slot = b & 1

    def start_runs(blk, s):
        _start_piece_copies(cnt_ref, loc_ref, glo_ref, blk, buf.at[s], xs_ref, sem.at[s], True)

    def wait_runs(blk, s):
        _wait_block_pieces(extra_ref, blk, buf.at[s], xs_ref, sem.at[s], True)

    @pl.when(b == 0)
    def _():
        _zero_fill_tail(zbuf, xs_ref, zsem)

    @pl.when(b >= 2)
    def _():
        wait_runs(b - 2, slot)

    def sort_block(h2_ref, lp_ref):
        col = lax.broadcasted_iota(jnp.int32, (TB, BLK_ROWS_BUF), 1)
        lp = lp_ref[...]
        pick = jnp.zeros((TB, BLK_ROWS_BUF), F32)
        for k in range(TOPK):
            pick = pick + jnp.where(col == lp[:, k:k + 1], 1.0, 0.0)
        sorted_rows = lax.dot_general(pick.astype(BF16), h2_ref[...],
                                      (((0,), (0,)), ((), ())), preferred_element_type=F32)
        _store_pairs(buf.at[slot], sorted_rows.astype(BF16), BLK_PAIRS_BUF)

    @pl.when(b < N_CTX_BLK)
    def _():
        sort_block(h2c_ref, lpc_ref)

    @pl.when(b >= N_CTX_BLK)
    def _():
        sort_block(h2l_ref, lpl_ref)

    start_runs(b, slot)

    @pl.when(b == N_BLK - 1)
    def _():
        wait_runs(b - 1, 1 - slot)
        wait_runs(b, slot)


def _dispatch(cnt, loc, glo, extra, h2c, h2l, lpc, lpl):
    ctx_idx = lambda i, *_: (jnp.minimum(i, N_CTX_BLK - 1), 0)
    lat_idx = lambda i, *_: (jnp.maximum(i - N_CTX_BLK, 0), 0)
    grid_spec = pltpu.PrefetchScalarGridSpec(
        num_scalar_prefetch=4,
        grid=(N_BLK,),
        in_specs=[
            pl.BlockSpec((TB, D), ctx_idx),
            pl.BlockSpec((TB, D), lat_idx),
            pl.BlockSpec((TB, LANES), ctx_idx),
            pl.BlockSpec((TB, LANES), lat_idx),
        ],
        out_specs=pl.BlockSpec(memory_space=pl.ANY),
        scratch_shapes=[pltpu.VMEM((2, BLK_PAIRS_BUF * SUB, LANES), U32),
                        pltpu.VMEM((TILE_PAIRS * SUB, LANES), U32),
                        pltpu.SemaphoreType.DMA((2,)),
                        pltpu.SemaphoreType.DMA(())],
    )
    return pl.pallas_call(
        _dispatch_kernel, grid_spec=grid_spec,
        out_shape=jax.ShapeDtypeStruct((NP_PAD * SUB, LANES), U32),
        compiler_params=_params(("arbitrary",)),
        name="dispatch",
    )(cnt, loc, glo, extra, h2c, h2l, lpc, lpl)


def _expert_kernel(start_ref, count_ref, xs_ref, wg_ref, bg_ref, wu_ref, bu_ref, wd_ref, bd_ref, ys_ref,
                   wbf, xbuf, ybuf, sem_in, sem_out):
    e = pl.program_id(0)
    pair0 = start_ref[e]

    def tiles_of(ex):
        return lax.shift_right_logical(count_ref[ex] + (TILE_PAIRS - 1), TILE_PAIRS.bit_length() - 1)

    n_tiles = tiles_of(e)

    def in_copy_of(ex, j, slot):
        return pltpu.make_async_copy(_pair_tiles(xs_ref, start_ref[ex] + j * TILE_PAIRS, TILE_PAIRS),
                                     xbuf.at[slot], sem_in.at[slot])

    def in_copy(j, slot):
        return in_copy_of(e, j, slot)

    def out_copy(j, slot):
        return pltpu.make_async_copy(ybuf.at[slot], _pair_tiles(ys_ref, pair0 + j * TILE_PAIRS, TILE_PAIRS),
                                     sem_out.at[slot])

    def start_first_tiles(ex):
        for j in range(RING - 1):
            @pl.when(tiles_of(ex) > j)
            def _():
                in_copy_of(ex, j, j).start()

    @pl.when(e == 0)
    def _():
        _zero_fill_tail(ybuf.at[0], ys_ref, sem_out.at[0])
        start_first_tiles(0)

    @pl.when(n_tiles > 0)
    def _():
        wbf[0] = wg_ref[0].astype(BF16)
        wbf[1] = wu_ref[0].astype(BF16)
        wbf[2] = wd_ref[0].astype(BF16)

        def tile(j, carry):
            slot = j & (RING - 1)

            ahead = j + (RING - 1)

            @pl.when(ahead < n_tiles)
            def _():
                in_copy(ahead, ahead & (RING - 1)).start()

            in_copy(j, slot).wait()

            @pl.when(j >= RING)
            def _():
                out_copy(j - RING, slot).wait()

            xb = _load_pairs(xbuf.at[slot], TILE_PAIRS)
            gt = jnp.minimum(_dot(xb, wbf[0]) + bg_ref[0], SWIGLU_LIMIT)
            up = jnp.clip(_dot(xb, wbf[1]) + bu_ref[0], -SWIGLU_LIMIT, SWIGLU_LIMIT)
            act = gt * jax.nn.sigmoid(SWIGLU_ALPHA * gt) * (up + 1.0)
            y = _dot(act.astype(BF16), wbf[2]) + bd_ref[0]
            _store_pairs(ybuf.at[slot], y.astype(BF16), TILE_PAIRS)
            out_copy(j, slot).start()
            return carry

        lax.fori_loop(0, n_tiles, tile, 0)

        for back in range(RING, 0, -1):
            @pl.when(n_tiles >= back)
            def _():
                out_copy(n_tiles - back, (n_tiles - back) & (RING - 1)).wait()

    @pl.when(e + 1 < N_EXP)
    def _():
        start_first_tiles(e + 1)


def _experts(start, count, xs, wg, bg, wu, bu, wd, bd):
    w_idx = lambda e, *_: (e, 0, 0)
    grid_spec = pltpu.PrefetchScalarGridSpec(
        num_scalar_prefetch=2,
        grid=(N_EXP,),
        in_specs=[
            pl.BlockSpec(memory_space=pl.ANY),
            pl.BlockSpec((1, D, D_EXP), w_idx),
            pl.BlockSpec((1, 1, D_EXP), w_idx),
            pl.BlockSpec((1, D, D_EXP), w_idx),
            pl.BlockSpec((1, 1, D_EXP), w_idx),
            pl.BlockSpec((1, D_EXP, D), w_idx),
            pl.BlockSpec((1, 1, D), w_idx),
        ],
        out_specs=pl.BlockSpec(memory_space=pl.ANY),
        scratch_shapes=[pltpu.VMEM((3, D, D_EXP), BF16),
                        pltpu.VMEM((RING, TILE_PAIRS * SUB, LANES), U32),
                        pltpu.VMEM((RING, TILE_PAIRS * SUB, LANES), U32),
                        pltpu.SemaphoreType.DMA((RING,)),
                        pltpu.SemaphoreType.DMA((RING,))],
    )
    return pl.pallas_call(
        _expert_kernel, grid_spec=grid_spec,
        out_shape=jax.ShapeDtypeStruct((NP_PAD * SUB, LANES), U32),
        compiler_params=_params(("arbitrary",)),
        name="experts",
    )(start, count, xs, wg, bg, wu, bu, wd, bd)


def _combine_kernel(cnt_ref, loc_ref, glo_ref, extra_ref, lp_ref, topw_ref, x1_ref, mod_ref, g_ref, b_ref,
                    ys_ref, o_ref, buf, sem, *, blk0, n_blk, blocks_per_seq, mod_row0, mod_row_step):
    i = pl.program_id(0)
    slot = i & 1
    gate2 = _mod_row(mod_ref, mod_row0, mod_row_step, i // blocks_per_seq)[5]

    def start_runs(step, s):
        _start_piece_copies(cnt_ref, loc_ref, glo_ref, blk0 + step, buf.at[s], ys_ref, sem.at[s], False)

    @pl.when(i == 0)
    def _():
        buf[...] = jnp.zeros(buf.shape, U32)
        start_runs(0, 0)

    @pl.when(i + 1 < n_blk)
    def _():
        start_runs(i + 1, 1 - slot)

    _wait_block_pieces(extra_ref, blk0 + i, buf.at[slot], ys_ref, sem.at[slot], False)

    y_sorted = _load_pairs(buf.at[slot], BLK_PAIRS_BUF)
    col = lax.broadcasted_iota(jnp.int32, (TB, BLK_ROWS_BUF), 1)
    lp = lp_ref[...]
    topw = topw_ref[...]
    weights = jnp.zeros((TB, BLK_ROWS_BUF), F32)
    for k in range(TOPK):
        weights = weights + jnp.where(col == lp[:, k:k + 1], topw[:, k:k + 1], 0.0)
    ffn = _dot(weights.astype(BF16), y_sorted)
    y = DN_ALPHA * x1_ref[...] + gate2 * ffn
    o_ref[...] = _norm_noaffine(y) * g_ref[...] + b_ref[...]


def _combine(cnt, loc, glo, extra, lpos, topw, x1, mod, ln2g, ln2b, ys, *, blk0, n_seq, seq_len,
             mod_row0, mod_row_step):
    bps = seq_len // TB
    n_blk = n_seq * bps
    tok = lambda i, *_: (i, 0)
    const = lambda i, *_: (0, 0)
    grid_spec = pltpu.PrefetchScalarGridSpec(
        num_scalar_prefetch=4,
        grid=(n_blk,),
        in_specs=[
            pl.BlockSpec((TB, LANES), tok),
            pl.BlockSpec((TB, LANES), tok),
            pl.BlockSpec((TB, D), tok),
            pl.BlockSpec((8, 6 * D), const),
            pl.BlockSpec((1, D), const),
            pl.BlockSpec((1, D), const),
            pl.BlockSpec(memory_space=pl.ANY),
        ],
        out_specs=pl.BlockSpec((TB, D), tok),
        scratch_shapes=[pltpu.VMEM((2, BLK_PAIRS_BUF * SUB, LANES), U32), pltpu.SemaphoreType.DMA((2,))],
    )
    kern = functools.partial(_combine_kernel, blk0=blk0, n_blk=n_blk, blocks_per_seq=bps,
                             mod_row0=mod_row0, mod_row_step=mod_row_step)
    return pl.pallas_call(
        kern, grid_spec=grid_spec,
        out_shape=jax.ShapeDtypeStruct((n_seq * seq_len, D), F32),
        compiler_params=_params(("arbitrary",)),
        name="combine_lat" if blk0 else "combine_ctx",
    )(cnt, loc, glo, extra, lpos, topw, x1, mod, ln2g, ln2b, ys)


def _dft_tables():
    def cs(n):
        k = np.arange(n, dtype=np.int64)
        ang = 2.0 * np.pi * ((k[:, None] * k[None, :]) % n).astype(np.float64) / n
        return np.cos(ang) / math.sqrt(n), np.sin(ang) / math.sqrt(n)

    c, s = cs(FG)
    cdft = np.concatenate([c, s], axis=1).astype(np.float32)
    pd = {}
    for t in (T_CTX, T_LAT):
        c, s = cs(t)
        pd[t] = np.concatenate([c, -s], axis=1).astype(np.float32)
    tri = np.tril(np.ones((TB, TB), np.float32), k=-1)
    upper = np.triu(np.ones((LANES, LANES), np.float32), k=1)
    return cdft, pd, tri, upper


_ROT_PERM = np.array(list(range(8, 16)) + list(range(0, 8)) + list(range(24, 32)) + list(range(16, 24)))
_ROT_SIGN = np.array([-1.0] * 8 + [1.0] * 8 + [-1.0] * 8 + [1.0] * 8, np.float32)


def _rope_tables():
    rows = T_LAT // GRID_W
    row = jnp.repeat(jnp.arange(rows, dtype=F32), GRID_W)
    col = jnp.tile(jnp.arange(GRID_W, dtype=F32), rows)
    axis_dim = ROPE // 2
    inv_freq = ROPE_THETA ** (-jnp.arange(0, axis_dim, 2, dtype=F32) / axis_dim)
    ang_r = row[:, None] * inv_freq[None, :]
    ang_c = col[:, None] * inv_freq[None, :]
    ang = jnp.concatenate([ang_r, ang_r, ang_c, ang_c], axis=-1)
    pad = ((0, 0), (0, LANES - ROPE))
    return jnp.pad(jnp.cos(ang), pad), jnp.pad(jnp.sin(ang), pad)


def kernel(x_prompt, x_sample, cache_ckv, cache_krope, c, c_ctx, w_mod, b_mod, w_in, q_norm_g, w_q_up,
           kv_norm_g, w_kv_up, w_fourier_o, w_mla_o, w_gate, b_gate, w_out, ln1_g, ln1_b, w_router,
           b_router, w_gate_e, b_gate_e, w_up_e, b_up_e, w_down_e, b_down_e, ln2_g, ln2_b):
    cdft_np, pdft_np, tri_np, upper_np = _dft_tables()
    upper = jnp.asarray(upper_np)
    cdft = jnp.asarray(cdft_np).astype(BF16)
    pdft_ctx = jnp.asarray(pdft_np[T_CTX]).astype(BF16)
    pdft_lat = jnp.asarray(pdft_np[T_LAT]).astype(BF16)
    tri = jnp.asarray(tri_np).astype(BF16)
    cos_pad, sin_pad = _rope_tables()

    w_in0 = w_in[0]
    kr_w = w_in0[:, FW + QL + KVL:]
    lane_pad = ((0, 0), (0, LANES - ROPE))
    w_in_ext = jnp.concatenate(
        [w_in0[:, :FW + QL + KVL], jnp.pad(kr_w, lane_pad),
         jnp.pad(kr_w[:, _ROT_PERM] * _ROT_SIGN, lane_pad)], axis=1).astype(BF16)
    wq3 = w_q_up[0].reshape(QL, HEADS, NOPE + ROPE)
    wq_nope = jnp.transpose(wq3[:, :, :NOPE], (1, 0, 2))
    wq_rope = wq3[:, :, NOPE:]
    head_pad = ((0, 0), (0, 0), (0, LANES - ROPE))
    wqr = jnp.concatenate(
        [jnp.pad(wq_rope, head_pad).reshape(QL, HEADS * LANES),
         jnp.pad(wq_rope[:, :, _ROT_PERM] * _ROT_SIGN, head_pad).reshape(QL, HEADS * LANES)],
        axis=1).astype(BF16)
    wkv3 = w_kv_up[0].reshape(KVL, HEADS, NOPE + VH)
    wk = jnp.transpose(wkv3[:, :, :NOPE], (1, 0, 2))
    wv = jnp.transpose(wkv3[:, :, NOPE:], (1, 0, 2))
    wo3 = w_mla_o[0].reshape(HEADS, VH, D)
    wqa, woa = _prep(wq_nope, wk, wv, wo3)

    c_all = jnp.concatenate([c_ctx[None, :], c, jnp.zeros((8 - 1 - N_LAT_SEQ, D), F32)], axis=0)
    mod = _modulation(c_all, w_mod[0], b_mod)

    qg = q_norm_g
    kvg = kv_norm_g
    wfo = w_fourier_o[0].astype(BF16)
    wg = w_gate[0].astype(BF16)
    wout = w_out[0].astype(BF16)
    wr_f32 = jnp.pad(w_router[0], ((0, 0), (0, LANES - N_EXP)))
    wr_hi = wr_f32.astype(BF16)
    wr = jnp.concatenate([wr_hi, (wr_f32 - wr_hi.astype(F32)).astype(BF16)], axis=1)
    br = jnp.pad(b_router, ((0, 0), (0, LANES - N_EXP)), constant_values=NEG_BIG)
    cache_k = jnp.concatenate(
        [cache_ckv[:, 0], jnp.pad(cache_krope[:, 0], ((0, 0), (0, 0), (0, LANES - ROPE)))],
        axis=-1).astype(BF16)

    xc2d = x_prompt.reshape(N_CTX, D)
    xl2d = x_sample.reshape(N_LAT, D)

    fcs_c, qcat_c, kcat_c, new_ckv, new_krope = _pre(
        xc2d, mod, w_in_ext, cdft, qg, kvg, wqa, wqr, None, None,
        n_seq=N_CTX_SEQ, seq_len=T_CTX, rope=False, mod_row0=0, mod_row_step=0)
    x1_c, h2_c, lpos_c, topw_c, bcnt_c = _mix(
        xc2d, mod, fcs_c, pdft_ctx, qcat_c, kcat_c, None, wfo, woa, wg, b_gate, wout, ln1_g, ln1_b,
        wr, br, tri, upper, rows=MIX_ROWS_CTX, n_seq=N_CTX_SEQ, seq_len=T_CTX, mod_row0=0,
        mod_row_step=0)

    fcs_l, qcat_l, kcat_l = _pre(
        xl2d, mod, w_in_ext, cdft, qg, kvg, wqa, wqr, cos_pad, sin_pad,
        n_seq=N_LAT_SEQ, seq_len=T_LAT, rope=True, mod_row0=1, mod_row_step=1)
    x1_l, h2_l, lpos_l, topw_l, bcnt_l = _mix(
        xl2d, mod, fcs_l, pdft_lat, qcat_l, kcat_l, cache_k, wfo, woa, wg, b_gate, wout, ln1_g,
        ln1_b, wr, br, tri, upper, rows=MIX_ROWS_LAT, n_seq=N_LAT_SEQ, seq_len=T_LAT, mod_row0=1,
        mod_row_step=1)

    blen = jnp.concatenate([bcnt_c[:, 0, :N_EXP], bcnt_l[:, 0, :N_EXP]], axis=0).astype(jnp.int32)
    blen = (blen + (blen & 1)) // 2
    lstart = jnp.cumsum(blen, axis=1) - blen
    count = jnp.sum(blen, axis=0)
    start = jnp.cumsum(count) - count
    gstart = start[None, :] + jnp.cumsum(blen, axis=0) - blen
    extra = (jnp.sum(blen, axis=1) - BLK_PAIRS).astype(jnp.int32)
    pieces = _piece_tables(lstart, blen, gstart) + (extra,)
    start = start.astype(jnp.int32)
    count = count.astype(jnp.int32)

    xs = _dispatch(*pieces, h2_c, h2_l, lpos_c, lpos_l)
    ys = _experts(start, count, xs,
                  w_gate_e[0], b_gate_e[0][:, None, :], w_up_e[0], b_up_e[0][:, None, :],
                  w_down_e[0], b_down_e[0][:, None, :])

    y_c = _combine(*pieces, lpos_c, topw_c, x1_c, mod, ln2_g, ln2_b, ys, blk0=0,
                   n_seq=N_CTX_SEQ, seq_len=T_CTX, mod_row0=0, mod_row_step=0)
    y_l = _combine(*pieces, lpos_l, topw_l, x1_l, mod, ln2_g, ln2_b, ys, blk0=N_CTX_BLK,
                   n_seq=N_LAT_SEQ, seq_len=T_LAT, mod_row0=1, mod_row_step=1)
    return (y_c.reshape(N_CTX_SEQ, T_CTX, D), y_l.reshape(N_LAT_SEQ, T_LAT, D), new_ckv, new_krope)
```

```python
import functools
import math

import numpy as np
import jax
import jax.numpy as jnp
from jax import lax
from jax.experimental import pallas as pl
from jax.experimental.pallas import tpu as pltpu

D = 1024
N_CTX_SEQ, T_CTX = 32, 256
N_LAT_SEQ, T_LAT = 4, 1024
PAST = 256
N_CTX = N_CTX_SEQ * T_CTX
N_LAT = N_LAT_SEQ * T_LAT
N_TOK = N_CTX + N_LAT
FW = 512
FG = 128
N_FG = FW // FG
HEADS = 8
QL = 256
KVL = 128
NOPE = 64
ROPE = 32
VH = 64
N_EXP = 32
TOPK = 4
D_EXP = 1024
SWIGLU_LIMIT = 7.0
SWIGLU_ALPHA = 1.702
LN_EPS = 1e-5
RMS_EPS = 1e-6
DN_ALPHA = 2.0 ** 0.25
ATT_SCALE = float(NOPE + ROPE) ** -0.5
ROPE_THETA = 10000.0
GRID_W = 64

LANES = 128
TB = 256
PRE_ROWS = 512
MIX_ROWS_CTX = 512
MIX_ROWS_LAT = 512
SUB = 8
ROW_TILE = 256
TILE_PAIRS = ROW_TILE // 2
CHUNK_TILES = 4
CHUNK_PAIRS = CHUNK_TILES * TILE_PAIRS
RING = 2
N_CTX_BLK = N_CTX // TB
N_BLK = N_TOK // TB
BLK_PAIRS = TB * TOPK // 2
PAD_PAIRS_MAX = N_EXP // 2
BLK_PAIRS_BUF = BLK_PAIRS + 64
BLK_ROWS_BUF = 2 * BLK_PAIRS_BUF
NP_SORT = N_TOK * TOPK // 2
NP_MAX = NP_SORT + N_BLK * PAD_PAIRS_MAX
NP_PAD = NP_MAX + CHUNK_PAIRS
RUN_BITS = (TB // 2).bit_length()
EXTRA_BITS = PAD_PAIRS_MAX.bit_length()
U32 = jnp.uint32
VMEM_LIMIT = 56 * 1024 * 1024
NEG_BIG = -1e30

F32 = jnp.float32
BF16 = jnp.bfloat16
HIGHEST = lax.Precision.HIGHEST


def _dot(a, b):
    return jnp.dot(a, b, preferred_element_type=F32)


def _dot_nt(a, b):
    return lax.dot_general(a, b, (((1,), (1,)), ((), ())), preferred_element_type=F32)


def _params(sem):
    return pltpu.CompilerParams(dimension_semantics=sem, vmem_limit_bytes=VMEM_LIMIT)


def _full(shape):
    n = len(shape)
    return pl.BlockSpec(shape, lambda *_: (0,) * n, pipeline_mode=pl.Buffered(1))


def _norm_noaffine(x):
    mu = jnp.mean(x, axis=-1, keepdims=True)
    xc = x - mu
    var = jnp.mean(xc * xc, axis=-1, keepdims=True)
    return xc * lax.rsqrt(var + LN_EPS)


def _sigmoid(x):
    return 0.5 * jnp.tanh(0.5 * x) + 0.5


def _rms(x, g):
    return x * lax.rsqrt(jnp.mean(x * x, axis=-1, keepdims=True) + RMS_EPS) * g


def _mod_row(mod_ref, row0, step, seq):
    row = row0 if step == 0 else row0 + step * seq
    m = mod_ref[pl.ds(row, 1), :]
    return [m[:, i * D:(i + 1) * D] for i in range(6)]


def _prep_kernel(wqn_ref, wk_ref, wv_ref, wo_ref, qabs_ref, oabs_ref):
    qabs = lax.dot_general(wqn_ref[0], wk_ref[0], (((1,), (1,)), ((), ())),
                           preferred_element_type=F32, precision=HIGHEST)
    qabs_ref[...] = qabs.astype(BF16)
    oabs = jnp.dot(wv_ref[0], wo_ref[0], preferred_element_type=F32, precision=HIGHEST)
    oabs_ref[...] = oabs.astype(BF16)


def _prep(wqn, wk, wv, wo):
    return pl.pallas_call(
        _prep_kernel,
        grid=(HEADS,),
        in_specs=[
            pl.BlockSpec((1, QL, NOPE), lambda h: (h, 0, 0)),
            pl.BlockSpec((1, KVL, NOPE), lambda h: (h, 0, 0)),
            pl.BlockSpec((1, KVL, VH), lambda h: (h, 0, 0)),
            pl.BlockSpec((1, VH, D), lambda h: (h, 0, 0)),
        ],
        out_specs=[
            pl.BlockSpec((QL, KVL), lambda h: (0, h)),
            pl.BlockSpec((KVL, D), lambda h: (h, 0)),
        ],
        out_shape=[jax.ShapeDtypeStruct((QL, HEADS * KVL), BF16),
                   jax.ShapeDtypeStruct((HEADS * KVL, D), BF16)],
        compiler_params=_params(("arbitrary",)),
        name="weight_prep",
    )(wqn, wk, wv, wo)


MOD_COLS = 1536


def _mod_kernel(c_ref, w_ref, b_ref, o_ref):
    c = c_ref[...]
    s = c * jax.nn.sigmoid(c)
    o_ref[...] = jnp.dot(s, w_ref[...], preferred_element_type=F32, precision=HIGHEST) + b_ref[...]


def _modulation(c_all, w_mod, b_mod):
    return pl.pallas_call(
        _mod_kernel,
        grid=(6 * D // MOD_COLS,),
        in_specs=[
            pl.BlockSpec((8, D), lambda i: (0, 0)),
            pl.BlockSpec((D, MOD_COLS), lambda i: (0, i)),
            pl.BlockSpec((1, MOD_COLS), lambda i: (0, i)),
        ],
        out_specs=pl.BlockSpec((8, MOD_COLS), lambda i: (0, i)),
        out_shape=jax.ShapeDtypeStruct((8, 6 * D), F32),
        compiler_params=_params(("arbitrary",)),
        name="modulation",
    )(c_all, w_mod, b_mod)


def _pre_kernel(*refs, rows, seq_len, seqs, rope, mod_row0, mod_row_step):
    if rope:
        (x_ref, mod_ref, win_ref, cdft_ref, qg_ref, kvg_ref, wqa_ref, wqr_ref, cos_ref, sin_ref,
         fcs_ref, qcat_ref, kcat_ref) = refs
    else:
        (x_ref, mod_ref, win_ref, cdft_ref, qg_ref, kvg_ref, wqa_ref, wqr_ref,
         fcs_ref, qcat_ref, kcat_ref, ckv_ref, krope_ref) = refs
    q_rows = rows // seqs
    b = pl.program_id(0) // (seq_len // q_rows)
    shift1, scale1 = _mod_row(mod_ref, mod_row0, mod_row_step, b)[:2]
    seq_rows = [slice(s * q_rows, (s + 1) * q_rows) for s in range(seqs)]

    x = x_ref[...]
    h1 = _norm_noaffine(x) * (1.0 + scale1) + shift1
    h1b = h1.astype(BF16)
    n_cols = 9 * LANES if rope else 8 * LANES
    proj = _dot(h1b, win_ref[:, :n_cols])

    f_b = proj[:, :FW].astype(BF16)
    for g in range(N_FG):
        r = _dot(f_b[:, g * FG:(g + 1) * FG], cdft_ref[...])
        for s, sl in enumerate(seq_rows):
            fcs_ref[s, 0, :, g * FG:(g + 1) * FG] = r[sl, :FG].astype(BF16)
            fcs_ref[s, 1, :, g * FG:(g + 1) * FG] = r[sl, FG:].astype(BF16)

    qn = _rms(proj[:, FW:FW + QL], qg_ref[...]).astype(BF16)
    ckv = _rms(proj[:, FW + QL:FW + QL + KVL], kvg_ref[...])
    kr = proj[:, 7 * LANES:8 * LANES]

    qa = _dot(qn, wqa_ref[...]) * ATT_SCALE
    if rope:
        cos = cos_ref[...]
        sin = sin_ref[...]
        qr2 = _dot(qn, wqr_ref[...])
        kr_keys = kr * cos + proj[:, 8 * LANES:9 * LANES] * sin
    else:
        qr2 = _dot(qn, wqr_ref[:, :HEADS * LANES])
        kr_keys = kr
    ckv_b = ckv.astype(BF16)
    kr_b = kr_keys.astype(BF16)
    for h in range(HEADS):
        qr_h = qr2[:, h * LANES:(h + 1) * LANES]
        if rope:
            qr_h = qr_h * cos + qr2[:, (HEADS + h) * LANES:(HEADS + h + 1) * LANES] * sin
        qa_h = qa[:, h * KVL:(h + 1) * KVL].astype(BF16)
        qr_h = (qr_h * ATT_SCALE).astype(BF16)
        for s, sl in enumerate(seq_rows):
            qcat_ref[s, h, :, :KVL] = qa_h[sl]
            qcat_ref[s, h, :, KVL:] = qr_h[sl]
    for s, sl in enumerate(seq_rows):
        kcat_ref[s, :, :KVL] = ckv_b[sl]
        kcat_ref[s, :, KVL:] = kr_b[sl]
        if not rope:
            ckv_ref[s, 0] = ckv[sl]
            krope_ref[s, 0] = kr[sl, :ROPE]


def _pre(x2d, mod, w_in_ext, cdft, qg, kvg, wqa, wqr, cos_pad, sin_pad, *, n_seq, seq_len, rope,
         mod_row0, mod_row_step):
    rows = PRE_ROWS
    seqs = max(1, rows // seq_len)
    q_rows = rows // seqs
    bps = seq_len // q_rows
    n_blk = n_seq * seq_len // rows
    seq_blk = lambda i: (i // bps, 0, i % bps, 0)
    in_specs = [
        pl.BlockSpec((rows, D), lambda i: (i, 0)),
        _full((8, 6 * D)),
        _full((D, 9 * LANES)),
        _full((FG, 2 * FG)),
        _full((1, QL)),
        _full((1, KVL)),
        _full((QL, HEADS * KVL)),
        _full((QL, 2 * HEADS * LANES)),
    ]
    args = [x2d, mod, w_in_ext, cdft, qg, kvg, wqa, wqr]
    out_specs = [
        pl.BlockSpec((seqs, 2, q_rows, FW), seq_blk),
        pl.BlockSpec((seqs, HEADS, q_rows, 2 * LANES), seq_blk),
        pl.BlockSpec((seqs, q_rows, 2 * LANES), lambda i: (i // bps, i % bps, 0)),
    ]
    out_shape = [
        jax.ShapeDtypeStruct((n_seq, 2, seq_len, FW), BF16),
        jax.ShapeDtypeStruct((n_seq, HEADS, seq_len, 2 * LANES), BF16),
        jax.ShapeDtypeStruct((n_seq, seq_len, 2 * LANES), BF16),
    ]
    if rope:
        in_specs += [pl.BlockSpec((q_rows, LANES), lambda i: (i % bps, 0)),
                     pl.BlockSpec((q_rows, LANES), lambda i: (i % bps, 0))]
        args += [cos_pad, sin_pad]
    else:
        out_specs += [pl.BlockSpec((seqs, 1, q_rows, KVL), seq_blk),
                      pl.BlockSpec((seqs, 1, q_rows, ROPE), seq_blk)]
        out_shape += [jax.ShapeDtypeStruct((n_seq, 1, seq_len, KVL), F32),
                      jax.ShapeDtypeStruct((n_seq, 1, seq_len, ROPE), F32)]
    kern = functools.partial(_pre_kernel, rows=rows, seq_len=seq_len, seqs=seqs, rope=rope,
                             mod_row0=mod_row0, mod_row_step=mod_row_step)
    return pl.pallas_call(
        kern, grid=(n_blk,), in_specs=in_specs, out_specs=out_specs, out_shape=out_shape,
        compiler_params=_params(("arbitrary",)),
        name="pre_lat" if rope else "pre_ctx",
    )(*args)


def _mix_kernel(*refs, rows, seq_len, seqs, cache, mod_row0, mod_row_step):
    if cache:
        (x_ref, mod_ref, fcs_ref, pdft_ref, qcat_ref, kcat_ref, cache_ref, wfo_ref, woa_ref,
         wg_ref, bg_ref, wout_ref, ln1g_ref, ln1b_ref, wr_ref, br_ref, tri_ref, upper_ref,
         x1_ref, h2_ref, lpos_ref, topw_ref, bcnt_ref) = refs
    else:
        (x_ref, mod_ref, fcs_ref, pdft_ref, qcat_ref, kcat_ref, wfo_ref, woa_ref,
         wg_ref, bg_ref, wout_ref, ln1g_ref, ln1b_ref, wr_ref, br_ref, tri_ref, upper_ref,
         x1_ref, h2_ref, lpos_ref, topw_ref, bcnt_ref) = refs
        cache_ref = None
    b = pl.program_id(0)
    shift1, scale1, gate1, shift2, scale2, _ = _mod_row(mod_ref, mod_row0, mod_row_step, b)

    x = x_ref[...]
    h1b = (_norm_noaffine(x) * (1.0 + scale1) + shift1).astype(BF16)

    mixed_rows, attn_rows = [], []
    for s in range(seqs):
        mixed_rows.append(_dot(pdft_ref[:, :seq_len], fcs_ref[s, 0])
                          + _dot(pdft_ref[:, seq_len:], fcs_ref[s, 1]))
        kc = kcat_ref[s]
        vals = kc[:, :KVL]
        if cache:
            kc2 = cache_ref[s]
            vals2 = kc2[:, :KVL]
        heads_out = []
        for h in range(HEADS):
            q = qcat_ref[s, h]
            s1 = _dot_nt(q, kc)
            m = jnp.max(s1, axis=-1, keepdims=True)
            if cache:
                s2 = _dot_nt(q, kc2)
                m = jnp.maximum(m, jnp.max(s2, axis=-1, keepdims=True))
            p1 = jnp.exp(s1 - m)
            l = jnp.sum(p1, axis=-1, keepdims=True)
            o = _dot(p1.astype(BF16), vals)
            if cache:
                p2 = jnp.exp(s2 - m)
                l = l + jnp.sum(p2, axis=-1, keepdims=True)
                o = o + _dot(p2.astype(BF16), vals2)
            heads_out.append((o / l).astype(BF16))
        attn_rows.append(jnp.concatenate(heads_out, axis=1))
    mixed = jnp.concatenate(mixed_rows, axis=0) if seqs > 1 else mixed_rows[0]
    attn = jnp.concatenate(attn_rows, axis=0) if seqs > 1 else attn_rows[0]
    f_out = _dot(mixed.astype(BF16), wfo_ref[...])
    m_out = _dot(attn, woa_ref[...])

    gates = _sigmoid(_dot(h1b, wg_ref[...]) + bg_ref[...])
    merged = gates[:, :D] * f_out + gates[:, D:] * m_out
    mix = _dot(merged.astype(BF16), wout_ref[...])

    x1 = _norm_noaffine(DN_ALPHA * x + gate1 * mix) * ln1g_ref[...] + ln1b_ref[...]
    x1_ref[...] = x1
    h2 = _norm_noaffine(x1) * (1.0 + scale2) + shift2
    h2_ref[...] = h2.astype(BF16)

    h_hi = h2.astype(BF16)
    h_lo = (h2 - h_hi.astype(F32)).astype(BF16)
    hi_part = _dot(h_hi, wr_ref[...])
    logits_all = (hi_part[:, :LANES] + hi_part[:, LANES:] + _dot(h_lo, wr_ref[:, :LANES])) + br_ref[...]
    lane = lax.broadcasted_iota(jnp.int32, (TB, LANES), 1).astype(F32)
    for r in range(rows // TB):
        blk = slice(r * TB, (r + 1) * TB)
        work = logits_all[blk]
        top_v, top_i = [], []
        for _ in range(TOPK):
            mk = jnp.max(work, axis=-1, keepdims=True)
            ik = jnp.min(jnp.where(work == mk, lane, float(LANES)), axis=-1, keepdims=True)
            work = jnp.where(lane == ik, -jnp.inf, work)
            top_v.append(mk)
            top_i.append(ik)
        exps = [jnp.exp(v - top_v[0]) for v in top_v]
        denom = exps[0] + exps[1] + exps[2] + exps[3]

        onehot = jnp.zeros((TB, LANES), F32)
        for ik in top_i:
            onehot = onehot + jnp.where(lane == ik, 1.0, 0.0)
        counts = jnp.sum(onehot, axis=0, keepdims=True)
        padded = counts + (counts - 2.0 * jnp.floor(0.5 * counts))
        lower = jnp.dot(jnp.broadcast_to(padded, (8, LANES)), upper_ref[...],
                        preferred_element_type=F32, precision=HIGHEST)[0:1, :]
        before = _dot(tri_ref[...], onehot.astype(BF16)) + lower
        lpos = jnp.zeros((TB, LANES), jnp.int32)
        topw = jnp.zeros((TB, LANES), F32)
        for k in range(TOPK):
            pos = jnp.sum(jnp.where(lane == top_i[k], before, 0.0), axis=-1, keepdims=True)
            lpos = jnp.where(lane == float(k), pos.astype(jnp.int32), lpos)
            topw = jnp.where(lane == float(k), exps[k] / denom, topw)
        lpos_ref[blk, :] = lpos
        topw_ref[blk, :] = topw
        bcnt_ref[r] = counts


def _mix(x2d, mod, fcs, pdft, qcat, kcat, cache_k, wfo, woa, wg, bg, wout, ln1g, ln1b, wr, br,
         tri, upper, *, rows, n_seq, seq_len, mod_row0, mod_row_step):
    cache = cache_k is not None
    seqs = max(1, rows // seq_len)
    q_rows = rows // seqs
    bps = seq_len // q_rows
    n_outer = n_seq // seqs
    in_specs = [
        pl.BlockSpec((rows, D), lambda b, j: (b * bps + j, 0)),
        _full((8, 6 * D)),
        pl.BlockSpec((seqs, 2, seq_len, FW), lambda b, j: (b, 0, 0, 0)),
        pl.BlockSpec((q_rows, 2 * seq_len), lambda b, j: (j, 0)),
        pl.BlockSpec((seqs, HEADS, q_rows, 2 * LANES), lambda b, j: (b, 0, j, 0)),
        pl.BlockSpec((seqs, seq_len, 2 * LANES), lambda b, j: (b, 0, 0)),
    ]
    args = [x2d, mod, fcs, pdft, qcat, kcat]
    if cache:
        in_specs.append(pl.BlockSpec((seqs, PAST, 2 * LANES), lambda b, j: (b, 0, 0)))
        args.append(cache_k)
    in_specs += [
        _full((FW, D)), _full((HEADS * KVL, D)), _full((D, 2 * D)), _full((1, 2 * D)),
        _full((D, D)), _full((1, D)), _full((1, D)), _full((D, 2 * LANES)), _full((1, LANES)),
        _full((TB, TB)), _full((LANES, LANES)),
    ]
    args += [wfo, woa, wg, bg, wout, ln1g, ln1b, wr, br, tri, upper]
    n_tok = n_seq * seq_len
    tok_spec = lambda w: pl.BlockSpec((rows, w), lambda b, j: (b * bps + j, 0))
    out_specs = [tok_spec(D), tok_spec(D), tok_spec(LANES), tok_spec(LANES),
                 pl.BlockSpec((rows // TB, 1, LANES), lambda b, j: (b * bps + j, 0, 0))]
    out_shape = [
        jax.ShapeDtypeStruct((n_tok, D), F32),
        jax.ShapeDtypeStruct((n_tok, D), BF16),
        jax.ShapeDtypeStruct((n_tok, LANES), jnp.int32),
        jax.ShapeDtypeStruct((n_tok, LANES), F32),
        jax.ShapeDtypeStruct((n_tok // TB, 1, LANES), F32),
    ]
    kern = functools.partial(_mix_kernel, rows=rows, seq_len=seq_len, seqs=seqs, cache=cache,
                             mod_row0=mod_row0, mod_row_step=mod_row_step)
    return pl.pallas_call(
        kern, grid=(n_outer, bps), in_specs=in_specs, out_specs=out_specs, out_shape=out_shape,
        compiler_params=_params(("arbitrary", "arbitrary")),
        name="mix_lat" if cache else "mix_ctx",
    )(*args)


def _pair_tiles(ref, pair0, n_pairs):
    return ref.at[pl.ds(pl.multiple_of(pair0 * SUB, SUB), n_pairs * SUB), :]


def _piece_tables(lstart, blen, gstart):
    bits = jnp.arange(RUN_BITS, dtype=jnp.int32)
    n = blen[:, :, None]
    valid = (n >> bits) & 1
    done = (n >> (bits + 1)) << (bits + 1)
    rank = jnp.cumsum(valid, axis=1) - valid
    place = (valid[..., None] == 1) & (rank[..., None] == jnp.arange(N_EXP, dtype=jnp.int32))
    def table(first_row):
        rows = (first_row[:, :, None] + done) * SUB
        listed = jnp.sum(jnp.where(place, rows[..., None], 0), axis=1)
        return listed.reshape(-1).astype(jnp.int32)
    return jnp.sum(valid, axis=1).reshape(-1).astype(jnp.int32), table(lstart), table(gstart)


def _start_piece_copies(cnt_ref, loc_ref, glo_ref, blk, local_ref, global_ref, sem, to_global):
    for bit in range(RUN_BITS):
        size = (1 << bit) * SUB
        base = (blk * RUN_BITS + bit) * N_EXP

        def piece(t, carry):
            loc = local_ref.at[pl.ds(pl.multiple_of(loc_ref[base + t], SUB), size), :]
            glo = global_ref.at[pl.ds(pl.multiple_of(glo_ref[base + t], SUB), size), :]
            (pltpu.make_async_copy(loc, glo, sem) if to_global
             else pltpu.make_async_copy(glo, loc, sem)).start()
            return carry

        lax.fori_loop(0, cnt_ref[blk * RUN_BITS + bit], piece, 0)


def _wait_block_pieces(extra_ref, blk, vmem_ref, hbm_ref, sem, to_global):
    def wait(n_pairs):
        loc = _pair_tiles(vmem_ref, 0, n_pairs)
        glo = _pair_tiles(hbm_ref, 0, n_pairs)
        (pltpu.make_async_copy(loc, glo, sem) if to_global else pltpu.make_async_copy(glo, loc, sem)).wait()

    wait(BLK_PAIRS)
    extra = extra_ref[blk]
    for bit in range(EXTRA_BITS):
        @pl.when((extra & (1 << bit)) != 0)
        def _():
            wait(1 << bit)


def _load_pairs(ref, n_pairs):
    words = jnp.concatenate([ref[pl.ds(s, n_pairs, stride=SUB), :] for s in range(SUB)], axis=1)
    return pltpu.bitcast(words, BF16)


def _store_pairs(ref, rows, n_pairs):
    words = pltpu.bitcast(rows, U32)
    for s in range(SUB):
        ref[pl.ds(s, n_pairs, stride=SUB), :] = words[:, s * LANES:(s + 1) * LANES]


def _zero_fill_tail(zero_ref, hbm_ref, sem):
    zero_ref[...] = jnp.zeros(zero_ref.shape, U32)
    copies = [pltpu.make_async_copy(zero_ref, _pair_tiles(hbm_ref, p, TILE_PAIRS), sem)
              for p in range(NP_SORT, NP_PAD, TILE_PAIRS)]
    for cp in copies:
        cp.start()
    for cp in copies:
        cp.wait()


def _dispatch_kernel(cnt_ref, loc_ref, glo_ref, extra_ref, h2c_ref, h2l_ref, lpc_ref, lpl_ref, xs_ref,
                     buf, zbuf, sem, zsem):
    b = pl.program_id(0)
    slot = b & 1

    def start_runs(blk, s):
        _start_piece_copies(cnt_ref, loc_ref, glo_ref, blk, buf.at[s], xs_ref, sem.at[s], True)

    def wait_runs(blk, s):
        _wait_block_pieces(extra_ref, blk, buf.at[s], xs_ref, sem.at[s], True)

    @pl.when(b == 0)
    def _():
        _zero_fill_tail(zbuf, xs_ref, zsem)

    @pl.when(b >= 2)
    def _():
        wait_runs(b - 2, slot)

    def sort_block(h2_ref, lp_ref):
        col = lax.broadcasted_iota(jnp.int32, (TB, BLK_ROWS_BUF), 1)
        lp = lp_ref[...]
        pick = jnp.zeros((TB, BLK_ROWS_BUF), F32)
        for k in range(TOPK):
            pick = pick + jnp.where(col == lp[:, k:k + 1], 1.0, 0.0)
        sorted_rows = lax.dot_general(pick.astype(BF16), h2_ref[...],
                                      (((0,), (0,)), ((), ())), preferred_element_type=F32)
        _store_pairs(buf.at[slot], sorted_rows.astype(BF16), BLK_PAIRS_BUF)

    @pl.when(b < N_CTX_BLK)
    def _():
        sort_block(h2c_ref, lpc_ref)

    @pl.when(b >= N_CTX_BLK)
    def _():
        sort_block(h2l_ref, lpl_ref)

    start_runs(b, slot)

    @pl.when(b == N_BLK - 1)
    def _():
        wait_runs(b - 1, 1 - slot)
        wait_runs(b, slot)


def _dispatch(cnt, loc, glo, extra, h2c, h2l, lpc, lpl):
    ctx_idx = lambda i, *_: (jnp.minimum(i, N_CTX_BLK - 1), 0)
    lat_idx = lambda i, *_: (jnp.maximum(i - N_CTX_BLK, 0), 0)
    grid_spec = pltpu.PrefetchScalarGridSpec(
        num_scalar_prefetch=4,
        grid=(N_BLK,),
        in_specs=[
            pl.BlockSpec((TB, D), ctx_idx),
            pl.BlockSpec((TB, D), lat_idx),
            pl.BlockSpec((TB, LANES), ctx_idx),
            pl.BlockSpec((TB, LANES), lat_idx),
        ],
        out_specs=pl.BlockSpec(memory_space=pl.ANY),
        scratch_shapes=[pltpu.VMEM((2, BLK_PAIRS_BUF * SUB, LANES), U32),
                        pltpu.VMEM((TILE_PAIRS * SUB, LANES), U32),
                        pltpu.SemaphoreType.DMA((2,)),
                        pltpu.SemaphoreType.DMA(())],
    )
    return pl.pallas_call(
        _dispatch_kernel, grid_spec=grid_spec,
        out_shape=jax.ShapeDtypeStruct((NP_PAD * SUB, LANES), U32),
        compiler_params=_params(("arbitrary",)),
        name="dispatch",
    )(cnt, loc, glo, extra, h2c, h2l, lpc, lpl)


def _expert_kernel(start_ref, count_ref, xs_ref, wg_ref, bg_ref, wu_ref, bu_ref, wd_ref, bd_ref, ys_ref,
                   wbf, xbuf, ybuf, sem_in, sem_out):
    e = pl.program_id(0)
    pair0 = start_ref[e]

    def tiles_of(ex):
        return lax.shift_right_logical(count_ref[ex] + (TILE_PAIRS - 1), TILE_PAIRS.bit_length() - 1)

    def chunks_of(ex):
        return lax.shift_right_logical(tiles_of(ex) + (CHUNK_TILES - 1), CHUNK_TILES.bit_length() - 1)

    n_tiles = tiles_of(e)
    n_chunks = chunks_of(e)

    def in_copy_of(ex, c, slot):
        return pltpu.make_async_copy(_pair_tiles(xs_ref, start_ref[ex] + c * CHUNK_PAIRS, CHUNK_PAIRS),
                                     xbuf.at[slot], sem_in.at[slot])

    def in_copy(c, slot):
        return in_copy_of(e, c, slot)

    def out_copy(c, slot):
        return pltpu.make_async_copy(ybuf.at[slot], _pair_tiles(ys_ref, pair0 + c * CHUNK_PAIRS, CHUNK_PAIRS),
                                     sem_out.at[slot])

    def start_first_chunks(ex):
        for c in range(RING):
            @pl.when(chunks_of(ex) > c)
            def _():
                in_copy_of(ex, c, c).start()

    @pl.when(e == 0)
    def _():
        ybuf[...] = jnp.zeros(ybuf.shape, U32)
        _zero_fill_tail(ybuf.at[0, pl.ds(0, TILE_PAIRS * SUB)], ys_ref, sem_out.at[0])
        start_first_chunks(0)

    @pl.when(n_tiles > 0)
    def _():
        wbf[0] = wg_ref[0].astype(BF16)
        wbf[1] = wu_ref[0].astype(BF16)
        wbf[2] = wd_ref[0].astype(BF16)

        def tile(j, carry):
            c = lax.shift_right_logical(j, CHUNK_TILES.bit_length() - 1)
            t = j & (CHUNK_TILES - 1)
            slot = c & (RING - 1)

            @pl.when(t == 0)
            def _():
                ahead = c + (RING - 1)

                @pl.when(jnp.logical_and(c >= 1, ahead < n_chunks))
                def _():
                    in_copy(ahead, ahead & (RING - 1)).start()

                in_copy(c, slot).wait()

                @pl.when(c >= RING)
                def _():
                    out_copy(c - RING, slot).wait()

            rows = pl.ds(pl.multiple_of(t * (TILE_PAIRS * SUB), TILE_PAIRS * SUB), TILE_PAIRS * SUB)
            xb = _load_pairs(xbuf.at[slot, rows], TILE_PAIRS)
            gt = jnp.minimum(_dot(xb, wbf[0]) + bg_ref[0], SWIGLU_LIMIT)
            up = jnp.clip(_dot(xb, wbf[1]) + bu_ref[0], -SWIGLU_LIMIT, SWIGLU_LIMIT)
            act = gt * _sigmoid(SWIGLU_ALPHA * gt) * (up + 1.0)
            y = _dot(act.astype(BF16), wbf[2]) + bd_ref[0]
            _store_pairs(ybuf.at[slot, rows], y.astype(BF16), TILE_PAIRS)

            @pl.when(jnp.logical_or(t == CHUNK_TILES - 1, j == n_tiles - 1))
            def _():
                out_copy(c, slot).start()
            return carry

        lax.fori_loop(0, n_tiles, tile, 0)

        for back in range(RING, 0, -1):
            @pl.when(n_chunks >= back)
            def _():
                out_copy(n_chunks - back, (n_chunks - back) & (RING - 1)).wait()

    @pl.when(e + 1 < N_EXP)
    def _():
        start_first_chunks(e + 1)


def _experts(start, count, xs, wg, bg, wu, bu, wd, bd):
    w_idx = lambda e, *_: (e, 0, 0)
    grid_spec = pltpu.PrefetchScalarGridSpec(
        num_scalar_prefetch=2,
        grid=(N_EXP,),
        in_specs=[
            pl.BlockSpec(memory_space=pl.ANY),
            pl.BlockSpec((1, D, D_EXP), w_idx),
            pl.BlockSpec((1, 1, D_EXP), w_idx),
            pl.BlockSpec((1, D, D_EXP), w_idx),
            pl.BlockSpec((1, 1, D_EXP), w_idx),
            pl.BlockSpec((1, D_EXP, D), w_idx),
            pl.BlockSpec((1, 1, D), w_idx),
        ],
        out_specs=pl.BlockSpec(memory_space=pl.ANY),
        scratch_shapes=[pltpu.VMEM((3, D, D_EXP), BF16),
                        pltpu.VMEM((RING, CHUNK_PAIRS * SUB, LANES), U32),
                        pltpu.VMEM((RING, CHUNK_PAIRS * SUB, LANES), U32),
                        pltpu.SemaphoreType.DMA((RING,)),
                        pltpu.SemaphoreType.DMA((RING,))],
    )
    return pl.pallas_call(
        _expert_kernel, grid_spec=grid_spec,
        out_shape=jax.ShapeDtypeStruct((NP_PAD * SUB, LANES), U32),
        compiler_params=_params(("arbitrary",)),
        name="experts",
    )(start, count, xs, wg, bg, wu, bu, wd, bd)


def _combine_kernel(cnt_ref, loc_ref, glo_ref, extra_ref, lp_ref, topw_ref, x1_ref, mod_ref, g_ref, b_ref,
                    ys_ref, o_ref, buf, sem, *, blk0, n_blk, blocks_per_seq, mod_row0, mod_row_step):
    i = pl.program_id(0)
    slot = i & 1
    gate2 = _mod_row(mod_ref, mod_row0, mod_row_step, i // blocks_per_seq)[5]

    def start_runs(step, s):
        _start_piece_copies(cnt_ref, loc_ref, glo_ref, blk0 + step, buf.at[s], ys_ref, sem.at[s], False)

    @pl.when(i == 0)
    def _():
        buf[...] = jnp.zeros(buf.shape, U32)
        start_runs(0, 0)

    @pl.when(i + 1 < n_blk)
    def _():
        start_runs(i + 1, 1 - slot)

    _wait_block_pieces(extra_ref, blk0 + i, buf.at[slot], ys_ref, sem.at[slot], False)

    y_sorted = _load_pairs(buf.at[slot], BLK_PAIRS_BUF)
    col = lax.broadcasted_iota(jnp.int32, (TB, BLK_ROWS_BUF), 1)
    lp = lp_ref[...]
    topw = topw_ref[...]
    weights = jnp.zeros((TB, BLK_ROWS_BUF), F32)
    for k in range(TOPK):
        weights = weights + jnp.where(col == lp[:, k:k + 1], topw[:, k:k + 1], 0.0)
    ffn = _dot(weights.astype(BF16), y_sorted)
    y = DN_ALPHA * x1_ref[...] + gate2 * ffn
    o_ref[...] = _norm_noaffine(y) * g_ref[...] + b_ref[...]


def _combine(cnt, loc, glo, extra, lpos, topw, x1, mod, ln2g, ln2b, ys, *, blk0, n_seq, seq_len,
             mod_row0, mod_row_step):
    bps = seq_len // TB
    n_blk = n_seq * bps
    tok = lambda i, *_: (i, 0)
    const = lambda i, *_: (0, 0)
    grid_spec = pltpu.PrefetchScalarGridSpec(
        num_scalar_prefetch=4,
        grid=(n_blk,),
        in_specs=[
            pl.BlockSpec((TB, LANES), tok),
            pl.BlockSpec((TB, LANES), tok),
            pl.BlockSpec((TB, D), tok),
            pl.BlockSpec((8, 6 * D), const),
            pl.BlockSpec((1, D), const),
            pl.BlockSpec((1, D), const),
            pl.BlockSpec(memory_space=pl.ANY),
        ],
        out_specs=pl.BlockSpec((TB, D), tok),
        scratch_shapes=[pltpu.VMEM((2, BLK_PAIRS_BUF * SUB, LANES), U32), pltpu.SemaphoreType.DMA((2,))],
    )
    kern = functools.partial(_combine_kernel, blk0=blk0, n_blk=n_blk, blocks_per_seq=bps,
                             mod_row0=mod_row0, mod_row_step=mod_row_step)
    return pl.pallas_call(
        kern, grid_spec=grid_spec,
        out_shape=jax.ShapeDtypeStruct((n_seq * seq_len, D), F32),
        compiler_params=_params(("arbitrary",)),
        name="combine_lat" if blk0 else "combine_ctx",
    )(cnt, loc, glo, extra, lpos, topw, x1, mod, ln2g, ln2b, ys)


def _dft_tables():
    def cs(n):
        k = np.arange(n, dtype=np.int64)
        ang = 2.0 * np.pi * ((k[:, None] * k[None, :]) % n).astype(np.float64) / n
        return np.cos(ang) / math.sqrt(n), np.sin(ang) / math.sqrt(n)

    c, s = cs(FG)
    cdft = np.concatenate([c, s], axis=1).astype(np.float32)
    pd = {}
    for t in (T_CTX, T_LAT):
        c, s = cs(t)
        pd[t] = np.concatenate([c, -s], axis=1).astype(np.float32)
    tri = np.tril(np.ones((TB, TB), np.float32), k=-1)
    upper = np.triu(np.ones((LANES, LANES), np.float32), k=1)
    return cdft, pd, tri, upper


_ROT_PERM = np.array(list(range(8, 16)) + list(range(0, 8)) + list(range(24, 32)) + list(range(16, 24)))
_ROT_SIGN = np.array([-1.0] * 8 + [1.0] * 8 + [-1.0] * 8 + [1.0] * 8, np.float32)


def _rope_tables():
    rows = T_LAT // GRID_W
    row = jnp.repeat(jnp.arange(rows, dtype=F32), GRID_W)
    col = jnp.tile(jnp.arange(GRID_W, dtype=F32), rows)
    axis_dim = ROPE // 2
    inv_freq = ROPE_THETA ** (-jnp.arange(0, axis_dim, 2, dtype=F32) / axis_dim)
    ang_r = row[:, None] * inv_freq[None, :]
    ang_c = col[:, None] * inv_freq[None, :]
    ang = jnp.concatenate([ang_r, ang_r, ang_c, ang_c], axis=-1)
    pad = ((0, 0), (0, LANES - ROPE))
    return jnp.pad(jnp.cos(ang), pad), jnp.pad(jnp.sin(ang), pad)


def kernel(x_prompt, x_sample, cache_ckv, cache_krope, c, c_ctx, w_mod, b_mod, w_in, q_norm_g, w_q_up,
           kv_norm_g, w_kv_up, w_fourier_o, w_mla_o, w_gate, b_gate, w_out, ln1_g, ln1_b, w_router,
           b_router, w_gate_e, b_gate_e, w_up_e, b_up_e, w_down_e, b_down_e, ln2_g, ln2_b):
    cdft_np, pdft_np, tri_np, upper_np = _dft_tables()
    upper = jnp.asarray(upper_np)
    cdft = jnp.asarray(cdft_np).astype(BF16)
    pdft_ctx = jnp.asarray(pdft_np[T_CTX]).astype(BF16)
    pdft_lat = jnp.asarray(pdft_np[T_LAT]).astype(BF16)
    tri = jnp.asarray(tri_np).astype(BF16)
    cos_pad, sin_pad = _rope_tables()

    w_in0 = w_in[0]
    kr_w = w_in0[:, FW + QL + KVL:]
    lane_pad = ((0, 0), (0, LANES - ROPE))
    w_in_ext = jnp.concatenate(
        [w_in0[:, :FW + QL + KVL], jnp.pad(kr_w, lane_pad),
         jnp.pad(kr_w[:, _ROT_PERM] * _ROT_SIGN, lane_pad)], axis=1).astype(BF16)
    wq3 = w_q_up[0].reshape(QL, HEADS, NOPE + ROPE)
    wq_nope = jnp.transpose(wq3[:, :, :NOPE], (1, 0, 2))
    wq_rope = wq3[:, :, NOPE:]
    head_pad = ((0, 0), (0, 0), (0, LANES - ROPE))
    wqr = jnp.concatenate(
        [jnp.pad(wq_rope, head_pad).reshape(QL, HEADS * LANES),
         jnp.pad(wq_rope[:, :, _ROT_PERM] * _ROT_SIGN, head_pad).reshape(QL, HEADS * LANES)],
        axis=1).astype(BF16)
    wkv3 = w_kv_up[0].reshape(KVL, HEADS, NOPE + VH)
    wk = jnp.transpose(wkv3[:, :, :NOPE], (1, 0, 2))
    wv = jnp.transpose(wkv3[:, :, NOPE:], (1, 0, 2))
    wo3 = w_mla_o[0].reshape(HEADS, VH, D)
    wqa, woa = _prep(wq_nope, wk, wv, wo3)

    c_all = jnp.concatenate([c_ctx[None, :], c, jnp.zeros((8 - 1 - N_LAT_SEQ, D), F32)], axis=0)
    mod = _modulation(c_all, w_mod[0], b_mod)

    qg = q_norm_g
    kvg = kv_norm_g
    wfo = w_fourier_o[0].astype(BF16)
    wg = w_gate[0].astype(BF16)
    wout = w_out[0].astype(BF16)
    wr_f32 = jnp.pad(w_router[0], ((0, 0), (0, LANES - N_EXP)))
    wr_hi = wr_f32.astype(BF16)
    wr = jnp.concatenate([wr_hi, (wr_f32 - wr_hi.astype(F32)).astype(BF16)], axis=1)
    br = jnp.pad(b_router, ((0, 0), (0, LANES - N_EXP)), constant_values=NEG_BIG)
    cache_k = jnp.concatenate(
        [cache_ckv[:, 0], jnp.pad(cache_krope[:, 0], ((0, 0), (0, 0), (0, LANES - ROPE)))],
        axis=-1).astype(BF16)

    xc2d = x_prompt.reshape(N_CTX, D)
    xl2d = x_sample.reshape(N_LAT, D)

    fcs_c, qcat_c, kcat_c, new_ckv, new_krope = _pre(
        xc2d, mod, w_in_ext, cdft, qg, kvg, wqa, wqr, None, None,
        n_seq=N_CTX_SEQ, seq_len=T_CTX, rope=False, mod_row0=0, mod_row_step=0)
    x1_c, h2_c, lpos_c, topw_c, bcnt_c = _mix(
        xc2d, mod, fcs_c, pdft_ctx, qcat_c, kcat_c, None, wfo, woa, wg, b_gate, wout, ln1_g, ln1_b,
        wr, br, tri, upper, rows=MIX_ROWS_CTX, n_seq=N_CTX_SEQ, seq_len=T_CTX, mod_row0=0,
        mod_row_step=0)

    fcs_l, qcat_l, kcat_l = _pre(
        xl2d, mod, w_in_ext, cdft, qg, kvg, wqa, wqr, cos_pad, sin_pad,
        n_seq=N_LAT_SEQ, seq_len=T_LAT, rope=True, mod_row0=1, mod_row_step=1)
    x1_l, h2_l, lpos_l, topw_l, bcnt_l = _mix(
        xl2d, mod, fcs_l, pdft_lat, qcat_l, kcat_l, cache_k, wfo, woa, wg, b_gate, wout, ln1_g,
        ln1_b, wr, br, tri, upper, rows=MIX_ROWS_LAT, n_seq=N_LAT_SEQ, seq_len=T_LAT, mod_row0=1,
        mod_row_step=1)

    blen = jnp.concatenate([bcnt_c[:, 0, :N_EXP], bcnt_l[:, 0, :N_EXP]], axis=0).astype(jnp.int32)
    blen = (blen + (blen & 1)) // 2
    lstart = jnp.cumsum(blen, axis=1) - blen
    count = jnp.sum(blen, axis=0)
    start = jnp.cumsum(count) - count
    gstart = start[None, :] + jnp.cumsum(blen, axis=0) - blen
    extra = (jnp.sum(blen, axis=1) - BLK_PAIRS).astype(jnp.int32)
    pieces = _piece_tables(lstart, blen, gstart) + (extra,)
    start = start.astype(jnp.int32)
    count = count.astype(jnp.int32)

    xs = _dispatch(*pieces, h2_c, h2_l, lpos_c, lpos_l)
    ys = _experts(start, count, xs,
                  w_gate_e[0], b_gate_e[0][:, None, :], w_up_e[0], b_up_e[0][:, None, :],
                  w_down_e[0], b_down_e[0][:, None, :])

    y_c = _combine(*pieces, lpos_c, topw_c, x1_c, mod, ln2_g, ln2_b, ys, blk0=0,
                   n_seq=N_CTX_SEQ, seq_len=T_CTX, mod_row0=0, mod_row_step=0)
    y_l = _combine(*pieces, lpos_l, topw_l, x1_l, mod, ln2_g, ln2_b, ys, blk0=N_CTX_BLK,
                   n_seq=N_LAT_SEQ, seq_len=T_LAT, mod_row0=1, mod_row_step=1)
    return (y_c.reshape(N_CTX_SEQ, T_CTX, D), y_l.reshape(N_LAT_SEQ, T_LAT, D), new_ckv, new_krope)
```

```python
import functools
import math

import numpy as np
import jax
import jax.numpy as jnp
from jax import lax
from jax.experimental import pallas as pl
from jax.experimental.pallas import tpu as pltpu

D = 1024
N_CTX_SEQ, T_CTX = 32, 256
N_LAT_SEQ, T_LAT = 4, 1024
PAST = 256
N_CTX = N_CTX_SEQ * T_CTX
N_LAT = N_LAT_SEQ * T_LAT
N_TOK = N_CTX + N_LAT
FW = 512
FG = 128
N_FG = FW // FG
HEADS = 8
QL = 256
KVL = 128
NOPE = 64
ROPE = 32
VH = 64
N_EXP = 32
TOPK = 4
D_EXP = 1024
SWIGLU_LIMIT = 7.0
SWIGLU_ALPHA = 1.702
LN_EPS = 1e-5
RMS_EPS = 1e-6
DN_ALPHA = 2.0 ** 0.25
ATT_SCALE = float(NOPE + ROPE) ** -0.5
ROPE_THETA = 10000.0
GRID_W = 64

LANES = 128
TB = 256
PRE_ROWS = 512
MIX_ROWS_CTX = 512
MIX_ROWS_LAT = 512
SUB = 8
ROW_TILE = 256
TILE_PAIRS = ROW_TILE // 2
RING = 4
N_CTX_BLK = N_CTX // TB
N_BLK = N_TOK // TB
BLK_PAIRS = TB * TOPK // 2
PAD_PAIRS_MAX = N_EXP // 2
BLK_PAIRS_BUF = BLK_PAIRS + 64
BLK_ROWS_BUF = 2 * BLK_PAIRS_BUF
NP_SORT = N_TOK * TOPK // 2
NP_MAX = NP_SORT + N_BLK * PAD_PAIRS_MAX
NP_PAD = NP_MAX + TILE_PAIRS
RUN_BITS = (TB // 2).bit_length()
EXTRA_BITS = PAD_PAIRS_MAX.bit_length()
U32 = jnp.uint32
VMEM_LIMIT = 56 * 1024 * 1024
NEG_BIG = -1e30

F32 = jnp.float32
BF16 = jnp.bfloat16
HIGHEST = lax.Precision.HIGHEST


def _dot(a, b):
    return jnp.dot(a, b, preferred_element_type=F32)


def _dot_nt(a, b):
    return lax.dot_general(a, b, (((1,), (1,)), ((), ())), preferred_element_type=F32)


def _params(sem):
    return pltpu.CompilerParams(dimension_semantics=sem, vmem_limit_bytes=VMEM_LIMIT)


def _full(shape):
    n = len(shape)
    return pl.BlockSpec(shape, lambda *_: (0,) * n, pipeline_mode=pl.Buffered(1))


def _norm_noaffine(x):
    mu = jnp.mean(x, axis=-1, keepdims=True)
    xc = x - mu
    var = jnp.mean(xc * xc, axis=-1, keepdims=True)
    return xc * lax.rsqrt(var + LN_EPS)


def _sigmoid(x):
    return 0.5 * jnp.tanh(0.5 * x) + 0.5


def _rms(x, g):
    return x * lax.rsqrt(jnp.mean(x * x, axis=-1, keepdims=True) + RMS_EPS) * g


def _mod_row(mod_ref, row0, step, seq):
    row = row0 if step == 0 else row0 + step * seq
    m = mod_ref[pl.ds(row, 1), :]
    return [m[:, i * D:(i + 1) * D] for i in range(6)]


def _prep_kernel(wqn_ref, wk_ref, wv_ref, wo_ref, qabs_ref, oabs_ref):
    for h in range(HEADS):
        qabs = lax.dot_general(wqn_ref[h], wk_ref[h], (((1,), (1,)), ((), ())),
                               preferred_element_type=F32, precision=HIGHEST)
        qabs_ref[:, h * KVL:(h + 1) * KVL] = qabs.astype(BF16)
        oabs = jnp.dot(wv_ref[h], wo_ref[h], preferred_element_type=F32, precision=HIGHEST)
        oabs_ref[h * KVL:(h + 1) * KVL, :] = oabs.astype(BF16)


def _prep(wqn, wk, wv, wo):
    return pl.pallas_call(
        _prep_kernel,
        out_shape=[jax.ShapeDtypeStruct((QL, HEADS * KVL), BF16),
                   jax.ShapeDtypeStruct((HEADS * KVL, D), BF16)],
        compiler_params=pltpu.CompilerParams(vmem_limit_bytes=VMEM_LIMIT),
        name="weight_prep",
    )(wqn, wk, wv, wo)


MOD_COLS = 1536


def _mod_kernel(c_ref, w_ref, b_ref, o_ref):
    c = c_ref[...]
    s = c * _sigmoid(c)
    s_hi = s.astype(BF16)
    s_lo = (s - s_hi.astype(F32)).astype(BF16)
    w = w_ref[...]
    w_hi = w.astype(BF16)
    w_lo = (w - w_hi.astype(F32)).astype(BF16)
    o_ref[...] = (_dot(s_hi, w_hi) + (_dot(s_hi, w_lo) + _dot(s_lo, w_hi))) + b_ref[...]


def _modulation(c_all, w_mod, b_mod):
    return pl.pallas_call(
        _mod_kernel,
        grid=(6 * D // MOD_COLS,),
        in_specs=[
            pl.BlockSpec((8, D), lambda i: (0, 0)),
            pl.BlockSpec((D, MOD_COLS), lambda i: (0, i)),
            pl.BlockSpec((1, MOD_COLS), lambda i: (0, i)),
        ],
        out_specs=pl.BlockSpec((8, MOD_COLS), lambda i: (0, i)),
        out_shape=jax.ShapeDtypeStruct((8, 6 * D), F32),
        compiler_params=_params(("arbitrary",)),
        name="modulation",
    )(c_all, w_mod, b_mod)


def _pre_kernel(*refs, rows, seq_len, seqs, rope, mod_row0, mod_row_step):
    if rope:
        (x_ref, mod_ref, win_ref, cdft_ref, qg_ref, kvg_ref, wqa_ref, wqr_ref, cos_ref, sin_ref,
         fcs_ref, qcat_ref, kcat_ref) = refs
    else:
        (x_ref, mod_ref, win_ref, cdft_ref, qg_ref, kvg_ref, wqa_ref, wqr_ref,
         fcs_ref, qcat_ref, kcat_ref, ckv_ref, krope_ref) = refs
    q_rows = rows // seqs
    b = pl.program_id(0) // (seq_len // q_rows)
    shift1, scale1 = _mod_row(mod_ref, mod_row0, mod_row_step, b)[:2]
    seq_rows = [slice(s * q_rows, (s + 1) * q_rows) for s in range(seqs)]

    x = x_ref[...]
    h1 = _norm_noaffine(x) * (1.0 + scale1) + shift1
    h1b = h1.astype(BF16)
    n_cols = 9 * LANES if rope else 8 * LANES
    proj = _dot(h1b, win_ref[:, :n_cols])

    f_b = proj[:, :FW].astype(BF16)
    for g in range(N_FG):
        r = _dot(f_b[:, g * FG:(g + 1) * FG], cdft_ref[...])
        for s, sl in enumerate(seq_rows):
            fcs_ref[s, 0, :, g * FG:(g + 1) * FG] = r[sl, :FG].astype(BF16)
            fcs_ref[s, 1, :, g * FG:(g + 1) * FG] = r[sl, FG:].astype(BF16)

    qn = _rms(proj[:, FW:FW + QL], qg_ref[...]).astype(BF16)
    ckv = _rms(proj[:, FW + QL:FW + QL + KVL], kvg_ref[...])
    kr = proj[:, 7 * LANES:8 * LANES]

    qa = _dot(qn, wqa_ref[...]) * ATT_SCALE
    if rope:
        cos = cos_ref[...]
        sin = sin_ref[...]
        qr2 = _dot(qn, wqr_ref[...])
        kr_keys = kr * cos + proj[:, 8 * LANES:9 * LANES] * sin
    else:
        qr2 = _dot(qn, wqr_ref[:, :HEADS * LANES])
        kr_keys = kr
    ckv_b = ckv.astype(BF16)
    kr_b = kr_keys.astype(BF16)
    for h in range(HEADS):
        qr_h = qr2[:, h * LANES:(h + 1) * LANES]
        if rope:
            qr_h = qr_h * cos + qr2[:, (HEADS + h) * LANES:(HEADS + h + 1) * LANES] * sin
        qa_h = qa[:, h * KVL:(h + 1) * KVL].astype(BF16)
        qr_h = (qr_h * ATT_SCALE).astype(BF16)
        for s, sl in enumerate(seq_rows):
            qcat_ref[s, h, :, :KVL] = qa_h[sl]
            qcat_ref[s, h, :, KVL:] = qr_h[sl]
    for s, sl in enumerate(seq_rows):
        kcat_ref[s, :, :KVL] = ckv_b[sl]
        kcat_ref[s, :, KVL:] = kr_b[sl]
        if not rope:
            ckv_ref[s, 0] = ckv[sl]
            krope_ref[s, 0] = kr[sl, :ROPE]


def _pre(x2d, mod, w_in_ext, cdft, qg, kvg, wqa, wqr, cos_pad, sin_pad, *, n_seq, seq_len, rope,
         mod_row0, mod_row_step):
    rows = PRE_ROWS
    seqs = max(1, rows // seq_len)
    q_rows = rows // seqs
    bps = seq_len // q_rows
    n_blk = n_seq * seq_len // rows
    seq_blk = lambda i: (i // bps, 0, i % bps, 0)
    in_specs = [
        pl.BlockSpec((rows, D), lambda i: (i, 0)),
        _full((8, 6 * D)),
        _full((D, 9 * LANES)),
        _full((FG, 2 * FG)),
        _full((1, QL)),
        _full((1, KVL)),
        _full((QL, HEADS * KVL)),
        _full((QL, 2 * HEADS * LANES)),
    ]
    args = [x2d, mod, w_in_ext, cdft, qg, kvg, wqa, wqr]
    out_specs = [
        pl.BlockSpec((seqs, 2, q_rows, FW), seq_blk),
        pl.BlockSpec((seqs, HEADS, q_rows, 2 * LANES), seq_blk),
        pl.BlockSpec((seqs, q_rows, 2 * LANES), lambda i: (i // bps, i % bps, 0)),
    ]
    out_shape = [
        jax.ShapeDtypeStruct((n_seq, 2, seq_len, FW), BF16),
        jax.ShapeDtypeStruct((n_seq, HEADS, seq_len, 2 * LANES), BF16),
        jax.ShapeDtypeStruct((n_seq, seq_len, 2 * LANES), BF16),
    ]
    if rope:
        in_specs += [pl.BlockSpec((q_rows, LANES), lambda i: (i % bps, 0)),
                     pl.BlockSpec((q_rows, LANES), lambda i: (i % bps, 0))]
        args += [cos_pad, sin_pad]
    else:
        out_specs += [pl.BlockSpec((seqs, 1, q_rows, KVL), seq_blk),
                      pl.BlockSpec((seqs, 1, q_rows, ROPE), seq_blk)]
        out_shape += [jax.ShapeDtypeStruct((n_seq, 1, seq_len, KVL), F32),
                      jax.ShapeDtypeStruct((n_seq, 1, seq_len, ROPE), F32)]
    kern = functools.partial(_pre_kernel, rows=rows, seq_len=seq_len, seqs=seqs, rope=rope,
                             mod_row0=mod_row0, mod_row_step=mod_row_step)
    return pl.pallas_call(
        kern, grid=(n_blk,), in_specs=in_specs, out_specs=out_specs, out_shape=out_shape,
        compiler_params=_params(("arbitrary",)),
        name="pre_lat" if rope else "pre_ctx",
    )(*args)


def _mix_kernel(*refs, rows, seq_len, seqs, cache, mod_row0, mod_row_step):
    if cache:
        (x_ref, mod_ref, fcs_ref, pdft_ref, qcat_ref, kcat_ref, cache_ref, wfo_ref, woa_ref,
         wg_ref, bg_ref, wout_ref, ln1g_ref, ln1b_ref, wr_ref, br_ref, tri_ref, upper_ref,
         x1_ref, h2_ref, lpos_ref, topw_ref, bcnt_ref) = refs
    else:
        (x_ref, mod_ref, fcs_ref, pdft_ref, qcat_ref, kcat_ref, wfo_ref, woa_ref,
         wg_ref, bg_ref, wout_ref, ln1g_ref, ln1b_ref, wr_ref, br_ref, tri_ref, upper_ref,
         x1_ref, h2_ref, lpos_ref, topw_ref, bcnt_ref) = refs
        cache_ref = None
    b = pl.program_id(0)
    shift1, scale1, gate1, shift2, scale2, _ = _mod_row(mod_ref, mod_row0, mod_row_step, b)

    x = x_ref[...]
    h1b = (_norm_noaffine(x) * (1.0 + scale1) + shift1).astype(BF16)

    mixed_rows, attn_rows = [], []
    for s in range(seqs):
        mixed_rows.append(_dot(pdft_ref[:, :seq_len], fcs_ref[s, 0])
                          + _dot(pdft_ref[:, seq_len:], fcs_ref[s, 1]))
        kc = kcat_ref[s]
        vals = kc[:, :KVL]
        if cache:
            kc2 = cache_ref[s]
            vals2 = kc2[:, :KVL]
        heads_out = []
        for h in range(HEADS):
            q = qcat_ref[s, h]
            s1 = _dot_nt(q, kc)
            m = jnp.max(s1, axis=-1, keepdims=True)
            if cache:
                s2 = _dot_nt(q, kc2)
                m = jnp.maximum(m, jnp.max(s2, axis=-1, keepdims=True))
            p1 = jnp.exp(s1 - m)
            l = jnp.sum(p1, axis=-1, keepdims=True)
            o = _dot(p1.astype(BF16), vals)
            if cache:
                p2 = jnp.exp(s2 - m)
                l = l + jnp.sum(p2, axis=-1, keepdims=True)
                o = o + _dot(p2.astype(BF16), vals2)
            heads_out.append((o / l).astype(BF16))
        attn_rows.append(jnp.concatenate(heads_out, axis=1))
    mixed = jnp.concatenate(mixed_rows, axis=0) if seqs > 1 else mixed_rows[0]
    attn = jnp.concatenate(attn_rows, axis=0) if seqs > 1 else attn_rows[0]
    f_out = _dot(mixed.astype(BF16), wfo_ref[...])
    m_out = _dot(attn, woa_ref[...])

    gates = _sigmoid(_dot(h1b, wg_ref[...]) + bg_ref[...])
    merged = gates[:, :D] * f_out + gates[:, D:] * m_out
    mix = _dot(merged.astype(BF16), wout_ref[...])

    x1 = _norm_noaffine(DN_ALPHA * x + gate1 * mix) * ln1g_ref[...] + ln1b_ref[...]
    x1_ref[...] = x1
    h2 = _norm_noaffine(x1) * (1.0 + scale2) + shift2
    h2_ref[...] = h2.astype(BF16)

    h_hi = h2.astype(BF16)
    h_lo = (h2 - h_hi.astype(F32)).astype(BF16)
    hi_part = _dot(h_hi, wr_ref[...])
    logits_all = (hi_part[:, :LANES] + hi_part[:, LANES:] + _dot(h_lo, wr_ref[:, :LANES])) + br_ref[...]
    lane = lax.broadcasted_iota(jnp.int32, (TB, LANES), 1).astype(F32)
    for r in range(rows // TB):
        blk = slice(r * TB, (r + 1) * TB)
        work = logits_all[blk]
        top_v, top_i = [], []
        for _ in range(TOPK):
            mk = jnp.max(work, axis=-1, keepdims=True)
            ik = jnp.min(jnp.where(work == mk, lane, float(LANES)), axis=-1, keepdims=True)
            work = jnp.where(lane == ik, -jnp.inf, work)
            top_v.append(mk)
            top_i.append(ik)
        exps = [jnp.exp(v - top_v[0]) for v in top_v]
        denom = exps[0] + exps[1] + exps[2] + exps[3]

        onehot = jnp.zeros((TB, LANES), F32)
        for ik in top_i:
            onehot = onehot + jnp.where(lane == ik, 1.0, 0.0)
        counts = jnp.sum(onehot, axis=0, keepdims=True)
        padded = counts + (counts - 2.0 * jnp.floor(0.5 * counts))
        lower = jnp.dot(jnp.broadcast_to(padded, (8, LANES)), upper_ref[...],
                        preferred_element_type=F32, precision=HIGHEST)[0:1, :]
        before = _dot(tri_ref[...], onehot.astype(BF16)) + lower
        lpos = jnp.zeros((TB, LANES), jnp.int32)
        topw = jnp.zeros((TB, LANES), F32)
        for k in range(TOPK):
            pos = jnp.sum(jnp.where(lane == top_i[k], before, 0.0), axis=-1, keepdims=True)
            lpos = jnp.where(lane == float(k), pos.astype(jnp.int32), lpos)
            topw = jnp.where(lane == float(k), exps[k] / denom, topw)
        lpos_ref[blk, :] = lpos
        topw_ref[blk, :] = topw
        bcnt_ref[r] = counts


def _mix(x2d, mod, fcs, pdft, qcat, kcat, cache_k, wfo, woa, wg, bg, wout, ln1g, ln1b, wr, br,
         tri, upper, *, rows, n_seq, seq_len, mod_row0, mod_row_step):
    cache = cache_k is not None
    seqs = max(1, rows // seq_len)
    q_rows = rows // seqs
    bps = seq_len // q_rows
    n_outer = n_seq // seqs
    in_specs = [
        pl.BlockSpec((rows, D), lambda b, j: (b * bps + j, 0)),
        _full((8, 6 * D)),
        pl.BlockSpec((seqs, 2, seq_len, FW), lambda b, j: (b, 0, 0, 0)),
        pl.BlockSpec((q_rows, 2 * seq_len), lambda b, j: (j, 0)),
        pl.BlockSpec((seqs, HEADS, q_rows, 2 * LANES), lambda b, j: (b, 0, j, 0)),
        pl.BlockSpec((seqs, seq_len, 2 * LANES), lambda b, j: (b, 0, 0)),
    ]
    args = [x2d, mod, fcs, pdft, qcat, kcat]
    if cache:
        in_specs.append(pl.BlockSpec((seqs, PAST, 2 * LANES), lambda b, j: (b, 0, 0)))
        args.append(cache_k)
    in_specs += [
        _full((FW, D)), _full((HEADS * KVL, D)), _full((D, 2 * D)), _full((1, 2 * D)),
        _full((D, D)), _full((1, D)), _full((1, D)), _full((D, 2 * LANES)), _full((1, LANES)),
        _full((TB, TB)), _full((LANES, LANES)),
    ]
    args += [wfo, woa, wg, bg, wout, ln1g, ln1b, wr, br, tri, upper]
    n_tok = n_seq * seq_len
    tok_spec = lambda w: pl.BlockSpec((rows, w), lambda b, j: (b * bps + j, 0))
    out_specs = [tok_spec(D), tok_spec(D), tok_spec(LANES), tok_spec(LANES),
                 pl.BlockSpec((rows // TB, 1, LANES), lambda b, j: (b * bps + j, 0, 0))]
    out_shape = [
        jax.ShapeDtypeStruct((n_tok, D), F32),
        jax.ShapeDtypeStruct((n_tok, D), BF16),
        jax.ShapeDtypeStruct((n_tok, LANES), jnp.int32),
        jax.ShapeDtypeStruct((n_tok, LANES), F32),
        jax.ShapeDtypeStruct((n_tok // TB, 1, LANES), F32),
    ]
    kern = functools.partial(_mix_kernel, rows=rows, seq_len=seq_len, seqs=seqs, cache=cache,
                             mod_row0=mod_row0, mod_row_step=mod_row_step)
    return pl.pallas_call(
        kern, grid=(n_outer, bps), in_specs=in_specs, out_specs=out_specs, out_shape=out_shape,
        compiler_params=_params(("arbitrary", "arbitrary")),
        name="mix_lat" if cache else "mix_ctx",
    )(*args)


CTX_ROWS = 512
CTX_SEQS = CTX_ROWS // T_CTX


def _ctx_kernel(x_ref, mod_ref, win_ref, cdft_ref, qg_ref, kvg_ref, wqa_ref, wqr_ref,
                pdft_ref, wfo_ref, woa_ref, wg_ref, bg_ref, wout_ref, ln1g_ref, ln1b_ref, wr_ref, br_ref,
                tri_ref, upper_ref,
                ckv_ref, krope_ref, x1_ref, h2_ref, lpos_ref, topw_ref, bcnt_ref,
                fcs_buf, qcat_buf, kcat_buf):
    stage = dict(rows=CTX_ROWS, seq_len=T_CTX, seqs=CTX_SEQS, mod_row0=0, mod_row_step=0)
    _pre_kernel(x_ref, mod_ref, win_ref, cdft_ref, qg_ref, kvg_ref, wqa_ref, wqr_ref,
                fcs_buf, qcat_buf, kcat_buf, ckv_ref, krope_ref, rope=False, **stage)
    _mix_kernel(x_ref, mod_ref, fcs_buf, pdft_ref, qcat_buf, kcat_buf, wfo_ref, woa_ref, wg_ref, bg_ref,
                wout_ref, ln1g_ref, ln1b_ref, wr_ref, br_ref, tri_ref, upper_ref,
                x1_ref, h2_ref, lpos_ref, topw_ref, bcnt_ref, cache=False, **stage)


def _ctx(x2d, mod, w_in_ext, cdft, qg, kvg, wqa, wqr, pdft, wfo, woa, wg, bg, wout, ln1g, ln1b, wr, br,
         tri, upper):
    rows, seqs = CTX_ROWS, CTX_SEQS
    tok = lambda w: pl.BlockSpec((rows, w), lambda i: (i, 0))
    seq4 = lambda a, b: pl.BlockSpec((seqs, 1, a, b), lambda i: (i, 0, 0, 0))
    in_specs = [
        tok(D), _full((8, 6 * D)), _full((D, 9 * LANES)), _full((FG, 2 * FG)), _full((1, QL)),
        _full((1, KVL)), _full((QL, HEADS * KVL)), _full((QL, 2 * HEADS * LANES)),
        _full((T_CTX, 2 * T_CTX)), _full((FW, D)), _full((HEADS * KVL, D)), _full((D, 2 * D)),
        _full((1, 2 * D)), _full((D, D)), _full((1, D)), _full((1, D)), _full((D, 2 * LANES)),
        _full((1, LANES)), _full((TB, TB)), _full((LANES, LANES)),
    ]
    out_specs = [seq4(T_CTX, KVL), seq4(T_CTX, ROPE), tok(D), tok(D), tok(LANES), tok(LANES),
                 pl.BlockSpec((rows // TB, 1, LANES), lambda i: (i, 0, 0))]
    out_shape = [
        jax.ShapeDtypeStruct((N_CTX_SEQ, 1, T_CTX, KVL), F32),
        jax.ShapeDtypeStruct((N_CTX_SEQ, 1, T_CTX, ROPE), F32),
        jax.ShapeDtypeStruct((N_CTX, D), F32),
        jax.ShapeDtypeStruct((N_CTX, D), BF16),
        jax.ShapeDtypeStruct((N_CTX, LANES), jnp.int32),
        jax.ShapeDtypeStruct((N_CTX, LANES), F32),
        jax.ShapeDtypeStruct((N_CTX // TB, 1, LANES), F32),
    ]
    return pl.pallas_call(
        _ctx_kernel, grid=(N_CTX // rows,), in_specs=in_specs, out_specs=out_specs, out_shape=out_shape,
        scratch_shapes=[pltpu.VMEM((seqs, 2, T_CTX, FW), BF16),
                        pltpu.VMEM((seqs, HEADS, T_CTX, 2 * LANES), BF16),
                        pltpu.VMEM((seqs, T_CTX, 2 * LANES), BF16)],
        compiler_params=_params(("arbitrary",)),
        name="ctx",
    )(x2d, mod, w_in_ext, cdft, qg, kvg, wqa, wqr, pdft, wfo, woa, wg, bg, wout, ln1g, ln1b, wr, br,
      tri, upper)


def _pair_tiles(ref, pair0, n_pairs):
    return ref.at[pl.ds(pl.multiple_of(pair0 * SUB, SUB), n_pairs * SUB), :]


def _piece_tables(lstart, blen, gstart):
    bits = jnp.arange(RUN_BITS, dtype=jnp.int32)
    n = blen[:, :, None]
    valid = (n >> bits) & 1
    done = (n >> (bits + 1)) << (bits + 1)
    rank = jnp.cumsum(valid, axis=1) - valid
    place = (valid[..., None] == 1) & (rank[..., None] == jnp.arange(N_EXP, dtype=jnp.int32))
    def table(first_row):
        rows = (first_row[:, :, None] + done) * SUB
        listed = jnp.sum(jnp.where(place, rows[..., None], 0), axis=1)
        return listed.reshape(-1).astype(jnp.int32)
    return jnp.sum(valid, axis=1).reshape(-1).astype(jnp.int32), table(lstart), table(gstart)


def _start_piece_copies(cnt_ref, loc_ref, glo_ref, blk, local_ref, global_ref, sem, to_global):
    for bit in range(RUN_BITS):
        size = (1 << bit) * SUB
        base = (blk * RUN_BITS + bit) * N_EXP

        def piece(t, carry):
            loc = local_ref.at[pl.ds(pl.multiple_of(loc_ref[base + t], SUB), size), :]
            glo = global_ref.at[pl.ds(pl.multiple_of(glo_ref[base + t], SUB), size), :]
            (pltpu.make_async_copy(loc, glo, sem) if to_global
             else pltpu.make_async_copy(glo, loc, sem)).start()
            return carry

        lax.fori_loop(0, cnt_ref[blk * RUN_BITS + bit], piece, 0)


def _wait_block_pieces(extra_ref, blk, vmem_ref, hbm_ref, sem, to_global):
    def wait(n_pairs):
        loc = _pair_tiles(vmem_ref, 0, n_pairs)
        glo = _pair_tiles(hbm_ref, 0, n_pairs)
        (pltpu.make_async_copy(loc, glo, sem) if to_global else pltpu.make_async_copy(glo, loc, sem)).wait()

    wait(BLK_PAIRS)
    extra = extra_ref[blk]
    for bit in range(EXTRA_BITS):
        @pl.when((extra & (1 << bit)) != 0)
        def _():
            wait(1 << bit)


def _load_pairs(ref, n_pairs):
    words = jnp.concatenate([ref[pl.ds(s, n_pairs, stride=SUB), :] for s in range(SUB)], axis=1)
    return pltpu.bitcast(words, BF16)


def _store_pairs(ref, rows, n_pairs):
    words = pltpu.bitcast(rows, U32)
    for s in range(SUB):
        ref[pl.ds(s, n_pairs, stride=SUB), :] = words[:, s * LANES:(s + 1) * LANES]


def _zero_fill_tail(zero_ref, hbm_ref, sem):
    zero_ref[...] = jnp.zeros(zero_ref.shape, U32)
    copies = [pltpu.make_async_copy(zero_ref, _pair_tiles(hbm_ref, p, TILE_PAIRS), sem)
              for p in range(NP_SORT, NP_PAD, TILE_PAIRS)]
    for cp in copies:
        cp.start()
    for cp in copies:
        cp.wait()


def _dispatch_kernel(cnt_ref, loc_ref, glo_ref, extra_ref, h2c_ref, h2l_ref, lpc_ref, lpl_ref, xs_ref,
                     buf, zbuf, sem, zsem):
    b = pl.program_id(0)
    slot = b & 1

    def start_runs(blk, s):
        _start_piece_copies(cnt_ref, loc_ref, glo_ref, blk, buf.at[s], xs_ref, sem.at[s], True)

    def wait_runs(blk, s):
        _wait_block_pieces(extra_ref, blk, buf.at[s], xs_ref, sem.at[s], True)

    @pl.when(b == 0)
    def _():
        _zero_fill_tail(zbuf, xs_ref, zsem)

    @pl.when(b >= 2)
    def _():
        wait_runs(b - 2, slot)

    def sort_block(h2_ref, lp_ref):
        col = lax.broadcasted_iota(jnp.int32, (TB, BLK_ROWS_BUF), 1)
        lp = lp_ref[...]
        pick = jnp.zeros((TB, BLK_ROWS_BUF), F32)
        for k in range(TOPK):
            pick = pick + jnp.where(col == lp[:, k:k + 1], 1.0, 0.0)
        sorted_rows = lax.dot_general(pick.astype(BF16), h2_ref[...],
                                      (((0,), (0,)), ((), ())), preferred_element_type=F32)
        _store_pairs(buf.at[slot], sorted_rows.astype(BF16), BLK_PAIRS_BUF)

    @pl.when(b < N_CTX_BLK)
    def _():
        sort_block(h2c_ref, lpc_ref)

    @pl.when(b >= N_CTX_BLK)
    def _():
        sort_block(h2l_ref, lpl_ref)

    start_runs(b, slot)

    @pl.when(b == N_BLK - 1)
    def _():
        wait_runs(b - 1, 1 - slot)
        wait_runs(b, slot)


def _dispatch(cnt, loc, glo, extra, h2c, h2l, lpc, lpl):
    ctx_idx = lambda i, *_: (jnp.minimum(i, N_CTX_BLK - 1), 0)
    lat_idx = lambda i, *_: (jnp.maximum(i - N_CTX_BLK, 0), 0)
    grid_spec = pltpu.PrefetchScalarGridSpec(
        num_scalar_prefetch=4,
        grid=(N_BLK,),
        in_specs=[
            pl.BlockSpec((TB, D), ctx_idx),
            pl.BlockSpec((TB, D), lat_idx),
            pl.BlockSpec((TB, LANES), ctx_idx),
            pl.BlockSpec((TB, LANES), lat_idx),
        ],
        out_specs=pl.BlockSpec(memory_space=pl.ANY),
        scratch_shapes=[pltpu.VMEM((2, BLK_PAIRS_BUF * SUB, LANES), U32),
                        pltpu.VMEM((TILE_PAIRS * SUB, LANES), U32),
                        pltpu.SemaphoreType.DMA((2,)),
                        pltpu.SemaphoreType.DMA(())],
    )
    return pl.pallas_call(
        _dispatch_kernel, grid_spec=grid_spec,
        out_shape=jax.ShapeDtypeStruct((NP_PAD * SUB, LANES), U32),
        compiler_params=_params(("arbitrary",)),
        name="dispatch",
    )(cnt, loc, glo, extra, h2c, h2l, lpc, lpl)


def _expert_kernel(start_ref, count_ref, xs_ref, wg_ref, bg_ref, wu_ref, bu_ref, wd_ref, bd_ref, ys_ref,
                   wbf, xbuf, ybuf, sem_in, sem_out):
    e = pl.program_id(0)
    pair0 = start_ref[e]

    def tiles_of(ex):
        return lax.shift_right_logical(count_ref[ex] + (TILE_PAIRS - 1), TILE_PAIRS.bit_length() - 1)

    n_tiles = tiles_of(e)

    def in_copy_of(ex, j, slot):
        return pltpu.make_async_copy(_pair_tiles(xs_ref, start_ref[ex] + j * TILE_PAIRS, TILE_PAIRS),
                                     xbuf.at[slot], sem_in.at[slot])

    def in_copy(j, slot):
        return in_copy_of(e, j, slot)

    def out_copy(j, slot):
        return pltpu.make_async_copy(ybuf.at[slot], _pair_tiles(ys_ref, pair0 + j * TILE_PAIRS, TILE_PAIRS),
                                     sem_out.at[slot])

    def start_first_tiles(ex):
        for j in range(RING - 1):
            @pl.when(tiles_of(ex) > j)
            def _():
                in_copy_of(ex, j, j).start()

    @pl.when(e == 0)
    def _():
        _zero_fill_tail(ybuf.at[0], ys_ref, sem_out.at[0])
        start_first_tiles(0)

    @pl.when(n_tiles > 0)
    def _():
        wbf[0] = wg_ref[...].astype(BF16)
        wbf[1] = wu_ref[...].astype(BF16)
        wbf[2] = wd_ref[...].astype(BF16)

        def tile(j, carry):
            slot = j & (RING - 1)

            ahead = j + (RING - 1)

            @pl.when(ahead < n_tiles)
            def _():
                in_copy(ahead, ahead & (RING - 1)).start()

            in_copy(j, slot).wait()

            @pl.when(j >= RING)
            def _():
                out_copy(j - RING, slot).wait()

            xb = _load_pairs(xbuf.at[slot], TILE_PAIRS)
            gt = jnp.minimum(_dot(xb, wbf[0]) + bg_ref[...], SWIGLU_LIMIT)
            up = jnp.clip(_dot(xb, wbf[1]) + bu_ref[...], -SWIGLU_LIMIT, SWIGLU_LIMIT)
            act = gt * _sigmoid(SWIGLU_ALPHA * gt) * (up + 1.0)
            y = _dot(act.astype(BF16), wbf[2]) + bd_ref[...]
            _store_pairs(ybuf.at[slot], y.astype(BF16), TILE_PAIRS)
            out_copy(j, slot).start()
            return carry

        lax.fori_loop(0, n_tiles, tile, 0)

        for back in range(RING, 0, -1):
            @pl.when(n_tiles >= back)
            def _():
                out_copy(n_tiles - back, (n_tiles - back) & (RING - 1)).wait()

    @pl.when(e + 1 < N_EXP)
    def _():
        start_first_tiles(e + 1)


def _experts(start, count, xs, wg, bg, wu, bu, wd, bd):
    w_idx = lambda e, *_: (0, e, 0, 0)
    grid_spec = pltpu.PrefetchScalarGridSpec(
        num_scalar_prefetch=2,
        grid=(N_EXP,),
        in_specs=[
            pl.BlockSpec(memory_space=pl.ANY),
            pl.BlockSpec((None, None, D, D_EXP), w_idx),
            pl.BlockSpec((None, None, 1, D_EXP), w_idx),
            pl.BlockSpec((None, None, D, D_EXP), w_idx),
            pl.BlockSpec((None, None, 1, D_EXP), w_idx),
            pl.BlockSpec((None, None, D_EXP, D), w_idx),
            pl.BlockSpec((None, None, 1, D), w_idx),
        ],
        out_specs=pl.BlockSpec(memory_space=pl.ANY),
        scratch_shapes=[pltpu.VMEM((3, D, D_EXP), BF16),
                        pltpu.VMEM((RING, TILE_PAIRS * SUB, LANES), U32),
                        pltpu.VMEM((RING, TILE_PAIRS * SUB, LANES), U32),
                        pltpu.SemaphoreType.DMA((RING,)),
                        pltpu.SemaphoreType.DMA((RING,))],
    )
    return pl.pallas_call(
        _expert_kernel, grid_spec=grid_spec,
        out_shape=jax.ShapeDtypeStruct((NP_PAD * SUB, LANES), U32),
        compiler_params=_params(("arbitrary",)),
        name="experts",
    )(start, count, xs, wg, bg, wu, bu, wd, bd)


def _combine_kernel(cnt_ref, loc_ref, glo_ref, extra_ref, lp_ref, topw_ref, x1_ref, mod_ref, g_ref, b_ref,
                    ys_ref, o_ref, buf, sem, *, blk0, n_blk, blocks_per_seq, mod_row0, mod_row_step):
    i = pl.program_id(0)
    slot = i & 1
    gate2 = _mod_row(mod_ref, mod_row0, mod_row_step, i // blocks_per_seq)[5]

    def start_runs(step, s):
        _start_piece_copies(cnt_ref, loc_ref, glo_ref, blk0 + step, buf.at[s], ys_ref, sem.at[s], False)

    @pl.when(i == 0)
    def _():
        buf[...] = jnp.zeros(buf.shape, U32)
        start_runs(0, 0)

    @pl.when(i + 1 < n_blk)
    def _():
        start_runs(i + 1, 1 - slot)

    _wait_block_pieces(extra_ref, blk0 + i, buf.at[slot], ys_ref, sem.at[slot], False)

    y_sorted = _load_pairs(buf.at[slot], BLK_PAIRS_BUF)
    col = lax.broadcasted_iota(jnp.int32, (TB, BLK_ROWS_BUF), 1)
    lp = lp_ref[...]
    topw = topw_ref[...]
    weights = jnp.zeros((TB, BLK_ROWS_BUF), F32)
    for k in range(TOPK):
        weights = weights + jnp.where(col == lp[:, k:k + 1], topw[:, k:k + 1], 0.0)
    ffn = _dot(weights.astype(BF16), y_sorted)
    y = DN_ALPHA * x1_ref[...] + gate2 * ffn
    o_ref[...] = _norm_noaffine(y) * g_ref[...] + b_ref[...]


def _combine(cnt, loc, glo, extra, lpos, topw, x1, mod, ln2g, ln2b, ys, *, blk0, n_seq, seq_len,
             mod_row0, mod_row_step):
    bps = seq_len // TB
    n_blk = n_seq * bps
    tok = lambda i, *_: (i, 0)
    const = lambda i, *_: (0, 0)
    grid_spec = pltpu.PrefetchScalarGridSpec(
        num_scalar_prefetch=4,
        grid=(n_blk,),
        in_specs=[
            pl.BlockSpec((TB, LANES), tok),
            pl.BlockSpec((TB, LANES), tok),
            pl.BlockSpec((TB, D), tok),
            pl.BlockSpec((8, 6 * D), const),
            pl.BlockSpec((1, D), const),
            pl.BlockSpec((1, D), const),
            pl.BlockSpec(memory_space=pl.ANY),
        ],
        out_specs=pl.BlockSpec((TB, D), tok),
        scratch_shapes=[pltpu.VMEM((2, BLK_PAIRS_BUF * SUB, LANES), U32), pltpu.SemaphoreType.DMA((2,))],
    )
    kern = functools.partial(_combine_kernel, blk0=blk0, n_blk=n_blk, blocks_per_seq=bps,
                             mod_row0=mod_row0, mod_row_step=mod_row_step)
    return pl.pallas_call(
        kern, grid_spec=grid_spec,
        out_shape=jax.ShapeDtypeStruct((n_seq * seq_len, D), F32),
        compiler_params=_params(("arbitrary",)),
        name="combine_lat" if blk0 else "combine_ctx",
    )(cnt, loc, glo, extra, lpos, topw, x1, mod, ln2g, ln2b, ys)


def _dft_tables():
    def cs(n):
        k = np.arange(n, dtype=np.int64)
        ang = 2.0 * np.pi * ((k[:, None] * k[None, :]) % n).astype(np.float64) / n
        return np.cos(ang) / math.sqrt(n), np.sin(ang) / math.sqrt(n)

    c, s = cs(FG)
    cdft = np.concatenate([c, s], axis=1).astype(np.float32)
    pd = {}
    for t in (T_CTX, T_LAT):
        c, s = cs(t)
        pd[t] = np.concatenate([c, -s], axis=1).astype(np.float32)
    tri = np.tril(np.ones((TB, TB), np.float32), k=-1)
    upper = np.triu(np.ones((LANES, LANES), np.float32), k=1)
    return cdft, pd, tri, upper


_ROT_PERM = np.array(list(range(8, 16)) + list(range(0, 8)) + list(range(24, 32)) + list(range(16, 24)))
_ROT_SIGN = np.array([-1.0] * 8 + [1.0] * 8 + [-1.0] * 8 + [1.0] * 8, np.float32)


def _rope_tables():
    rows = T_LAT // GRID_W
    row = jnp.repeat(jnp.arange(rows, dtype=F32), GRID_W)
    col = jnp.tile(jnp.arange(GRID_W, dtype=F32), rows)
    axis_dim = ROPE // 2
    inv_freq = ROPE_THETA ** (-jnp.arange(0, axis_dim, 2, dtype=F32) / axis_dim)
    ang_r = row[:, None] * inv_freq[None, :]
    ang_c = col[:, None] * inv_freq[None, :]
    ang = jnp.concatenate([ang_r, ang_r, ang_c, ang_c], axis=-1)
    pad = ((0, 0), (0, LANES - ROPE))
    return jnp.pad(jnp.cos(ang), pad), jnp.pad(jnp.sin(ang), pad)


def kernel(x_prompt, x_sample, cache_ckv, cache_krope, c, c_ctx, w_mod, b_mod, w_in, q_norm_g, w_q_up,
           kv_norm_g, w_kv_up, w_fourier_o, w_mla_o, w_gate, b_gate, w_out, ln1_g, ln1_b, w_router,
           b_router, w_gate_e, b_gate_e, w_up_e, b_up_e, w_down_e, b_down_e, ln2_g, ln2_b):
    cdft_np, pdft_np, tri_np, upper_np = _dft_tables()
    upper = jnp.asarray(upper_np)
    cdft = jnp.asarray(cdft_np).astype(BF16)
    pdft_ctx = jnp.asarray(pdft_np[T_CTX]).astype(BF16)
    pdft_lat = jnp.asarray(pdft_np[T_LAT]).astype(BF16)
    tri = jnp.asarray(tri_np).astype(BF16)
    cos_pad, sin_pad = _rope_tables()

    w_in0 = w_in[0]
    kr_w = w_in0[:, FW + QL + KVL:]
    lane_pad = ((0, 0), (0, LANES - ROPE))
    w_in_ext = jnp.concatenate(
        [w_in0[:, :FW + QL + KVL], jnp.pad(kr_w, lane_pad),
         jnp.pad(kr_w[:, _ROT_PERM] * _ROT_SIGN, lane_pad)], axis=1).astype(BF16)
    wq3 = w_q_up[0].reshape(QL, HEADS, NOPE + ROPE)
    wq_nope = jnp.transpose(wq3[:, :, :NOPE], (1, 0, 2))
    wq_rope = wq3[:, :, NOPE:]
    head_pad = ((0, 0), (0, 0), (0, LANES - ROPE))
    wqr = jnp.concatenate(
        [jnp.pad(wq_rope, head_pad).reshape(QL, HEADS * LANES),
         jnp.pad(wq_rope[:, :, _ROT_PERM] * _ROT_SIGN, head_pad).reshape(QL, HEADS * LANES)],
        axis=1).astype(BF16)
    wkv3 = w_kv_up[0].reshape(KVL, HEADS, NOPE + VH)
    wk = jnp.transpose(wkv3[:, :, :NOPE], (1, 0, 2))
    wv = jnp.transpose(wkv3[:, :, NOPE:], (1, 0, 2))
    wo3 = w_mla_o[0].reshape(HEADS, VH, D)
    wqa, woa = _prep(wq_nope, wk, wv, wo3)

    c_all = jnp.concatenate([c_ctx[None, :], c, jnp.zeros((8 - 1 - N_LAT_SEQ, D), F32)], axis=0)
    mod = _modulation(c_all, w_mod[0], b_mod)

    qg = q_norm_g
    kvg = kv_norm_g
    wfo = w_fourier_o[0].astype(BF16)
    wg = w_gate[0].astype(BF16)
    wout = w_out[0].astype(BF16)
    wr_f32 = jnp.pad(w_router[0], ((0, 0), (0, LANES - N_EXP)))
    wr_hi = wr_f32.astype(BF16)
    wr = jnp.concatenate([wr_hi, (wr_f32 - wr_hi.astype(F32)).astype(BF16)], axis=1)
    br = jnp.pad(b_router, ((0, 0), (0, LANES - N_EXP)), constant_values=NEG_BIG)
    cache_k = jnp.concatenate(
        [cache_ckv[:, 0], jnp.pad(cache_krope[:, 0], ((0, 0), (0, 0), (0, LANES - ROPE)))],
        axis=-1).astype(BF16)

    xc2d = x_prompt.reshape(N_CTX, D)
    xl2d = x_sample.reshape(N_LAT, D)

    new_ckv, new_krope, x1_c, h2_c, lpos_c, topw_c, bcnt_c = _ctx(
        xc2d, mod, w_in_ext, cdft, qg, kvg, wqa, wqr, pdft_ctx, wfo, woa, wg, b_gate, wout, ln1_g, ln1_b,
        wr, br, tri, upper)

    fcs_l, qcat_l, kcat_l = _pre(
        xl2d, mod, w_in_ext, cdft, qg, kvg, wqa, wqr, cos_pad, sin_pad,
        n_seq=N_LAT_SEQ, seq_len=T_LAT, rope=True, mod_row0=1, mod_row_step=1)
    x1_l, h2_l, lpos_l, topw_l, bcnt_l = _mix(
        xl2d, mod, fcs_l, pdft_lat, qcat_l, kcat_l, cache_k, wfo, woa, wg, b_gate, wout, ln1_g,
        ln1_b, wr, br, tri, upper, rows=MIX_ROWS_LAT, n_seq=N_LAT_SEQ, seq_len=T_LAT, mod_row0=1,
        mod_row_step=1)

    blen = jnp.concatenate([bcnt_c[:, 0, :N_EXP], bcnt_l[:, 0, :N_EXP]], axis=0).astype(jnp.int32)
    blen = (blen + (blen & 1)) // 2
    lstart = jnp.cumsum(blen, axis=1) - blen
    count = jnp.sum(blen, axis=0)
    start = jnp.cumsum(count) - count
    gstart = start[None, :] + jnp.cumsum(blen, axis=0) - blen
    extra = (jnp.sum(blen, axis=1) - BLK_PAIRS).astype(jnp.int32)
    pieces = _piece_tables(lstart, blen, gstart) + (extra,)
    start = start.astype(jnp.int32)
    count = count.astype(jnp.int32)

    xs = _dispatch(*pieces, h2_c, h2_l, lpos_c, lpos_l)
    ys = _experts(start, count, xs,
                  w_gate_e, b_gate_e[:, :, None, :], w_up_e, b_up_e[:, :, None, :],
                  w_down_e, b_down_e[:, :, None, :])

    y_c = _combine(*pieces, lpos_c, topw_c, x1_c, mod, ln2_g, ln2_b, ys, blk0=0,
                   n_seq=N_CTX_SEQ, seq_len=T_CTX, mod_row0=0, mod_row_step=0)
    y_l = _combine(*pieces, lpos_l, topw_l, x1_l, mod, ln2_g, ln2_b, ys, blk0=N_CTX_BLK,
                   n_seq=N_LAT_SEQ, seq_len=T_LAT, mod_row0=1, mod_row_step=1)
    return (y_c.reshape(N_CTX_SEQ, T_CTX, D), y_l.reshape(N_LAT_SEQ, T_LAT, D), new_ckv, new_krope)
```

```python
import functools
import math

import numpy as np
import jax
import jax.numpy as jnp
from jax import lax
from jax.experimental import pallas as pl
from jax.experimental.pallas import tpu as pltpu

D = 1024
N_CTX_SEQ, T_CTX = 32, 256
N_LAT_SEQ, T_LAT = 4, 1024
PAST = 256
N_CTX = N_CTX_SEQ * T_CTX
N_LAT = N_LAT_SEQ * T_LAT
N_TOK = N_CTX + N_LAT
FW = 512
FG = 128
N_FG = FW // FG
HEADS = 8
QL = 256
KVL = 128
NOPE = 64
ROPE = 32
VH = 64
N_EXP = 32
TOPK = 4
D_EXP = 1024
SWIGLU_LIMIT = 7.0
SWIGLU_ALPHA = 1.702
LN_EPS = 1e-5
RMS_EPS = 1e-6
DN_ALPHA = 2.0 ** 0.25
ATT_SCALE = float(NOPE + ROPE) ** -0.5
ROPE_THETA = 10000.0
GRID_W = 64

LANES = 128
TB = 256
PRE_ROWS = 512
MIX_ROWS_CTX = 512
MIX_ROWS_LAT = 512
SUB = 8
ROW_TILE = 256
TILE_PAIRS = ROW_TILE // 2
RING = 4
N_CTX_BLK = N_CTX // TB
N_BLK = N_TOK // TB
BLK_PAIRS = TB * TOPK // 2
PAD_PAIRS_MAX = N_EXP // 2
BLK_PAIRS_BUF = BLK_PAIRS + 64
BLK_ROWS_BUF = 2 * BLK_PAIRS_BUF
NP_SORT = N_TOK * TOPK // 2
NP_MAX = NP_SORT + N_BLK * PAD_PAIRS_MAX
NP_PAD = NP_MAX + TILE_PAIRS
RUN_BITS = (TB // 2).bit_length()
EXTRA_BITS = PAD_PAIRS_MAX.bit_length()
U32 = jnp.uint32
VMEM_LIMIT = 56 * 1024 * 1024
NEG_BIG = -1e30

F32 = jnp.float32
BF16 = jnp.bfloat16
HIGHEST = lax.Precision.HIGHEST


def _dot(a, b):
    return jnp.dot(a, b, preferred_element_type=F32)


def _dot_nt(a, b):
    return lax.dot_general(a, b, (((1,), (1,)), ((), ())), preferred_element_type=F32)


def _params(sem):
    return pltpu.CompilerParams(dimension_semantics=sem, vmem_limit_bytes=VMEM_LIMIT)


def _full(shape):
    n = len(shape)
    return pl.BlockSpec(shape, lambda *_: (0,) * n, pipeline_mode=pl.Buffered(1))


def _norm_noaffine(x):
    mu = jnp.mean(x, axis=-1, keepdims=True)
    xc = x - mu
    var = jnp.mean(xc * xc, axis=-1, keepdims=True)
    return xc * lax.rsqrt(var + LN_EPS)


def _sigmoid(x):
    return 0.5 * jnp.tanh(0.5 * x) + 0.5


def _rms(x, g):
    return x * lax.rsqrt(jnp.mean(x * x, axis=-1, keepdims=True) + RMS_EPS) * g


def _mod_row(mod_ref, row0, step, seq):
    row = row0 if step == 0 else row0 + step * seq
    m = mod_ref[pl.ds(row, 1), :]
    return [m[:, i * D:(i + 1) * D] for i in range(6)]


def _prep_kernel(wqn_ref, wk_ref, wv_ref, wo_ref, qabs_ref, oabs_ref):
    for h in range(HEADS):
        qabs = lax.dot_general(wqn_ref[h], wk_ref[h], (((1,), (1,)), ((), ())),
                               preferred_element_type=F32, precision=HIGHEST)
        qabs_ref[:, h * KVL:(h + 1) * KVL] = qabs.astype(BF16)
        oabs = jnp.dot(wv_ref[h], wo_ref[h], preferred_element_type=F32, precision=HIGHEST)
        oabs_ref[h * KVL:(h + 1) * KVL, :] = oabs.astype(BF16)


def _prep(wqn, wk, wv, wo):
    return pl.pallas_call(
        _prep_kernel,
        out_shape=[jax.ShapeDtypeStruct((QL, HEADS * KVL), BF16),
                   jax.ShapeDtypeStruct((HEADS * KVL, D), BF16)],
        compiler_params=pltpu.CompilerParams(vmem_limit_bytes=VMEM_LIMIT),
        name="weight_prep",
    )(wqn, wk, wv, wo)


MOD_COLS = 1536


def _mod_kernel(c_ref, w_ref, b_ref, o_ref):
    c = c_ref[...]
    s = c * _sigmoid(c)
    s_hi = s.astype(BF16)
    s_lo = (s - s_hi.astype(F32)).astype(BF16)
    w = w_ref[...]
    w_hi = w.astype(BF16)
    w_lo = (w - w_hi.astype(F32)).astype(BF16)
    o_ref[...] = (_dot(s_hi, w_hi) + (_dot(s_hi, w_lo) + _dot(s_lo, w_hi))) + b_ref[...]


def _modulation(c_all, w_mod, b_mod):
    return pl.pallas_call(
        _mod_kernel,
        grid=(6 * D // MOD_COLS,),
        in_specs=[
            pl.BlockSpec((8, D), lambda i: (0, 0)),
            pl.BlockSpec((D, MOD_COLS), lambda i: (0, i)),
            pl.BlockSpec((1, MOD_COLS), lambda i: (0, i)),
        ],
        out_specs=pl.BlockSpec((8, MOD_COLS), lambda i: (0, i)),
        out_shape=jax.ShapeDtypeStruct((8, 6 * D), F32),
        compiler_params=_params(("arbitrary",)),
        name="modulation",
    )(c_all, w_mod, b_mod)


def _pre_kernel(*refs, rows, seq_len, seqs, rope, mod_row0, mod_row_step):
    if rope:
        (x_ref, mod_ref, win_ref, cdft_ref, qg_ref, kvg_ref, wqa_ref, wqr_ref, cos_ref, sin_ref,
         fcs_ref, qcat_ref, kcat_ref) = refs
    else:
        (x_ref, mod_ref, win_ref, cdft_ref, qg_ref, kvg_ref, wqa_ref, wqr_ref,
         fcs_ref, qcat_ref, kcat_ref, ckv_ref, krope_ref) = refs
    q_rows = rows // seqs
    b = pl.program_id(0) // (seq_len // q_rows)
    shift1, scale1 = _mod_row(mod_ref, mod_row0, mod_row_step, b)[:2]
    seq_rows = [slice(s * q_rows, (s + 1) * q_rows) for s in range(seqs)]

    x = x_ref[...]
    h1 = _norm_noaffine(x) * (1.0 + scale1) + shift1
    h1b = h1.astype(BF16)
    n_cols = 9 * LANES if rope else 8 * LANES
    proj = _dot(h1b, win_ref[:, :n_cols])

    f_b = proj[:, :FW].astype(BF16)
    for g in range(N_FG):
        r = _dot(f_b[:, g * FG:(g + 1) * FG], cdft_ref[...])
        for s, sl in enumerate(seq_rows):
            fcs_ref[s, 0, :, g * FG:(g + 1) * FG] = r[sl, :FG].astype(BF16)
            fcs_ref[s, 1, :, g * FG:(g + 1) * FG] = r[sl, FG:].astype(BF16)

    qn = _rms(proj[:, FW:FW + QL], qg_ref[...]).astype(BF16)
    ckv = _rms(proj[:, FW + QL:FW + QL + KVL], kvg_ref[...])
    kr = proj[:, 7 * LANES:8 * LANES]

    qa = _dot(qn, wqa_ref[...]) * ATT_SCALE
    if rope:
        cos = cos_ref[...]
        sin = sin_ref[...]
        qr2 = _dot(qn, wqr_ref[...])
        kr_keys = kr * cos + proj[:, 8 * LANES:9 * LANES] * sin
    else:
        qr2 = _dot(qn, wqr_ref[:, :HEADS * LANES])
        kr_keys = kr
    ckv_b = ckv.astype(BF16)
    kr_b = kr_keys.astype(BF16)
    for h in range(HEADS):
        qr_h = qr2[:, h * LANES:(h + 1) * LANES]
        if rope:
            qr_h = qr_h * cos + qr2[:, (HEADS + h) * LANES:(HEADS + h + 1) * LANES] * sin
        qa_h = qa[:, h * KVL:(h + 1) * KVL].astype(BF16)
        qr_h = (qr_h * ATT_SCALE).astype(BF16)
        for s, sl in enumerate(seq_rows):
            qcat_ref[s, h, :, :KVL] = qa_h[sl]
            qcat_ref[s, h, :, KVL:] = qr_h[sl]
    for s, sl in enumerate(seq_rows):
        kcat_ref[s, :, :KVL] = ckv_b[sl]
        kcat_ref[s, :, KVL:] = kr_b[sl]
        if not rope:
            ckv_ref[s, 0] = ckv[sl]
            krope_ref[s, 0] = kr[sl, :ROPE]


def _pre(x2d, mod, w_in_ext, cdft, qg, kvg, wqa, wqr, cos_pad, sin_pad, *, n_seq, seq_len, rope,
         mod_row0, mod_row_step):
    rows = PRE_ROWS
    seqs = max(1, rows // seq_len)
    q_rows = rows // seqs
    bps = seq_len // q_rows
    n_blk = n_seq * seq_len // rows
    seq_blk = lambda i: (i // bps, 0, i % bps, 0)
    in_specs = [
        pl.BlockSpec((rows, D), lambda i: (i, 0)),
        _full((8, 6 * D)),
        _full((D, 9 * LANES)),
        _full((FG, 2 * FG)),
        _full((1, QL)),
        _full((1, KVL)),
        _full((QL, HEADS * KVL)),
        _full((QL, 2 * HEADS * LANES)),
    ]
    args = [x2d, mod, w_in_ext, cdft, qg, kvg, wqa, wqr]
    out_specs = [
        pl.BlockSpec((seqs, 2, q_rows, FW), seq_blk),
        pl.BlockSpec((seqs, HEADS, q_rows, 2 * LANES), seq_blk),
        pl.BlockSpec((seqs, q_rows, 2 * LANES), lambda i: (i // bps, i % bps, 0)),
    ]
    out_shape = [
        jax.ShapeDtypeStruct((n_seq, 2, seq_len, FW), BF16),
        jax.ShapeDtypeStruct((n_seq, HEADS, seq_len, 2 * LANES), BF16),
        jax.ShapeDtypeStruct((n_seq, seq_len, 2 * LANES), BF16),
    ]
    if rope:
        in_specs += [pl.BlockSpec((q_rows, LANES), lambda i: (i % bps, 0)),
                     pl.BlockSpec((q_rows, LANES), lambda i: (i % bps, 0))]
        args += [cos_pad, sin_pad]
    else:
        out_specs += [pl.BlockSpec((seqs, 1, q_rows, KVL), seq_blk),
                      pl.BlockSpec((seqs, 1, q_rows, ROPE), seq_blk)]
        out_shape += [jax.ShapeDtypeStruct((n_seq, 1, seq_len, KVL), F32),
                      jax.ShapeDtypeStruct((n_seq, 1, seq_len, ROPE), F32)]
    kern = functools.partial(_pre_kernel, rows=rows, seq_len=seq_len, seqs=seqs, rope=rope,
                             mod_row0=mod_row0, mod_row_step=mod_row_step)
    return pl.pallas_call(
        kern, grid=(n_blk,), in_specs=in_specs, out_specs=out_specs, out_shape=out_shape,
        compiler_params=_params(("arbitrary",)),
        name="pre_lat" if rope else "pre_ctx",
    )(*args)


def _mix_kernel(*refs, rows, seq_len, seqs, cache, mod_row0, mod_row_step):
    if cache:
        (x_ref, mod_ref, fcs_ref, pdft_ref, qcat_ref, kcat_ref, cache_ref, wfo_ref, woa_ref,
         wg_ref, bg_ref, wout_ref, ln1g_ref, ln1b_ref, wr_ref, br_ref, tri_ref, upper_ref,
         x1_ref, h2_ref, lpos_ref, topw_ref, bcnt_ref) = refs
    else:
        (x_ref, mod_ref, fcs_ref, pdft_ref, qcat_ref, kcat_ref, wfo_ref, woa_ref,
         wg_ref, bg_ref, wout_ref, ln1g_ref, ln1b_ref, wr_ref, br_ref, tri_ref, upper_ref,
         x1_ref, h2_ref, lpos_ref, topw_ref, bcnt_ref) = refs
        cache_ref = None
    b = pl.program_id(0)
    shift1, scale1, gate1, shift2, scale2, _ = _mod_row(mod_ref, mod_row0, mod_row_step, b)

    x = x_ref[...]
    h1b = (_norm_noaffine(x) * (1.0 + scale1) + shift1).astype(BF16)

    mixed_rows, attn_rows = [], []
    for s in range(seqs):
        mixed_rows.append(_dot(pdft_ref[:, :seq_len], fcs_ref[s, 0])
                          + _dot(pdft_ref[:, seq_len:], fcs_ref[s, 1]))
        kc = kcat_ref[s]
        vals = kc[:, :KVL]
        if cache:
            kc2 = cache_ref[s]
            vals2 = kc2[:, :KVL]
        heads_out = []
        for h in range(HEADS):
            q = qcat_ref[s, h]
            s1 = _dot_nt(q, kc)
            m = jnp.max(s1, axis=-1, keepdims=True)
            if cache:
                s2 = _dot_nt(q, kc2)
                m = jnp.maximum(m, jnp.max(s2, axis=-1, keepdims=True))
            p1 = jnp.exp(s1 - m)
            l = jnp.sum(p1, axis=-1, keepdims=True)
            o = _dot(p1.astype(BF16), vals)
            if cache:
                p2 = jnp.exp(s2 - m)
                l = l + jnp.sum(p2, axis=-1, keepdims=True)
                o = o + _dot(p2.astype(BF16), vals2)
            heads_out.append((o / l).astype(BF16))
        attn_rows.append(jnp.concatenate(heads_out, axis=1))
    mixed = jnp.concatenate(mixed_rows, axis=0) if seqs > 1 else mixed_rows[0]
    attn = jnp.concatenate(attn_rows, axis=0) if seqs > 1 else attn_rows[0]
    f_out = _dot(mixed.astype(BF16), wfo_ref[...])
    m_out = _dot(attn, woa_ref[...])

    gates = _sigmoid(_dot(h1b, wg_ref[...]) + bg_ref[...])
    merged = gates[:, :D] * f_out + gates[:, D:] * m_out
    mix = _dot(merged.astype(BF16), wout_ref[...])

    x1 = _norm_noaffine(DN_ALPHA * x + gate1 * mix) * ln1g_ref[...] + ln1b_ref[...]
    x1_ref[...] = x1
    h2 = _norm_noaffine(x1) * (1.0 + scale2) + shift2
    h2_ref[...] = h2.astype(BF16)

    h_hi = h2.astype(BF16)
    h_lo = (h2 - h_hi.astype(F32)).astype(BF16)
    hi_part = _dot(h_hi, wr_ref[...])
    logits_all = (hi_part[:, :LANES] + hi_part[:, LANES:] + _dot(h_lo, wr_ref[:, :LANES])) + br_ref[...]
    lane = lax.broadcasted_iota(jnp.int32, (TB, LANES), 1).astype(F32)
    for r in range(rows // TB):
        blk = slice(r * TB, (r + 1) * TB)
        work = logits_all[blk]
        top_v, top_i = [], []
        for _ in range(TOPK):
            mk = jnp.max(work, axis=-1, keepdims=True)
            ik = jnp.min(jnp.where(work == mk, lane, float(LANES)), axis=-1, keepdims=True)
            work = jnp.where(lane == ik, -jnp.inf, work)
            top_v.append(mk)
            top_i.append(ik)
        exps = [jnp.exp(v - top_v[0]) for v in top_v]
        denom = exps[0] + exps[1] + exps[2] + exps[3]

        onehot = jnp.zeros((TB, LANES), F32)
        for ik in top_i:
            onehot = onehot + jnp.where(lane == ik, 1.0, 0.0)
        counts = jnp.sum(onehot, axis=0, keepdims=True)
        padded = counts + (counts - 2.0 * jnp.floor(0.5 * counts))
        lower = jnp.dot(jnp.broadcast_to(padded, (8, LANES)), upper_ref[...],
                        preferred_element_type=F32, precision=HIGHEST)[0:1, :]
        before = _dot(tri_ref[...], onehot.astype(BF16)) + lower
        lpos = jnp.zeros((TB, LANES), jnp.int32)
        topw = jnp.zeros((TB, LANES), F32)
        for k in range(TOPK):
            pos = jnp.sum(jnp.where(lane == top_i[k], before, 0.0), axis=-1, keepdims=True)
            lpos = jnp.where(lane == float(k), pos.astype(jnp.int32), lpos)
            topw = jnp.where(lane == float(k), exps[k] / denom, topw)
        lpos_ref[blk, :] = lpos
        topw_ref[blk, :] = topw
        bcnt_ref[r] = counts


def _mix(x2d, mod, fcs, pdft, qcat, kcat, cache_k, wfo, woa, wg, bg, wout, ln1g, ln1b, wr, br,
         tri, upper, *, rows, n_seq, seq_len, mod_row0, mod_row_step):
    cache = cache_k is not None
    seqs = max(1, rows // seq_len)
    q_rows = rows // seqs
    bps = seq_len // q_rows
    n_outer = n_seq // seqs
    in_specs = [
        pl.BlockSpec((rows, D), lambda b, j: (b * bps + j, 0)),
        _full((8, 6 * D)),
        pl.BlockSpec((seqs, 2, seq_len, FW), lambda b, j: (b, 0, 0, 0)),
        pl.BlockSpec((q_rows, 2 * seq_len), lambda b, j: (j, 0)),
        pl.BlockSpec((seqs, HEADS, q_rows, 2 * LANES), lambda b, j: (b, 0, j, 0)),
        pl.BlockSpec((seqs, seq_len, 2 * LANES), lambda b, j: (b, 0, 0)),
    ]
    args = [x2d, mod, fcs, pdft, qcat, kcat]
    if cache:
        in_specs.append(pl.BlockSpec((seqs, PAST, 2 * LANES), lambda b, j: (b, 0, 0)))
        args.append(cache_k)
    in_specs += [
        _full((FW, D)), _full((HEADS * KVL, D)), _full((D, 2 * D)), _full((1, 2 * D)),
        _full((D, D)), _full((1, D)), _full((1, D)), _full((D, 2 * LANES)), _full((1, LANES)),
        _full((TB, TB)), _full((LANES, LANES)),
    ]
    args += [wfo, woa, wg, bg, wout, ln1g, ln1b, wr, br, tri, upper]
    n_tok = n_seq * seq_len
    tok_spec = lambda w: pl.BlockSpec((rows, w), lambda b, j: (b * bps + j, 0))
    out_specs = [tok_spec(D), tok_spec(D), tok_spec(LANES), tok_spec(LANES),
                 pl.BlockSpec((rows // TB, 1, LANES), lambda b, j: (b * bps + j, 0, 0))]
    out_shape = [
        jax.ShapeDtypeStruct((n_tok, D), F32),
        jax.ShapeDtypeStruct((n_tok, D), BF16),
        jax.ShapeDtypeStruct((n_tok, LANES), jnp.int32),
        jax.ShapeDtypeStruct((n_tok, LANES), F32),
        jax.ShapeDtypeStruct((n_tok // TB, 1, LANES), F32),
    ]
    kern = functools.partial(_mix_kernel, rows=rows, seq_len=seq_len, seqs=seqs, cache=cache,
                             mod_row0=mod_row0, mod_row_step=mod_row_step)
    return pl.pallas_call(
        kern, grid=(n_outer, bps), in_specs=in_specs, out_specs=out_specs, out_shape=out_shape,
        compiler_params=_params(("arbitrary", "arbitrary")),
        name="mix_lat" if cache else "mix_ctx",
    )(*args)


CTX_ROWS = 512
CTX_SEQS = CTX_ROWS // T_CTX


def _ctx_kernel(x_ref, mod_ref, win_ref, cdft_ref, qg_ref, kvg_ref, wqa_ref, wqr_ref,
                pdft_ref, wfo_ref, woa_ref, wg_ref, bg_ref, wout_ref, ln1g_ref, ln1b_ref, wr_ref, br_ref,
                tri_ref, upper_ref,
                ckv_ref, krope_ref, x1_ref, h2_ref, lpos_ref, topw_ref, bcnt_ref,
                fcs_buf, qcat_buf, kcat_buf):
    stage = dict(rows=CTX_ROWS, seq_len=T_CTX, seqs=CTX_SEQS, mod_row0=0, mod_row_step=0)
    _pre_kernel(x_ref, mod_ref, win_ref, cdft_ref, qg_ref, kvg_ref, wqa_ref, wqr_ref,
                fcs_buf, qcat_buf, kcat_buf, ckv_ref, krope_ref, rope=False, **stage)
    _mix_kernel(x_ref, mod_ref, fcs_buf, pdft_ref, qcat_buf, kcat_buf, wfo_ref, woa_ref, wg_ref, bg_ref,
                wout_ref, ln1g_ref, ln1b_ref, wr_ref, br_ref, tri_ref, upper_ref,
                x1_ref, h2_ref, lpos_ref, topw_ref, bcnt_ref, cache=False, **stage)


def _ctx(x2d, mod, w_in_ext, cdft, qg, kvg, wqa, wqr, pdft, wfo, woa, wg, bg, wout, ln1g, ln1b, wr, br,
         tri, upper):
    rows, seqs = CTX_ROWS, CTX_SEQS
    tok = lambda w: pl.BlockSpec((rows, w), lambda i: (i, 0))
    seq4 = lambda a, b: pl.BlockSpec((seqs, 1, a, b), lambda i: (i, 0, 0, 0))
    in_specs = [
        tok(D), _full((8, 6 * D)), _full((D, 9 * LANES)), _full((FG, 2 * FG)), _full((1, QL)),
        _full((1, KVL)), _full((QL, HEADS * KVL)), _full((QL, 2 * HEADS * LANES)),
        _full((T_CTX, 2 * T_CTX)), _full((FW, D)), _full((HEADS * KVL, D)), _full((D, 2 * D)),
        _full((1, 2 * D)), _full((D, D)), _full((1, D)), _full((1, D)), _full((D, 2 * LANES)),
        _full((1, LANES)), _full((TB, TB)), _full((LANES, LANES)),
    ]
    out_specs = [seq4(T_CTX, KVL), seq4(T_CTX, ROPE), tok(D), tok(D), tok(LANES), tok(LANES),
                 pl.BlockSpec((rows // TB, 1, LANES), lambda i: (i, 0, 0))]
    out_shape = [
        jax.ShapeDtypeStruct((N_CTX_SEQ, 1, T_CTX, KVL), F32),
        jax.ShapeDtypeStruct((N_CTX_SEQ, 1, T_CTX, ROPE), F32),
        jax.ShapeDtypeStruct((N_CTX, D), F32),
        jax.ShapeDtypeStruct((N_CTX, D), BF16),
        jax.ShapeDtypeStruct((N_CTX, LANES), jnp.int32),
        jax.ShapeDtypeStruct((N_CTX, LANES), F32),
        jax.ShapeDtypeStruct((N_CTX // TB, 1, LANES), F32),
    ]
    return pl.pallas_call(
        _ctx_kernel, grid=(N_CTX // rows,), in_specs=in_specs, out_specs=out_specs, out_shape=out_shape,
        scratch_shapes=[pltpu.VMEM((seqs, 2, T_CTX, FW), BF16),
                        pltpu.VMEM((seqs, HEADS, T_CTX, 2 * LANES), BF16),
                        pltpu.VMEM((seqs, T_CTX, 2 * LANES), BF16)],
        compiler_params=_params(("arbitrary",)),
        name="ctx",
    )(x2d, mod, w_in_ext, cdft, qg, kvg, wqa, wqr, pdft, wfo, woa, wg, bg, wout, ln1g, ln1b, wr, br,
      tri, upper)


def _pair_tiles(ref, pair0, n_pairs):
    return ref.at[pl.ds(pl.multiple_of(pair0 * SUB, SUB), n_pairs * SUB), :]


def _piece_tables(lstart, blen, gstart):
    bits = jnp.arange(RUN_BITS, dtype=jnp.int32)
    n = blen[:, :, None]
    valid = (n >> bits) & 1
    done = (n >> (bits + 1)) << (bits + 1)
    rank = jnp.cumsum(valid, axis=1) - valid
    place = (valid[..., None] == 1) & (rank[..., None] == jnp.arange(N_EXP, dtype=jnp.int32))
    def table(first_row):
        rows = (first_row[:, :, None] + done) * SUB
        listed = jnp.sum(jnp.where(place, rows[..., None], 0), axis=1)
        return listed.reshape(-1).astype(jnp.int32)
    return jnp.sum(valid, axis=1).reshape(-1).astype(jnp.int32), table(lstart), table(gstart)


def _start_piece_copies(cnt_ref, loc_ref, glo_ref, blk, local_ref, global_ref, sem, to_global):
    for bit in range(RUN_BITS):
        size = (1 << bit) * SUB
        base = (blk * RUN_BITS + bit) * N_EXP

        def piece(t, carry):
            loc = local_ref.at[pl.ds(pl.multiple_of(loc_ref[base + t], SUB), size), :]
            glo = global_ref.at[pl.ds(pl.multiple_of(glo_ref[base + t], SUB), size), :]
            (pltpu.make_async_copy(loc, glo, sem) if to_global
             else pltpu.make_async_copy(glo, loc, sem)).start()
            return carry

        lax.fori_loop(0, cnt_ref[blk * RUN_BITS + bit], piece, 0)


def _wait_block_pieces(extra_ref, blk, vmem_ref, hbm_ref, sem, to_global):
    def wait(n_pairs):
        loc = _pair_tiles(vmem_ref, 0, n_pairs)
        glo = _pair_tiles(hbm_ref, 0, n_pairs)
        (pltpu.make_async_copy(loc, glo, sem) if to_global else pltpu.make_async_copy(glo, loc, sem)).wait()

    wait(BLK_PAIRS)
    extra = extra_ref[blk]
    for bit in range(EXTRA_BITS):
        @pl.when((extra & (1 << bit)) != 0)
        def _():
            wait(1 << bit)


def _load_pairs(ref, n_pairs):
    words = jnp.concatenate([ref[pl.ds(s, n_pairs, stride=SUB), :] for s in range(SUB)], axis=1)
    return pltpu.bitcast(words, BF16)


def _store_pairs(ref, rows, n_pairs):
    words = pltpu.bitcast(rows, U32)
    for s in range(SUB):
        ref[pl.ds(s, n_pairs, stride=SUB), :] = words[:, s * LANES:(s + 1) * LANES]


def _zero_fill_tail(zero_ref, hbm_ref, sem):
    zero_ref[...] = jnp.zeros(zero_ref.shape, U32)
    copies = [pltpu.make_async_copy(zero_ref, _pair_tiles(hbm_ref, p, TILE_PAIRS), sem)
              for p in range(NP_SORT, NP_PAD, TILE_PAIRS)]
    for cp in copies:
        cp.start()
    for cp in copies:
        cp.wait()


def _dispatch_kernel(cnt_ref, loc_ref, glo_ref, extra_ref, h2c_ref, h2l_ref, lpc_ref, lpl_ref, xs_ref,
                     buf, zbuf, sem, zsem):
    b = pl.program_id(0)
    slot = b & 1

    def start_runs(blk, s):
        _start_piece_copies(cnt_ref, loc_ref, glo_ref, blk, buf.at[s], xs_ref, sem.at[s], True)

    def wait_runs(blk, s):
        _wait_block_pieces(extra_ref, blk, buf.at[s], xs_ref, sem.at[s], True)

    @pl.when(b == 0)
    def _():
        _zero_fill_tail(zbuf, xs_ref, zsem)

    @pl.when(b >= 2)
    def _():
        wait_runs(b - 2, slot)

    def sort_block(h2_ref, lp_ref):
        col = lax.broadcasted_iota(jnp.int32, (TB, BLK_ROWS_BUF), 1)
        lp = lp_ref[...]
        hit = col == lp[:, 0:1]
        for k in range(1, TOPK):
            hit = jnp.logical_or(hit, col == lp[:, k:k + 1])
        pick = jnp.where(hit, 1.0, 0.0)
        sorted_rows = lax.dot_general(pick.astype(BF16), h2_ref[...],
                                      (((0,), (0,)), ((), ())), preferred_element_type=F32)
        _store_pairs(buf.at[slot], sorted_rows.astype(BF16), BLK_PAIRS_BUF)

    @pl.when(b < N_CTX_BLK)
    def _():
        sort_block(h2c_ref, lpc_ref)

    @pl.when(b >= N_CTX_BLK)
    def _():
        sort_block(h2l_ref, lpl_ref)

    start_runs(b, slot)

    @pl.when(b == N_BLK - 1)
    def _():
        wait_runs(b - 1, 1 - slot)
        wait_runs(b, slot)


def _dispatch(cnt, loc, glo, extra, h2c, h2l, lpc, lpl):
    ctx_idx = lambda i, *_: (jnp.minimum(i, N_CTX_BLK - 1), 0)
    lat_idx = lambda i, *_: (jnp.maximum(i - N_CTX_BLK, 0), 0)
    grid_spec = pltpu.PrefetchScalarGridSpec(
        num_scalar_prefetch=4,
        grid=(N_BLK,),
        in_specs=[
            pl.BlockSpec((TB, D), ctx_idx),
            pl.BlockSpec((TB, D), lat_idx),
            pl.BlockSpec((TB, LANES), ctx_idx),
            pl.BlockSpec((TB, LANES), lat_idx),
        ],
        out_specs=pl.BlockSpec(memory_space=pl.ANY),
        scratch_shapes=[pltpu.VMEM((2, BLK_PAIRS_BUF * SUB, LANES), U32),
                        pltpu.VMEM((TILE_PAIRS * SUB, LANES), U32),
                        pltpu.SemaphoreType.DMA((2,)),
                        pltpu.SemaphoreType.DMA(())],
    )
    return pl.pallas_call(
        _dispatch_kernel, grid_spec=grid_spec,
        out_shape=jax.ShapeDtypeStruct((NP_PAD * SUB, LANES), U32),
        compiler_params=_params(("arbitrary",)),
        name="dispatch",
    )(cnt, loc, glo, extra, h2c, h2l, lpc, lpl)


def _expert_kernel(start_ref, count_ref, xs_ref, wg_ref, bg_ref, wu_ref, bu_ref, wd_ref, bd_ref, ys_ref,
                   xbuf, ybuf, sem_in, sem_out):
    e = pl.program_id(0)
    pair0 = start_ref[e]

    def tiles_of(ex):
        return lax.shift_right_logical(count_ref[ex] + (TILE_PAIRS - 1), TILE_PAIRS.bit_length() - 1)

    n_tiles = tiles_of(e)

    def in_copy_of(ex, j, slot):
        return pltpu.make_async_copy(_pair_tiles(xs_ref, start_ref[ex] + j * TILE_PAIRS, TILE_PAIRS),
                                     xbuf.at[slot], sem_in.at[slot])

    def in_copy(j, slot):
        return in_copy_of(e, j, slot)

    def out_copy(j, slot):
        return pltpu.make_async_copy(ybuf.at[slot], _pair_tiles(ys_ref, pair0 + j * TILE_PAIRS, TILE_PAIRS),
                                     sem_out.at[slot])

    def start_first_tiles(ex):
        for j in range(RING - 1):
            @pl.when(tiles_of(ex) > j)
            def _():
                in_copy_of(ex, j, j).start()

    @pl.when(e == 0)
    def _():
        _zero_fill_tail(ybuf.at[0], ys_ref, sem_out.at[0])
        start_first_tiles(0)

    @pl.when(n_tiles > 0)
    def _():
        def tile(j, carry):
            slot = j & (RING - 1)

            ahead = j + (RING - 1)

            @pl.when(ahead < n_tiles)
            def _():
                in_copy(ahead, ahead & (RING - 1)).start()

            in_copy(j, slot).wait()

            @pl.when(j >= RING)
            def _():
                out_copy(j - RING, slot).wait()

            xb = _load_pairs(xbuf.at[slot], TILE_PAIRS)
            gt = jnp.minimum(_dot(xb, wg_ref[...].astype(BF16)) + bg_ref[...], SWIGLU_LIMIT)
            up = jnp.clip(_dot(xb, wu_ref[...].astype(BF16)) + bu_ref[...], -SWIGLU_LIMIT, SWIGLU_LIMIT)
            act = gt * _sigmoid(SWIGLU_ALPHA * gt) * (up + 1.0)
            y = _dot(act.astype(BF16), wd_ref[...].astype(BF16)) + bd_ref[...]
            _store_pairs(ybuf.at[slot], y.astype(BF16), TILE_PAIRS)
            out_copy(j, slot).start()
            return carry

        lax.fori_loop(0, n_tiles, tile, 0)

        for back in range(RING, 0, -1):
            @pl.when(n_tiles >= back)
            def _():
                out_copy(n_tiles - back, (n_tiles - back) & (RING - 1)).wait()

    @pl.when(e + 1 < N_EXP)
    def _():
        start_first_tiles(e + 1)


def _experts(start, count, xs, wg, bg, wu, bu, wd, bd):
    w_idx = lambda e, *_: (0, e, 0, 0)
    grid_spec = pltpu.PrefetchScalarGridSpec(
        num_scalar_prefetch=2,
        grid=(N_EXP,),
        in_specs=[
            pl.BlockSpec(memory_space=pl.ANY),
            pl.BlockSpec((None, None, D, D_EXP), w_idx),
            pl.BlockSpec((None, None, 1, D_EXP), w_idx),
            pl.BlockSpec((None, None, D, D_EXP), w_idx),
            pl.BlockSpec((None, None, 1, D_EXP), w_idx),
            pl.BlockSpec((None, None, D_EXP, D), w_idx),
            pl.BlockSpec((None, None, 1, D), w_idx),
        ],
        out_specs=pl.BlockSpec(memory_space=pl.ANY),
        scratch_shapes=[pltpu.VMEM((RING, TILE_PAIRS * SUB, LANES), U32),
                        pltpu.VMEM((RING, TILE_PAIRS * SUB, LANES), U32),
                        pltpu.SemaphoreType.DMA((RING,)),
                        pltpu.SemaphoreType.DMA((RING,))],
    )
    return pl.pallas_call(
        _expert_kernel, grid_spec=grid_spec,
        out_shape=jax.ShapeDtypeStruct((NP_PAD * SUB, LANES), U32),
        compiler_params=_params(("arbitrary",)),
        name="experts",
    )(start, count, xs, wg, bg, wu, bu, wd, bd)


def _combine_kernel(cnt_ref, loc_ref, glo_ref, extra_ref, lp_ref, topw_ref, x1_ref, mod_ref, g_ref, b_ref,
                    ys_ref, o_ref, buf, sem, *, blk0, n_blk, blocks_per_seq, mod_row0, mod_row_step):
    i = pl.program_id(0)
    slot = i & 1
    gate2 = _mod_row(mod_ref, mod_row0, mod_row_step, i // blocks_per_seq)[5]

    def start_runs(step, s):
        _start_piece_copies(cnt_ref, loc_ref, glo_ref, blk0 + step, buf.at[s], ys_ref, sem.at[s], False)

    @pl.when(i == 0)
    def _():
        buf[...] = jnp.zeros(buf.shape, U32)
        start_runs(0, 0)

    @pl.when(i + 1 < n_blk)
    def _():
        start_runs(i + 1, 1 - slot)

    _wait_block_pieces(extra_ref, blk0 + i, buf.at[slot], ys_ref, sem.at[slot], False)

    y_sorted = _load_pairs(buf.at[slot], BLK_PAIRS_BUF)
    col = lax.broadcasted_iota(jnp.int32, (TB, BLK_ROWS_BUF), 1)
    lp = lp_ref[...]
    topw = topw_ref[...]
    weights = jnp.zeros((TB, BLK_ROWS_BUF), F32)
    for k in range(TOPK):
        weights = jnp.where(col == lp[:, k:k + 1], topw[:, k:k + 1], weights)
    ffn = _dot(weights.astype(BF16), y_sorted)
    y = DN_ALPHA * x1_ref[...] + gate2 * ffn
    o_ref[...] = _norm_noaffine(y) * g_ref[...] + b_ref[...]


def _combine(cnt, loc, glo, extra, lpos, topw, x1, mod, ln2g, ln2b, ys, *, blk0, n_seq, seq_len,
             mod_row0, mod_row_step):
    bps = seq_len // TB
    n_blk = n_seq * bps
    tok = lambda i, *_: (i, 0)
    const = lambda i, *_: (0, 0)
    grid_spec = pltpu.PrefetchScalarGridSpec(
        num_scalar_prefetch=4,
        grid=(n_blk,),
        in_specs=[
            pl.BlockSpec((TB, LANES), tok),
            pl.BlockSpec((TB, LANES), tok),
            pl.BlockSpec((TB, D), tok),
            pl.BlockSpec((8, 6 * D), const),
            pl.BlockSpec((1, D), const),
            pl.BlockSpec((1, D), const),
            pl.BlockSpec(memory_space=pl.ANY),
        ],
        out_specs=pl.BlockSpec((TB, D), tok),
        scratch_shapes=[pltpu.VMEM((2, BLK_PAIRS_BUF * SUB, LANES), U32), pltpu.SemaphoreType.DMA((2,))],
    )
    kern = functools.partial(_combine_kernel, blk0=blk0, n_blk=n_blk, blocks_per_seq=bps,
                             mod_row0=mod_row0, mod_row_step=mod_row_step)
    return pl.pallas_call(
        kern, grid_spec=grid_spec,
        out_shape=jax.ShapeDtypeStruct((n_seq * seq_len, D), F32),
        compiler_params=_params(("arbitrary",)),
        name="combine_lat" if blk0 else "combine_ctx",
    )(cnt, loc, glo, extra, lpos, topw, x1, mod, ln2g, ln2b, ys)


def _dft_tables():
    def cs(n):
        k = np.arange(n, dtype=np.int64)
        ang = 2.0 * np.pi * ((k[:, None] * k[None, :]) % n).astype(np.float64) / n
        return np.cos(ang) / math.sqrt(n), np.sin(ang) / math.sqrt(n)

    c, s = cs(FG)
    cdft = np.concatenate([c, s], axis=1).astype(np.float32)
    pd = {}
    for t in (T_CTX, T_LAT):
        c, s = cs(t)
        pd[t] = np.concatenate([c, -s], axis=1).astype(np.float32)
    tri = np.tril(np.ones((TB, TB), np.float32), k=-1)
    upper = np.triu(np.ones((LANES, LANES), np.float32), k=1)
    return cdft, pd, tri, upper


_ROT_PERM = np.array(list(range(8, 16)) + list(range(0, 8)) + list(range(24, 32)) + list(range(16, 24)))
_ROT_SIGN = np.array([-1.0] * 8 + [1.0] * 8 + [-1.0] * 8 + [1.0] * 8, np.float32)


def _rope_tables():
    rows = T_LAT // GRID_W
    row = jnp.repeat(jnp.arange(rows, dtype=F32), GRID_W)
    col = jnp.tile(jnp.arange(GRID_W, dtype=F32), rows)
    axis_dim = ROPE // 2
    inv_freq = ROPE_THETA ** (-jnp.arange(0, axis_dim, 2, dtype=F32) / axis_dim)
    ang_r = row[:, None] * inv_freq[None, :]
    ang_c = col[:, None] * inv_freq[None, :]
    ang = jnp.concatenate([ang_r, ang_r, ang_c, ang_c], axis=-1)
    pad = ((0, 0), (0, LANES - ROPE))
    return jnp.pad(jnp.cos(ang), pad), jnp.pad(jnp.sin(ang), pad)


def kernel(x_prompt, x_sample, cache_ckv, cache_krope, c, c_ctx, w_mod, b_mod, w_in, q_norm_g, w_q_up,
           kv_norm_g, w_kv_up, w_fourier_o, w_mla_o, w_gate, b_gate, w_out, ln1_g, ln1_b, w_router,
           b_router, w_gate_e, b_gate_e, w_up_e, b_up_e, w_down_e, b_down_e, ln2_g, ln2_b):
    cdft_np, pdft_np, tri_np, upper_np = _dft_tables()
    upper = jnp.asarray(upper_np)
    cdft = jnp.asarray(cdft_np).astype(BF16)
    pdft_ctx = jnp.asarray(pdft_np[T_CTX]).astype(BF16)
    pdft_lat = jnp.asarray(pdft_np[T_LAT]).astype(BF16)
    tri = jnp.asarray(tri_np).astype(BF16)
    cos_pad, sin_pad = _rope_tables()

    w_in0 = w_in[0]
    kr_w = w_in0[:, FW + QL + KVL:]
    lane_pad = ((0, 0), (0, LANES - ROPE))
    w_in_ext = jnp.concatenate(
        [w_in0[:, :FW + QL + KVL], jnp.pad(kr_w, lane_pad),
         jnp.pad(kr_w[:, _ROT_PERM] * _ROT_SIGN, lane_pad)], axis=1).astype(BF16)
    wq3 = w_q_up[0].reshape(QL, HEADS, NOPE + ROPE)
    wq_nope = jnp.transpose(wq3[:, :, :NOPE], (1, 0, 2))
    wq_rope = wq3[:, :, NOPE:]
    head_pad = ((0, 0), (0, 0), (0, LANES - ROPE))
    wqr = jnp.concatenate(
        [jnp.pad(wq_rope, head_pad).reshape(QL, HEADS * LANES),
         jnp.pad(wq_rope[:, :, _ROT_PERM] * _ROT_SIGN, head_pad).reshape(QL, HEADS * LANES)],
        axis=1).astype(BF16)
    wkv3 = w_kv_up[0].reshape(KVL, HEADS, NOPE + VH)
    wk = jnp.transpose(wkv3[:, :, :NOPE], (1, 0, 2))
    wv = jnp.transpose(wkv3[:, :, NOPE:], (1, 0, 2))
    wo3 = w_mla_o[0].reshape(HEADS, VH, D)
    wqa, woa = _prep(wq_nope, wk, wv, wo3)

    c_all = jnp.concatenate([c_ctx[None, :], c, jnp.zeros((8 - 1 - N_LAT_SEQ, D), F32)], axis=0)
    mod = _modulation(c_all, w_mod[0], b_mod)

    qg = q_norm_g
    kvg = kv_norm_g
    wfo = w_fourier_o[0].astype(BF16)
    wg = w_gate[0].astype(BF16)
    wout = w_out[0].astype(BF16)
    wr_f32 = jnp.pad(w_router[0], ((0, 0), (0, LANES - N_EXP)))
    wr_hi = wr_f32.astype(BF16)
    wr = jnp.concatenate([wr_hi, (wr_f32 - wr_hi.astype(F32)).astype(BF16)], axis=1)
    br = jnp.pad(b_router, ((0, 0), (0, LANES - N_EXP)), constant_values=NEG_BIG)
    cache_k = jnp.concatenate(
        [cache_ckv[:, 0], jnp.pad(cache_krope[:, 0], ((0, 0), (0, 0), (0, LANES - ROPE)))],
        axis=-1).astype(BF16)

    xc2d = x_prompt.reshape(N_CTX, D)
    xl2d = x_sample.reshape(N_LAT, D)

    new_ckv, new_krope, x1_c, h2_c, lpos_c, topw_c, bcnt_c = _ctx(
        xc2d, mod, w_in_ext, cdft, qg, kvg, wqa, wqr, pdft_ctx, wfo, woa, wg, b_gate, wout, ln1_g, ln1_b,
        wr, br, tri, upper)

    fcs_l, qcat_l, kcat_l = _pre(
        xl2d, mod, w_in_ext, cdft, qg, kvg, wqa, wqr, cos_pad, sin_pad,
        n_seq=N_LAT_SEQ, seq_len=T_LAT, rope=True, mod_row0=1, mod_row_step=1)
    x1_l, h2_l, lpos_l, topw_l, bcnt_l = _mix(
        xl2d, mod, fcs_l, pdft_lat, qcat_l, kcat_l, cache_k, wfo, woa, wg, b_gate, wout, ln1_g,
        ln1_b, wr, br, tri, upper, rows=MIX_ROWS_LAT, n_seq=N_LAT_SEQ, seq_len=T_LAT, mod_row0=1,
        mod_row_step=1)

    blen = jnp.concatenate([bcnt_c[:, 0, :N_EXP], bcnt_l[:, 0, :N_EXP]], axis=0).astype(jnp.int32)
    blen = (blen + (blen & 1)) // 2
    lstart = jnp.cumsum(blen, axis=1) - blen
    count = jnp.sum(blen, axis=0)
    start = jnp.cumsum(count) - count
    gstart = start[None, :] + jnp.cumsum(blen, axis=0) - blen
    extra = (jnp.sum(blen, axis=1) - BLK_PAIRS).astype(jnp.int32)
    pieces = _piece_tables(lstart, blen, gstart) + (extra,)
    start = start.astype(jnp.int32)
    count = count.astype(jnp.int32)

    xs = _dispatch(*pieces, h2_c, h2_l, lpos_c, lpos_l)
    ys = _experts(start, count, xs,
                  w_gate_e, b_gate_e[:, :, None, :], w_up_e, b_up_e[:, :, None, :],
                  w_down_e, b_down_e[:, :, None, :])

    y_c = _combine(*pieces, lpos_c, topw_c, x1_c, mod, ln2_g, ln2_b, ys, blk0=0,
                   n_seq=N_CTX_SEQ, seq_len=T_CTX, mod_row0=0, mod_row_step=0)
    y_l = _combine(*pieces, lpos_l, topw_l, x1_l, mod, ln2_g, ln2_b, ys, blk0=N_CTX_BLK,
                   n_seq=N_LAT_SEQ, seq_len=T_LAT, mod_row0=1, mod_row_step=1)
    return (y_c.reshape(N_CTX_SEQ, T_CTX, D), y_l.reshape(N_LAT_SEQ, T_LAT, D), new_ckv, new_krope)
```

```python
import functools
import math

import numpy as np
import jax
import jax.numpy as jnp
from jax import lax
from jax.experimental import pallas as pl
from jax.experimental.pallas import tpu as pltpu

D = 1024
N_CTX_SEQ, T_CTX = 32, 256
N_LAT_SEQ, T_LAT = 4, 1024
PAST = 256
N_CTX = N_CTX_SEQ * T_CTX
N_LAT = N_LAT_SEQ * T_LAT
N_TOK = N_CTX + N_LAT
FW = 512
FG = 128
N_FG = FW // FG
HEADS = 8
QL = 256
KVL = 128
NOPE = 64
ROPE = 32
VH = 64
N_EXP = 32
TOPK = 4
D_EXP = 1024
SWIGLU_LIMIT = 7.0
SWIGLU_ALPHA = 1.702
LN_EPS = 1e-5
RMS_EPS = 1e-6
DN_ALPHA = 2.0 ** 0.25
ATT_SCALE = float(NOPE + ROPE) ** -0.5
ROPE_THETA = 10000.0
GRID_W = 64

LANES = 128
TB = 256
PRE_ROWS = 512
MIX_ROWS_CTX = 512
MIX_ROWS_LAT = 512
SUB = 8
ROW_TILE = 256
TILE_PAIRS = ROW_TILE // 2
RING = 4
N_CTX_BLK = N_CTX // TB
N_BLK = N_TOK // TB
BLK_PAIRS = TB * TOPK // 2
PAD_PAIRS_MAX = N_EXP // 2
BLK_PAIRS_BUF = BLK_PAIRS + 64
BLK_ROWS_BUF = 2 * BLK_PAIRS_BUF
NP_SORT = N_TOK * TOPK // 2
NP_MAX = NP_SORT + N_BLK * PAD_PAIRS_MAX
NP_PAD = NP_MAX + TILE_PAIRS
RUN_BITS = (TB // 2).bit_length()
EXTRA_BITS = PAD_PAIRS_MAX.bit_length()
U32 = jnp.uint32
VMEM_LIMIT = 56 * 1024 * 1024
NEG_BIG = -1e30

F32 = jnp.float32
BF16 = jnp.bfloat16
HIGHEST = lax.Precision.HIGHEST


def _dot(a, b):
    return jnp.dot(a, b, preferred_element_type=F32)


def _dot_nt(a, b):
    return lax.dot_general(a, b, (((1,), (1,)), ((), ())), preferred_element_type=F32)


def _params(sem):
    return pltpu.CompilerParams(dimension_semantics=sem, vmem_limit_bytes=VMEM_LIMIT)


def _full(shape):
    n = len(shape)
    return pl.BlockSpec(shape, lambda *_: (0,) * n, pipeline_mode=pl.Buffered(1))


def _norm_noaffine(x):
    mu = jnp.mean(x, axis=-1, keepdims=True)
    xc = x - mu
    var = jnp.mean(xc * xc, axis=-1, keepdims=True)
    return xc * lax.rsqrt(var + LN_EPS)


def _sigmoid(x):
    return 0.5 * jnp.tanh(0.5 * x) + 0.5


def _rms(x, g):
    return x * lax.rsqrt(jnp.mean(x * x, axis=-1, keepdims=True) + RMS_EPS) * g


def _mod_row(mod_ref, row0, step, seq):
    row = row0 if step == 0 else row0 + step * seq
    m = mod_ref[pl.ds(row, 1), :]
    return [m[:, i * D:(i + 1) * D] for i in range(6)]


def _prep_kernel(wqn_ref, wk_ref, wv_ref, wo_ref, qabs_ref, oabs_ref):
    for h in range(HEADS):
        qabs = lax.dot_general(wqn_ref[h], wk_ref[h], (((1,), (1,)), ((), ())),
                               preferred_element_type=F32, precision=HIGHEST)
        qabs_ref[:, h * KVL:(h + 1) * KVL] = qabs.astype(BF16)
        oabs = jnp.dot(wv_ref[h], wo_ref[h], preferred_element_type=F32, precision=HIGHEST)
        oabs_ref[h * KVL:(h + 1) * KVL, :] = oabs.astype(BF16)


def _prep(wqn, wk, wv, wo):
    return pl.pallas_call(
        _prep_kernel,
        out_shape=[jax.ShapeDtypeStruct((QL, HEADS * KVL), BF16),
                   jax.ShapeDtypeStruct((HEADS * KVL, D), BF16)],
        compiler_params=pltpu.CompilerParams(vmem_limit_bytes=VMEM_LIMIT),
        name="weight_prep",
    )(wqn, wk, wv, wo)


MOD_COLS = 1536


def _mod_kernel(c_ref, w_ref, b_ref, o_ref):
    c = c_ref[...]
    s = c * _sigmoid(c)
    s_hi = s.astype(BF16)
    s_lo = (s - s_hi.astype(F32)).astype(BF16)
    w = w_ref[...]
    w_hi = w.astype(BF16)
    w_lo = (w - w_hi.astype(F32)).astype(BF16)
    o_ref[...] = (_dot(s_hi, w_hi) + (_dot(s_hi, w_lo) + _dot(s_lo, w_hi))) + b_ref[...]


def _modulation(c_all, w_mod, b_mod):
    return pl.pallas_call(
        _mod_kernel,
        grid=(6 * D // MOD_COLS,),
        in_specs=[
            pl.BlockSpec((8, D), lambda i: (0, 0)),
            pl.BlockSpec((D, MOD_COLS), lambda i: (0, i)),
            pl.BlockSpec((1, MOD_COLS), lambda i: (0, i)),
        ],
        out_specs=pl.BlockSpec((8, MOD_COLS), lambda i: (0, i)),
        out_shape=jax.ShapeDtypeStruct((8, 6 * D), F32),
        compiler_params=_params(("arbitrary",)),
        name="modulation",
    )(c_all, w_mod, b_mod)


def _pre_kernel(*refs, rows, seq_len, seqs, rope, mod_row0, mod_row_step):
    if rope:
        (x_ref, mod_ref, win_ref, cdft_ref, qg_ref, kvg_ref, wqa_ref, wqr_ref, cos_ref, sin_ref,
         fcs_ref, qcat_ref, kcat_ref) = refs
    else:
        (x_ref, mod_ref, win_ref, cdft_ref, qg_ref, kvg_ref, wqa_ref, wqr_ref,
         fcs_ref, qcat_ref, kcat_ref, ckv_ref, krope_ref) = refs
    q_rows = rows // seqs
    b = pl.program_id(0) // (seq_len // q_rows)
    shift1, scale1 = _mod_row(mod_ref, mod_row0, mod_row_step, b)[:2]
    seq_rows = [slice(s * q_rows, (s + 1) * q_rows) for s in range(seqs)]

    x = x_ref[...]
    h1 = _norm_noaffine(x) * (1.0 + scale1) + shift1
    h1b = h1.astype(BF16)
    n_cols = 9 * LANES if rope else 8 * LANES
    proj = _dot(h1b, win_ref[:, :n_cols])

    f_b = proj[:, :FW].astype(BF16)
    for g in range(N_FG):
        r = _dot(f_b[:, g * FG:(g + 1) * FG], cdft_ref[...])
        for s, sl in enumerate(seq_rows):
            fcs_ref[s, 0, :, g * FG:(g + 1) * FG] = r[sl, :FG].astype(BF16)
            fcs_ref[s, 1, :, g * FG:(g + 1) * FG] = r[sl, FG:].astype(BF16)

    qn = _rms(proj[:, FW:FW + QL], qg_ref[...]).astype(BF16)
    ckv = _rms(proj[:, FW + QL:FW + QL + KVL], kvg_ref[...])
    kr = proj[:, 7 * LANES:8 * LANES]

    qa = _dot(qn, wqa_ref[...]) * ATT_SCALE
    if rope:
        cos = cos_ref[...]
        sin = sin_ref[...]
        qr2 = _dot(qn, wqr_ref[...])
        kr_keys = kr * cos + proj[:, 8 * LANES:9 * LANES] * sin
    else:
        qr2 = _dot(qn, wqr_ref[:, :HEADS * LANES])
        kr_keys = kr
    ckv_b = ckv.astype(BF16)
    kr_b = kr_keys.astype(BF16)
    for h in range(HEADS):
        qr_h = qr2[:, h * LANES:(h + 1) * LANES]
        if rope:
            qr_h = qr_h * cos + qr2[:, (HEADS + h) * LANES:(HEADS + h + 1) * LANES] * sin
        qa_h = qa[:, h * KVL:(h + 1) * KVL].astype(BF16)
        qr_h = (qr_h * ATT_SCALE).astype(BF16)
        for s, sl in enumerate(seq_rows):
            qcat_ref[s, h, :, :KVL] = qa_h[sl]
            qcat_ref[s, h, :, KVL:] = qr_h[sl]
    for s, sl in enumerate(seq_rows):
        kcat_ref[s, :, :KVL] = ckv_b[sl]
        kcat_ref[s, :, KVL:] = kr_b[sl]
        if not rope:
            ckv_ref[s, 0] = ckv[sl]
            krope_ref[s, 0] = kr[sl, :ROPE]


def _pre(x2d, mod, w_in_ext, cdft, qg, kvg, wqa, wqr, cos_pad, sin_pad, *, n_seq, seq_len, rope,
         mod_row0, mod_row_step):
    rows = PRE_ROWS
    seqs = max(1, rows // seq_len)
    q_rows = rows // seqs
    bps = seq_len // q_rows
    n_blk = n_seq * seq_len // rows
    seq_blk = lambda i: (i // bps, 0, i % bps, 0)
    in_specs = [
        pl.BlockSpec((rows, D), lambda i: (i, 0)),
        _full((8, 6 * D)),
        _full((D, 9 * LANES)),
        _full((FG, 2 * FG)),
        _full((1, QL)),
        _full((1, KVL)),
        _full((QL, HEADS * KVL)),
        _full((QL, 2 * HEADS * LANES)),
    ]
    args = [x2d, mod, w_in_ext, cdft, qg, kvg, wqa, wqr]
    out_specs = [
        pl.BlockSpec((seqs, 2, q_rows, FW), seq_blk),
        pl.BlockSpec((seqs, HEADS, q_rows, 2 * LANES), seq_blk),
        pl.BlockSpec((seqs, q_rows, 2 * LANES), lambda i: (i // bps, i % bps, 0)),
    ]
    out_shape = [
        jax.ShapeDtypeStruct((n_seq, 2, seq_len, FW), BF16),
        jax.ShapeDtypeStruct((n_seq, HEADS, seq_len, 2 * LANES), BF16),
        jax.ShapeDtypeStruct((n_seq, seq_len, 2 * LANES), BF16),
    ]
    if rope:
        in_specs += [pl.BlockSpec((q_rows, LANES), lambda i: (i % bps, 0)),
                     pl.BlockSpec((q_rows, LANES), lambda i: (i % bps, 0))]
        args += [cos_pad, sin_pad]
    else:
        out_specs += [pl.BlockSpec((seqs, 1, q_rows, KVL), seq_blk),
                      pl.BlockSpec((seqs, 1, q_rows, ROPE), seq_blk)]
        out_shape += [jax.ShapeDtypeStruct((n_seq, 1, seq_len, KVL), F32),
                      jax.ShapeDtypeStruct((n_seq, 1, seq_len, ROPE), F32)]
    kern = functools.partial(_pre_kernel, rows=rows, seq_len=seq_len, seqs=seqs, rope=rope,
                             mod_row0=mod_row0, mod_row_step=mod_row_step)
    return pl.pallas_call(
        kern, grid=(n_blk,), in_specs=in_specs, out_specs=out_specs, out_shape=out_shape,
        compiler_params=_params(("arbitrary",)),
        name="pre_lat" if rope else "pre_ctx",
    )(*args)


def _mix_kernel(*refs, rows, seq_len, seqs, cache, mod_row0, mod_row_step):
    if cache:
        (x_ref, mod_ref, fcs_ref, pdft_ref, qcat_ref, kcat_ref, cache_ref, wfo_ref, woa_ref,
         wg_ref, bg_ref, wout_ref, ln1g_ref, ln1b_ref, wr_ref, br_ref, tri_ref, upper_ref,
         x1_ref, h2_ref, lpos_ref, topw_ref, bcnt_ref) = refs
    else:
        (x_ref, mod_ref, fcs_ref, pdft_ref, qcat_ref, kcat_ref, wfo_ref, woa_ref,
         wg_ref, bg_ref, wout_ref, ln1g_ref, ln1b_ref, wr_ref, br_ref, tri_ref, upper_ref,
         x1_ref, h2_ref, lpos_ref, topw_ref, bcnt_ref) = refs
        cache_ref = None
    b = pl.program_id(0)
    shift1, scale1, gate1, shift2, scale2, _ = _mod_row(mod_ref, mod_row0, mod_row_step, b)

    x = x_ref[...]
    h1b = (_norm_noaffine(x) * (1.0 + scale1) + shift1).astype(BF16)

    mixed_rows, attn_rows = [], []
    for s in range(seqs):
        mixed_rows.append(_dot(pdft_ref[:, :seq_len], fcs_ref[s, 0])
                          + _dot(pdft_ref[:, seq_len:], fcs_ref[s, 1]))
        kc = kcat_ref[s]
        vals = kc[:, :KVL]
        if cache:
            kc2 = cache_ref[s]
            vals2 = kc2[:, :KVL]
        heads_out = []
        for h in range(HEADS):
            q = qcat_ref[s, h]
            s1 = _dot_nt(q, kc)
            m = jnp.max(s1, axis=-1, keepdims=True)
            if cache:
                s2 = _dot_nt(q, kc2)
                m = jnp.maximum(m, jnp.max(s2, axis=-1, keepdims=True))
            p1 = jnp.exp(s1 - m)
            l = jnp.sum(p1, axis=-1, keepdims=True)
            o = _dot(p1.astype(BF16), vals)
            if cache:
                p2 = jnp.exp(s2 - m)
                l = l + jnp.sum(p2, axis=-1, keepdims=True)
                o = o + _dot(p2.astype(BF16), vals2)
            heads_out.append((o / l).astype(BF16))
        attn_rows.append(jnp.concatenate(heads_out, axis=1))
    mixed = jnp.concatenate(mixed_rows, axis=0) if seqs > 1 else mixed_rows[0]
    attn = jnp.concatenate(attn_rows, axis=0) if seqs > 1 else attn_rows[0]
    f_out = _dot(mixed.astype(BF16), wfo_ref[...])
    m_out = _dot(attn, woa_ref[...])

    gates = _sigmoid(_dot(h1b, wg_ref[...]) + bg_ref[...])
    merged = gates[:, :D] * f_out + gates[:, D:] * m_out
    mix = _dot(merged.astype(BF16), wout_ref[...])

    x1 = _norm_noaffine(DN_ALPHA * x + gate1 * mix) * ln1g_ref[...] + ln1b_ref[...]
    x1_ref[...] = x1
    h2 = _norm_noaffine(x1) * (1.0 + scale2) + shift2
    h2_ref[...] = h2.astype(BF16)

    h_hi = h2.astype(BF16)
    h_lo = (h2 - h_hi.astype(F32)).astype(BF16)
    hi_part = _dot(h_hi, wr_ref[...])
    logits_all = (hi_part[:, :LANES] + hi_part[:, LANES:] + _dot(h_lo, wr_ref[:, :LANES])) + br_ref[...]
    lane = lax.broadcasted_iota(jnp.int32, (TB, LANES), 1).astype(F32)
    for r in range(rows // TB):
        blk = slice(r * TB, (r + 1) * TB)
        work = logits_all[blk]
        top_v, top_i = [], []
        for _ in range(TOPK):
            mk = jnp.max(work, axis=-1, keepdims=True)
            ik = jnp.min(jnp.where(work == mk, lane, float(LANES)), axis=-1, keepdims=True)
            work = jnp.where(lane == ik, -jnp.inf, work)
            top_v.append(mk)
            top_i.append(ik)
        exps = [jnp.exp(v - top_v[0]) for v in top_v]
        denom = exps[0] + exps[1] + exps[2] + exps[3]

        onehot = jnp.zeros((TB, LANES), F32)
        for ik in top_i:
            onehot = onehot + jnp.where(lane == ik, 1.0, 0.0)
        counts = jnp.sum(onehot, axis=0, keepdims=True)
        padded = counts + (counts - 2.0 * jnp.floor(0.5 * counts))
        lower = jnp.dot(jnp.broadcast_to(padded, (8, LANES)), upper_ref[...],
                        preferred_element_type=F32, precision=HIGHEST)[0:1, :]
        before = _dot(tri_ref[...], onehot.astype(BF16)) + lower
        lpos = jnp.zeros((TB, LANES), jnp.int32)
        topw = jnp.zeros((TB, LANES), F32)
        for k in range(TOPK):
            pos = jnp.sum(jnp.where(lane == top_i[k], before, 0.0), axis=-1, keepdims=True)
            lpos = jnp.where(lane == float(k), pos.astype(jnp.int32), lpos)
            topw = jnp.where(lane == float(k), exps[k] / denom, topw)
        lpos_ref[blk, :] = lpos
        topw_ref[blk, :] = topw
        bcnt_ref[r] = counts


def _mix(x2d, mod, fcs, pdft, qcat, kcat, cache_k, wfo, woa, wg, bg, wout, ln1g, ln1b, wr, br,
         tri, upper, *, rows, n_seq, seq_len, mod_row0, mod_row_step):
    cache = cache_k is not None
    seqs = max(1, rows // seq_len)
    q_rows = rows // seqs
    bps = seq_len // q_rows
    n_outer = n_seq // seqs
    in_specs = [
        pl.BlockSpec((rows, D), lambda b, j: (b * bps + j, 0)),
        _full((8, 6 * D)),
        pl.BlockSpec((seqs, 2, seq_len, FW), lambda b, j: (b, 0, 0, 0)),
        pl.BlockSpec((q_rows, 2 * seq_len), lambda b, j: (j, 0)),
        pl.BlockSpec((seqs, HEADS, q_rows, 2 * LANES), lambda b, j: (b, 0, j, 0)),
        pl.BlockSpec((seqs, seq_len, 2 * LANES), lambda b, j: (b, 0, 0)),
    ]
    args = [x2d, mod, fcs, pdft, qcat, kcat]
    if cache:
        in_specs.append(pl.BlockSpec((seqs, PAST, 2 * LANES), lambda b, j: (b, 0, 0)))
        args.append(cache_k)
    in_specs += [
        _full((FW, D)), _full((HEADS * KVL, D)), _full((D, 2 * D)), _full((1, 2 * D)),
        _full((D, D)), _full((1, D)), _full((1, D)), _full((D, 2 * LANES)), _full((1, LANES)),
        _full((TB, TB)), _full((LANES, LANES)),
    ]
    args += [wfo, woa, wg, bg, wout, ln1g, ln1b, wr, br, tri, upper]
    n_tok = n_seq * seq_len
    tok_spec = lambda w: pl.BlockSpec((rows, w), lambda b, j: (b * bps + j, 0))
    out_specs = [tok_spec(D), tok_spec(D), tok_spec(LANES), tok_spec(LANES),
                 pl.BlockSpec((rows // TB, 1, LANES), lambda b, j: (b * bps + j, 0, 0))]
    out_shape = [
        jax.ShapeDtypeStruct((n_tok, D), F32),
        jax.ShapeDtypeStruct((n_tok, D), BF16),
        jax.ShapeDtypeStruct((n_tok, LANES), jnp.int32),
        jax.ShapeDtypeStruct((n_tok, LANES), F32),
        jax.ShapeDtypeStruct((n_tok // TB, 1, LANES), F32),
    ]
    kern = functools.partial(_mix_kernel, rows=rows, seq_len=seq_len, seqs=seqs, cache=cache,
                             mod_row0=mod_row0, mod_row_step=mod_row_step)
    return pl.pallas_call(
        kern, grid=(n_outer, bps), in_specs=in_specs, out_specs=out_specs, out_shape=out_shape,
        compiler_params=_params(("arbitrary", "arbitrary")),
        name="mix_lat" if cache else "mix_ctx",
    )(*args)


CTX_ROWS = 512
CTX_SEQS = CTX_ROWS // T_CTX


def _ctx_kernel(x_ref, mod_ref, win_ref, cdft_ref, qg_ref, kvg_ref, wqa_ref, wqr_ref,
                pdft_ref, wfo_ref, woa_ref, wg_ref, bg_ref, wout_ref, ln1g_ref, ln1b_ref, wr_ref, br_ref,
                tri_ref, upper_ref,
                ckv_ref, krope_ref, x1_ref, h2_ref, lpos_ref, topw_ref, bcnt_ref,
                fcs_buf, qcat_buf, kcat_buf):
    stage = dict(rows=CTX_ROWS, seq_len=T_CTX, seqs=CTX_SEQS, mod_row0=0, mod_row_step=0)
    _pre_kernel(x_ref, mod_ref, win_ref, cdft_ref, qg_ref, kvg_ref, wqa_ref, wqr_ref,
                fcs_buf, qcat_buf, kcat_buf, ckv_ref, krope_ref, rope=False, **stage)
    _mix_kernel(x_ref, mod_ref, fcs_buf, pdft_ref, qcat_buf, kcat_buf, wfo_ref, woa_ref, wg_ref, bg_ref,
                wout_ref, ln1g_ref, ln1b_ref, wr_ref, br_ref, tri_ref, upper_ref,
                x1_ref, h2_ref, lpos_ref, topw_ref, bcnt_ref, cache=False, **stage)


def _ctx(x2d, mod, w_in_ext, cdft, qg, kvg, wqa, wqr, pdft, wfo, woa, wg, bg, wout, ln1g, ln1b, wr, br,
         tri, upper):
    rows, seqs = CTX_ROWS, CTX_SEQS
    tok = lambda w: pl.BlockSpec((rows, w), lambda i: (i, 0))
    seq4 = lambda a, b: pl.BlockSpec((seqs, 1, a, b), lambda i: (i, 0, 0, 0))
    in_specs = [
        tok(D), _full((8, 6 * D)), _full((D, 9 * LANES)), _full((FG, 2 * FG)), _full((1, QL)),
        _full((1, KVL)), _full((QL, HEADS * KVL)), _full((QL, 2 * HEADS * LANES)),
        _full((T_CTX, 2 * T_CTX)), _full((FW, D)), _full((HEADS * KVL, D)), _full((D, 2 * D)),
        _full((1, 2 * D)), _full((D, D)), _full((1, D)), _full((1, D)), _full((D, 2 * LANES)),
        _full((1, LANES)), _full((TB, TB)), _full((LANES, LANES)),
    ]
    out_specs = [seq4(T_CTX, KVL), seq4(T_CTX, ROPE), tok(D), tok(D), tok(LANES), tok(LANES),
                 pl.BlockSpec((rows // TB, 1, LANES), lambda i: (i, 0, 0))]
    out_shape = [
        jax.ShapeDtypeStruct((N_CTX_SEQ, 1, T_CTX, KVL), F32),
        jax.ShapeDtypeStruct((N_CTX_SEQ, 1, T_CTX, ROPE), F32),
        jax.ShapeDtypeStruct((N_CTX, D), F32),
        jax.ShapeDtypeStruct((N_CTX, D), BF16),
        jax.ShapeDtypeStruct((N_CTX, LANES), jnp.int32),
        jax.ShapeDtypeStruct((N_CTX, LANES), F32),
        jax.ShapeDtypeStruct((N_CTX // TB, 1, LANES), F32),
    ]
    return pl.pallas_call(
        _ctx_kernel, grid=(N_CTX // rows,), in_specs=in_specs, out_specs=out_specs, out_shape=out_shape,
        scratch_shapes=[pltpu.VMEM((seqs, 2, T_CTX, FW), BF16),
                        pltpu.VMEM((seqs, HEADS, T_CTX, 2 * LANES), BF16),
                        pltpu.VMEM((seqs, T_CTX, 2 * LANES), BF16)],
        compiler_params=_params(("arbitrary",)),
        name="ctx",
    )(x2d, mod, w_in_ext, cdft, qg, kvg, wqa, wqr, pdft, wfo, woa, wg, bg, wout, ln1g, ln1b, wr, br,
      tri, upper)


def _pair_tiles(ref, pair0, n_pairs):
    return ref.at[pl.ds(pl.multiple_of(pair0 * SUB, SUB), n_pairs * SUB), :]


def _piece_tables(lstart, blen, gstart):
    bits = jnp.arange(RUN_BITS, dtype=jnp.int32)
    n = blen[:, :, None]
    valid = (n >> bits) & 1
    done = (n >> (bits + 1)) << (bits + 1)
    rank = jnp.cumsum(valid, axis=1) - valid
    place = (valid[..., None] == 1) & (rank[..., None] == jnp.arange(N_EXP, dtype=jnp.int32))
    def table(first_row):
        rows = (first_row[:, :, None] + done) * SUB
        listed = jnp.sum(jnp.where(place, rows[..., None], 0), axis=1)
        return listed.reshape(-1).astype(jnp.int32)
    return jnp.sum(valid, axis=1).reshape(-1).astype(jnp.int32), table(lstart), table(gstart)


def _start_piece_copies(cnt_ref, loc_ref, glo_ref, blk, local_ref, global_ref, sem, to_global):
    for bit in range(RUN_BITS):
        size = (1 << bit) * SUB
        base = (blk * RUN_BITS + bit) * N_EXP

        def piece(t, carry):
            loc = local_ref.at[pl.ds(pl.multiple_of(loc_ref[base + t], SUB), size), :]
            glo = global_ref.at[pl.ds(pl.multiple_of(glo_ref[base + t], SUB), size), :]
            (pltpu.make_async_copy(loc, glo, sem) if to_global
             else pltpu.make_async_copy(glo, loc, sem)).start()
            return carry

        lax.fori_loop(0, cnt_ref[blk * RUN_BITS + bit], piece, 0)


def _wait_block_pieces(extra_ref, blk, vmem_ref, hbm_ref, sem, to_global):
    def wait(n_pairs):
        loc = _pair_tiles(vmem_ref, 0, n_pairs)
        glo = _pair_tiles(hbm_ref, 0, n_pairs)
        (pltpu.make_async_copy(loc, glo, sem) if to_global else pltpu.make_async_copy(glo, loc, sem)).wait()

    wait(BLK_PAIRS)
    extra = extra_ref[blk]
    for bit in range(EXTRA_BITS):
        @pl.when((extra & (1 << bit)) != 0)
        def _():
            wait(1 << bit)


def _load_pairs(ref, n_pairs):
    words = jnp.concatenate([ref[pl.ds(s, n_pairs, stride=SUB), :] for s in range(SUB)], axis=1)
    return pltpu.bitcast(words, BF16)


def _store_pairs(ref, rows, n_pairs):
    words = pltpu.bitcast(rows, U32)
    for s in range(SUB):
        ref[pl.ds(s, n_pairs, stride=SUB), :] = words[:, s * LANES:(s + 1) * LANES]


def _zero_fill_tail(zero_ref, hbm_ref, sem):
    zero_ref[...] = jnp.zeros(zero_ref.shape, U32)
    copies = [pltpu.make_async_copy(zero_ref, _pair_tiles(hbm_ref, p, TILE_PAIRS), sem)
              for p in range(NP_SORT, NP_PAD, TILE_PAIRS)]
    for cp in copies:
        cp.start()
    for cp in copies:
        cp.wait()


def _dispatch_kernel(cnt_ref, loc_ref, glo_ref, extra_ref, h2c_ref, h2l_ref, lpc_ref, lpl_ref, xs_ref,
                     buf, zbuf, sem, zsem):
    b = pl.program_id(0)
    slot = b & 1

    def start_runs(blk, s):
        _start_piece_copies(cnt_ref, loc_ref, glo_ref, blk, buf.at[s], xs_ref, sem.at[s], True)

    def wait_runs(blk, s):
        _wait_block_pieces(extra_ref, blk, buf.at[s], xs_ref, sem.at[s], True)

    @pl.when(b == 0)
    def _():
        _zero_fill_tail(zbuf, xs_ref, zsem)

    @pl.when(b >= 2)
    def _():
        wait_runs(b - 2, slot)

    def sort_block(h2_ref, lp_ref):
        col = lax.broadcasted_iota(jnp.int32, (TB, BLK_ROWS_BUF), 1)
        lp = lp_ref[...]
        pick = jnp.zeros((TB, BLK_ROWS_BUF), F32)
        for k in range(TOPK):
            pick = pick + jnp.where(col == lp[:, k:k + 1], 1.0, 0.0)
        sorted_rows = lax.dot_general(pick.astype(BF16), h2_ref[...],
                                      (((0,), (0,)), ((), ())), preferred_element_type=F32)
        _store_pairs(buf.at[slot], sorted_rows.astype(BF16), BLK_PAIRS_BUF)

    @pl.when(b < N_CTX_BLK)
    def _():
        sort_block(h2c_ref, lpc_ref)

    @pl.when(b >= N_CTX_BLK)
    def _():
        sort_block(h2l_ref, lpl_ref)

    start_runs(b, slot)

    @pl.when(b == N_BLK - 1)
    def _():
        wait_runs(b - 1, 1 - slot)
        wait_runs(b, slot)


def _dispatch(cnt, loc, glo, extra, h2c, h2l, lpc, lpl):
    ctx_idx = lambda i, *_: (jnp.minimum(i, N_CTX_BLK - 1), 0)
    lat_idx = lambda i, *_: (jnp.maximum(i - N_CTX_BLK, 0), 0)
    grid_spec = pltpu.PrefetchScalarGridSpec(
        num_scalar_prefetch=4,
        grid=(N_BLK,),
        in_specs=[
            pl.BlockSpec((TB, D), ctx_idx),
            pl.BlockSpec((TB, D), lat_idx),
            pl.BlockSpec((TB, LANES), ctx_idx),
            pl.BlockSpec((TB, LANES), lat_idx),
        ],
        out_specs=pl.BlockSpec(memory_space=pl.ANY),
        scratch_shapes=[pltpu.VMEM((2, BLK_PAIRS_BUF * SUB, LANES), U32),
                        pltpu.VMEM((TILE_PAIRS * SUB, LANES), U32),
                        pltpu.SemaphoreType.DMA((2,)),
                        pltpu.SemaphoreType.DMA(())],
    )
    return pl.pallas_call(
        _dispatch_kernel, grid_spec=grid_spec,
        out_shape=jax.ShapeDtypeStruct((NP_PAD * SUB, LANES), U32),
        compiler_params=_params(("arbitrary",)),
        name="dispatch",
    )(cnt, loc, glo, extra, h2c, h2l, lpc, lpl)


def _expert_kernel(start_ref, count_ref, xs_ref, wg_ref, bg_ref, wu_ref, bu_ref, wd_ref, bd_ref, ys_ref,
                   wbf, xbuf, ybuf, sem_in, sem_out):
    e = pl.program_id(0)
    pair0 = start_ref[e]

    def tiles_of(ex):
        return lax.shift_right_logical(count_ref[ex] + (TILE_PAIRS - 1), TILE_PAIRS.bit_length() - 1)

    n_tiles = tiles_of(e)

    def in_copy_of(ex, j, slot):
        return pltpu.make_async_copy(_pair_tiles(xs_ref, start_ref[ex] + j * TILE_PAIRS, TILE_PAIRS),
                                     xbuf.at[slot], sem_in.at[slot])

    def in_copy(j, slot):
        return in_copy_of(e, j, slot)

    def out_copy(j, slot):
        return pltpu.make_async_copy(ybuf.at[slot], _pair_tiles(ys_ref, pair0 + j * TILE_PAIRS, TILE_PAIRS),
                                     sem_out.at[slot])

    def start_first_tiles(ex):
        for j in range(RING - 1):
            @pl.when(tiles_of(ex) > j)
            def _():
                in_copy_of(ex, j, j).start()

    @pl.when(e == 0)
    def _():
        ybuf[...] = jnp.zeros(ybuf.shape, U32)
        _zero_fill_tail(ybuf.at[0], ys_ref, sem_out.at[0])
        start_first_tiles(0)

    @pl.when(n_tiles > 0)
    def _():
        wbf[0] = wg_ref[...].astype(BF16)
        wbf[1] = wu_ref[...].astype(BF16)
        wbf[2] = wd_ref[...].astype(BF16)

        def tile(j, carry):
            slot = j & (RING - 1)

            ahead = j + (RING - 1)

            @pl.when(ahead < n_tiles)
            def _():
                in_copy(ahead, ahead & (RING - 1)).start()

            in_copy(j, slot).wait()

            @pl.when(j >= RING)
            def _():
                out_copy(j - RING, slot).wait()

            def mlp(n_pairs):
                rows = pl.ds(0, n_pairs * SUB)
                xb = _load_pairs(xbuf.at[slot, rows], n_pairs)
                gt = jnp.minimum(_dot(xb, wbf[0]) + bg_ref[...], SWIGLU_LIMIT)
                up = jnp.clip(_dot(xb, wbf[1]) + bu_ref[...], -SWIGLU_LIMIT, SWIGLU_LIMIT)
                act = gt * _sigmoid(SWIGLU_ALPHA * gt) * (up + 1.0)
                y = _dot(act.astype(BF16), wbf[2]) + bd_ref[...]
                _store_pairs(ybuf.at[slot, rows], y.astype(BF16), n_pairs)

            owned = count_ref[e] - j * TILE_PAIRS

            @pl.when(owned > TILE_PAIRS // 2)
            def _():
                mlp(TILE_PAIRS)

            @pl.when(owned <= TILE_PAIRS // 2)
            def _():
                mlp(TILE_PAIRS // 2)

            out_copy(j, slot).start()
            return carry

        lax.fori_loop(0, n_tiles, tile, 0)

        for back in range(RING, 0, -1):
            @pl.when(n_tiles >= back)
            def _():
                out_copy(n_tiles - back, (n_tiles - back) & (RING - 1)).wait()

    @pl.when(e + 1 < N_EXP)
    def _():
        start_first_tiles(e + 1)


def _experts(start, count, xs, wg, bg, wu, bu, wd, bd):
    w_idx = lambda e, *_: (0, e, 0, 0)
    grid_spec = pltpu.PrefetchScalarGridSpec(
        num_scalar_prefetch=2,
        grid=(N_EXP,),
        in_specs=[
            pl.BlockSpec(memory_space=pl.ANY),
            pl.BlockSpec((None, None, D, D_EXP), w_idx),
            pl.BlockSpec((None, None, 1, D_EXP), w_idx),
            pl.BlockSpec((None, None, D, D_EXP), w_idx),
            pl.BlockSpec((None, None, 1, D_EXP), w_idx),
            pl.BlockSpec((None, None, D_EXP, D), w_idx),
            pl.BlockSpec((None, None, 1, D), w_idx),
        ],
        out_specs=pl.BlockSpec(memory_space=pl.ANY),
        scratch_shapes=[pltpu.VMEM((3, D, D_EXP), BF16),
                        pltpu.VMEM((RING, TILE_PAIRS * SUB, LANES), U32),
                        pltpu.VMEM((RING, TILE_PAIRS * SUB, LANES), U32),
                        pltpu.SemaphoreType.DMA((RING,)),
                        pltpu.SemaphoreType.DMA((RING,))],
    )
    return pl.pallas_call(
        _expert_kernel, grid_spec=grid_spec,
        out_shape=jax.ShapeDtypeStruct((NP_PAD * SUB, LANES), U32),
        compiler_params=_params(("arbitrary",)),
        name="experts",
    )(start, count, xs, wg, bg, wu, bu, wd, bd)


def _combine_kernel(cnt_ref, loc_ref, glo_ref, extra_ref, lp_ref, topw_ref, x1_ref, mod_ref, g_ref, b_ref,
                    ys_ref, o_ref, buf, sem, *, blk0, n_blk, blocks_per_seq, mod_row0, mod_row_step):
    i = pl.program_id(0)
    slot = i & 1
    gate2 = _mod_row(mod_ref, mod_row0, mod_row_step, i // blocks_per_seq)[5]

    def start_runs(step, s):
        _start_piece_copies(cnt_ref, loc_ref, glo_ref, blk0 + step, buf.at[s], ys_ref, sem.at[s], False)

    @pl.when(i == 0)
    def _():
        buf[...] = jnp.zeros(buf.shape, U32)
        start_runs(0, 0)

    @pl.when(i + 1 < n_blk)
    def _():
        start_runs(i + 1, 1 - slot)

    _wait_block_pieces(extra_ref, blk0 + i, buf.at[slot], ys_ref, sem.at[slot], False)

    y_sorted = _load_pairs(buf.at[slot], BLK_PAIRS_BUF)
    col = lax.broadcasted_iota(jnp.int32, (TB, BLK_ROWS_BUF), 1)
    lp = lp_ref[...]
    topw = topw_ref[...]
    weights = jnp.zeros((TB, BLK_ROWS_BUF), F32)
    for k in range(TOPK):
        weights = jnp.where(col == lp[:, k:k + 1], topw[:, k:k + 1], weights)
    ffn = _dot(weights.astype(BF16), y_sorted)
    y = DN_ALPHA * x1_ref[...] + gate2 * ffn
    o_ref[...] = _norm_noaffine(y) * g_ref[...] + b_ref[...]


def _combine(cnt, loc, glo, extra, lpos, topw, x1, mod, ln2g, ln2b, ys, *, blk0, n_seq, seq_len,
             mod_row0, mod_row_step):
    bps = seq_len // TB
    n_blk = n_seq * bps
    tok = lambda i, *_: (i, 0)
    const = lambda i, *_: (0, 0)
    grid_spec = pltpu.PrefetchScalarGridSpec(
        num_scalar_prefetch=4,
        grid=(n_blk,),
        in_specs=[
            pl.BlockSpec((TB, LANES), tok),
            pl.BlockSpec((TB, LANES), tok),
            pl.BlockSpec((TB, D), tok),
            pl.BlockSpec((8, 6 * D), const),
            pl.BlockSpec((1, D), const),
            pl.BlockSpec((1, D), const),
            pl.BlockSpec(memory_space=pl.ANY),
        ],
        out_specs=pl.BlockSpec((TB, D), tok),
        scratch_shapes=[pltpu.VMEM((2, BLK_PAIRS_BUF * SUB, LANES), U32), pltpu.SemaphoreType.DMA((2,))],
    )
    kern = functools.partial(_combine_kernel, blk0=blk0, n_blk=n_blk, blocks_per_seq=bps,
                             mod_row0=mod_row0, mod_row_step=mod_row_step)
    return pl.pallas_call(
        kern, grid_spec=grid_spec,
        out_shape=jax.ShapeDtypeStruct((n_seq * seq_len, D), F32),
        compiler_params=_params(("arbitrary",)),
        name="combine_lat" if blk0 else "combine_ctx",
    )(cnt, loc, glo, extra, lpos, topw, x1, mod, ln2g, ln2b, ys)


def _dft_tables():
    def cs(n):
        k = np.arange(n, dtype=np.int64)
        ang = 2.0 * np.pi * ((k[:, None] * k[None, :]) % n).astype(np.float64) / n
        return np.cos(ang) / math.sqrt(n), np.sin(ang) / math.sqrt(n)

    c, s = cs(FG)
    cdft = np.concatenate([c, s], axis=1).astype(np.float32)
    pd = {}
    for t in (T_CTX, T_LAT):
        c, s = cs(t)
        pd[t] = np.concatenate([c, -s], axis=1).astype(np.float32)
    tri = np.tril(np.ones((TB, TB), np.float32), k=-1)
    upper = np.triu(np.ones((LANES, LANES), np.float32), k=1)
    return cdft, pd, tri, upper


_ROT_PERM = np.array(list(range(8, 16)) + list(range(0, 8)) + list(range(24, 32)) + list(range(16, 24)))
_ROT_SIGN = np.array([-1.0] * 8 + [1.0] * 8 + [-1.0] * 8 + [1.0] * 8, np.float32)


def _rope_tables():
    rows = T_LAT // GRID_W
    row = jnp.repeat(jnp.arange(rows, dtype=F32), GRID_W)
    col = jnp.tile(jnp.arange(GRID_W, dtype=F32), rows)
    axis_dim = ROPE // 2
    inv_freq = ROPE_THETA ** (-jnp.arange(0, axis_dim, 2, dtype=F32) / axis_dim)
    ang_r = row[:, None] * inv_freq[None, :]
    ang_c = col[:, None] * inv_freq[None, :]
    ang = jnp.concatenate([ang_r, ang_r, ang_c, ang_c], axis=-1)
    pad = ((0, 0), (0, LANES - ROPE))
    return jnp.pad(jnp.cos(ang), pad), jnp.pad(jnp.sin(ang), pad)


def kernel(x_prompt, x_sample, cache_ckv, cache_krope, c, c_ctx, w_mod, b_mod, w_in, q_norm_g, w_q_up,
           kv_norm_g, w_kv_up, w_fourier_o, w_mla_o, w_gate, b_gate, w_out, ln1_g, ln1_b, w_router,
           b_router, w_gate_e, b_gate_e, w_up_e, b_up_e, w_down_e, b_down_e, ln2_g, ln2_b):
    cdft_np, pdft_np, tri_np, upper_np = _dft_tables()
    upper = jnp.asarray(upper_np)
    cdft = jnp.asarray(cdft_np).astype(BF16)
    pdft_ctx = jnp.asarray(pdft_np[T_CTX]).astype(BF16)
    pdft_lat = jnp.asarray(pdft_np[T_LAT]).astype(BF16)
    tri = jnp.asarray(tri_np).astype(BF16)
    cos_pad, sin_pad = _rope_tables()

    w_in0 = w_in[0]
    kr_w = w_in0[:, FW + QL + KVL:]
    lane_pad = ((0, 0), (0, LANES - ROPE))
    w_in_ext = jnp.concatenate(
        [w_in0[:, :FW + QL + KVL], jnp.pad(kr_w, lane_pad),
         jnp.pad(kr_w[:, _ROT_PERM] * _ROT_SIGN, lane_pad)], axis=1).astype(BF16)
    wq3 = w_q_up[0].reshape(QL, HEADS, NOPE + ROPE)
    wq_nope = jnp.transpose(wq3[:, :, :NOPE], (1, 0, 2))
    wq_rope = wq3[:, :, NOPE:]
    head_pad = ((0, 0), (0, 0), (0, LANES - ROPE))
    wqr = jnp.concatenate(
        [jnp.pad(wq_rope, head_pad).reshape(QL, HEADS * LANES),
         jnp.pad(wq_rope[:, :, _ROT_PERM] * _ROT_SIGN, head_pad).reshape(QL, HEADS * LANES)],
        axis=1).astype(BF16)
    wkv3 = w_kv_up[0].reshape(KVL, HEADS, NOPE + VH)
    wk = jnp.transpose(wkv3[:, :, :NOPE], (1, 0, 2))
    wv = jnp.transpose(wkv3[:, :, NOPE:], (1, 0, 2))
    wo3 = w_mla_o[0].reshape(HEADS, VH, D)
    wqa, woa = _prep(wq_nope, wk, wv, wo3)

    c_all = jnp.concatenate([c_ctx[None, :], c, jnp.zeros((8 - 1 - N_LAT_SEQ, D), F32)], axis=0)
    mod = _modulation(c_all, w_mod[0], b_mod)

    qg = q_norm_g
    kvg = kv_norm_g
    wfo = w_fourier_o[0].astype(BF16)
    wg = w_gate[0].astype(BF16)
    wout = w_out[0].astype(BF16)
    wr_f32 = jnp.pad(w_router[0], ((0, 0), (0, LANES - N_EXP)))
    wr_hi = wr_f32.astype(BF16)
    wr = jnp.concatenate([wr_hi, (wr_f32 - wr_hi.astype(F32)).astype(BF16)], axis=1)
    br = jnp.pad(b_router, ((0, 0), (0, LANES - N_EXP)), constant_values=NEG_BIG)
    cache_k = jnp.concatenate(
        [cache_ckv[:, 0], jnp.pad(cache_krope[:, 0], ((0, 0), (0, 0), (0, LANES - ROPE)))],
        axis=-1).astype(BF16)

    xc2d = x_prompt.reshape(N_CTX, D)
    xl2d = x_sample.reshape(N_LAT, D)

    new_ckv, new_krope, x1_c, h2_c, lpos_c, topw_c, bcnt_c = _ctx(
        xc2d, mod, w_in_ext, cdft, qg, kvg, wqa, wqr, pdft_ctx, wfo, woa, wg, b_gate, wout, ln1_g, ln1_b,
        wr, br, tri, upper)

    fcs_l, qcat_l, kcat_l = _pre(
        xl2d, mod, w_in_ext, cdft, qg, kvg, wqa, wqr, cos_pad, sin_pad,
        n_seq=N_LAT_SEQ, seq_len=T_LAT, rope=True, mod_row0=1, mod_row_step=1)
    x1_l, h2_l, lpos_l, topw_l, bcnt_l = _mix(
        xl2d, mod, fcs_l, pdft_lat, qcat_l, kcat_l, cache_k, wfo, woa, wg, b_gate, wout, ln1_g,
        ln1_b, wr, br, tri, upper, rows=MIX_ROWS_LAT, n_seq=N_LAT_SEQ, seq_len=T_LAT, mod_row0=1,
        mod_row_step=1)

    blen = jnp.concatenate([bcnt_c[:, 0, :N_EXP], bcnt_l[:, 0, :N_EXP]], axis=0).astype(jnp.int32)
    blen = (blen + (blen & 1)) // 2
    lstart = jnp.cumsum(blen, axis=1) - blen
    count = jnp.sum(blen, axis=0)
    start = jnp.cumsum(count) - count
    gstart = start[None, :] + jnp.cumsum(blen, axis=0) - blen
    extra = (jnp.sum(blen, axis=1) - BLK_PAIRS).astype(jnp.int32)
    pieces = _piece_tables(lstart, blen, gstart) + (extra,)
    start = start.astype(jnp.int32)
    count = count.astype(jnp.int32)

    xs = _dispatch(*pieces, h2_c, h2_l, lpos_c, lpos_l)
    ys = _experts(start, count, xs,
                  w_gate_e, b_gate_e[:, :, None, :], w_up_e, b_up_e[:, :, None, :],
                  w_down_e, b_down_e[:, :, None, :])

    y_c = _combine(*pieces, lpos_c, topw_c, x1_c, mod, ln2_g, ln2_b, ys, blk0=0,
                   n_seq=N_CTX_SEQ, seq_len=T_CTX, mod_row0=0, mod_row_step=0)
    y_l = _combine(*pieces, lpos_l, topw_l, x1_l, mod, ln2_g, ln2_b, ys, blk0=N_CTX_BLK,
                   n_seq=N_LAT_SEQ, seq_len=T_LAT, mod_row0=1, mod_row_step=1)
    return (y_c.reshape(N_CTX_SEQ, T_CTX, D), y_l.reshape(N_LAT_SEQ, T_LAT, D), new_ckv, new_krope)
```

```python
import functools
import math

import numpy as np
import jax
import jax.numpy as jnp
from jax import lax
from jax.experimental import pallas as pl
from jax.experimental.pallas import tpu as pltpu

D = 1024
N_CTX_SEQ, T_CTX = 32, 256
N_LAT_SEQ, T_LAT = 4, 1024
PAST = 256
N_CTX = N_CTX_SEQ * T_CTX
N_LAT = N_LAT_SEQ * T_LAT
N_TOK = N_CTX + N_LAT
FW = 512
FG = 128
N_FG = FW // FG
HEADS = 8
QL = 256
KVL = 128
NOPE = 64
ROPE = 32
VH = 64
N_EXP = 32
TOPK = 4
D_EXP = 1024
SWIGLU_LIMIT = 7.0
SWIGLU_ALPHA = 1.702
LN_EPS = 1e-5
RMS_EPS = 1e-6
DN_ALPHA = 2.0 ** 0.25
ATT_SCALE = float(NOPE + ROPE) ** -0.5
ROPE_THETA = 10000.0
GRID_W = 64

LANES = 128
TB = 256
PRE_ROWS = 512
MIX_ROWS_LAT = 512
SUB = 8
ROW_TILE = 256
TILE_PAIRS = ROW_TILE // 2
RING = 4
N_CTX_BLK = N_CTX // TB
N_BLK = N_TOK // TB
BLK_PAIRS = TB * TOPK // 2
PAD_PAIRS_MAX = N_EXP // 2
BLK_PAIRS_BUF = -(-(BLK_PAIRS + PAD_PAIRS_MAX) // (LANES // 2)) * (LANES // 2)
BLK_ROWS_BUF = 2 * BLK_PAIRS_BUF
NP_SORT = N_TOK * TOPK // 2
NP_MAX = NP_SORT + N_BLK * PAD_PAIRS_MAX
NP_PAD = NP_MAX + TILE_PAIRS
RUN_BITS = (TB // 2).bit_length()
EXTRA_BITS = PAD_PAIRS_MAX.bit_length()
U32 = jnp.uint32
VMEM_LIMIT = 56 * 1024 * 1024
NEG_BIG = -1e30

F32 = jnp.float32
BF16 = jnp.bfloat16
HIGHEST = lax.Precision.HIGHEST


def _dot(a, b):
    return jnp.dot(a, b, preferred_element_type=F32)


def _dot_nt(a, b):
    return lax.dot_general(a, b, (((1,), (1,)), ((), ())), preferred_element_type=F32)


def _params(sem):
    return pltpu.CompilerParams(dimension_semantics=sem, vmem_limit_bytes=VMEM_LIMIT)


def _full(shape):
    n = len(shape)
    return pl.BlockSpec(shape, lambda *_: (0,) * n, pipeline_mode=pl.Buffered(1))


def _norm_noaffine(x):
    mu = jnp.mean(x, axis=-1, keepdims=True)
    xc = x - mu
    var = jnp.mean(xc * xc, axis=-1, keepdims=True)
    return xc * lax.rsqrt(var + LN_EPS)


def _sigmoid(x):
    return 0.5 * jnp.tanh(0.5 * x) + 0.5


def _rms(x, g):
    return x * lax.rsqrt(jnp.mean(x * x, axis=-1, keepdims=True) + RMS_EPS) * g


def _mod_row(mod_ref, row0, step, seq):
    row = row0 if step == 0 else row0 + step * seq
    m = mod_ref[pl.ds(row, 1), :]
    return [m[:, i * D:(i + 1) * D] for i in range(6)]


def _prep_kernel(wqn_ref, wk_ref, wv_ref, wo_ref, qabs_ref, oabs_ref):
    for h in range(HEADS):
        qabs = lax.dot_general(wqn_ref[h], wk_ref[h], (((1,), (1,)), ((), ())),
                               preferred_element_type=F32, precision=HIGHEST)
        qabs_ref[:, h * KVL:(h + 1) * KVL] = qabs.astype(BF16)
        oabs = jnp.dot(wv_ref[h], wo_ref[h], preferred_element_type=F32, precision=HIGHEST)
        oabs_ref[h * KVL:(h + 1) * KVL, :] = oabs.astype(BF16)


def _prep(wqn, wk, wv, wo):
    return pl.pallas_call(
        _prep_kernel,
        out_shape=[jax.ShapeDtypeStruct((QL, HEADS * KVL), BF16),
                   jax.ShapeDtypeStruct((HEADS * KVL, D), BF16)],
        compiler_params=pltpu.CompilerParams(vmem_limit_bytes=VMEM_LIMIT),
        name="weight_prep",
    )(wqn, wk, wv, wo)


MOD_COLS = 1536


def _mod_kernel(c_ref, w_ref, b_ref, o_ref):
    c = c_ref[...]
    s = c * _sigmoid(c)
    s_hi = s.astype(BF16)
    s_lo = (s - s_hi.astype(F32)).astype(BF16)
    w = w_ref[...]
    w_hi = w.astype(BF16)
    w_lo = (w - w_hi.astype(F32)).astype(BF16)
    o_ref[...] = (_dot(s_hi, w_hi) + (_dot(s_hi, w_lo) + _dot(s_lo, w_hi))) + b_ref[...]


def _modulation(c_all, w_mod, b_mod):
    return pl.pallas_call(
        _mod_kernel,
        grid=(6 * D // MOD_COLS,),
        in_specs=[
            pl.BlockSpec((8, D), lambda i: (0, 0)),
            pl.BlockSpec((D, MOD_COLS), lambda i: (0, i)),
            pl.BlockSpec((1, MOD_COLS), lambda i: (0, i)),
        ],
        out_specs=pl.BlockSpec((8, MOD_COLS), lambda i: (0, i)),
        out_shape=jax.ShapeDtypeStruct((8, 6 * D), F32),
        compiler_params=_params(("arbitrary",)),
        name="modulation",
    )(c_all, w_mod, b_mod)


def _pre_kernel(*refs, rows, seq_len, seqs, rope, mod_row0, mod_row_step, seq_is_axis0=False):
    if rope:
        (x_ref, mod_ref, win_ref, cdft_ref, qg_ref, kvg_ref, wqa_ref, wqr_ref, cos_ref, sin_ref,
         fcs_ref, qcat_ref, kcat_ref) = refs
    else:
        (x_ref, mod_ref, win_ref, cdft_ref, qg_ref, kvg_ref, wqa_ref, wqr_ref,
         fcs_ref, qcat_ref, kcat_ref, ckv_ref, krope_ref) = refs
    q_rows = rows // seqs
    b = pl.program_id(0) if seq_is_axis0 else pl.program_id(0) // (seq_len // q_rows)
    shift1, scale1 = _mod_row(mod_ref, mod_row0, mod_row_step, b)[:2]
    seq_rows = [slice(s * q_rows, (s + 1) * q_rows) for s in range(seqs)]

    x = x_ref[...]
    h1 = _norm_noaffine(x) * (1.0 + scale1) + shift1
    h1b = h1.astype(BF16)
    n_cols = 9 * LANES if rope else 8 * LANES
    proj = _dot(h1b, win_ref[:, :n_cols])

    f_b = proj[:, :FW].astype(BF16)
    for g in range(N_FG):
        r = _dot(f_b[:, g * FG:(g + 1) * FG], cdft_ref[...])
        for s, sl in enumerate(seq_rows):
            fcs_ref[s, 0, :, g * FG:(g + 1) * FG] = r[sl, :FG].astype(BF16)
            fcs_ref[s, 1, :, g * FG:(g + 1) * FG] = r[sl, FG:].astype(BF16)

    qn = _rms(proj[:, FW:FW + QL], qg_ref[...]).astype(BF16)
    ckv = _rms(proj[:, FW + QL:FW + QL + KVL], kvg_ref[...])
    kr = proj[:, 7 * LANES:8 * LANES]

    qa = _dot(qn, wqa_ref[...]) * ATT_SCALE
    if rope:
        cos = cos_ref[...]
        sin = sin_ref[...]
        qr2 = _dot(qn, wqr_ref[...])
        kr_keys = kr * cos + proj[:, 8 * LANES:9 * LANES] * sin
    else:
        qr2 = _dot(qn, wqr_ref[:, :HEADS * LANES])
        kr_keys = kr
    ckv_b = ckv.astype(BF16)
    kr_b = kr_keys.astype(BF16)
    for h in range(HEADS):
        qr_h = qr2[:, h * LANES:(h + 1) * LANES]
        if rope:
            qr_h = qr_h * cos + qr2[:, (HEADS + h) * LANES:(HEADS + h + 1) * LANES] * sin
        qa_h = qa[:, h * KVL:(h + 1) * KVL].astype(BF16)
        qr_h = (qr_h * ATT_SCALE).astype(BF16)
        for s, sl in enumerate(seq_rows):
            qcat_ref[s, h, :, :KVL] = qa_h[sl]
            qcat_ref[s, h, :, KVL:] = qr_h[sl]
    for s, sl in enumerate(seq_rows):
        kcat_ref[s, :, :KVL] = ckv_b[sl]
        kcat_ref[s, :, KVL:] = kr_b[sl]
        if not rope:
            ckv_ref[s, 0] = ckv[sl]
            krope_ref[s, 0] = kr[sl, :ROPE]


def _pre(x2d, mod, w_in_ext, cdft, qg, kvg, wqa, wqr, cos_pad, sin_pad, *, n_seq, seq_len, rope,
         mod_row0, mod_row_step):
    rows = PRE_ROWS
    seqs = max(1, rows // seq_len)
    q_rows = rows // seqs
    bps = seq_len // q_rows
    n_blk = n_seq * seq_len // rows
    seq_blk = lambda i: (i // bps, 0, i % bps, 0)
    in_specs = [
        pl.BlockSpec((rows, D), lambda i: (i, 0)),
        _full((8, 6 * D)),
        _full((D, 9 * LANES)),
        _full((FG, 2 * FG)),
        _full((1, QL)),
        _full((1, KVL)),
        _full((QL, HEADS * KVL)),
        _full((QL, 2 * HEADS * LANES)),
    ]
    args = [x2d, mod, w_in_ext, cdft, qg, kvg, wqa, wqr]
    out_specs = [
        pl.BlockSpec((seqs, 2, q_rows, FW), seq_blk),
        pl.BlockSpec((seqs, HEADS, q_rows, 2 * LANES), seq_blk),
        pl.BlockSpec((seqs, q_rows, 2 * LANES), lambda i: (i // bps, i % bps, 0)),
    ]
    out_shape = [
        jax.ShapeDtypeStruct((n_seq, 2, seq_len, FW), BF16),
        jax.ShapeDtypeStruct((n_seq, HEADS, seq_len, 2 * LANES), BF16),
        jax.ShapeDtypeStruct((n_seq, seq_len, 2 * LANES), BF16),
    ]
    if rope:
        in_specs += [pl.BlockSpec((q_rows, LANES), lambda i: (i % bps, 0)),
                     pl.BlockSpec((q_rows, LANES), lambda i: (i % bps, 0))]
        args += [cos_pad, sin_pad]
    else:
        out_specs += [pl.BlockSpec((seqs, 1, q_rows, KVL), seq_blk),
                      pl.BlockSpec((seqs, 1, q_rows, ROPE), seq_blk)]
        out_shape += [jax.ShapeDtypeStruct((n_seq, 1, seq_len, KVL), F32),
                      jax.ShapeDtypeStruct((n_seq, 1, seq_len, ROPE), F32)]
    kern = functools.partial(_pre_kernel, rows=rows, seq_len=seq_len, seqs=seqs, rope=rope,
                             mod_row0=mod_row0, mod_row_step=mod_row_step)
    return pl.pallas_call(
        kern, grid=(n_blk,), in_specs=in_specs, out_specs=out_specs, out_shape=out_shape,
        compiler_params=_params(("arbitrary",)),
        name="pre_lat" if rope else "pre_ctx",
    )(*args)


def _mix_kernel(*refs, rows, seq_len, seqs, cache, mod_row0, mod_row_step):
    if cache:
        (x_ref, mod_ref, fcs_ref, pdft_ref, qcat_ref, kcat_ref, cache_ref, wfo_ref, woa_ref,
         wg_ref, bg_ref, wout_ref, ln1g_ref, ln1b_ref, wr_ref, br_ref, tri_ref, upper_ref,
         x1_ref, h2_ref, lpos_ref, topw_ref, bcnt_ref) = refs
    else:
        (x_ref, mod_ref, fcs_ref, pdft_ref, qcat_ref, kcat_ref, wfo_ref, woa_ref,
         wg_ref, bg_ref, wout_ref, ln1g_ref, ln1b_ref, wr_ref, br_ref, tri_ref, upper_ref,
         x1_ref, h2_ref, lpos_ref, topw_ref, bcnt_ref) = refs
        cache_ref = None
    b = pl.program_id(0)
    shift1, scale1, gate1, shift2, scale2, _ = _mod_row(mod_ref, mod_row0, mod_row_step, b)

    x = x_ref[...]
    h1b = (_norm_noaffine(x) * (1.0 + scale1) + shift1).astype(BF16)

    mixed_rows, attn_rows = [], []
    for s in range(seqs):
        mixed_rows.append(_dot(pdft_ref[:, :seq_len], fcs_ref[s, 0])
                          + _dot(pdft_ref[:, seq_len:], fcs_ref[s, 1]))
        kc = kcat_ref[s]
        vals = kc[:, :KVL]
        if cache:
            kc2 = cache_ref[s]
            vals2 = kc2[:, :KVL]
        heads_out = []
        for h in range(HEADS):
            q = qcat_ref[s, h]
            s1 = _dot_nt(q, kc)
            m = jnp.max(s1, axis=-1, keepdims=True)
            if cache:
                s2 = _dot_nt(q, kc2)
                m = jnp.maximum(m, jnp.max(s2, axis=-1, keepdims=True))
            p1 = jnp.exp(s1 - m)
            l = jnp.sum(p1, axis=-1, keepdims=True)
            o = _dot(p1.astype(BF16), vals)
            if cache:
                p2 = jnp.exp(s2 - m)
                l = l + jnp.sum(p2, axis=-1, keepdims=True)
                o = o + _dot(p2.astype(BF16), vals2)
            heads_out.append((o / l).astype(BF16))
        attn_rows.append(jnp.concatenate(heads_out, axis=1))
    mixed = jnp.concatenate(mixed_rows, axis=0) if seqs > 1 else mixed_rows[0]
    attn = jnp.concatenate(attn_rows, axis=0) if seqs > 1 else attn_rows[0]
    f_out = _dot(mixed.astype(BF16), wfo_ref[...])
    m_out = _dot(attn, woa_ref[...])

    gates = _sigmoid(_dot(h1b, wg_ref[...]) + bg_ref[...])
    merged = gates[:, :D] * f_out + gates[:, D:] * m_out
    mix = _dot(merged.astype(BF16), wout_ref[...])

    x1 = _norm_noaffine(DN_ALPHA * x + gate1 * mix) * ln1g_ref[...] + ln1b_ref[...]
    x1_ref[...] = x1
    h2 = _norm_noaffine(x1) * (1.0 + scale2) + shift2
    h2_ref[...] = h2.astype(BF16)

    h_hi = h2.astype(BF16)
    h_lo = (h2 - h_hi.astype(F32)).astype(BF16)
    hi_part = _dot(h_hi, wr_ref[...])
    logits_all = (hi_part[:, :LANES] + hi_part[:, LANES:] + _dot(h_lo, wr_ref[:, :LANES])) + br_ref[...]
    lane = lax.broadcasted_iota(jnp.int32, (TB, LANES), 1).astype(F32)
    for r in range(rows // TB):
        blk = slice(r * TB, (r + 1) * TB)
        work = logits_all[blk]
        top_v, top_i = [], []
        for _ in range(TOPK):
            mk = jnp.max(work, axis=-1, keepdims=True)
            ik = jnp.min(jnp.where(work == mk, lane, float(LANES)), axis=-1, keepdims=True)
            work = jnp.where(lane == ik, -jnp.inf, work)
            top_v.append(mk)
            top_i.append(ik)
        exps = [jnp.exp(v - top_v[0]) for v in top_v]
        denom = exps[0] + exps[1] + exps[2] + exps[3]

        onehot = jnp.zeros((TB, LANES), F32)
        for ik in top_i:
            onehot = onehot + jnp.where(lane == ik, 1.0, 0.0)
        counts = jnp.sum(onehot, axis=0, keepdims=True)
        padded = counts + (counts - 2.0 * jnp.floor(0.5 * counts))
        lower = jnp.dot(jnp.broadcast_to(padded, (8, LANES)), upper_ref[...],
                        preferred_element_type=F32, precision=HIGHEST)[0:1, :]
        before = _dot(tri_ref[...], onehot.astype(BF16)) + lower
        lpos = jnp.zeros((TB, LANES), jnp.int32)
        topw = jnp.zeros((TB, LANES), F32)
        for k in range(TOPK):
            pos = jnp.sum(jnp.where(lane == top_i[k], before, 0.0), axis=-1, keepdims=True)
            lpos = jnp.where(lane == float(k), pos.astype(jnp.int32), lpos)
            topw = jnp.where(lane == float(k), exps[k] / denom, topw)
        lpos_ref[blk, :] = lpos
        topw_ref[blk, :] = topw
        bcnt_ref[r] = counts


def _mix(x2d, mod, fcs, pdft, qcat, kcat, cache_k, wfo, woa, wg, bg, wout, ln1g, ln1b, wr, br,
         tri, upper, *, rows, n_seq, seq_len, mod_row0, mod_row_step):
    cache = cache_k is not None
    seqs = max(1, rows // seq_len)
    q_rows = rows // seqs
    bps = seq_len // q_rows
    n_outer = n_seq // seqs
    in_specs = [
        pl.BlockSpec((rows, D), lambda b, j: (b * bps + j, 0)),
        _full((8, 6 * D)),
        pl.BlockSpec((seqs, 2, seq_len, FW), lambda b, j: (b, 0, 0, 0)),
        pl.BlockSpec((q_rows, 2 * seq_len), lambda b, j: (j, 0)),
        pl.BlockSpec((seqs, HEADS, q_rows, 2 * LANES), lambda b, j: (b, 0, j, 0)),
        pl.BlockSpec((seqs, seq_len, 2 * LANES), lambda b, j: (b, 0, 0)),
    ]
    args = [x2d, mod, fcs, pdft, qcat, kcat]
    if cache:
        in_specs.append(pl.BlockSpec((seqs, PAST, 2 * LANES), lambda b, j: (b, 0, 0)))
        args.append(cache_k)
    in_specs += [
        _full((FW, D)), _full((HEADS * KVL, D)), _full((D, 2 * D)), _full((1, 2 * D)),
        _full((D, D)), _full((1, D)), _full((1, D)), _full((D, 2 * LANES)), _full((1, LANES)),
        _full((TB, TB)), _full((LANES, LANES)),
    ]
    args += [wfo, woa, wg, bg, wout, ln1g, ln1b, wr, br, tri, upper]
    n_tok = n_seq * seq_len
    tok_spec = lambda w: pl.BlockSpec((rows, w), lambda b, j: (b * bps + j, 0))
    out_specs = [tok_spec(D), tok_spec(D), tok_spec(LANES), tok_spec(LANES),
                 pl.BlockSpec((rows // TB, 1, LANES), lambda b, j: (b * bps + j, 0, 0))]
    out_shape = [
        jax.ShapeDtypeStruct((n_tok, D), F32),
        jax.ShapeDtypeStruct((n_tok, D), BF16),
        jax.ShapeDtypeStruct((n_tok, LANES), jnp.int32),
        jax.ShapeDtypeStruct((n_tok, LANES), F32),
        jax.ShapeDtypeStruct((n_tok // TB, 1, LANES), F32),
    ]
    kern = functools.partial(_mix_kernel, rows=rows, seq_len=seq_len, seqs=seqs, cache=cache,
                             mod_row0=mod_row0, mod_row_step=mod_row_step)
    return pl.pallas_call(
        kern, grid=(n_outer, bps), in_specs=in_specs, out_specs=out_specs, out_shape=out_shape,
        compiler_params=_params(("arbitrary", "arbitrary")),
        name="mix_lat" if cache else "mix_ctx",
    )(*args)


CTX_ROWS = 512
CTX_SEQS = CTX_ROWS // T_CTX


def _ctx_kernel(x_ref, mod_ref, win_ref, cdft_ref, qg_ref, kvg_ref, wqa_ref, wqr_ref,
                pdft_ref, wfo_ref, woa_ref, wg_ref, bg_ref, wout_ref, ln1g_ref, ln1b_ref, wr_ref, br_ref,
                tri_ref, upper_ref,
                ckv_ref, krope_ref, x1_ref, h2_ref, lpos_ref, topw_ref, bcnt_ref,
                fcs_buf, qcat_buf, kcat_buf):
    stage = dict(rows=CTX_ROWS, seq_len=T_CTX, seqs=CTX_SEQS, mod_row0=0, mod_row_step=0)
    _pre_kernel(x_ref, mod_ref, win_ref, cdft_ref, qg_ref, kvg_ref, wqa_ref, wqr_ref,
                fcs_buf, qcat_buf, kcat_buf, ckv_ref, krope_ref, rope=False, **stage)
    _mix_kernel(x_ref, mod_ref, fcs_buf, pdft_ref, qcat_buf, kcat_buf, wfo_ref, woa_ref, wg_ref, bg_ref,
                wout_ref, ln1g_ref, ln1b_ref, wr_ref, br_ref, tri_ref, upper_ref,
                x1_ref, h2_ref, lpos_ref, topw_ref, bcnt_ref, cache=False, **stage)


def _ctx(x2d, mod, w_in_ext, cdft, qg, kvg, wqa, wqr, pdft, wfo, woa, wg, bg, wout, ln1g, ln1b, wr, br,
         tri, upper):
    rows, seqs = CTX_ROWS, CTX_SEQS
    tok = lambda w: pl.BlockSpec((rows, w), lambda i: (i, 0))
    seq4 = lambda a, b: pl.BlockSpec((seqs, 1, a, b), lambda i: (i, 0, 0, 0))
    in_specs = [
        tok(D), _full((8, 6 * D)), _full((D, 9 * LANES)), _full((FG, 2 * FG)), _full((1, QL)),
        _full((1, KVL)), _full((QL, HEADS * KVL)), _full((QL, 2 * HEADS * LANES)),
        _full((T_CTX, 2 * T_CTX)), _full((FW, D)), _full((HEADS * KVL, D)), _full((D, 2 * D)),
        _full((1, 2 * D)), _full((D, D)), _full((1, D)), _full((1, D)), _full((D, 2 * LANES)),
        _full((1, LANES)), _full((TB, TB)), _full((LANES, LANES)),
    ]
    out_specs = [seq4(T_CTX, KVL), seq4(T_CTX, ROPE), tok(D), tok(D), tok(LANES), tok(LANES),
                 pl.BlockSpec((rows // TB, 1, LANES), lambda i: (i, 0, 0))]
    out_shape = [
        jax.ShapeDtypeStruct((N_CTX_SEQ, 1, T_CTX, KVL), F32),
        jax.ShapeDtypeStruct((N_CTX_SEQ, 1, T_CTX, ROPE), F32),
        jax.ShapeDtypeStruct((N_CTX, D), F32),
        jax.ShapeDtypeStruct((N_CTX, D), BF16),
        jax.ShapeDtypeStruct((N_CTX, LANES), jnp.int32),
        jax.ShapeDtypeStruct((N_CTX, LANES), F32),
        jax.ShapeDtypeStruct((N_CTX // TB, 1, LANES), F32),
    ]
    return pl.pallas_call(
        _ctx_kernel, grid=(N_CTX // rows,), in_specs=in_specs, out_specs=out_specs, out_shape=out_shape,
        scratch_shapes=[pltpu.VMEM((seqs, 2, T_CTX, FW), BF16),
                        pltpu.VMEM((seqs, HEADS, T_CTX, 2 * LANES), BF16),
                        pltpu.VMEM((seqs, T_CTX, 2 * LANES), BF16)],
        compiler_params=_params(("arbitrary",)),
        name="ctx",
    )(x2d, mod, w_in_ext, cdft, qg, kvg, wqa, wqr, pdft, wfo, woa, wg, bg, wout, ln1g, ln1b, wr, br,
      tri, upper)


LAT_ROWS = 512
LAT_VMEM_LIMIT = 60 * 1024 * 1024
LAT_BLOCKS = T_LAT // LAT_ROWS


def _lat_kernel(x_ref, mod_ref, win_ref, cdft_ref, qg_ref, kvg_ref, wqa_ref, wqr_ref, cos_ref, sin_ref,
                pdft_ref, cache_ref, wfo_ref, woa_ref, wg_ref, bg_ref, wout_ref, ln1g_ref, ln1b_ref,
                wr_ref, br_ref, tri_ref, upper_ref,
                x1_ref, h2_ref, lpos_ref, topw_ref, bcnt_ref,
                fcs_buf, qcat_buf, kcat_buf):
    p = pl.program_id(1)
    stage = dict(rows=LAT_ROWS, seq_len=T_LAT, seqs=1, mod_row0=1, mod_row_step=1)

    @pl.when(p < LAT_BLOCKS)
    def _():
        r = pl.ds(pl.multiple_of(p * LAT_ROWS, LAT_ROWS), LAT_ROWS)
        _pre_kernel(x_ref, mod_ref, win_ref, cdft_ref, qg_ref, kvg_ref, wqa_ref, wqr_ref, cos_ref, sin_ref,
                    fcs_buf.at[:, :, r, :], qcat_buf.at[:, :, r, :], kcat_buf.at[:, r, :],
                    rope=True, seq_is_axis0=True, **stage)

    @pl.when(p >= LAT_BLOCKS)
    def _():
        r = pl.ds(pl.multiple_of((p - LAT_BLOCKS) * LAT_ROWS, LAT_ROWS), LAT_ROWS)
        _mix_kernel(x_ref, mod_ref, fcs_buf, pdft_ref, qcat_buf.at[:, :, r, :], kcat_buf, cache_ref,
                    wfo_ref, woa_ref, wg_ref, bg_ref, wout_ref, ln1g_ref, ln1b_ref, wr_ref, br_ref,
                    tri_ref, upper_ref, x1_ref, h2_ref, lpos_ref, topw_ref, bcnt_ref, cache=True, **stage)


def _lat(x2d, mod, w_in_ext, cdft, qg, kvg, wqa, wqr, cos_pad, sin_pad, pdft, cache_k, wfo, woa, wg, bg,
         wout, ln1g, ln1b, wr, br, tri, upper):
    rows, nb = LAT_ROWS, LAT_BLOCKS
    row_blk = lambda b, p: b * nb + p % nb
    out_blk = lambda b, p: b * nb + jnp.maximum(p - nb, 0)
    tok_in = pl.BlockSpec((rows, D), lambda b, p: (row_blk(b, p), 0))
    tok_out = lambda w: pl.BlockSpec((rows, w), lambda b, p: (out_blk(b, p), 0))
    in_specs = [
        tok_in, _full((8, 6 * D)), _full((D, 9 * LANES)), _full((FG, 2 * FG)), _full((1, QL)),
        _full((1, KVL)), _full((QL, HEADS * KVL)), _full((QL, 2 * HEADS * LANES)),
        pl.BlockSpec((rows, LANES), lambda b, p: (p % nb, 0)),
        pl.BlockSpec((rows, LANES), lambda b, p: (p % nb, 0)),
        pl.BlockSpec((rows, 2 * T_LAT), lambda b, p: (jnp.maximum(p - nb, 0), 0)),
        pl.BlockSpec((1, PAST, 2 * LANES), lambda b, p: (b, 0, 0)),
        _full((FW, D)), _full((HEADS * KVL, D)), _full((D, 2 * D)), _full((1, 2 * D)),
        _full((D, D)), _full((1, D)), _full((1, D)), _full((D, 2 * LANES)), _full((1, LANES)),
        _full((TB, TB)), _full((LANES, LANES)),
    ]
    out_specs = [tok_out(D), tok_out(D), tok_out(LANES), tok_out(LANES),
                 pl.BlockSpec((rows // TB, 1, LANES), lambda b, p: (out_blk(b, p), 0, 0))]
    out_shape = [
        jax.ShapeDtypeStruct((N_LAT, D), F32),
        jax.ShapeDtypeStruct((N_LAT, D), BF16),
        jax.ShapeDtypeStruct((N_LAT, LANES), jnp.int32),
        jax.ShapeDtypeStruct((N_LAT, LANES), F32),
        jax.ShapeDtypeStruct((N_LAT // TB, 1, LANES), F32),
    ]
    return pl.pallas_call(
        _lat_kernel, grid=(N_LAT_SEQ, 2 * nb), in_specs=in_specs, out_specs=out_specs, out_shape=out_shape,
        scratch_shapes=[pltpu.VMEM((1, 2, T_LAT, FW), BF16),
                        pltpu.VMEM((1, HEADS, T_LAT, 2 * LANES), BF16),
                        pltpu.VMEM((1, T_LAT, 2 * LANES), BF16)],
        compiler_params=pltpu.CompilerParams(dimension_semantics=("arbitrary", "arbitrary"),
                                             vmem_limit_bytes=LAT_VMEM_LIMIT),
        name="lat",
    )(x2d, mod, w_in_ext, cdft, qg, kvg, wqa, wqr, cos_pad, sin_pad, pdft, cache_k, wfo, woa, wg, bg,
      wout, ln1g, ln1b, wr, br, tri, upper)


def _pair_tiles(ref, pair0, n_pairs):
    return ref.at[pl.ds(pl.multiple_of(pair0 * SUB, SUB), n_pairs * SUB), :]


def _piece_tables(lstart, blen, gstart):
    bits = jnp.arange(RUN_BITS, dtype=jnp.int32)
    n = blen[:, :, None]
    valid = (n >> bits) & 1
    done = (n >> (bits + 1)) << (bits + 1)
    rank = jnp.cumsum(valid, axis=1) - valid
    place = (valid[..., None] == 1) & (rank[..., None] == jnp.arange(N_EXP, dtype=jnp.int32))
    def table(first_row):
        rows = (first_row[:, :, None] + done) * SUB
        listed = jnp.sum(jnp.where(place, rows[..., None], 0), axis=1)
        return listed.reshape(-1).astype(jnp.int32)
    return jnp.sum(valid, axis=1).reshape(-1).astype(jnp.int32), table(lstart), table(gstart)


def _start_piece_copies(cnt_ref, loc_ref, glo_ref, blk, local_ref, global_ref, sem, to_global):
    for bit in range(RUN_BITS):
        size = (1 << bit) * SUB
        base = (blk * RUN_BITS + bit) * N_EXP

        def piece(t, carry):
            loc = local_ref.at[pl.ds(pl.multiple_of(loc_ref[base + t], SUB), size), :]
            glo = global_ref.at[pl.ds(pl.multiple_of(glo_ref[base + t], SUB), size), :]
            (pltpu.make_async_copy(loc, glo, sem) if to_global
             else pltpu.make_async_copy(glo, loc, sem)).start()
            return carry

        lax.fori_loop(0, cnt_ref[blk * RUN_BITS + bit], piece, 0)


def _wait_block_pieces(extra_ref, blk, vmem_ref, hbm_ref, sem, to_global):
    def wait(n_pairs):
        loc = _pair_tiles(vmem_ref, 0, n_pairs)
        glo = _pair_tiles(hbm_ref, 0, n_pairs)
        (pltpu.make_async_copy(loc, glo, sem) if to_global else pltpu.make_async_copy(glo, loc, sem)).wait()

    wait(BLK_PAIRS)
    extra = extra_ref[blk]
    for bit in range(EXTRA_BITS):
        @pl.when((extra & (1 << bit)) != 0)
        def _():
            wait(1 << bit)


def _load_pairs(ref, n_pairs):
    words = jnp.concatenate([ref[pl.ds(s, n_pairs, stride=SUB), :] for s in range(SUB)], axis=1)
    return pltpu.bitcast(words, BF16)


def _store_pairs(ref, rows, n_pairs):
    words = pltpu.bitcast(rows, U32)
    for s in range(SUB):
        ref[pl.ds(s, n_pairs, stride=SUB), :] = words[:, s * LANES:(s + 1) * LANES]


def _zero_fill_tail(zero_ref, hbm_ref, sem):
    zero_ref[...] = jnp.zeros(zero_ref.shape, U32)
    copies = [pltpu.make_async_copy(zero_ref, _pair_tiles(hbm_ref, p, TILE_PAIRS), sem)
              for p in range(NP_SORT, NP_PAD, TILE_PAIRS)]
    for cp in copies:
        cp.start()
    for cp in copies:
        cp.wait()


def _dispatch_kernel(cnt_ref, loc_ref, glo_ref, extra_ref, h2c_ref, h2l_ref, lpc_ref, lpl_ref, xs_ref,
                     buf, zbuf, sem, zsem):
    b = pl.program_id(0)
    slot = b & 1

    def start_runs(blk, s):
        _start_piece_copies(cnt_ref, loc_ref, glo_ref, blk, buf.at[s], xs_ref, sem.at[s], True)

    def wait_runs(blk, s):
        _wait_block_pieces(extra_ref, blk, buf.at[s], xs_ref, sem.at[s], True)

    @pl.when(b == 0)
    def _():
        _zero_fill_tail(zbuf, xs_ref, zsem)

    @pl.when(b >= 2)
    def _():
        wait_runs(b - 2, slot)

    def sort_block(h2_ref, lp_ref):
        row = lax.broadcasted_iota(jnp.int32, (BLK_ROWS_BUF, TB), 0)
        lp_t = lp_ref[...].T
        pick = jnp.zeros((BLK_ROWS_BUF, TB), F32)
        for k in range(TOPK):
            pick = pick + jnp.where(row == lp_t[k:k + 1, :], 1.0, 0.0)
        sorted_rows = _dot(pick.astype(BF16), h2_ref[...])
        _store_pairs(buf.at[slot], sorted_rows.astype(BF16), BLK_PAIRS_BUF)

    @pl.when(b < N_CTX_BLK)
    def _():
        sort_block(h2c_ref, lpc_ref)

    @pl.when(b >= N_CTX_BLK)
    def _():
        sort_block(h2l_ref, lpl_ref)

    start_runs(b, slot)

    @pl.when(b == N_BLK - 1)
    def _():
        wait_runs(b - 1, 1 - slot)
        wait_runs(b, slot)


def _dispatch(cnt, loc, glo, extra, h2c, h2l, lpc, lpl):
    ctx_idx = lambda i, *_: (jnp.minimum(i, N_CTX_BLK - 1), 0)
    lat_idx = lambda i, *_: (jnp.maximum(i - N_CTX_BLK, 0), 0)
    grid_spec = pltpu.PrefetchScalarGridSpec(
        num_scalar_prefetch=4,
        grid=(N_BLK,),
        in_specs=[
            pl.BlockSpec((TB, D), ctx_idx),
            pl.BlockSpec((TB, D), lat_idx),
            pl.BlockSpec((TB, LANES), ctx_idx),
            pl.BlockSpec((TB, LANES), lat_idx),
        ],
        out_specs=pl.BlockSpec(memory_space=pl.ANY),
        scratch_shapes=[pltpu.VMEM((2, BLK_PAIRS_BUF * SUB, LANES), U32),
                        pltpu.VMEM((TILE_PAIRS * SUB, LANES), U32),
                        pltpu.SemaphoreType.DMA((2,)),
                        pltpu.SemaphoreType.DMA(())],
    )
    return pl.pallas_call(
        _dispatch_kernel, grid_spec=grid_spec,
        out_shape=jax.ShapeDtypeStruct((NP_PAD * SUB, LANES), U32),
        compiler_params=_params(("arbitrary",)),
        name="dispatch",
    )(cnt, loc, glo, extra, h2c, h2l, lpc, lpl)


def _expert_kernel(start_ref, count_ref, xs_ref, wg_ref, bg_ref, wu_ref, bu_ref, wd_ref, bd_ref, ys_ref,
                   wbf, xbuf, ybuf, sem_in, sem_out):
    e = pl.program_id(0)
    pair0 = start_ref[e]

    def tiles_of(ex):
        return lax.shift_right_logical(count_ref[ex] + (TILE_PAIRS - 1), TILE_PAIRS.bit_length() - 1)

    n_tiles = tiles_of(e)

    def in_copy_of(ex, j, slot):
        return pltpu.make_async_copy(_pair_tiles(xs_ref, start_ref[ex] + j * TILE_PAIRS, TILE_PAIRS),
                                     xbuf.at[slot], sem_in.at[slot])

    def in_copy(j, slot):
        return in_copy_of(e, j, slot)

    def out_copy(j, slot):
        return pltpu.make_async_copy(ybuf.at[slot], _pair_tiles(ys_ref, pair0 + j * TILE_PAIRS, TILE_PAIRS),
                                     sem_out.at[slot])

    def start_first_tiles(ex):
        for j in range(RING - 1):
            @pl.when(tiles_of(ex) > j)
            def _():
                in_copy_of(ex, j, j).start()

    @pl.when(e == 0)
    def _():
        ybuf[...] = jnp.zeros(ybuf.shape, U32)
        _zero_fill_tail(ybuf.at[0], ys_ref, sem_out.at[0])
        start_first_tiles(0)

    @pl.when(n_tiles > 0)
    def _():
        wbf[0] = wg_ref[...].astype(BF16)
        wbf[1] = wu_ref[...].astype(BF16)
        wbf[2] = wd_ref[...].astype(BF16)

        def tile(j, carry):
            slot = j & (RING - 1)

            ahead = j + (RING - 1)

            @pl.when(ahead < n_tiles)
            def _():
                in_copy(ahead, ahead & (RING - 1)).start()

            in_copy(j, slot).wait()

            @pl.when(j >= RING)
            def _():
                out_copy(j - RING, slot).wait()

            def mlp(n_pairs):
                rows = pl.ds(0, n_pairs * SUB)
                xb = _load_pairs(xbuf.at[slot, rows], n_pairs)
                mine = pl.ds(e, 1)
                gt = jnp.minimum(_dot(xb, wbf[0]) + bg_ref[mine, :], SWIGLU_LIMIT)
                up = jnp.clip(_dot(xb, wbf[1]) + bu_ref[mine, :], -SWIGLU_LIMIT, SWIGLU_LIMIT)
                act = gt * _sigmoid(SWIGLU_ALPHA * gt) * (up + 1.0)
                y = _dot(act.astype(BF16), wbf[2]) + bd_ref[mine, :]
                _store_pairs(ybuf.at[slot, rows], y.astype(BF16), n_pairs)

            owned = count_ref[e] - j * TILE_PAIRS

            @pl.when(owned > TILE_PAIRS // 2)
            def _():
                mlp(TILE_PAIRS)

            @pl.when(owned <= TILE_PAIRS // 2)
            def _():
                mlp(TILE_PAIRS // 2)

            out_copy(j, slot).start()
            return carry

        lax.fori_loop(0, n_tiles, tile, 0)

        for back in range(RING, 0, -1):
            @pl.when(n_tiles >= back)
            def _():
                out_copy(n_tiles - back, (n_tiles - back) & (RING - 1)).wait()

    @pl.when(e + 1 < N_EXP)
    def _():
        start_first_tiles(e + 1)


def _experts(start, count, xs, wg, bg, wu, bu, wd, bd):
    w_idx = lambda e, *_: (0, e, 0, 0)
    b_spec = lambda n: pl.BlockSpec((None, N_EXP, n), lambda e, *_: (0, 0, 0), pipeline_mode=pl.Buffered(1))
    grid_spec = pltpu.PrefetchScalarGridSpec(
        num_scalar_prefetch=2,
        grid=(N_EXP,),
        in_specs=[
            pl.BlockSpec(memory_space=pl.ANY),
            pl.BlockSpec((None, None, D, D_EXP), w_idx), b_spec(D_EXP),
            pl.BlockSpec((None, None, D, D_EXP), w_idx), b_spec(D_EXP),
            pl.BlockSpec((None, None, D_EXP, D), w_idx), b_spec(D),
        ],
        out_specs=pl.BlockSpec(memory_space=pl.ANY),
        scratch_shapes=[pltpu.VMEM((3, D, D_EXP), BF16),
                        pltpu.VMEM((RING, TILE_PAIRS * SUB, LANES), U32),
                        pltpu.VMEM((RING, TILE_PAIRS * SUB, LANES), U32),
                        pltpu.SemaphoreType.DMA((RING,)),
                        pltpu.SemaphoreType.DMA((RING,))],
    )
    return pl.pallas_call(
        _expert_kernel, grid_spec=grid_spec,
        out_shape=jax.ShapeDtypeStruct((NP_PAD * SUB, LANES), U32),
        compiler_params=_params(("arbitrary",)),
        name="experts",
    )(start, count, xs, wg, bg, wu, bu, wd, bd)


def _combine_kernel(cnt_ref, loc_ref, glo_ref, extra_ref, lp_ref, topw_ref, x1_ref, mod_ref, g_ref, b_ref,
                    ys_ref, o_ref, buf, sem, *, blk0, n_blk, blocks_per_seq, mod_row0, mod_row_step):
    i = pl.program_id(0)
    slot = i & 1
    gate2 = _mod_row(mod_ref, mod_row0, mod_row_step, i // blocks_per_seq)[5]

    def start_runs(step, s):
        _start_piece_copies(cnt_ref, loc_ref, glo_ref, blk0 + step, buf.at[s], ys_ref, sem.at[s], False)

    @pl.when(i == 0)
    def _():
        buf[...] = jnp.zeros(buf.shape, U32)
        start_runs(0, 0)

    @pl.when(i + 1 < n_blk)
    def _():
        start_runs(i + 1, 1 - slot)

    _wait_block_pieces(extra_ref, blk0 + i, buf.at[slot], ys_ref, sem.at[slot], False)

    y_sorted = _load_pairs(buf.at[slot], BLK_PAIRS_BUF)
    col = lax.broadcasted_iota(jnp.int32, (TB, BLK_ROWS_BUF), 1)
    lp = lp_ref[...]
    topw = topw_ref[...]
    weights = jnp.zeros((TB, BLK_ROWS_BUF), F32)
    for k in range(TOPK):
        weights = jnp.where(col == lp[:, k:k + 1], topw[:, k:k + 1], weights)
    ffn = _dot(weights.astype(BF16), y_sorted)
    y = DN_ALPHA * x1_ref[...] + gate2 * ffn
    o_ref[...] = _norm_noaffine(y) * g_ref[...] + b_ref[...]


def _combine(cnt, loc, glo, extra, lpos, topw, x1, mod, ln2g, ln2b, ys, *, blk0, n_seq, seq_len,
             mod_row0, mod_row_step):
    bps = seq_len // TB
    n_blk = n_seq * bps
    tok = lambda i, *_: (i, 0)
    const = lambda i, *_: (0, 0)
    grid_spec = pltpu.PrefetchScalarGridSpec(
        num_scalar_prefetch=4,
        grid=(n_blk,),
        in_specs=[
            pl.BlockSpec((TB, LANES), tok),
            pl.BlockSpec((TB, LANES), tok),
            pl.BlockSpec((TB, D), tok),
            pl.BlockSpec((8, 6 * D), const),
            pl.BlockSpec((1, D), const),
            pl.BlockSpec((1, D), const),
            pl.BlockSpec(memory_space=pl.ANY),
        ],
        out_specs=pl.BlockSpec((TB, D), tok),
        scratch_shapes=[pltpu.VMEM((2, BLK_PAIRS_BUF * SUB, LANES), U32), pltpu.SemaphoreType.DMA((2,))],
    )
    kern = functools.partial(_combine_kernel, blk0=blk0, n_blk=n_blk, blocks_per_seq=bps,
                             mod_row0=mod_row0, mod_row_step=mod_row_step)
    return pl.pallas_call(
        kern, grid_spec=grid_spec,
        out_shape=jax.ShapeDtypeStruct((n_seq * seq_len, D), F32),
        compiler_params=_params(("arbitrary",)),
        name="combine_lat" if blk0 else "combine_ctx",
    )(cnt, loc, glo, extra, lpos, topw, x1, mod, ln2g, ln2b, ys)


def _dft_tables():
    def cs(n):
        k = np.arange(n, dtype=np.int64)
        ang = 2.0 * np.pi * ((k[:, None] * k[None, :]) % n).astype(np.float64) / n
        return np.cos(ang) / math.sqrt(n), np.sin(ang) / math.sqrt(n)

    c, s = cs(FG)
    cdft = np.concatenate([c, s], axis=1).astype(np.float32)
    pd = {}
    for t in (T_CTX, T_LAT):
        c, s = cs(t)
        pd[t] = np.concatenate([c, -s], axis=1).astype(np.float32)
    tri = np.tril(np.ones((TB, TB), np.float32), k=-1)
    upper = np.triu(np.ones((LANES, LANES), np.float32), k=1)
    return cdft, pd, tri, upper


_ROT_PERM = np.array(list(range(8, 16)) + list(range(0, 8)) + list(range(24, 32)) + list(range(16, 24)))
_ROT_SIGN = np.array([-1.0] * 8 + [1.0] * 8 + [-1.0] * 8 + [1.0] * 8, np.float32)


def _rope_tables():
    rows = T_LAT // GRID_W
    row = jnp.repeat(jnp.arange(rows, dtype=F32), GRID_W)
    col = jnp.tile(jnp.arange(GRID_W, dtype=F32), rows)
    axis_dim = ROPE // 2
    inv_freq = ROPE_THETA ** (-jnp.arange(0, axis_dim, 2, dtype=F32) / axis_dim)
    ang_r = row[:, None] * inv_freq[None, :]
    ang_c = col[:, None] * inv_freq[None, :]
    ang = jnp.concatenate([ang_r, ang_r, ang_c, ang_c], axis=-1)
    pad = ((0, 0), (0, LANES - ROPE))
    return jnp.pad(jnp.cos(ang), pad), jnp.pad(jnp.sin(ang), pad)


def kernel(x_prompt, x_sample, cache_ckv, cache_krope, c, c_ctx, w_mod, b_mod, w_in, q_norm_g, w_q_up,
           kv_norm_g, w_kv_up, w_fourier_o, w_mla_o, w_gate, b_gate, w_out, ln1_g, ln1_b, w_router,
           b_router, w_gate_e, b_gate_e, w_up_e, b_up_e, w_down_e, b_down_e, ln2_g, ln2_b):
    cdft_np, pdft_np, tri_np, upper_np = _dft_tables()
    upper = jnp.asarray(upper_np)
    cdft = jnp.asarray(cdft_np).astype(BF16)
    pdft_ctx = jnp.asarray(pdft_np[T_CTX]).astype(BF16)
    pdft_lat = jnp.asarray(pdft_np[T_LAT]).astype(BF16)
    tri = jnp.asarray(tri_np).astype(BF16)
    cos_pad, sin_pad = _rope_tables()

    w_in0 = w_in[0]
    kr_w = w_in0[:, FW + QL + KVL:]
    lane_pad = ((0, 0), (0, LANES - ROPE))
    w_in_ext = jnp.concatenate(
        [w_in0[:, :FW + QL + KVL], jnp.pad(kr_w, lane_pad),
         jnp.pad(kr_w[:, _ROT_PERM] * _ROT_SIGN, lane_pad)], axis=1).astype(BF16)
    wq3 = w_q_up[0].reshape(QL, HEADS, NOPE + ROPE)
    wq_nope = jnp.transpose(wq3[:, :, :NOPE], (1, 0, 2))
    wq_rope = wq3[:, :, NOPE:]
    head_pad = ((0, 0), (0, 0), (0, LANES - ROPE))
    wqr = jnp.concatenate(
        [jnp.pad(wq_rope, head_pad).reshape(QL, HEADS * LANES),
         jnp.pad(wq_rope[:, :, _ROT_PERM] * _ROT_SIGN, head_pad).reshape(QL, HEADS * LANES)],
        axis=1).astype(BF16)
    wkv3 = w_kv_up[0].reshape(KVL, HEADS, NOPE + VH)
    wk = jnp.transpose(wkv3[:, :, :NOPE], (1, 0, 2))
    wv = jnp.transpose(wkv3[:, :, NOPE:], (1, 0, 2))
    wo3 = w_mla_o[0].reshape(HEADS, VH, D)
    wqa, woa = _prep(wq_nope, wk, wv, wo3)

    c_all = jnp.concatenate([c_ctx[None, :], c, jnp.zeros((8 - 1 - N_LAT_SEQ, D), F32)], axis=0)
    mod = _modulation(c_all, w_mod[0], b_mod)

    qg = q_norm_g
    kvg = kv_norm_g
    wfo = w_fourier_o[0].astype(BF16)
    wg = w_gate[0].astype(BF16)
    wout = w_out[0].astype(BF16)
    wr_f32 = jnp.pad(w_router[0], ((0, 0), (0, LANES - N_EXP)))
    wr_hi = wr_f32.astype(BF16)
    wr = jnp.concatenate([wr_hi, (wr_f32 - wr_hi.astype(F32)).astype(BF16)], axis=1)
    br = jnp.pad(b_router, ((0, 0), (0, LANES - N_EXP)), constant_values=NEG_BIG)
    cache_k = jnp.concatenate(
        [cache_ckv[:, 0], jnp.pad(cache_krope[:, 0], ((0, 0), (0, 0), (0, LANES - ROPE)))],
        axis=-1).astype(BF16)

    xc2d = x_prompt.reshape(N_CTX, D)
    xl2d = x_sample.reshape(N_LAT, D)

    new_ckv, new_krope, x1_c, h2_c, lpos_c, topw_c, bcnt_c = _ctx(
        xc2d, mod, w_in_ext, cdft, qg, kvg, wqa, wqr, pdft_ctx, wfo, woa, wg, b_gate, wout, ln1_g, ln1_b,
        wr, br, tri, upper)

    x1_l, h2_l, lpos_l, topw_l, bcnt_l = _lat(
        xl2d, mod, w_in_ext, cdft, qg, kvg, wqa, wqr, cos_pad, sin_pad, pdft_lat, cache_k, wfo, woa, wg,
        b_gate, wout, ln1_g, ln1_b, wr, br, tri, upper)

    blen = jnp.concatenate([bcnt_c[:, 0, :N_EXP], bcnt_l[:, 0, :N_EXP]], axis=0).astype(jnp.int32)
    blen = (blen + (blen & 1)) // 2
    lstart = jnp.cumsum(blen, axis=1) - blen
    count = jnp.sum(blen, axis=0)
    start = jnp.cumsum(count) - count
    gstart = start[None, :] + jnp.cumsum(blen, axis=0) - blen
    extra = (jnp.sum(blen, axis=1) - BLK_PAIRS).astype(jnp.int32)
    pieces = _piece_tables(lstart, blen, gstart) + (extra,)
    start = start.astype(jnp.int32)
    count = count.astype(jnp.int32)

    xs = _dispatch(*pieces, h2_c, h2_l, lpos_c, lpos_l)
    ys = _experts(start, count, xs,
                  w_gate_e, b_gate_e, w_up_e, b_up_e, w_down_e, b_down_e)

    y_c = _combine(*pieces, lpos_c, topw_c, x1_c, mod, ln2_g, ln2_b, ys, blk0=0,
                   n_seq=N_CTX_SEQ, seq_len=T_CTX, mod_row0=0, mod_row_step=0)
    y_l = _combine(*pieces, lpos_l, topw_l, x1_l, mod, ln2_g, ln2_b, ys, blk0=N_CTX_BLK,
                   n_seq=N_LAT_SEQ, seq_len=T_LAT, mod_row0=1, mod_row_step=1)
    return (y_c.reshape(N_CTX_SEQ, T_CTX, D), y_l.reshape(N_LAT_SEQ, T_LAT, D), new_ckv, new_krope)
```

```python
import functools
import math

import numpy as np
import jax
import jax.numpy as jnp
from jax import lax
from jax.experimental import pallas as pl
from jax.experimental.pallas import tpu as pltpu

D = 1024
N_CTX_SEQ, T_CTX = 32, 256
N_LAT_SEQ, T_LAT = 4, 1024
PAST = 256
N_CTX = N_CTX_SEQ * T_CTX
N_LAT = N_LAT_SEQ * T_LAT
N_TOK = N_CTX + N_LAT
FW = 512
FG = 128
N_FG = FW // FG
HEADS = 8
QL = 256
KVL = 128
NOPE = 64
ROPE = 32
VH = 64
N_EXP = 32
TOPK = 4
D_EXP = 1024
SWIGLU_LIMIT = 7.0
SWIGLU_ALPHA = 1.702
LN_EPS = 1e-5
RMS_EPS = 1e-6
DN_ALPHA = 2.0 ** 0.25
ATT_SCALE = float(NOPE + ROPE) ** -0.5
ROPE_THETA = 10000.0
GRID_W = 64

LANES = 128
TB = 256
PRE_ROWS = 512
MIX_ROWS_LAT = 512
SUB = 8
ROW_TILE = 256
TILE_PAIRS = ROW_TILE // 2
RING = 4
N_CTX_BLK = N_CTX // TB
N_BLK = N_TOK // TB
BLK_PAIRS = TB * TOPK // 2
PAD_PAIRS_MAX = N_EXP // 2
BLK_PAIRS_BUF = -(-(BLK_PAIRS + PAD_PAIRS_MAX) // (LANES // 2)) * (LANES // 2)
BLK_ROWS_BUF = 2 * BLK_PAIRS_BUF
NP_SORT = N_TOK * TOPK // 2
NP_MAX = NP_SORT + N_BLK * PAD_PAIRS_MAX
NP_PAD = NP_MAX + TILE_PAIRS
RUN_BITS = (TB // 2).bit_length()
EXTRA_BITS = PAD_PAIRS_MAX.bit_length()
U32 = jnp.uint32
VMEM_LIMIT = 56 * 1024 * 1024
NEG_BIG = -1e30

F32 = jnp.float32
BF16 = jnp.bfloat16
HIGHEST = lax.Precision.HIGHEST


def _dot(a, b):
    return jnp.dot(a, b, preferred_element_type=F32)


def _dot_nt(a, b):
    return lax.dot_general(a, b, (((1,), (1,)), ((), ())), preferred_element_type=F32)


def _params(sem):
    return pltpu.CompilerParams(dimension_semantics=sem, vmem_limit_bytes=VMEM_LIMIT)


def _full(shape):
    n = len(shape)
    return pl.BlockSpec(shape, lambda *_: (0,) * n, pipeline_mode=pl.Buffered(1))


def _norm_noaffine(x):
    mu = jnp.mean(x, axis=-1, keepdims=True)
    xc = x - mu
    var = jnp.mean(xc * xc, axis=-1, keepdims=True)
    return xc * lax.rsqrt(var + LN_EPS)


def _sigmoid(x):
    return 0.5 * jnp.tanh(0.5 * x) + 0.5


def _rms(x, g):
    return x * lax.rsqrt(jnp.mean(x * x, axis=-1, keepdims=True) + RMS_EPS) * g


def _mod_row(mod_ref, row0, step, seq):
    row = row0 if step == 0 else row0 + step * seq
    m = mod_ref[pl.ds(row, 1), :]
    return [m[:, i * D:(i + 1) * D] for i in range(6)]


def _prep_kernel(wqn_ref, wk_ref, wv_ref, wo_ref, qabs_ref, oabs_ref):
    for h in range(HEADS):
        qabs = lax.dot_general(wqn_ref[h], wk_ref[h], (((1,), (1,)), ((), ())),
                               preferred_element_type=F32, precision=HIGHEST)
        qabs_ref[:, h * KVL:(h + 1) * KVL] = qabs.astype(BF16)
        oabs = jnp.dot(wv_ref[h], wo_ref[h], preferred_element_type=F32, precision=HIGHEST)
        oabs_ref[h * KVL:(h + 1) * KVL, :] = oabs.astype(BF16)


def _prep(wqn, wk, wv, wo):
    return pl.pallas_call(
        _prep_kernel,
        out_shape=[jax.ShapeDtypeStruct((QL, HEADS * KVL), BF16),
                   jax.ShapeDtypeStruct((HEADS * KVL, D), BF16)],
        compiler_params=pltpu.CompilerParams(vmem_limit_bytes=VMEM_LIMIT),
        name="weight_prep",
    )(wqn, wk, wv, wo)


MOD_COLS = 1536


def _mod_kernel(c_ref, w_ref, b_ref, o_ref):
    c = c_ref[...]
    s = c * _sigmoid(c)
    s_hi = s.astype(BF16)
    s_lo = (s - s_hi.astype(F32)).astype(BF16)
    w = w_ref[...]
    w_hi = w.astype(BF16)
    w_lo = (w - w_hi.astype(F32)).astype(BF16)
    o_ref[...] = (_dot(s_hi, w_hi) + (_dot(s_hi, w_lo) + _dot(s_lo, w_hi))) + b_ref[...]


def _modulation(c_all, w_mod, b_mod):
    return pl.pallas_call(
        _mod_kernel,
        grid=(6 * D // MOD_COLS,),
        in_specs=[
            pl.BlockSpec((8, D), lambda i: (0, 0)),
            pl.BlockSpec((D, MOD_COLS), lambda i: (0, i)),
            pl.BlockSpec((1, MOD_COLS), lambda i: (0, i)),
        ],
        out_specs=pl.BlockSpec((8, MOD_COLS), lambda i: (0, i)),
        out_shape=jax.ShapeDtypeStruct((8, 6 * D), F32),
        compiler_params=_params(("arbitrary",)),
        name="modulation",
    )(c_all, w_mod, b_mod)


def _pre_kernel(*refs, rows, seq_len, seqs, rope, mod_row0, mod_row_step):
    if rope:
        (x_ref, mod_ref, win_ref, cdft_ref, qg_ref, kvg_ref, wqa_ref, wqr_ref, cos_ref, sin_ref,
         fcs_ref, qcat_ref, kcat_ref) = refs
    else:
        (x_ref, mod_ref, win_ref, cdft_ref, qg_ref, kvg_ref, wqa_ref, wqr_ref,
         fcs_ref, qcat_ref, kcat_ref, ckv_ref, krope_ref) = refs
    q_rows = rows // seqs
    b = pl.program_id(0) // (seq_len // q_rows)
    shift1, scale1 = _mod_row(mod_ref, mod_row0, mod_row_step, b)[:2]
    seq_rows = [slice(s * q_rows, (s + 1) * q_rows) for s in range(seqs)]

    x = x_ref[...]
    h1 = _norm_noaffine(x) * (1.0 + scale1) + shift1
    h1b = h1.astype(BF16)
    n_cols = 9 * LANES if rope else 8 * LANES
    proj = _dot(h1b, win_ref[:, :n_cols])

    f_b = proj[:, :FW].astype(BF16)
    for g in range(N_FG):
        r = _dot(f_b[:, g * FG:(g + 1) * FG], cdft_ref[...])
        for s, sl in enumerate(seq_rows):
            fcs_ref[s, 0, :, g * FG:(g + 1) * FG] = r[sl, :FG].astype(BF16)
            fcs_ref[s, 1, :, g * FG:(g + 1) * FG] = r[sl, FG:].astype(BF16)

    qn = _rms(proj[:, FW:FW + QL], qg_ref[...]).astype(BF16)
    ckv = _rms(proj[:, FW + QL:FW + QL + KVL], kvg_ref[...])
    kr = proj[:, 7 * LANES:8 * LANES]

    qa = _dot(qn, wqa_ref[...]) * ATT_SCALE
    if rope:
        cos = cos_ref[...]
        sin = sin_ref[...]
        qr2 = _dot(qn, wqr_ref[...])
        kr_keys = kr * cos + proj[:, 8 * LANES:9 * LANES] * sin
    else:
        qr2 = _dot(qn, wqr_ref[:, :HEADS * LANES])
        kr_keys = kr
    ckv_b = ckv.astype(BF16)
    kr_b = kr_keys.astype(BF16)
    for h in range(HEADS):
        qr_h = qr2[:, h * LANES:(h + 1) * LANES]
        if rope:
            qr_h = qr_h * cos + qr2[:, (HEADS + h) * LANES:(HEADS + h + 1) * LANES] * sin
        qa_h = qa[:, h * KVL:(h + 1) * KVL].astype(BF16)
        qr_h = (qr_h * ATT_SCALE).astype(BF16)
        for s, sl in enumerate(seq_rows):
            qcat_ref[s, h, :, :KVL] = qa_h[sl]
            qcat_ref[s, h, :, KVL:] = qr_h[sl]
    for s, sl in enumerate(seq_rows):
        kcat_ref[s, :, :KVL] = ckv_b[sl]
        kcat_ref[s, :, KVL:] = kr_b[sl]
        if not rope:
            ckv_ref[s, 0] = ckv[sl]
            krope_ref[s, 0] = kr[sl, :ROPE]


def _pre(x2d, mod, w_in_ext, cdft, qg, kvg, wqa, wqr, cos_pad, sin_pad, *, n_seq, seq_len, rope,
         mod_row0, mod_row_step):
    rows = PRE_ROWS
    seqs = max(1, rows // seq_len)
    q_rows = rows // seqs
    bps = seq_len // q_rows
    n_blk = n_seq * seq_len // rows
    seq_blk = lambda i: (i // bps, 0, i % bps, 0)
    in_specs = [
        pl.BlockSpec((rows, D), lambda i: (i, 0)),
        _full((8, 6 * D)),
        _full((D, 9 * LANES)),
        _full((FG, 2 * FG)),
        _full((1, QL)),
        _full((1, KVL)),
        _full((QL, HEADS * KVL)),
        _full((QL, 2 * HEADS * LANES)),
    ]
    args = [x2d, mod, w_in_ext, cdft, qg, kvg, wqa, wqr]
    out_specs = [
        pl.BlockSpec((seqs, 2, q_rows, FW), seq_blk),
        pl.BlockSpec((seqs, HEADS, q_rows, 2 * LANES), seq_blk),
        pl.BlockSpec((seqs, q_rows, 2 * LANES), lambda i: (i // bps, i % bps, 0)),
    ]
    out_shape = [
        jax.ShapeDtypeStruct((n_seq, 2, seq_len, FW), BF16),
        jax.ShapeDtypeStruct((n_seq, HEADS, seq_len, 2 * LANES), BF16),
        jax.ShapeDtypeStruct((n_seq, seq_len, 2 * LANES), BF16),
    ]
    if rope:
        in_specs += [pl.BlockSpec((q_rows, LANES), lambda i: (i % bps, 0)),
                     pl.BlockSpec((q_rows, LANES), lambda i: (i % bps, 0))]
        args += [cos_pad, sin_pad]
    else:
        out_specs += [pl.BlockSpec((seqs, 1, q_rows, KVL), seq_blk),
                      pl.BlockSpec((seqs, 1, q_rows, ROPE), seq_blk)]
        out_shape += [jax.ShapeDtypeStruct((n_seq, 1, seq_len, KVL), F32),
                      jax.ShapeDtypeStruct((n_seq, 1, seq_len, ROPE), F32)]
    kern = functools.partial(_pre_kernel, rows=rows, seq_len=seq_len, seqs=seqs, rope=rope,
                             mod_row0=mod_row0, mod_row_step=mod_row_step)
    return pl.pallas_call(
        kern, grid=(n_blk,), in_specs=in_specs, out_specs=out_specs, out_shape=out_shape,
        compiler_params=_params(("arbitrary",)),
        name="pre_lat" if rope else "pre_ctx",
    )(*args)


def _mix_kernel(*refs, rows, seq_len, seqs, cache, mod_row0, mod_row_step):
    if cache:
        (x_ref, mod_ref, fcs_ref, pdft_ref, qcat_ref, kcat_ref, cache_ref, wfo_ref, woa_ref,
         wg_ref, bg_ref, wout_ref, ln1g_ref, ln1b_ref, wr_ref, br_ref, tri_ref, upper_ref,
         x1_ref, h2_ref, lpos_ref, topw_ref, bcnt_ref) = refs
    else:
        (x_ref, mod_ref, fcs_ref, pdft_ref, qcat_ref, kcat_ref, wfo_ref, woa_ref,
         wg_ref, bg_ref, wout_ref, ln1g_ref, ln1b_ref, wr_ref, br_ref, tri_ref, upper_ref,
         x1_ref, h2_ref, lpos_ref, topw_ref, bcnt_ref) = refs
        cache_ref = None
    b = pl.program_id(0)
    shift1, scale1, gate1, shift2, scale2, _ = _mod_row(mod_ref, mod_row0, mod_row_step, b)

    x = x_ref[...]
    h1b = (_norm_noaffine(x) * (1.0 + scale1) + shift1).astype(BF16)

    mixed_rows, attn_rows = [], []
    for s in range(seqs):
        mixed_rows.append(_dot(pdft_ref[:, :seq_len], fcs_ref[s, 0])
                          + _dot(pdft_ref[:, seq_len:], fcs_ref[s, 1]))
        kc = kcat_ref[s]
        vals = kc[:, :KVL]
        if cache:
            kc2 = cache_ref[s]
            vals2 = kc2[:, :KVL]
        heads_out = []
        for h in range(HEADS):
            q = qcat_ref[s, h]
            s1 = _dot_nt(q, kc)
            m = jnp.max(s1, axis=-1, keepdims=True)
            if cache:
                s2 = _dot_nt(q, kc2)
                m = jnp.maximum(m, jnp.max(s2, axis=-1, keepdims=True))
            p1 = jnp.exp(s1 - m)
            l = jnp.sum(p1, axis=-1, keepdims=True)
            o = _dot(p1.astype(BF16), vals)
            if cache:
                p2 = jnp.exp(s2 - m)
                l = l + jnp.sum(p2, axis=-1, keepdims=True)
                o = o + _dot(p2.astype(BF16), vals2)
            heads_out.append((o / l).astype(BF16))
        attn_rows.append(jnp.concatenate(heads_out, axis=1))
    mixed = jnp.concatenate(mixed_rows, axis=0) if seqs > 1 else mixed_rows[0]
    attn = jnp.concatenate(attn_rows, axis=0) if seqs > 1 else attn_rows[0]
    f_out = _dot(mixed.astype(BF16), wfo_ref[...])
    m_out = _dot(attn, woa_ref[...])

    gates = _sigmoid(_dot(h1b, wg_ref[...]) + bg_ref[...])
    merged = gates[:, :D] * f_out + gates[:, D:] * m_out
    mix = _dot(merged.astype(BF16), wout_ref[...])

    x1 = _norm_noaffine(DN_ALPHA * x + gate1 * mix) * ln1g_ref[...] + ln1b_ref[...]
    x1_ref[...] = x1
    h2 = _norm_noaffine(x1) * (1.0 + scale2) + shift2
    h2_ref[...] = h2.astype(BF16)

    h_hi = h2.astype(BF16)
    h_lo = (h2 - h_hi.astype(F32)).astype(BF16)
    hi_part = _dot(h_hi, wr_ref[...])
    logits_all = (hi_part[:, :LANES] + hi_part[:, LANES:] + _dot(h_lo, wr_ref[:, :LANES])) + br_ref[...]
    lane = lax.broadcasted_iota(jnp.int32, (TB, LANES), 1).astype(F32)
    for r in range(rows // TB):
        blk = slice(r * TB, (r + 1) * TB)
        work = logits_all[blk]
        top_v, top_i = [], []
        for _ in range(TOPK):
            mk = jnp.max(work, axis=-1, keepdims=True)
            ik = jnp.min(jnp.where(work == mk, lane, float(LANES)), axis=-1, keepdims=True)
            work = jnp.where(lane == ik, -jnp.inf, work)
            top_v.append(mk)
            top_i.append(ik)
        exps = [jnp.exp(v - top_v[0]) for v in top_v]
        denom = exps[0] + exps[1] + exps[2] + exps[3]

        onehot = jnp.zeros((TB, LANES), F32)
        for ik in top_i:
            onehot = onehot + jnp.where(lane == ik, 1.0, 0.0)
        counts = jnp.sum(onehot, axis=0, keepdims=True)
        padded = counts + (counts - 2.0 * jnp.floor(0.5 * counts))
        lower = jnp.dot(jnp.broadcast_to(padded, (8, LANES)), upper_ref[...],
                        preferred_element_type=F32, precision=HIGHEST)[0:1, :]
        before = _dot(tri_ref[...], onehot.astype(BF16)) + lower
        lpos = jnp.zeros((TB, LANES), jnp.int32)
        topw = jnp.zeros((TB, LANES), F32)
        for k in range(TOPK):
            pos = jnp.sum(jnp.where(lane == top_i[k], before, 0.0), axis=-1, keepdims=True)
            lpos = jnp.where(lane == float(k), pos.astype(jnp.int32), lpos)
            topw = jnp.where(lane == float(k), exps[k] / denom, topw)
        lpos_ref[blk, :] = lpos
        topw_ref[blk, :] = topw
        bcnt_ref[r] = counts


def _mix(x2d, mod, fcs, pdft, qcat, kcat, cache_k, wfo, woa, wg, bg, wout, ln1g, ln1b, wr, br,
         tri, upper, *, rows, n_seq, seq_len, mod_row0, mod_row_step):
    cache = cache_k is not None
    seqs = max(1, rows // seq_len)
    q_rows = rows // seqs
    bps = seq_len // q_rows
    n_outer = n_seq // seqs
    in_specs = [
        pl.BlockSpec((rows, D), lambda b, j: (b * bps + j, 0)),
        _full((8, 6 * D)),
        pl.BlockSpec((seqs, 2, seq_len, FW), lambda b, j: (b, 0, 0, 0)),
        pl.BlockSpec((q_rows, 2 * seq_len), lambda b, j: (j, 0)),
        pl.BlockSpec((seqs, HEADS, q_rows, 2 * LANES), lambda b, j: (b, 0, j, 0)),
        pl.BlockSpec((seqs, seq_len, 2 * LANES), lambda b, j: (b, 0, 0)),
    ]
    args = [x2d, mod, fcs, pdft, qcat, kcat]
    if cache:
        in_specs.append(pl.BlockSpec((seqs, PAST, 2 * LANES), lambda b, j: (b, 0, 0)))
        args.append(cache_k)
    in_specs += [
        _full((FW, D)), _full((HEADS * KVL, D)), _full((D, 2 * D)), _full((1, 2 * D)),
        _full((D, D)), _full((1, D)), _full((1, D)), _full((D, 2 * LANES)), _full((1, LANES)),
        _full((TB, TB)), _full((LANES, LANES)),
    ]
    args += [wfo, woa, wg, bg, wout, ln1g, ln1b, wr, br, tri, upper]
    n_tok = n_seq * seq_len
    tok_spec = lambda w: pl.BlockSpec((rows, w), lambda b, j: (b * bps + j, 0))
    out_specs = [tok_spec(D), tok_spec(D), tok_spec(LANES), tok_spec(LANES),
                 pl.BlockSpec((rows // TB, 1, LANES), lambda b, j: (b * bps + j, 0, 0))]
    out_shape = [
        jax.ShapeDtypeStruct((n_tok, D), F32),
        jax.ShapeDtypeStruct((n_tok, D), BF16),
        jax.ShapeDtypeStruct((n_tok, LANES), jnp.int32),
        jax.ShapeDtypeStruct((n_tok, LANES), F32),
        jax.ShapeDtypeStruct((n_tok // TB, 1, LANES), F32),
    ]
    kern = functools.partial(_mix_kernel, rows=rows, seq_len=seq_len, seqs=seqs, cache=cache,
                             mod_row0=mod_row0, mod_row_step=mod_row_step)
    return pl.pallas_call(
        kern, grid=(n_outer, bps), in_specs=in_specs, out_specs=out_specs, out_shape=out_shape,
        compiler_params=_params(("arbitrary", "arbitrary")),
        name="mix_lat" if cache else "mix_ctx",
    )(*args)


CTX_ROWS = 512
CTX_SEQS = CTX_ROWS // T_CTX


def _ctx_kernel(x_ref, mod_ref, win_ref, cdft_ref, qg_ref, kvg_ref, wqa_ref, wqr_ref,
                pdft_ref, wfo_ref, woa_ref, wg_ref, bg_ref, wout_ref, ln1g_ref, ln1b_ref, wr_ref, br_ref,
                tri_ref, upper_ref,
                ckv_ref, krope_ref, x1_ref, h2_ref, lpos_ref, topw_ref, bcnt_ref,
                fcs_buf, qcat_buf, kcat_buf):
    stage = dict(rows=CTX_ROWS, seq_len=T_CTX, seqs=CTX_SEQS, mod_row0=0, mod_row_step=0)
    _pre_kernel(x_ref, mod_ref, win_ref, cdft_ref, qg_ref, kvg_ref, wqa_ref, wqr_ref,
                fcs_buf, qcat_buf, kcat_buf, ckv_ref, krope_ref, rope=False, **stage)
    _mix_kernel(x_ref, mod_ref, fcs_buf, pdft_ref, qcat_buf, kcat_buf, wfo_ref, woa_ref, wg_ref, bg_ref,
                wout_ref, ln1g_ref, ln1b_ref, wr_ref, br_ref, tri_ref, upper_ref,
                x1_ref, h2_ref, lpos_ref, topw_ref, bcnt_ref, cache=False, **stage)


def _ctx(x2d, mod, w_in_ext, cdft, qg, kvg, wqa, wqr, pdft, wfo, woa, wg, bg, wout, ln1g, ln1b, wr, br,
         tri, upper):
    rows, seqs = CTX_ROWS, CTX_SEQS
    tok = lambda w: pl.BlockSpec((rows, w), lambda i: (i, 0))
    seq4 = lambda a, b: pl.BlockSpec((seqs, 1, a, b), lambda i: (i, 0, 0, 0))
    in_specs = [
        tok(D), _full((8, 6 * D)), _full((D, 9 * LANES)), _full((FG, 2 * FG)), _full((1, QL)),
        _full((1, KVL)), _full((QL, HEADS * KVL)), _full((QL, 2 * HEADS * LANES)),
        _full((T_CTX, 2 * T_CTX)), _full((FW, D)), _full((HEADS * KVL, D)), _full((D, 2 * D)),
        _full((1, 2 * D)), _full((D, D)), _full((1, D)), _full((1, D)), _full((D, 2 * LANES)),
        _full((1, LANES)), _full((TB, TB)), _full((LANES, LANES)),
    ]
    out_specs = [seq4(T_CTX, KVL), seq4(T_CTX, ROPE), tok(D), tok(D), tok(LANES), tok(LANES),
                 pl.BlockSpec((rows // TB, 1, LANES), lambda i: (i, 0, 0))]
    out_shape = [
        jax.ShapeDtypeStruct((N_CTX_SEQ, 1, T_CTX, KVL), F32),
        jax.ShapeDtypeStruct((N_CTX_SEQ, 1, T_CTX, ROPE), F32),
        jax.ShapeDtypeStruct((N_CTX, D), F32),
        jax.ShapeDtypeStruct((N_CTX, D), BF16),
        jax.ShapeDtypeStruct((N_CTX, LANES), jnp.int32),
        jax.ShapeDtypeStruct((N_CTX, LANES), F32),
        jax.ShapeDtypeStruct((N_CTX // TB, 1, LANES), F32),
    ]
    return pl.pallas_call(
        _ctx_kernel, grid=(N_CTX // rows,), in_specs=in_specs, out_specs=out_specs, out_shape=out_shape,
        scratch_shapes=[pltpu.VMEM((seqs, 2, T_CTX, FW), BF16),
                        pltpu.VMEM((seqs, HEADS, T_CTX, 2 * LANES), BF16),
                        pltpu.VMEM((seqs, T_CTX, 2 * LANES), BF16)],
        compiler_params=_params(("arbitrary",)),
        name="ctx",
    )(x2d, mod, w_in_ext, cdft, qg, kvg, wqa, wqr, pdft, wfo, woa, wg, bg, wout, ln1g, ln1b, wr, br,
      tri, upper)


def _pair_tiles(ref, pair0, n_pairs):
    return ref.at[pl.ds(pl.multiple_of(pair0 * SUB, SUB), n_pairs * SUB), :]


def _piece_tables(lstart, blen, gstart):
    bits = jnp.arange(RUN_BITS, dtype=jnp.int32)
    n = blen[:, :, None]
    valid = (n >> bits) & 1
    done = (n >> (bits + 1)) << (bits + 1)
    rank = jnp.cumsum(valid, axis=1) - valid
    place = (valid[..., None] == 1) & (rank[..., None] == jnp.arange(N_EXP, dtype=jnp.int32))
    def table(first_row):
        rows = (first_row[:, :, None] + done) * SUB
        listed = jnp.sum(jnp.where(place, rows[..., None], 0), axis=1)
        return listed.reshape(-1).astype(jnp.int32)
    return jnp.sum(valid, axis=1).reshape(-1).astype(jnp.int32), table(lstart), table(gstart)


def _start_piece_copies(cnt_ref, loc_ref, glo_ref, blk, local_ref, global_ref, sem, to_global):
    for bit in range(RUN_BITS):
        size = (1 << bit) * SUB
        base = (blk * RUN_BITS + bit) * N_EXP

        def piece(t, carry):
            loc = local_ref.at[pl.ds(pl.multiple_of(loc_ref[base + t], SUB), size), :]
            glo = global_ref.at[pl.ds(pl.multiple_of(glo_ref[base + t], SUB), size), :]
            (pltpu.make_async_copy(loc, glo, sem) if to_global
             else pltpu.make_async_copy(glo, loc, sem)).start()
            return carry

        lax.fori_loop(0, cnt_ref[blk * RUN_BITS + bit], piece, 0)


def _wait_block_pieces(extra_ref, blk, vmem_ref, hbm_ref, sem, to_global):
    def wait(n_pairs):
        loc = _pair_tiles(vmem_ref, 0, n_pairs)
        glo = _pair_tiles(hbm_ref, 0, n_pairs)
        (pltpu.make_async_copy(loc, glo, sem) if to_global else pltpu.make_async_copy(glo, loc, sem)).wait()

    wait(BLK_PAIRS)
    extra = extra_ref[blk]
    for bit in range(EXTRA_BITS):
        @pl.when((extra & (1 << bit)) != 0)
        def _():
            wait(1 << bit)


def _load_pairs(ref, n_pairs):
    words = jnp.concatenate([ref[pl.ds(s, n_pairs, stride=SUB), :] for s in range(SUB)], axis=1)
    return pltpu.bitcast(words, BF16)


def _store_pairs(ref, rows, n_pairs):
    words = pltpu.bitcast(rows, U32)
    for s in range(SUB):
        ref[pl.ds(s, n_pairs, stride=SUB), :] = words[:, s * LANES:(s + 1) * LANES]


def _zero_fill_tail(zero_ref, hbm_ref, sem):
    zero_ref[...] = jnp.zeros(zero_ref.shape, U32)
    copies = [pltpu.make_async_copy(zero_ref, _pair_tiles(hbm_ref, p, TILE_PAIRS), sem)
              for p in range(NP_SORT, NP_PAD, TILE_PAIRS)]
    for cp in copies:
        cp.start()
    for cp in copies:
        cp.wait()


def _dispatch_kernel(cnt_ref, loc_ref, glo_ref, extra_ref, h2c_ref, h2l_ref, lpc_ref, lpl_ref, xs_ref,
                     buf, zbuf, sem, zsem):
    b = pl.program_id(0)
    slot = b & 1

    def start_runs(blk, s):
        _start_piece_copies(cnt_ref, loc_ref, glo_ref, blk, buf.at[s], xs_ref, sem.at[s], True)

    def wait_runs(blk, s):
        _wait_block_pieces(extra_ref, blk, buf.at[s], xs_ref, sem.at[s], True)

    @pl.when(b == 0)
    def _():
        _zero_fill_tail(zbuf, xs_ref, zsem)

    @pl.when(b >= 2)
    def _():
        wait_runs(b - 2, slot)

    def sort_block(h2_ref, lp_ref):
        row = lax.broadcasted_iota(jnp.int32, (BLK_ROWS_BUF, TB), 0)
        lp_t = lp_ref[...].T
        pick = jnp.zeros((BLK_ROWS_BUF, TB), F32)
        for k in range(TOPK):
            pick = pick + jnp.where(row == lp_t[k:k + 1, :], 1.0, 0.0)
        sorted_rows = _dot(pick.astype(BF16), h2_ref[...])
        _store_pairs(buf.at[slot], sorted_rows.astype(BF16), BLK_PAIRS_BUF)

    @pl.when(b < N_CTX_BLK)
    def _():
        sort_block(h2c_ref, lpc_ref)

    @pl.when(b >= N_CTX_BLK)
    def _():
        sort_block(h2l_ref, lpl_ref)

    start_runs(b, slot)

    @pl.when(b == N_BLK - 1)
    def _():
        wait_runs(b - 1, 1 - slot)
        wait_runs(b, slot)


def _dispatch(cnt, loc, glo, extra, h2c, h2l, lpc, lpl):
    ctx_idx = lambda i, *_: (jnp.minimum(i, N_CTX_BLK - 1), 0)
    lat_idx = lambda i, *_: (jnp.maximum(i - N_CTX_BLK, 0), 0)
    grid_spec = pltpu.PrefetchScalarGridSpec(
        num_scalar_prefetch=4,
        grid=(N_BLK,),
        in_specs=[
            pl.BlockSpec((TB, D), ctx_idx),
            pl.BlockSpec((TB, D), lat_idx),
            pl.BlockSpec((TB, LANES), ctx_idx),
            pl.BlockSpec((TB, LANES), lat_idx),
        ],
        out_specs=pl.BlockSpec(memory_space=pl.ANY),
        scratch_shapes=[pltpu.VMEM((2, BLK_PAIRS_BUF * SUB, LANES), U32),
                        pltpu.VMEM((TILE_PAIRS * SUB, LANES), U32),
                        pltpu.SemaphoreType.DMA((2,)),
                        pltpu.SemaphoreType.DMA(())],
    )
    return pl.pallas_call(
        _dispatch_kernel, grid_spec=grid_spec,
        out_shape=jax.ShapeDtypeStruct((NP_PAD * SUB, LANES), U32),
        compiler_params=_params(("arbitrary",)),
        name="dispatch",
    )(cnt, loc, glo, extra, h2c, h2l, lpc, lpl)


def _expert_kernel(start_ref, count_ref, xs_ref, wg_ref, bg_ref, wu_ref, bu_ref, wd_ref, bd_ref, ys_ref,
                   wbf, xbuf, ybuf, sem_in, sem_out):
    e = pl.program_id(0)
    pair0 = start_ref[e]

    def tiles_of(ex):
        return lax.shift_right_logical(count_ref[ex] + (TILE_PAIRS - 1), TILE_PAIRS.bit_length() - 1)

    n_tiles = tiles_of(e)

    def in_copy_of(ex, j, slot):
        return pltpu.make_async_copy(_pair_tiles(xs_ref, start_ref[ex] + j * TILE_PAIRS, TILE_PAIRS),
                                     xbuf.at[slot], sem_in.at[slot])

    def in_copy(j, slot):
        return in_copy_of(e, j, slot)

    def out_copy(j, slot):
        return pltpu.make_async_copy(ybuf.at[slot], _pair_tiles(ys_ref, pair0 + j * TILE_PAIRS, TILE_PAIRS),
                                     sem_out.at[slot])

    def start_first_tiles(ex):
        for j in range(RING - 1):
            @pl.when(tiles_of(ex) > j)
            def _():
                in_copy_of(ex, j, j).start()

    @pl.when(e == 0)
    def _():
        ybuf[...] = jnp.zeros(ybuf.shape, U32)
        _zero_fill_tail(ybuf.at[0], ys_ref, sem_out.at[0])
        start_first_tiles(0)

    @pl.when(n_tiles > 0)
    def _():
        wbf[0] = wg_ref[...].astype(BF16)
        wbf[1] = wu_ref[...].astype(BF16)
        wbf[2] = wd_ref[...].astype(BF16)

        def tile(j, carry):
            slot = j & (RING - 1)

            ahead = j + (RING - 1)

            @pl.when(ahead < n_tiles)
            def _():
                in_copy(ahead, ahead & (RING - 1)).start()

            in_copy(j, slot).wait()

            @pl.when(j >= RING)
            def _():
                out_copy(j - RING, slot).wait()

            def mlp(n_pairs):
                rows = pl.ds(0, n_pairs * SUB)
                xb = _load_pairs(xbuf.at[slot, rows], n_pairs)
                mine = pl.ds(e, 1)
                gt = jnp.minimum(_dot(xb, wbf[0]) + bg_ref[mine, :], SWIGLU_LIMIT)
                up = jnp.clip(_dot(xb, wbf[1]) + bu_ref[mine, :], -SWIGLU_LIMIT, SWIGLU_LIMIT)
                act = gt * _sigmoid(SWIGLU_ALPHA * gt) * (up + 1.0)
                y = _dot(act.astype(BF16), wbf[2]) + bd_ref[mine, :]
                _store_pairs(ybuf.at[slot, rows], y.astype(BF16), n_pairs)

            owned = count_ref[e] - j * TILE_PAIRS

            @pl.when(owned > TILE_PAIRS // 2)
            def _():
                mlp(TILE_PAIRS)

            @pl.when(owned <= TILE_PAIRS // 2)
            def _():
                mlp(TILE_PAIRS // 2)

            out_copy(j, slot).start()
            return carry

        lax.fori_loop(0, n_tiles, tile, 0)

        for back in range(RING, 0, -1):
            @pl.when(n_tiles >= back)
            def _():
                out_copy(n_tiles - back, (n_tiles - back) & (RING - 1)).wait()

    @pl.when(e + 1 < N_EXP)
    def _():
        start_first_tiles(e + 1)


def _experts(start, count, xs, wg, bg, wu, bu, wd, bd):
    w_idx = lambda e, *_: (0, e, 0, 0)
    b_spec = lambda n: pl.BlockSpec((None, N_EXP, n), lambda e, *_: (0, 0, 0), pipeline_mode=pl.Buffered(1))
    grid_spec = pltpu.PrefetchScalarGridSpec(
        num_scalar_prefetch=2,
        grid=(N_EXP,),
        in_specs=[
            pl.BlockSpec(memory_space=pl.ANY),
            pl.BlockSpec((None, None, D, D_EXP), w_idx), b_spec(D_EXP),
            pl.BlockSpec((None, None, D, D_EXP), w_idx), b_spec(D_EXP),
            pl.BlockSpec((None, None, D_EXP, D), w_idx), b_spec(D),
        ],
        out_specs=pl.BlockSpec(memory_space=pl.ANY),
        scratch_shapes=[pltpu.VMEM((3, D, D_EXP), BF16),
                        pltpu.VMEM((RING, TILE_PAIRS * SUB, LANES), U32),
                        pltpu.VMEM((RING, TILE_PAIRS * SUB, LANES), U32),
                        pltpu.SemaphoreType.DMA((RING,)),
                        pltpu.SemaphoreType.DMA((RING,))],
    )
    return pl.pallas_call(
        _expert_kernel, grid_spec=grid_spec,
        out_shape=jax.ShapeDtypeStruct((NP_PAD * SUB, LANES), U32),
        compiler_params=_params(("arbitrary",)),
        name="experts",
    )(start, count, xs, wg, bg, wu, bu, wd, bd)


def _combine_kernel(cnt_ref, loc_ref, glo_ref, extra_ref, lp_ref, topw_ref, x1_ref, mod_ref, g_ref, b_ref,
                    ys_ref, o_ref, buf, sem, *, blk0, n_blk, blocks_per_seq, mod_row0, mod_row_step):
    i = pl.program_id(0)
    slot = i & 1
    gate2 = _mod_row(mod_ref, mod_row0, mod_row_step, i // blocks_per_seq)[5]

    def start_runs(step, s):
        _start_piece_copies(cnt_ref, loc_ref, glo_ref, blk0 + step, buf.at[s], ys_ref, sem.at[s], False)

    @pl.when(i == 0)
    def _():
        buf[...] = jnp.zeros(buf.shape, U32)
        start_runs(0, 0)

    @pl.when(i + 1 < n_blk)
    def _():
        start_runs(i + 1, 1 - slot)

    _wait_block_pieces(extra_ref, blk0 + i, buf.at[slot], ys_ref, sem.at[slot], False)

    y_sorted = _load_pairs(buf.at[slot], BLK_PAIRS_BUF)
    col = lax.broadcasted_iota(jnp.int32, (TB, BLK_ROWS_BUF), 1)
    lp = lp_ref[...]
    topw = topw_ref[...]
    weights = jnp.zeros((TB, BLK_ROWS_BUF), F32)
    for k in range(TOPK):
        weights = jnp.where(col == lp[:, k:k + 1], topw[:, k:k + 1], weights)
    ffn = _dot(weights.astype(BF16), y_sorted)
    y = DN_ALPHA * x1_ref[...] + gate2 * ffn
    o_ref[...] = _norm_noaffine(y) * g_ref[...] + b_ref[...]


def _combine(cnt, loc, glo, extra, lpos, topw, x1, mod, ln2g, ln2b, ys, *, blk0, n_seq, seq_len,
             mod_row0, mod_row_step):
    bps = seq_len // TB
    n_blk = n_seq * bps
    tok = lambda i, *_: (i, 0)
    const = lambda i, *_: (0, 0)
    grid_spec = pltpu.PrefetchScalarGridSpec(
        num_scalar_prefetch=4,
        grid=(n_blk,),
        in_specs=[
            pl.BlockSpec((TB, LANES), tok),
            pl.BlockSpec((TB, LANES), tok),
            pl.BlockSpec((TB, D), tok),
            pl.BlockSpec((8, 6 * D), const),
            pl.BlockSpec((1, D), const),
            pl.BlockSpec((1, D), const),
            pl.BlockSpec(memory_space=pl.ANY),
        ],
        out_specs=pl.BlockSpec((TB, D), tok),
        scratch_shapes=[pltpu.VMEM((2, BLK_PAIRS_BUF * SUB, LANES), U32), pltpu.SemaphoreType.DMA((2,))],
    )
    kern = functools.partial(_combine_kernel, blk0=blk0, n_blk=n_blk, blocks_per_seq=bps,
                             mod_row0=mod_row0, mod_row_step=mod_row_step)
    return pl.pallas_call(
        kern, grid_spec=grid_spec,
        out_shape=jax.ShapeDtypeStruct((n_seq * seq_len, D), F32),
        compiler_params=_params(("arbitrary",)),
        name="combine_lat" if blk0 else "combine_ctx",
    )(cnt, loc, glo, extra, lpos, topw, x1, mod, ln2g, ln2b, ys)


def _combine_all_kernel(cnt_ref, loc_ref, glo_ref, extra_ref, lpc_ref, lpl_ref, twc_ref, twl_ref, x1c_ref,
                        x1l_ref, mod_ref, g_ref, b_ref, ys_ref, oc_ref, ol_ref, buf, sem):
    i = pl.program_id(0)
    slot = i & 1

    def start_runs(blk, s):
        _start_piece_copies(cnt_ref, loc_ref, glo_ref, blk, buf.at[s], ys_ref, sem.at[s], False)

    @pl.when(i == 0)
    def _():
        buf[...] = jnp.zeros(buf.shape, U32)
        start_runs(0, 0)

    @pl.when(i + 1 < N_BLK)
    def _():
        start_runs(i + 1, 1 - slot)

    _wait_block_pieces(extra_ref, i, buf.at[slot], ys_ref, sem.at[slot], False)

    def finish(lp_ref, topw_ref, x1_ref, o_ref, gate2):
        y_sorted = _load_pairs(buf.at[slot], BLK_PAIRS_BUF)
        col = lax.broadcasted_iota(jnp.int32, (TB, BLK_ROWS_BUF), 1)
        lp = lp_ref[...]
        topw = topw_ref[...]
        weights = jnp.zeros((TB, BLK_ROWS_BUF), F32)
        for k in range(TOPK):
            weights = jnp.where(col == lp[:, k:k + 1], topw[:, k:k + 1], weights)
        ffn = _dot(weights.astype(BF16), y_sorted)
        y = DN_ALPHA * x1_ref[...] + gate2 * ffn
        o_ref[...] = _norm_noaffine(y) * g_ref[...] + b_ref[...]

    @pl.when(i < N_CTX_BLK)
    def _():
        finish(lpc_ref, twc_ref, x1c_ref, oc_ref, _mod_row(mod_ref, 0, 0, 0)[5])

    @pl.when(i >= N_CTX_BLK)
    def _():
        seq = (i - N_CTX_BLK) // (T_LAT // TB)
        finish(lpl_ref, twl_ref, x1l_ref, ol_ref, _mod_row(mod_ref, 1, 1, seq)[5])


def _combine_all(cnt, loc, glo, extra, lpc, lpl, twc, twl, x1c, x1l, mod, ln2g, ln2b, ys):
    ctx_idx = lambda i, *_: (jnp.minimum(i, N_CTX_BLK - 1), 0)
    lat_idx = lambda i, *_: (jnp.maximum(i - N_CTX_BLK, 0), 0)
    const = lambda i, *_: (0, 0)
    grid_spec = pltpu.PrefetchScalarGridSpec(
        num_scalar_prefetch=4,
        grid=(N_BLK,),
        in_specs=[
            pl.BlockSpec((TB, LANES), ctx_idx), pl.BlockSpec((TB, LANES), lat_idx),
            pl.BlockSpec((TB, LANES), ctx_idx), pl.BlockSpec((TB, LANES), lat_idx),
            pl.BlockSpec((TB, D), ctx_idx), pl.BlockSpec((TB, D), lat_idx),
            pl.BlockSpec((8, 6 * D), const), pl.BlockSpec((1, D), const), pl.BlockSpec((1, D), const),
            pl.BlockSpec(memory_space=pl.ANY),
        ],
        out_specs=[pl.BlockSpec((TB, D), ctx_idx), pl.BlockSpec((TB, D), lat_idx)],
        scratch_shapes=[pltpu.VMEM((2, BLK_PAIRS_BUF * SUB, LANES), U32), pltpu.SemaphoreType.DMA((2,))],
    )
    return pl.pallas_call(
        _combine_all_kernel, grid_spec=grid_spec,
        out_shape=[jax.ShapeDtypeStruct((N_CTX, D), F32), jax.ShapeDtypeStruct((N_LAT, D), F32)],
        compiler_params=_params(("arbitrary",)),
        name="combine_all",
    )(cnt, loc, glo, extra, lpc, lpl, twc, twl, x1c, x1l, mod, ln2g, ln2b, ys)


def _dft_tables():
    def cs(n):
        k = np.arange(n, dtype=np.int64)
        ang = 2.0 * np.pi * ((k[:, None] * k[None, :]) % n).astype(np.float64) / n
        return np.cos(ang) / math.sqrt(n), np.sin(ang) / math.sqrt(n)

    c, s = cs(FG)
    cdft = np.concatenate([c, s], axis=1).astype(np.float32)
    pd = {}
    for t in (T_CTX, T_LAT):
        c, s = cs(t)
        pd[t] = np.concatenate([c, -s], axis=1).astype(np.float32)
    tri = np.tril(np.ones((TB, TB), np.float32), k=-1)
    upper = np.triu(np.ones((LANES, LANES), np.float32), k=1)
    return cdft, pd, tri, upper


_ROT_PERM = np.array(list(range(8, 16)) + list(range(0, 8)) + list(range(24, 32)) + list(range(16, 24)))
_ROT_SIGN = np.array([-1.0] * 8 + [1.0] * 8 + [-1.0] * 8 + [1.0] * 8, np.float32)


def _rope_tables():
    rows = T_LAT // GRID_W
    row = jnp.repeat(jnp.arange(rows, dtype=F32), GRID_W)
    col = jnp.tile(jnp.arange(GRID_W, dtype=F32), rows)
    axis_dim = ROPE // 2
    inv_freq = ROPE_THETA ** (-jnp.arange(0, axis_dim, 2, dtype=F32) / axis_dim)
    ang_r = row[:, None] * inv_freq[None, :]
    ang_c = col[:, None] * inv_freq[None, :]
    ang = jnp.concatenate([ang_r, ang_r, ang_c, ang_c], axis=-1)
    pad = ((0, 0), (0, LANES - ROPE))
    return jnp.pad(jnp.cos(ang), pad), jnp.pad(jnp.sin(ang), pad)


def kernel(x_prompt, x_sample, cache_ckv, cache_krope, c, c_ctx, w_mod, b_mod, w_in, q_norm_g, w_q_up,
           kv_norm_g, w_kv_up, w_fourier_o, w_mla_o, w_gate, b_gate, w_out, ln1_g, ln1_b, w_router,
           b_router, w_gate_e, b_gate_e, w_up_e, b_up_e, w_down_e, b_down_e, ln2_g, ln2_b):
    cdft_np, pdft_np, tri_np, upper_np = _dft_tables()
    upper = jnp.asarray(upper_np)
    cdft = jnp.asarray(cdft_np).astype(BF16)
    pdft_ctx = jnp.asarray(pdft_np[T_CTX]).astype(BF16)
    pdft_lat = jnp.asarray(pdft_np[T_LAT]).astype(BF16)
    tri = jnp.asarray(tri_np).astype(BF16)
    cos_pad, sin_pad = _rope_tables()

    w_in0 = w_in[0]
    kr_w = w_in0[:, FW + QL + KVL:]
    lane_pad = ((0, 0), (0, LANES - ROPE))
    w_in_ext = jnp.concatenate(
        [w_in0[:, :FW + QL + KVL], jnp.pad(kr_w, lane_pad),
         jnp.pad(kr_w[:, _ROT_PERM] * _ROT_SIGN, lane_pad)], axis=1).astype(BF16)
    wq3 = w_q_up[0].reshape(QL, HEADS, NOPE + ROPE)
    wq_nope = jnp.transpose(wq3[:, :, :NOPE], (1, 0, 2))
    wq_rope = wq3[:, :, NOPE:]
    head_pad = ((0, 0), (0, 0), (0, LANES - ROPE))
    wqr = jnp.concatenate(
        [jnp.pad(wq_rope, head_pad).reshape(QL, HEADS * LANES),
         jnp.pad(wq_rope[:, :, _ROT_PERM] * _ROT_SIGN, head_pad).reshape(QL, HEADS * LANES)],
        axis=1).astype(BF16)
    wkv3 = w_kv_up[0].reshape(KVL, HEADS, NOPE + VH)
    wk = jnp.transpose(wkv3[:, :, :NOPE], (1, 0, 2))
    wv = jnp.transpose(wkv3[:, :, NOPE:], (1, 0, 2))
    wo3 = w_mla_o[0].reshape(HEADS, VH, D)
    wqa, woa = _prep(wq_nope, wk, wv, wo3)

    c_all = jnp.concatenate([c_ctx[None, :], c, jnp.zeros((8 - 1 - N_LAT_SEQ, D), F32)], axis=0)
    mod = _modulation(c_all, w_mod[0], b_mod)

    qg = q_norm_g
    kvg = kv_norm_g
    wfo = w_fourier_o[0].astype(BF16)
    wg = w_gate[0].astype(BF16)
    wout = w_out[0].astype(BF16)
    wr_f32 = jnp.pad(w_router[0], ((0, 0), (0, LANES - N_EXP)))
    wr_hi = wr_f32.astype(BF16)
    wr = jnp.concatenate([wr_hi, (wr_f32 - wr_hi.astype(F32)).astype(BF16)], axis=1)
    br = jnp.pad(b_router, ((0, 0), (0, LANES - N_EXP)), constant_values=NEG_BIG)
    cache_k = jnp.concatenate(
        [cache_ckv[:, 0], jnp.pad(cache_krope[:, 0], ((0, 0), (0, 0), (0, LANES - ROPE)))],
        axis=-1).astype(BF16)

    xc2d = x_prompt.reshape(N_CTX, D)
    xl2d = x_sample.reshape(N_LAT, D)

    new_ckv, new_krope, x1_c, h2_c, lpos_c, topw_c, bcnt_c = _ctx(
        xc2d, mod, w_in_ext, cdft, qg, kvg, wqa, wqr, pdft_ctx, wfo, woa, wg, b_gate, wout, ln1_g, ln1_b,
        wr, br, tri, upper)

    fcs_l, qcat_l, kcat_l = _pre(
        xl2d, mod, w_in_ext, cdft, qg, kvg, wqa, wqr, cos_pad, sin_pad,
        n_seq=N_LAT_SEQ, seq_len=T_LAT, rope=True, mod_row0=1, mod_row_step=1)
    x1_l, h2_l, lpos_l, topw_l, bcnt_l = _mix(
        xl2d, mod, fcs_l, pdft_lat, qcat_l, kcat_l, cache_k, wfo, woa, wg, b_gate, wout, ln1_g,
        ln1_b, wr, br, tri, upper, rows=MIX_ROWS_LAT, n_seq=N_LAT_SEQ, seq_len=T_LAT, mod_row0=1,
        mod_row_step=1)

    blen = jnp.concatenate([bcnt_c[:, 0, :N_EXP], bcnt_l[:, 0, :N_EXP]], axis=0).astype(jnp.int32)
    blen = (blen + (blen & 1)) // 2
    lstart = jnp.cumsum(blen, axis=1) - blen
    count = jnp.sum(blen, axis=0)
    start = jnp.cumsum(count) - count
    gstart = start[None, :] + jnp.cumsum(blen, axis=0) - blen
    extra = (jnp.sum(blen, axis=1) - BLK_PAIRS).astype(jnp.int32)
    pieces = _piece_tables(lstart, blen, gstart) + (extra,)
    start = start.astype(jnp.int32)
    count = count.astype(jnp.int32)

    xs = _dispatch(*pieces, h2_c, h2_l, lpos_c, lpos_l)
    ys = _experts(start, count, xs,
                  w_gate_e, b_gate_e, w_up_e, b_up_e, w_down_e, b_down_e)

    y_c, y_l = _combine_all(*pieces, lpos_c, lpos_l, topw_c, topw_l, x1_c, x1_l, mod, ln2_g, ln2_b, ys)
    return (y_c.reshape(N_CTX_SEQ, T_CTX, D), y_l.reshape(N_LAT_SEQ, T_LAT, D), new_ckv, new_krope)
```

```python
import functools
import math

import numpy as np
import jax
import jax.numpy as jnp
from jax import lax
from jax.experimental import pallas as pl
from jax.experimental.pallas import tpu as pltpu

D = 1024
N_CTX_SEQ, T_CTX = 32, 256
N_LAT_SEQ, T_LAT = 4, 1024
PAST = 256
N_CTX = N_CTX_SEQ * T_CTX
N_LAT = N_LAT_SEQ * T_LAT
N_TOK = N_CTX + N_LAT
FW = 512
FG = 128
N_FG = FW // FG
HEADS = 8
QL = 256
KVL = 128
NOPE = 64
ROPE = 32
VH = 64
N_EXP = 32
TOPK = 4
D_EXP = 1024
SWIGLU_LIMIT = 7.0
SWIGLU_ALPHA = 1.702
LN_EPS = 1e-5
RMS_EPS = 1e-6
DN_ALPHA = 2.0 ** 0.25
ATT_SCALE = float(NOPE + ROPE) ** -0.5
ROPE_THETA = 10000.0
GRID_W = 64

LANES = 128
TB = 256
PRE_ROWS = 512
MIX_ROWS_LAT = 512
SUB = 8
ROW_TILE = 256
TILE_PAIRS = ROW_TILE // 2
RING = 4
N_CTX_BLK = N_CTX // TB
N_BLK = N_TOK // TB
BLK_PAIRS = TB * TOPK // 2
PAD_PAIRS_MAX = N_EXP // 2
BLK_PAIRS_BUF = -(-(BLK_PAIRS + PAD_PAIRS_MAX) // (LANES // 2)) * (LANES // 2)
BLK_ROWS_BUF = 2 * BLK_PAIRS_BUF
NP_SORT = N_TOK * TOPK // 2
NP_MAX = NP_SORT + N_BLK * PAD_PAIRS_MAX
NP_PAD = NP_MAX + TILE_PAIRS
RUN_BITS = (TB // 2).bit_length()
EXTRA_BITS = PAD_PAIRS_MAX.bit_length()
U32 = jnp.uint32
VMEM_LIMIT = 56 * 1024 * 1024
NEG_BIG = -1e30

F32 = jnp.float32
BF16 = jnp.bfloat16
HIGHEST = lax.Precision.HIGHEST


def _dot(a, b):
    return jnp.dot(a, b, preferred_element_type=F32)


def _dot_nt(a, b):
    return lax.dot_general(a, b, (((1,), (1,)), ((), ())), preferred_element_type=F32)


def _params(sem):
    return pltpu.CompilerParams(dimension_semantics=sem, vmem_limit_bytes=VMEM_LIMIT)


def _full(shape):
    n = len(shape)
    return pl.BlockSpec(shape, lambda *_: (0,) * n, pipeline_mode=pl.Buffered(1))


def _norm_noaffine(x):
    mu = jnp.mean(x, axis=-1, keepdims=True)
    xc = x - mu
    var = jnp.mean(xc * xc, axis=-1, keepdims=True)
    return xc * lax.rsqrt(var + LN_EPS)


def _sigmoid(x):
    return 0.5 * jnp.tanh(0.5 * x) + 0.5


def _rms(x, g):
    return x * lax.rsqrt(jnp.mean(x * x, axis=-1, keepdims=True) + RMS_EPS) * g


def _mod_row(mod_ref, row0, step, seq):
    row = row0 if step == 0 else row0 + step * seq
    m = mod_ref[pl.ds(row, 1), :]
    return [m[:, i * D:(i + 1) * D] for i in range(6)]


def _prep_kernel(wqn_ref, wk_ref, wv_ref, wo_ref, qabs_ref, oabs_ref):
    for h in range(HEADS):
        qabs = lax.dot_general(wqn_ref[h], wk_ref[h], (((1,), (1,)), ((), ())),
                               preferred_element_type=F32, precision=HIGHEST)
        qabs_ref[:, h * KVL:(h + 1) * KVL] = qabs.astype(BF16)
        oabs = jnp.dot(wv_ref[h], wo_ref[h], preferred_element_type=F32, precision=HIGHEST)
        oabs_ref[h * KVL:(h + 1) * KVL, :] = oabs.astype(BF16)


def _prep(wqn, wk, wv, wo):
    return pl.pallas_call(
        _prep_kernel,
        out_shape=[jax.ShapeDtypeStruct((QL, HEADS * KVL), BF16),
                   jax.ShapeDtypeStruct((HEADS * KVL, D), BF16)],
        compiler_params=pltpu.CompilerParams(vmem_limit_bytes=VMEM_LIMIT),
        name="weight_prep",
    )(wqn, wk, wv, wo)


MOD_COLS = 1536


def _mod_kernel(c_ref, w_ref, b_ref, o_ref):
    c = c_ref[...]
    s = c * _sigmoid(c)
    s_hi = s.astype(BF16)
    s_lo = (s - s_hi.astype(F32)).astype(BF16)
    w = w_ref[...]
    w_hi = w.astype(BF16)
    w_lo = (w - w_hi.astype(F32)).astype(BF16)
    o_ref[...] = (_dot(s_hi, w_hi) + (_dot(s_hi, w_lo) + _dot(s_lo, w_hi))) + b_ref[...]


def _modulation(c_all, w_mod, b_mod):
    return pl.pallas_call(
        _mod_kernel,
        grid=(6 * D // MOD_COLS,),
        in_specs=[
            pl.BlockSpec((8, D), lambda i: (0, 0)),
            pl.BlockSpec((D, MOD_COLS), lambda i: (0, i)),
            pl.BlockSpec((1, MOD_COLS), lambda i: (0, i)),
        ],
        out_specs=pl.BlockSpec((8, MOD_COLS), lambda i: (0, i)),
        out_shape=jax.ShapeDtypeStruct((8, 6 * D), F32),
        compiler_params=_params(("arbitrary",)),
        name="modulation",
    )(c_all, w_mod, b_mod)


def _pre_kernel(*refs, rows, seq_len, seqs, rope, mod_row0, mod_row_step, seq_is_axis0=False):
    if rope:
        (x_ref, mod_ref, win_ref, cdft_ref, qg_ref, kvg_ref, wqa_ref, wqr_ref, cos_ref, sin_ref,
         fcs_ref, qcat_ref, kcat_ref) = refs
    else:
        (x_ref, mod_ref, win_ref, cdft_ref, qg_ref, kvg_ref, wqa_ref, wqr_ref,
         fcs_ref, qcat_ref, kcat_ref, ckv_ref, krope_ref) = refs
    q_rows = rows // seqs
    b = pl.program_id(0) if seq_is_axis0 else pl.program_id(0) // (seq_len // q_rows)
    shift1, scale1 = _mod_row(mod_ref, mod_row0, mod_row_step, b)[:2]
    seq_rows = [slice(s * q_rows, (s + 1) * q_rows) for s in range(seqs)]

    x = x_ref[...]
    h1 = _norm_noaffine(x) * (1.0 + scale1) + shift1
    h1b = h1.astype(BF16)
    n_cols = 9 * LANES if rope else 8 * LANES
    proj = _dot(h1b, win_ref[:, :n_cols])

    f_b = proj[:, :FW].astype(BF16)
    for g in range(N_FG):
        r = _dot(f_b[:, g * FG:(g + 1) * FG], cdft_ref[...])
        for s, sl in enumerate(seq_rows):
            fcs_ref[s, 0, :, g * FG:(g + 1) * FG] = r[sl, :FG].astype(BF16)
            fcs_ref[s, 1, :, g * FG:(g + 1) * FG] = r[sl, FG:].astype(BF16)

    qn = _rms(proj[:, FW:FW + QL], qg_ref[...]).astype(BF16)
    ckv = _rms(proj[:, FW + QL:FW + QL + KVL], kvg_ref[...])
    kr = proj[:, 7 * LANES:8 * LANES]

    qa = _dot(qn, wqa_ref[...]) * ATT_SCALE
    if rope:
        cos = cos_ref[...]
        sin = sin_ref[...]
        qr2 = _dot(qn, wqr_ref[...])
        kr_keys = kr * cos + proj[:, 8 * LANES:9 * LANES] * sin
    else:
        qr2 = _dot(qn, wqr_ref[:, :HEADS * LANES])
        kr_keys = kr
    ckv_b = ckv.astype(BF16)
    kr_b = kr_keys.astype(BF16)
    for h in range(HEADS):
        qr_h = qr2[:, h * LANES:(h + 1) * LANES]
        if rope:
            qr_h = qr_h * cos + qr2[:, (HEADS + h) * LANES:(HEADS + h + 1) * LANES] * sin
        qa_h = qa[:, h * KVL:(h + 1) * KVL].astype(BF16)
        qr_h = (qr_h * ATT_SCALE).astype(BF16)
        for s, sl in enumerate(seq_rows):
            qcat_ref[s, h, :, :KVL] = qa_h[sl]
            qcat_ref[s, h, :, KVL:] = qr_h[sl]
    for s, sl in enumerate(seq_rows):
        kcat_ref[s, :, :KVL] = ckv_b[sl]
        kcat_ref[s, :, KVL:] = kr_b[sl]
        if not rope:
            ckv_ref[s, 0] = ckv[sl]
            krope_ref[s, 0] = kr[sl, :ROPE]


def _pre(x2d, mod, w_in_ext, cdft, qg, kvg, wqa, wqr, cos_pad, sin_pad, *, n_seq, seq_len, rope,
         mod_row0, mod_row_step):
    rows = PRE_ROWS
    seqs = max(1, rows // seq_len)
    q_rows = rows // seqs
    bps = seq_len // q_rows
    n_blk = n_seq * seq_len // rows
    seq_blk = lambda i: (i // bps, 0, i % bps, 0)
    in_specs = [
        pl.BlockSpec((rows, D), lambda i: (i, 0)),
        _full((8, 6 * D)),
        _full((D, 9 * LANES)),
        _full((FG, 2 * FG)),
        _full((1, QL)),
        _full((1, KVL)),
        _full((QL, HEADS * KVL)),
        _full((QL, 2 * HEADS * LANES)),
    ]
    args = [x2d, mod, w_in_ext, cdft, qg, kvg, wqa, wqr]
    out_specs = [
        pl.BlockSpec((seqs, 2, q_rows, FW), seq_blk),
        pl.BlockSpec((seqs, HEADS, q_rows, 2 * LANES), seq_blk),
        pl.BlockSpec((seqs, q_rows, 2 * LANES), lambda i: (i // bps, i % bps, 0)),
    ]
    out_shape = [
        jax.ShapeDtypeStruct((n_seq, 2, seq_len, FW), BF16),
        jax.ShapeDtypeStruct((n_seq, HEADS, seq_len, 2 * LANES), BF16),
        jax.ShapeDtypeStruct((n_seq, seq_len, 2 * LANES), BF16),
    ]
    if rope:
        in_specs += [pl.BlockSpec((q_rows, LANES), lambda i: (i % bps, 0)),
                     pl.BlockSpec((q_rows, LANES), lambda i: (i % bps, 0))]
        args += [cos_pad, sin_pad]
    else:
        out_specs += [pl.BlockSpec((seqs, 1, q_rows, KVL), seq_blk),
                      pl.BlockSpec((seqs, 1, q_rows, ROPE), seq_blk)]
        out_shape += [jax.ShapeDtypeStruct((n_seq, 1, seq_len, KVL), F32),
                      jax.ShapeDtypeStruct((n_seq, 1, seq_len, ROPE), F32)]
    kern = functools.partial(_pre_kernel, rows=rows, seq_len=seq_len, seqs=seqs, rope=rope,
                             mod_row0=mod_row0, mod_row_step=mod_row_step)
    return pl.pallas_call(
        kern, grid=(n_blk,), in_specs=in_specs, out_specs=out_specs, out_shape=out_shape,
        compiler_params=_params(("arbitrary",)),
        name="pre_lat" if rope else "pre_ctx",
    )(*args)


def _mix_kernel(*refs, rows, seq_len, seqs, cache, mod_row0, mod_row_step):
    if cache:
        (x_ref, mod_ref, fcs_ref, pdft_ref, qcat_ref, kcat_ref, cache_ref, wfo_ref, woa_ref,
         wg_ref, bg_ref, wout_ref, ln1g_ref, ln1b_ref, wr_ref, br_ref, tri_ref, upper_ref,
         x1_ref, h2_ref, lpos_ref, topw_ref, bcnt_ref) = refs
    else:
        (x_ref, mod_ref, fcs_ref, pdft_ref, qcat_ref, kcat_ref, wfo_ref, woa_ref,
         wg_ref, bg_ref, wout_ref, ln1g_ref, ln1b_ref, wr_ref, br_ref, tri_ref, upper_ref,
         x1_ref, h2_ref, lpos_ref, topw_ref, bcnt_ref) = refs
        cache_ref = None
    b = pl.program_id(0)
    shift1, scale1, gate1, shift2, scale2, _ = _mod_row(mod_ref, mod_row0, mod_row_step, b)

    x = x_ref[...]
    h1b = (_norm_noaffine(x) * (1.0 + scale1) + shift1).astype(BF16)

    mixed_rows, attn_rows = [], []
    for s in range(seqs):
        mixed_rows.append(_dot(pdft_ref[:, :seq_len], fcs_ref[s, 0])
                          + _dot(pdft_ref[:, seq_len:], fcs_ref[s, 1]))
        kc = kcat_ref[s]
        vals = kc[:, :KVL]
        if cache:
            kc2 = cache_ref[s]
            vals2 = kc2[:, :KVL]
        heads_out = []
        for h in range(HEADS):
            q = qcat_ref[s, h]
            s1 = _dot_nt(q, kc)
            m = jnp.max(s1, axis=-1, keepdims=True)
            if cache:
                s2 = _dot_nt(q, kc2)
                m = jnp.maximum(m, jnp.max(s2, axis=-1, keepdims=True))
            p1 = jnp.exp(s1 - m)
            l = jnp.sum(p1, axis=-1, keepdims=True)
            o = _dot(p1.astype(BF16), vals)
            if cache:
                p2 = jnp.exp(s2 - m)
                l = l + jnp.sum(p2, axis=-1, keepdims=True)
                o = o + _dot(p2.astype(BF16), vals2)
            heads_out.append((o / l).astype(BF16))
        attn_rows.append(jnp.concatenate(heads_out, axis=1))
    mixed = jnp.concatenate(mixed_rows, axis=0) if seqs > 1 else mixed_rows[0]
    attn = jnp.concatenate(attn_rows, axis=0) if seqs > 1 else attn_rows[0]
    f_out = _dot(mixed.astype(BF16), wfo_ref[...])
    m_out = _dot(attn, woa_ref[...])

    gates = _sigmoid(_dot(h1b, wg_ref[...]) + bg_ref[...])
    merged = gates[:, :D] * f_out + gates[:, D:] * m_out
    mix = _dot(merged.astype(BF16), wout_ref[...])

    x1 = _norm_noaffine(DN_ALPHA * x + gate1 * mix) * ln1g_ref[...] + ln1b_ref[...]
    x1_ref[...] = x1
    h2 = _norm_noaffine(x1) * (1.0 + scale2) + shift2
    h2_ref[...] = h2.astype(BF16)

    h_hi = h2.astype(BF16)
    h_lo = (h2 - h_hi.astype(F32)).astype(BF16)
    hi_part = _dot(h_hi, wr_ref[...])
    logits_all = (hi_part[:, :LANES] + hi_part[:, LANES:] + _dot(h_lo, wr_ref[:, :LANES])) + br_ref[...]
    lane = lax.broadcasted_iota(jnp.int32, (TB, LANES), 1).astype(F32)
    for r in range(rows // TB):
        blk = slice(r * TB, (r + 1) * TB)
        work = logits_all[blk]
        top_v, top_i = [], []
        for _ in range(TOPK):
            mk = jnp.max(work, axis=-1, keepdims=True)
            ik = jnp.min(jnp.where(work == mk, lane, float(LANES)), axis=-1, keepdims=True)
            work = jnp.where(lane == ik, -jnp.inf, work)
            top_v.append(mk)
            top_i.append(ik)
        exps = [jnp.exp(v - top_v[0]) for v in top_v]
        denom = exps[0] + exps[1] + exps[2] + exps[3]

        onehot = jnp.zeros((TB, LANES), F32)
        for ik in top_i:
            onehot = onehot + jnp.where(lane == ik, 1.0, 0.0)
        counts = jnp.sum(onehot, axis=0, keepdims=True)
        padded = counts + (counts - 2.0 * jnp.floor(0.5 * counts))
        lower = jnp.dot(jnp.broadcast_to(padded, (8, LANES)), upper_ref[...],
                        preferred_element_type=F32, precision=HIGHEST)[0:1, :]
        before = _dot(tri_ref[...], onehot.astype(BF16)) + lower
        lpos = jnp.zeros((TB, LANES), jnp.int32)
        topw = jnp.zeros((TB, LANES), F32)
        for k in range(TOPK):
            pos = jnp.sum(jnp.where(lane == top_i[k], before, 0.0), axis=-1, keepdims=True)
            lpos = jnp.where(lane == float(k), pos.astype(jnp.int32), lpos)
            topw = jnp.where(lane == float(k), exps[k] / denom, topw)
        lpos_ref[blk, :] = lpos
        topw_ref[blk, :] = topw
        bcnt_ref[r] = counts


def _mix(x2d, mod, fcs, pdft, qcat, kcat, cache_k, wfo, woa, wg, bg, wout, ln1g, ln1b, wr, br,
         tri, upper, *, rows, n_seq, seq_len, mod_row0, mod_row_step):
    cache = cache_k is not None
    seqs = max(1, rows // seq_len)
    q_rows = rows // seqs
    bps = seq_len // q_rows
    n_outer = n_seq // seqs
    in_specs = [
        pl.BlockSpec((rows, D), lambda b, j: (b * bps + j, 0)),
        _full((8, 6 * D)),
        pl.BlockSpec((seqs, 2, seq_len, FW), lambda b, j: (b, 0, 0, 0)),
        pl.BlockSpec((q_rows, 2 * seq_len), lambda b, j: (j, 0)),
        pl.BlockSpec((seqs, HEADS, q_rows, 2 * LANES), lambda b, j: (b, 0, j, 0)),
        pl.BlockSpec((seqs, seq_len, 2 * LANES), lambda b, j: (b, 0, 0)),
    ]
    args = [x2d, mod, fcs, pdft, qcat, kcat]
    if cache:
        in_specs.append(pl.BlockSpec((seqs, PAST, 2 * LANES), lambda b, j: (b, 0, 0)))
        args.append(cache_k)
    in_specs += [
        _full((FW, D)), _full((HEADS * KVL, D)), _full((D, 2 * D)), _full((1, 2 * D)),
        _full((D, D)), _full((1, D)), _full((1, D)), _full((D, 2 * LANES)), _full((1, LANES)),
        _full((TB, TB)), _full((LANES, LANES)),
    ]
    args += [wfo, woa, wg, bg, wout, ln1g, ln1b, wr, br, tri, upper]
    n_tok = n_seq * seq_len
    tok_spec = lambda w: pl.BlockSpec((rows, w), lambda b, j: (b * bps + j, 0))
    out_specs = [tok_spec(D), tok_spec(D), tok_spec(LANES), tok_spec(LANES),
                 pl.BlockSpec((rows // TB, 1, LANES), lambda b, j: (b * bps + j, 0, 0))]
    out_shape = [
        jax.ShapeDtypeStruct((n_tok, D), F32),
        jax.ShapeDtypeStruct((n_tok, D), BF16),
        jax.ShapeDtypeStruct((n_tok, LANES), jnp.int32),
        jax.ShapeDtypeStruct((n_tok, LANES), F32),
        jax.ShapeDtypeStruct((n_tok // TB, 1, LANES), F32),
    ]
    kern = functools.partial(_mix_kernel, rows=rows, seq_len=seq_len, seqs=seqs, cache=cache,
                             mod_row0=mod_row0, mod_row_step=mod_row_step)
    return pl.pallas_call(
        kern, grid=(n_outer, bps), in_specs=in_specs, out_specs=out_specs, out_shape=out_shape,
        compiler_params=_params(("arbitrary", "arbitrary")),
        name="mix_lat" if cache else "mix_ctx",
    )(*args)


CTX_ROWS = 512
CTX_SEQS = CTX_ROWS // T_CTX


def _ctx_kernel(x_ref, mod_ref, win_ref, cdft_ref, qg_ref, kvg_ref, wqa_ref, wqr_ref,
                pdft_ref, wfo_ref, woa_ref, wg_ref, bg_ref, wout_ref, ln1g_ref, ln1b_ref, wr_ref, br_ref,
                tri_ref, upper_ref,
                ckv_ref, krope_ref, x1_ref, h2_ref, lpos_ref, topw_ref, bcnt_ref,
                fcs_buf, qcat_buf, kcat_buf):
    stage = dict(rows=CTX_ROWS, seq_len=T_CTX, seqs=CTX_SEQS, mod_row0=0, mod_row_step=0)
    _pre_kernel(x_ref, mod_ref, win_ref, cdft_ref, qg_ref, kvg_ref, wqa_ref, wqr_ref,
                fcs_buf, qcat_buf, kcat_buf, ckv_ref, krope_ref, rope=False, **stage)
    _mix_kernel(x_ref, mod_ref, fcs_buf, pdft_ref, qcat_buf, kcat_buf, wfo_ref, woa_ref, wg_ref, bg_ref,
                wout_ref, ln1g_ref, ln1b_ref, wr_ref, br_ref, tri_ref, upper_ref,
                x1_ref, h2_ref, lpos_ref, topw_ref, bcnt_ref, cache=False, **stage)


def _ctx(x2d, mod, w_in_ext, cdft, qg, kvg, wqa, wqr, pdft, wfo, woa, wg, bg, wout, ln1g, ln1b, wr, br,
         tri, upper):
    rows, seqs = CTX_ROWS, CTX_SEQS
    tok = lambda w: pl.BlockSpec((rows, w), lambda i: (i, 0))
    seq4 = lambda a, b: pl.BlockSpec((seqs, 1, a, b), lambda i: (i, 0, 0, 0))
    in_specs = [
        tok(D), _full((8, 6 * D)), _full((D, 9 * LANES)), _full((FG, 2 * FG)), _full((1, QL)),
        _full((1, KVL)), _full((QL, HEADS * KVL)), _full((QL, 2 * HEADS * LANES)),
        _full((T_CTX, 2 * T_CTX)), _full((FW, D)), _full((HEADS * KVL, D)), _full((D, 2 * D)),
        _full((1, 2 * D)), _full((D, D)), _full((1, D)), _full((1, D)), _full((D, 2 * LANES)),
        _full((1, LANES)), _full((TB, TB)), _full((LANES, LANES)),
    ]
    out_specs = [seq4(T_CTX, KVL), seq4(T_CTX, ROPE), tok(D), tok(D), tok(LANES), tok(LANES),
                 pl.BlockSpec((rows // TB, 1, LANES), lambda i: (i, 0, 0))]
    out_shape = [
        jax.ShapeDtypeStruct((N_CTX_SEQ, 1, T_CTX, KVL), F32),
        jax.ShapeDtypeStruct((N_CTX_SEQ, 1, T_CTX, ROPE), F32),
        jax.ShapeDtypeStruct((N_CTX, D), F32),
        jax.ShapeDtypeStruct((N_CTX, D), BF16),
        jax.ShapeDtypeStruct((N_CTX, LANES), jnp.int32),
        jax.ShapeDtypeStruct((N_CTX, LANES), F32),
        jax.ShapeDtypeStruct((N_CTX // TB, 1, LANES), F32),
    ]
    return pl.pallas_call(
        _ctx_kernel, grid=(N_CTX // rows,), in_specs=in_specs, out_specs=out_specs, out_shape=out_shape,
        scratch_shapes=[pltpu.VMEM((seqs, 2, T_CTX, FW), BF16),
                        pltpu.VMEM((seqs, HEADS, T_CTX, 2 * LANES), BF16),
                        pltpu.VMEM((seqs, T_CTX, 2 * LANES), BF16)],
        compiler_params=_params(("arbitrary",)),
        name="ctx",
    )(x2d, mod, w_in_ext, cdft, qg, kvg, wqa, wqr, pdft, wfo, woa, wg, bg, wout, ln1g, ln1b, wr, br,
      tri, upper)


LAT_ROWS = 512
LAT_VMEM_LIMIT = 60 * 1024 * 1024
LAT_BLOCKS = T_LAT // LAT_ROWS


def _lat_kernel(x_ref, mod_ref, win_ref, cdft_ref, qg_ref, kvg_ref, wqa_ref, wqr_ref, cos_ref, sin_ref,
                pdft_ref, cache_ref, wfo_ref, woa_ref, wg_ref, bg_ref, wout_ref, ln1g_ref, ln1b_ref,
                wr_ref, br_ref, tri_ref, upper_ref,
                x1_ref, h2_ref, lpos_ref, topw_ref, bcnt_ref,
                fcs_buf, qcat_buf, kcat_buf):
    p = pl.program_id(1)
    stage = dict(rows=LAT_ROWS, seq_len=T_LAT, seqs=1, mod_row0=1, mod_row_step=1)

    @pl.when(p < LAT_BLOCKS)
    def _():
        r = pl.ds(pl.multiple_of(p * LAT_ROWS, LAT_ROWS), LAT_ROWS)
        _pre_kernel(x_ref, mod_ref, win_ref, cdft_ref, qg_ref, kvg_ref, wqa_ref, wqr_ref, cos_ref, sin_ref,
                    fcs_buf.at[:, :, r, :], qcat_buf.at[:, :, r, :], kcat_buf.at[:, r, :],
                    rope=True, seq_is_axis0=True, **stage)

    @pl.when(p >= LAT_BLOCKS)
    def _():
        r = pl.ds(pl.multiple_of((p - LAT_BLOCKS) * LAT_ROWS, LAT_ROWS), LAT_ROWS)
        _mix_kernel(x_ref, mod_ref, fcs_buf, pdft_ref, qcat_buf.at[:, :, r, :], kcat_buf, cache_ref,
                    wfo_ref, woa_ref, wg_ref, bg_ref, wout_ref, ln1g_ref, ln1b_ref, wr_ref, br_ref,
                    tri_ref, upper_ref, x1_ref, h2_ref, lpos_ref, topw_ref, bcnt_ref, cache=True, **stage)


def _lat(x2d, mod, w_in_ext, cdft, qg, kvg, wqa, wqr, cos_pad, sin_pad, pdft, cache_k, wfo, woa, wg, bg,
         wout, ln1g, ln1b, wr, br, tri, upper):
    rows, nb = LAT_ROWS, LAT_BLOCKS
    row_blk = lambda b, p: b * nb + p % nb
    out_blk = lambda b, p: b * nb + jnp.maximum(p - nb, 0)
    tok_in = pl.BlockSpec((rows, D), lambda b, p: (row_blk(b, p), 0))
    tok_out = lambda w: pl.BlockSpec((rows, w), lambda b, p: (out_blk(b, p), 0))
    in_specs = [
        tok_in, _full((8, 6 * D)), _full((D, 9 * LANES)), _full((FG, 2 * FG)), _full((1, QL)),
        _full((1, KVL)), _full((QL, HEADS * KVL)), _full((QL, 2 * HEADS * LANES)),
        pl.BlockSpec((rows, LANES), lambda b, p: (p % nb, 0)),
        pl.BlockSpec((rows, LANES), lambda b, p: (p % nb, 0)),
        pl.BlockSpec((rows, 2 * T_LAT), lambda b, p: (jnp.maximum(p - nb, 0), 0)),
        pl.BlockSpec((1, PAST, 2 * LANES), lambda b, p: (b, 0, 0)),
        _full((FW, D)), _full((HEADS * KVL, D)), _full((D, 2 * D)), _full((1, 2 * D)),
        _full((D, D)), _full((1, D)), _full((1, D)), _full((D, 2 * LANES)), _full((1, LANES)),
        _full((TB, TB)), _full((LANES, LANES)),
    ]
    out_specs = [tok_out(D), tok_out(D), tok_out(LANES), tok_out(LANES),
                 pl.BlockSpec((rows // TB, 1, LANES), lambda b, p: (out_blk(b, p), 0, 0))]
    out_shape = [
        jax.ShapeDtypeStruct((N_LAT, D), F32),
        jax.ShapeDtypeStruct((N_LAT, D), BF16),
        jax.ShapeDtypeStruct((N_LAT, LANES), jnp.int32),
        jax.ShapeDtypeStruct((N_LAT, LANES), F32),
        jax.ShapeDtypeStruct((N_LAT // TB, 1, LANES), F32),
    ]
    return pl.pallas_call(
        _lat_kernel, grid=(N_LAT_SEQ, 2 * nb), in_specs=in_specs, out_specs=out_specs, out_shape=out_shape,
        scratch_shapes=[pltpu.VMEM((1, 2, T_LAT, FW), BF16),
                        pltpu.VMEM((1, HEADS, T_LAT, 2 * LANES), BF16),
                        pltpu.VMEM((1, T_LAT, 2 * LANES), BF16)],
        compiler_params=pltpu.CompilerParams(dimension_semantics=("arbitrary", "arbitrary"),
                                             vmem_limit_bytes=LAT_VMEM_LIMIT),
        name="lat",
    )(x2d, mod, w_in_ext, cdft, qg, kvg, wqa, wqr, cos_pad, sin_pad, pdft, cache_k, wfo, woa, wg, bg,
      wout, ln1g, ln1b, wr, br, tri, upper)


def _pair_tiles(ref, pair0, n_pairs):
    return ref.at[pl.ds(pl.multiple_of(pair0 * SUB, SUB), n_pairs * SUB), :]


def _piece_tables(lstart, blen, gstart):
    bits = jnp.arange(RUN_BITS, dtype=jnp.int32)
    n = blen[:, :, None]
    valid = (n >> bits) & 1
    done = (n >> (bits + 1)) << (bits + 1)
    rank = jnp.cumsum(valid, axis=1) - valid
    place = (valid[..., None] == 1) & (rank[..., None] == jnp.arange(N_EXP, dtype=jnp.int32))
    def table(first_row):
        rows = (first_row[:, :, None] + done) * SUB
        listed = jnp.sum(jnp.where(place, rows[..., None], 0), axis=1)
        return listed.reshape(-1).astype(jnp.int32)
    return jnp.sum(valid, axis=1).reshape(-1).astype(jnp.int32), table(lstart), table(gstart)


def _start_piece_copies(cnt_ref, loc_ref, glo_ref, blk, local_ref, global_ref, sem, to_global):
    for bit in range(RUN_BITS):
        size = (1 << bit) * SUB
        base = (blk * RUN_BITS + bit) * N_EXP

        def piece(t, carry):
            loc = local_ref.at[pl.ds(pl.multiple_of(loc_ref[base + t], SUB), size), :]
            glo = global_ref.at[pl.ds(pl.multiple_of(glo_ref[base + t], SUB), size), :]
            (pltpu.make_async_copy(loc, glo, sem) if to_global
             else pltpu.make_async_copy(glo, loc, sem)).start()
            return carry

        lax.fori_loop(0, cnt_ref[blk * RUN_BITS + bit], piece, 0)


def _wait_block_pieces(extra_ref, blk, vmem_ref, hbm_ref, sem, to_global):
    def wait(n_pairs):
        loc = _pair_tiles(vmem_ref, 0, n_pairs)
        glo = _pair_tiles(hbm_ref, 0, n_pairs)
        (pltpu.make_async_copy(loc, glo, sem) if to_global else pltpu.make_async_copy(glo, loc, sem)).wait()

    wait(BLK_PAIRS)
    extra = extra_ref[blk]
    for bit in range(EXTRA_BITS):
        @pl.when((extra & (1 << bit)) != 0)
        def _():
            wait(1 << bit)


def _load_pairs(ref, n_pairs):
    words = jnp.concatenate([ref[pl.ds(s, n_pairs, stride=SUB), :] for s in range(SUB)], axis=1)
    return pltpu.bitcast(words, BF16)


def _store_pairs(ref, rows, n_pairs):
    words = pltpu.bitcast(rows, U32)
    for s in range(SUB):
        ref[pl.ds(s, n_pairs, stride=SUB), :] = words[:, s * LANES:(s + 1) * LANES]


def _zero_fill_tail(zero_ref, hbm_ref, sem):
    zero_ref[...] = jnp.zeros(zero_ref.shape, U32)
    copies = [pltpu.make_async_copy(zero_ref, _pair_tiles(hbm_ref, p, TILE_PAIRS), sem)
              for p in range(NP_SORT, NP_PAD, TILE_PAIRS)]
    for cp in copies:
        cp.start()
    for cp in copies:
        cp.wait()


def _dispatch_kernel(cnt_ref, loc_ref, glo_ref, extra_ref, h2c_ref, h2l_ref, lpc_ref, lpl_ref, xs_ref,
                     buf, zbuf, sem, zsem):
    b = pl.program_id(0)
    slot = b & 1

    def start_runs(blk, s):
        _start_piece_copies(cnt_ref, loc_ref, glo_ref, blk, buf.at[s], xs_ref, sem.at[s], True)

    def wait_runs(blk, s):
        _wait_block_pieces(extra_ref, blk, buf.at[s], xs_ref, sem.at[s], True)

    @pl.when(b == 0)
    def _():
        _zero_fill_tail(zbuf, xs_ref, zsem)

    @pl.when(b >= 2)
    def _():
        wait_runs(b - 2, slot)

    def sort_block(h2_ref, lp_ref):
        row = lax.broadcasted_iota(jnp.int32, (BLK_ROWS_BUF, TB), 0)
        lp_t = lp_ref[...].T
        pick = jnp.zeros((BLK_ROWS_BUF, TB), F32)
        for k in range(TOPK):
            pick = pick + jnp.where(row == lp_t[k:k + 1, :], 1.0, 0.0)
        sorted_rows = _dot(pick.astype(BF16), h2_ref[...])
        _store_pairs(buf.at[slot], sorted_rows.astype(BF16), BLK_PAIRS_BUF)

    @pl.when(b < N_CTX_BLK)
    def _():
        sort_block(h2c_ref, lpc_ref)

    @pl.when(b >= N_CTX_BLK)
    def _():
        sort_block(h2l_ref, lpl_ref)

    start_runs(b, slot)

    @pl.when(b == N_BLK - 1)
    def _():
        wait_runs(b - 1, 1 - slot)
        wait_runs(b, slot)


def _dispatch(cnt, loc, glo, extra, h2c, h2l, lpc, lpl):
    ctx_idx = lambda i, *_: (jnp.minimum(i, N_CTX_BLK - 1), 0)
    lat_idx = lambda i, *_: (jnp.maximum(i - N_CTX_BLK, 0), 0)
    grid_spec = pltpu.PrefetchScalarGridSpec(
        num_scalar_prefetch=4,
        grid=(N_BLK,),
        in_specs=[
            pl.BlockSpec((TB, D), ctx_idx),
            pl.BlockSpec((TB, D), lat_idx),
            pl.BlockSpec((TB, LANES), ctx_idx),
            pl.BlockSpec((TB, LANES), lat_idx),
        ],
        out_specs=pl.BlockSpec(memory_space=pl.ANY),
        scratch_shapes=[pltpu.VMEM((2, BLK_PAIRS_BUF * SUB, LANES), U32),
                        pltpu.VMEM((TILE_PAIRS * SUB, LANES), U32),
                        pltpu.SemaphoreType.DMA((2,)),
                        pltpu.SemaphoreType.DMA(())],
    )
    return pl.pallas_call(
        _dispatch_kernel, grid_spec=grid_spec,
        out_shape=jax.ShapeDtypeStruct((NP_PAD * SUB, LANES), U32),
        compiler_params=_params(("arbitrary",)),
        name="dispatch",
    )(cnt, loc, glo, extra, h2c, h2l, lpc, lpl)


def _expert_kernel(start_ref, count_ref, xs_ref, wg_ref, bg_ref, wu_ref, bu_ref, wd_ref, bd_ref, ys_ref,
                   wbf, xbuf, ybuf, sem_in, sem_out):
    e = pl.program_id(0)
    pair0 = start_ref[e]

    def tiles_of(ex):
        return lax.shift_right_logical(count_ref[ex] + (TILE_PAIRS - 1), TILE_PAIRS.bit_length() - 1)

    n_tiles = tiles_of(e)

    def in_copy_of(ex, j, slot):
        return pltpu.make_async_copy(_pair_tiles(xs_ref, start_ref[ex] + j * TILE_PAIRS, TILE_PAIRS),
                                     xbuf.at[slot], sem_in.at[slot])

    def in_copy(j, slot):
        return in_copy_of(e, j, slot)

    def out_copy(j, slot):
        return pltpu.make_async_copy(ybuf.at[slot], _pair_tiles(ys_ref, pair0 + j * TILE_PAIRS, TILE_PAIRS),
                                     sem_out.at[slot])

    def start_first_tiles(ex):
        for j in range(RING - 1):
            @pl.when(tiles_of(ex) > j)
            def _():
                in_copy_of(ex, j, j).start()

    @pl.when(e == 0)
    def _():
        ybuf[...] = jnp.zeros(ybuf.shape, U32)
        _zero_fill_tail(ybuf.at[0], ys_ref, sem_out.at[0])
        start_first_tiles(0)

    @pl.when(n_tiles > 0)
    def _():
        wbf[0] = wg_ref[...].astype(BF16)
        wbf[1] = wu_ref[...].astype(BF16)
        wbf[2] = wd_ref[...].astype(BF16)

        def tile(j, carry):
            slot = j & (RING - 1)

            ahead = j + (RING - 1)

            @pl.when(ahead < n_tiles)
            def _():
                in_copy(ahead, ahead & (RING - 1)).start()

            in_copy(j, slot).wait()

            @pl.when(j >= RING)
            def _():
                out_copy(j - RING, slot).wait()

            def mlp(n_pairs):
                rows = pl.ds(0, n_pairs * SUB)
                xb = _load_pairs(xbuf.at[slot, rows], n_pairs)
                mine = pl.ds(e, 1)
                gt = jnp.minimum(_dot(xb, wbf[0]) + bg_ref[mine, :], SWIGLU_LIMIT)
                up = jnp.clip(_dot(xb, wbf[1]) + bu_ref[mine, :], -SWIGLU_LIMIT, SWIGLU_LIMIT)
                act = gt * _sigmoid(SWIGLU_ALPHA * gt) * (up + 1.0)
                y = _dot(act.astype(BF16), wbf[2]) + bd_ref[mine, :]
                _store_pairs(ybuf.at[slot, rows], y.astype(BF16), n_pairs)

            owned = count_ref[e] - j * TILE_PAIRS

            @pl.when(owned > TILE_PAIRS // 2)
            def _():
                mlp(TILE_PAIRS)

            @pl.when(owned <= TILE_PAIRS // 2)
            def _():
                mlp(TILE_PAIRS // 2)

            out_copy(j, slot).start()
            return carry

        lax.fori_loop(0, n_tiles, tile, 0)

        for back in range(RING, 0, -1):
            @pl.when(n_tiles >= back)
            def _():
                out_copy(n_tiles - back, (n_tiles - back) & (RING - 1)).wait()

    @pl.when(e + 1 < N_EXP)
    def _():
        start_first_tiles(e + 1)


def _experts(start, count, xs, wg, bg, wu, bu, wd, bd):
    w_idx = lambda e, *_: (0, e, 0, 0)
    b_spec = lambda n: pl.BlockSpec((None, N_EXP, n), lambda e, *_: (0, 0, 0), pipeline_mode=pl.Buffered(1))
    grid_spec = pltpu.PrefetchScalarGridSpec(
        num_scalar_prefetch=2,
        grid=(N_EXP,),
        in_specs=[
            pl.BlockSpec(memory_space=pl.ANY),
            pl.BlockSpec((None, None, D, D_EXP), w_idx), b_spec(D_EXP),
            pl.BlockSpec((None, None, D, D_EXP), w_idx), b_spec(D_EXP),
            pl.BlockSpec((None, None, D_EXP, D), w_idx), b_spec(D),
        ],
        out_specs=pl.BlockSpec(memory_space=pl.ANY),
        scratch_shapes=[pltpu.VMEM((3, D, D_EXP), BF16),
                        pltpu.VMEM((RING, TILE_PAIRS * SUB, LANES), U32),
                        pltpu.VMEM((RING, TILE_PAIRS * SUB, LANES), U32),
                        pltpu.SemaphoreType.DMA((RING,)),
                        pltpu.SemaphoreType.DMA((RING,))],
    )
    return pl.pallas_call(
        _expert_kernel, grid_spec=grid_spec,
        out_shape=jax.ShapeDtypeStruct((NP_PAD * SUB, LANES), U32),
        compiler_params=_params(("arbitrary",)),
        name="experts",
    )(start, count, xs, wg, bg, wu, bu, wd, bd)


def _combine_kernel(cnt_ref, loc_ref, glo_ref, extra_ref, lp_ref, topw_ref, x1_ref, mod_ref, g_ref, b_ref,
                    ys_ref, o_ref, buf, sem, *, blk0, n_blk, blocks_per_seq, mod_row0, mod_row_step):
    i = pl.program_id(0)
    slot = i & 1
    gate2 = _mod_row(mod_ref, mod_row0, mod_row_step, i // blocks_per_seq)[5]

    def start_runs(step, s):
        _start_piece_copies(cnt_ref, loc_ref, glo_ref, blk0 + step, buf.at[s], ys_ref, sem.at[s], False)

    @pl.when(i == 0)
    def _():
        buf[...] = jnp.zeros(buf.shape, U32)
        start_runs(0, 0)

    @pl.when(i + 1 < n_blk)
    def _():
        start_runs(i + 1, 1 - slot)

    _wait_block_pieces(extra_ref, blk0 + i, buf.at[slot], ys_ref, sem.at[slot], False)

    y_sorted = _load_pairs(buf.at[slot], BLK_PAIRS_BUF)
    col = lax.broadcasted_iota(jnp.int32, (TB, BLK_ROWS_BUF), 1)
    lp = lp_ref[...]
    topw = topw_ref[...]
    weights = jnp.zeros((TB, BLK_ROWS_BUF), F32)
    for k in range(TOPK):
        weights = jnp.where(col == lp[:, k:k + 1], topw[:, k:k + 1], weights)
    ffn = _dot(weights.astype(BF16), y_sorted)
    y = DN_ALPHA * x1_ref[...] + gate2 * ffn
    o_ref[...] = _norm_noaffine(y) * g_ref[...] + b_ref[...]


def _combine(cnt, loc, glo, extra, lpos, topw, x1, mod, ln2g, ln2b, ys, *, blk0, n_seq, seq_len,
             mod_row0, mod_row_step):
    bps = seq_len // TB
    n_blk = n_seq * bps
    tok = lambda i, *_: (i, 0)
    const = lambda i, *_: (0, 0)
    grid_spec = pltpu.PrefetchScalarGridSpec(
        num_scalar_prefetch=4,
        grid=(n_blk,),
        in_specs=[
            pl.BlockSpec((TB, LANES), tok),
            pl.BlockSpec((TB, LANES), tok),
            pl.BlockSpec((TB, D), tok),
            pl.BlockSpec((8, 6 * D), const),
            pl.BlockSpec((1, D), const),
            pl.BlockSpec((1, D), const),
            pl.BlockSpec(memory_space=pl.ANY),
        ],
        out_specs=pl.BlockSpec((TB, D), tok),
        scratch_shapes=[pltpu.VMEM((2, BLK_PAIRS_BUF * SUB, LANES), U32), pltpu.SemaphoreType.DMA((2,))],
    )
    kern = functools.partial(_combine_kernel, blk0=blk0, n_blk=n_blk, blocks_per_seq=bps,
                             mod_row0=mod_row0, mod_row_step=mod_row_step)
    return pl.pallas_call(
        kern, grid_spec=grid_spec,
        out_shape=jax.ShapeDtypeStruct((n_seq * seq_len, D), F32),
        compiler_params=_params(("arbitrary",)),
        name="combine_lat" if blk0 else "combine_ctx",
    )(cnt, loc, glo, extra, lpos, topw, x1, mod, ln2g, ln2b, ys)


def _combine_all_kernel(cnt_ref, loc_ref, glo_ref, extra_ref, lpc_ref, lpl_ref, twc_ref, twl_ref, x1c_ref,
                        x1l_ref, mod_ref, g_ref, b_ref, ys_ref, oc_ref, ol_ref, buf, sem):
    i = pl.program_id(0)
    slot = i & 1

    def start_runs(blk, s):
        _start_piece_copies(cnt_ref, loc_ref, glo_ref, blk, buf.at[s], ys_ref, sem.at[s], False)

    @pl.when(i == 0)
    def _():
        buf[...] = jnp.zeros(buf.shape, U32)
        start_runs(0, 0)

    @pl.when(i + 1 < N_BLK)
    def _():
        start_runs(i + 1, 1 - slot)

    _wait_block_pieces(extra_ref, i, buf.at[slot], ys_ref, sem.at[slot], False)

    def finish(lp_ref, topw_ref, x1_ref, o_ref, gate2):
        y_sorted = _load_pairs(buf.at[slot], BLK_PAIRS_BUF)
        col = lax.broadcasted_iota(jnp.int32, (TB, BLK_ROWS_BUF), 1)
        lp = lp_ref[...]
        topw = topw_ref[...]
        weights = jnp.zeros((TB, BLK_ROWS_BUF), F32)
        for k in range(TOPK):
            weights = jnp.where(col == lp[:, k:k + 1], topw[:, k:k + 1], weights)
        ffn = _dot(weights.astype(BF16), y_sorted)
        y = DN_ALPHA * x1_ref[...] + gate2 * ffn
        o_ref[...] = _norm_noaffine(y) * g_ref[...] + b_ref[...]

    @pl.when(i < N_CTX_BLK)
    def _():
        finish(lpc_ref, twc_ref, x1c_ref, oc_ref, _mod_row(mod_ref, 0, 0, 0)[5])

    @pl.when(i >= N_CTX_BLK)
    def _():
        seq = (i - N_CTX_BLK) // (T_LAT // TB)
        finish(lpl_ref, twl_ref, x1l_ref, ol_ref, _mod_row(mod_ref, 1, 1, seq)[5])


def _combine_all(cnt, loc, glo, extra, lpc, lpl, twc, twl, x1c, x1l, mod, ln2g, ln2b, ys):
    ctx_idx = lambda i, *_: (jnp.minimum(i, N_CTX_BLK - 1), 0)
    lat_idx = lambda i, *_: (jnp.maximum(i - N_CTX_BLK, 0), 0)
    const = lambda i, *_: (0, 0)
    grid_spec = pltpu.PrefetchScalarGridSpec(
        num_scalar_prefetch=4,
        grid=(N_BLK,),
        in_specs=[
            pl.BlockSpec((TB, LANES), ctx_idx), pl.BlockSpec((TB, LANES), lat_idx),
            pl.BlockSpec((TB, LANES), ctx_idx), pl.BlockSpec((TB, LANES), lat_idx),
            pl.BlockSpec((TB, D), ctx_idx), pl.BlockSpec((TB, D), lat_idx),
            pl.BlockSpec((8, 6 * D), const), pl.BlockSpec((1, D), const), pl.BlockSpec((1, D), const),
            pl.BlockSpec(memory_space=pl.ANY),
        ],
        out_specs=[pl.BlockSpec((TB, D), ctx_idx), pl.BlockSpec((TB, D), lat_idx)],
        scratch_shapes=[pltpu.VMEM((2, BLK_PAIRS_BUF * SUB, LANES), U32), pltpu.SemaphoreType.DMA((2,))],
    )
    return pl.pallas_call(
        _combine_all_kernel, grid_spec=grid_spec,
        out_shape=[jax.ShapeDtypeStruct((N_CTX, D), F32), jax.ShapeDtypeStruct((N_LAT, D), F32)],
        compiler_params=_params(("arbitrary",)),
        name="combine_all",
    )(cnt, loc, glo, extra, lpc, lpl, twc, twl, x1c, x1l, mod, ln2g, ln2b, ys)


def _dft_tables():
    def cs(n):
        k = np.arange(n, dtype=np.int64)
        ang = 2.0 * np.pi * ((k[:, None] * k[None, :]) % n).astype(np.float64) / n
        return np.cos(ang) / math.sqrt(n), np.sin(ang) / math.sqrt(n)

    c, s = cs(FG)
    cdft = np.concatenate([c, s], axis=1).astype(np.float32)
    pd = {}
    for t in (T_CTX, T_LAT):
        c, s = cs(t)
        pd[t] = np.concatenate([c, -s], axis=1).astype(np.float32)
    tri = np.tril(np.ones((TB, TB), np.float32), k=-1)
    upper = np.triu(np.ones((LANES, LANES), np.float32), k=1)
    return cdft, pd, tri, upper


_ROT_PERM = np.array(list(range(8, 16)) + list(range(0, 8)) + list(range(24, 32)) + list(range(16, 24)))
_ROT_SIGN = np.array([-1.0] * 8 + [1.0] * 8 + [-1.0] * 8 + [1.0] * 8, np.float32)


def _rope_tables():
    rows = T_LAT // GRID_W
    row = jnp.repeat(jnp.arange(rows, dtype=F32), GRID_W)
    col = jnp.tile(jnp.arange(GRID_W, dtype=F32), rows)
    axis_dim = ROPE // 2
    inv_freq = ROPE_THETA ** (-jnp.arange(0, axis_dim, 2, dtype=F32) / axis_dim)
    ang_r = row[:, None] * inv_freq[None, :]
    ang_c = col[:, None] * inv_freq[None, :]
    ang = jnp.concatenate([ang_r, ang_r, ang_c, ang_c], axis=-1)
    pad = ((0, 0), (0, LANES - ROPE))
    return jnp.pad(jnp.cos(ang), pad), jnp.pad(jnp.sin(ang), pad)


def kernel(x_prompt, x_sample, cache_ckv, cache_krope, c, c_ctx, w_mod, b_mod, w_in, q_norm_g, w_q_up,
           kv_norm_g, w_kv_up, w_fourier_o, w_mla_o, w_gate, b_gate, w_out, ln1_g, ln1_b, w_router,
           b_router, w_gate_e, b_gate_e, w_up_e, b_up_e, w_down_e, b_down_e, ln2_g, ln2_b):
    cdft_np, pdft_np, tri_np, upper_np = _dft_tables()
    upper = jnp.asarray(upper_np)
    cdft = jnp.asarray(cdft_np).astype(BF16)
    pdft_ctx = jnp.asarray(pdft_np[T_CTX]).astype(BF16)
    pdft_lat = jnp.asarray(pdft_np[T_LAT]).astype(BF16)
    tri = jnp.asarray(tri_np).astype(BF16)
    cos_pad, sin_pad = _rope_tables()

    w_in0 = w_in[0]
    kr_w = w_in0[:, FW + QL + KVL:]
    lane_pad = ((0, 0), (0, LANES - ROPE))
    w_in_ext = jnp.concatenate(
        [w_in0[:, :FW + QL + KVL], jnp.pad(kr_w, lane_pad),
         jnp.pad(kr_w[:, _ROT_PERM] * _ROT_SIGN, lane_pad)], axis=1).astype(BF16)
    wq3 = w_q_up[0].reshape(QL, HEADS, NOPE + ROPE)
    wq_nope = jnp.transpose(wq3[:, :, :NOPE], (1, 0, 2))
    wq_rope = wq3[:, :, NOPE:]
    head_pad = ((0, 0), (0, 0), (0, LANES - ROPE))
    wqr = jnp.concatenate(
        [jnp.pad(wq_rope, head_pad).reshape(QL, HEADS * LANES),
         jnp.pad(wq_rope[:, :, _ROT_PERM] * _ROT_SIGN, head_pad).reshape(QL, HEADS * LANES)],
        axis=1).astype(BF16)
    wkv3 = w_kv_up[0].reshape(KVL, HEADS, NOPE + VH)
    wk = jnp.transpose(wkv3[:, :, :NOPE], (1, 0, 2))
    wv = jnp.transpose(wkv3[:, :, NOPE:], (1, 0, 2))
    wo3 = w_mla_o[0].reshape(HEADS, VH, D)
    wqa, woa = _prep(wq_nope, wk, wv, wo3)

    c_all = jnp.concatenate([c_ctx[None, :], c, jnp.zeros((8 - 1 - N_LAT_SEQ, D), F32)], axis=0)
    mod = _modulation(c_all, w_mod[0], b_mod)

    qg = q_norm_g
    kvg = kv_norm_g
    wfo = w_fourier_o[0].astype(BF16)
    wg = w_gate[0].astype(BF16)
    wout = w_out[0].astype(BF16)
    wr_f32 = jnp.pad(w_router[0], ((0, 0), (0, LANES - N_EXP)))
    wr_hi = wr_f32.astype(BF16)
    wr = jnp.concatenate([wr_hi, (wr_f32 - wr_hi.astype(F32)).astype(BF16)], axis=1)
    br = jnp.pad(b_router, ((0, 0), (0, LANES - N_EXP)), constant_values=NEG_BIG)
    cache_k = jnp.concatenate(
        [cache_ckv[:, 0], jnp.pad(cache_krope[:, 0], ((0, 0), (0, 0), (0, LANES - ROPE)))],
        axis=-1).astype(BF16)

    xc2d = x_prompt.reshape(N_CTX, D)
    xl2d = x_sample.reshape(N_LAT, D)

    new_ckv, new_krope, x1_c, h2_c, lpos_c, topw_c, bcnt_c = _ctx(
        xc2d, mod, w_in_ext, cdft, qg, kvg, wqa, wqr, pdft_ctx, wfo, woa, wg, b_gate, wout, ln1_g, ln1_b,
        wr, br, tri, upper)

    x1_l, h2_l, lpos_l, topw_l, bcnt_l = _lat(
        xl2d, mod, w_in_ext, cdft, qg, kvg, wqa, wqr, cos_pad, sin_pad, pdft_lat, cache_k, wfo, woa, wg,
        b_gate, wout, ln1_g, ln1_b, wr, br, tri, upper)

    blen = jnp.concatenate([bcnt_c[:, 0, :N_EXP], bcnt_l[:, 0, :N_EXP]], axis=0).astype(jnp.int32)
    blen = (blen + (blen & 1)) // 2
    lstart = jnp.cumsum(blen, axis=1) - blen
    count = jnp.sum(blen, axis=0)
    start = jnp.cumsum(count) - count
    gstart = start[None, :] + jnp.cumsum(blen, axis=0) - blen
    extra = (jnp.sum(blen, axis=1) - BLK_PAIRS).astype(jnp.int32)
    pieces = _piece_tables(lstart, blen, gstart) + (extra,)
    start = start.astype(jnp.int32)
    count = count.astype(jnp.int32)

    xs = _dispatch(*pieces, h2_c, h2_l, lpos_c, lpos_l)
    ys = _experts(start, count, xs,
                  w_gate_e, b_gate_e, w_up_e, b_up_e, w_down_e, b_down_e)

    y_c, y_l = _combine_all(*pieces, lpos_c, lpos_l, topw_c, topw_l, x1_c, x1_l, mod, ln2_g, ln2_b, ys)
    return (y_c.reshape(N_CTX_SEQ, T_CTX, D), y_l.reshape(N_LAT_SEQ, T_LAT, D), new_ckv, new_krope)
```
